```python
import math
import jax
import jax.numpy as jnp
from jax import lax
import numpy as np

D_MODEL = 2048
BATCH = 4
SEQ = 2048
DEPTH = 4

GRID_W = 64
CTX_LEN = 256
D_FF = 5632
N_MOD = 9
EPS = 1e-6

POOL_WIDTH = 512
POOL_WINDOWS = (2, 4, 8, 16)
POOL_GC = POOL_WIDTH // len(POOL_WINDOWS)

HEAD_DIM = 128
N_Q_HEADS = 4
N_KV_HEADS = 2
Q_GROUP = N_Q_HEADS // N_KV_HEADS
ATTN_WIDTH = N_Q_HEADS * HEAD_DIM
KV_WIDTH = N_KV_HEADS * HEAD_DIM
Q_BLOCK = 128
ROPE_THETA = 10000.0
ROPE_AXIS_DIM = HEAD_DIM // 2
ROPE_FREQS = ROPE_AXIS_DIM // 2

HYENA_WIDTH = 512
HYENA_EMB = 33
HYENA_BANDS = (HYENA_EMB - 1) // 2
HYENA_HIDDEN = 64
HYENA_TARGET = 1e-2
HYENA_FAST_PCT = 0.3
HYENA_SLOW_PCT = 1.5

S5_WIDTH = 512
S5_GC = 16
S5_GROUPS = S5_WIDTH // S5_GC
S5_STATE = 64

N_BRANCH = 4
BRANCH_WIDTH = 512

OFF_POOL = 0
OFF_Q = OFF_POOL + POOL_WIDTH
OFF_K = OFF_Q + ATTN_WIDTH
OFF_V = OFF_K + KV_WIDTH
OFF_HY = OFF_V + KV_WIDTH
OFF_S5 = OFF_HY + 3 * HYENA_WIDTH
IN_WIDTH = OFF_S5 + S5_WIDTH

kernel_name = 'hybrid_pool_gqa_hyena_s5_dit'

F32 = jnp.float32


def rmsnorm(x, g):
    x32 = x.astype(F32)
    y = x32 * lax.rsqrt(jnp.mean(x32 * x32, axis=-1, keepdims=True) + EPS)
    return (y * g.astype(F32)).astype(x.dtype)


def modulate(x, g, shift, scale):
    return rmsnorm(x, g) * (1 + scale) + shift


def swiglu(u, wi, wo):
    a, b = jnp.split(u @ wi, 2, axis=-1)
    return (jax.nn.silu(a) * b) @ wo


def ffn_sublayer(x, mod, g, wi, wo, base):
    u = modulate(x, g, mod[:, base], mod[:, base + 1])
    return x + 0.5 * mod[:, base + 2] * swiglu(u, wi, wo)


def rope_2d(x, rows, cols):
    freqs = ROPE_THETA ** (-jnp.arange(ROPE_FREQS, dtype=F32) / ROPE_FREQS)

    def rot(xh, pos):
        ang = pos[:, None] * freqs[None, :]
        cos = jnp.cos(ang)[None, :, None, :]
        sin = jnp.sin(ang)[None, :, None, :]
        x1, x2 = xh[..., :ROPE_FREQS], xh[..., ROPE_FREQS:]
        return jnp.concatenate([x1 * cos - x2 * sin, x1 * sin + x2 * cos], axis=-1)

    x32 = x.astype(F32)
    out = jnp.concatenate([rot(x32[..., :ROPE_AXIS_DIM], rows), rot(x32[..., ROPE_AXIS_DIM:], cols)], axis=-1)
    return out.astype(x.dtype)


def block_attention(q, k, v):
    b_, lq = q.shape[:2]
    nb = lq // Q_BLOCK
    qb = q.reshape(b_, nb, Q_BLOCK, N_KV_HEADS, Q_GROUP, HEAD_DIM).transpose(1, 0, 3, 4, 2, 5)
    scale = 1.0 / math.sqrt(HEAD_DIM)

    def one_block(qblk):
        s = jnp.einsum('bhgqd,bkhd->bhgqk', qblk, k).astype(F32) * scale
        p = jax.nn.softmax(s, axis=-1).astype(v.dtype)
        return jnp.einsum('bhgqk,bkhd->bhgqd', p, v)

    o = lax.map(one_block, qb)
    return o.transpose(1, 0, 4, 2, 3, 5).reshape(b_, lq, ATTN_WIDTH)


def pool_mix(a, pool_w, pool_scale):
    b_, l_, _ = a.shape
    a32 = a.astype(F32)
    cs = jnp.concatenate([jnp.zeros((b_, 1, POOL_WIDTH), F32), jnp.cumsum(a32, axis=1)], axis=1)
    t = jnp.arange(l_)
    means = []
    for gi, w in enumerate(POOL_WINDOWS):
        lo = jnp.clip(t - w // 2, 0, l_)
        hi = jnp.clip(t - w // 2 + w, 0, l_)
        cg = cs[..., gi * POOL_GC:(gi + 1) * POOL_GC]
        s = jnp.take(cg, hi, axis=1) - jnp.take(cg, lo, axis=1)
        means.append(s / (hi - lo).astype(F32)[None, :, None])
    pooled = jnp.concatenate(means, axis=-1) - a32
    y = jnp.einsum('blgc,gcd->blgd', pooled.reshape(b_, l_, len(POOL_WINDOWS), POOL_GC), pool_w.astype(F32))
    return (y.reshape(b_, l_, POOL_WIDTH) * pool_scale.astype(F32)).astype(a.dtype)


def short_conv3(x, w, b):
    xp = jnp.pad(x, ((0, 0), (1, 1), (0, 0)))
    return xp[:, :-2] * w[0] + xp[:, 1:-1] * w[1] + xp[:, 2:] * w[2] + b


def hyena_filter(l_, lp):
    t = jnp.linspace(0.0, 1.0, l_, dtype=F32)[:, None]
    f = jnp.linspace(1e-4, HYENA_BANDS - 1, HYENA_BANDS, dtype=F32)
    w = 2.0 * math.pi * jnp.arange(l_, dtype=F32) / l_
    fw = w[:, None] * f[None, :]
    z = jnp.concatenate([t, jnp.cos(fw), -jnp.sin(fw)], axis=-1)
    freq = lp['hy_freq'].astype(F32)
    h = jnp.sin(freq * (z @ lp['hy_f1_w'].astype(F32) + lp['hy_f1_b'].astype(F32)))
    h = jnp.sin(freq * (h @ lp['hy_f2_w'].astype(F32) + lp['hy_f2_b'].astype(F32)))
    h = h @ lp['hy_f3_w'].astype(F32)
    max_decay = math.log(HYENA_TARGET) / HYENA_FAST_PCT
    min_decay = math.log(HYENA_TARGET) / HYENA_SLOW_PCT
    deltas = jnp.abs(jnp.linspace(min_decay, max_decay, HYENA_WIDTH, dtype=F32))
    decay = jnp.exp(-t * deltas[None, :])
    return h[:, :HYENA_WIDTH] * decay, h[:, HYENA_WIDTH:] * decay


def hyena_mix(z_in, lp):
    l_ = z_in.shape[1]
    z = short_conv3(z_in, lp['hy_short_w'], lp['hy_short_b'])
    x0, x1, v = jnp.split(z, 3, axis=-1)
    hf, hb = hyena_filter(l_, lp)
    k2 = jnp.concatenate([hf, jnp.zeros((1, HYENA_WIDTH), F32), jnp.flip(hb[1:], axis=0)], axis=0)
    vx = (v * x1).astype(F32)
    n = 2 * l_
    y = jnp.fft.irfft(jnp.fft.rfft(vx, n=n, axis=1) * jnp.fft.rfft(k2, n=n, axis=0)[None], n=n, axis=1)[:, :l_]
    y = y + vx * lp['hy_bias'].astype(F32)
    return (y * x0.astype(F32)).astype(z_in.dtype)


def s5_direction(u, lp, d, h0, reverse, need_out):
    b_, l_, _ = u.shape
    ug = u.astype(F32).reshape(b_, l_, S5_GROUPS, S5_GC)
    if reverse:
        ug = jnp.flip(ug, axis=1)
    lam = lax.complex(lp['s5_a_re'][d].astype(F32), lp['s5_a_im'][d].astype(F32))
    dt = jnp.exp(lp['s5_log_dt'][d].astype(F32))[:, None]
    lam_dt = lam * dt
    a_bar = jnp.exp(lam_dt)
    b_mat = lax.complex(lp['s5_b_re'][d].astype(F32), lp['s5_b_im'][d].astype(F32))
    b_bar = ((a_bar - 1.0) / lam)[..., None] * b_mat
    bu = jnp.einsum('blgc,gpc->blgp', ug, b_bar)
    a_full = jnp.broadcast_to(a_bar, bu.shape)

    def binop(e1, e2):
        a1, b1 = e1
        a2, b2 = e2
        return a1 * a2, a2 * b1 + b2

    _, xs = lax.associative_scan(binop, (a_full, bu), axis=1)
    if h0 is not None:
        steps = jnp.arange(1, l_ + 1, dtype=F32)
        powers = jnp.exp(lam_dt[None] * steps[:, None, None])
        xs = xs + powers[None] * h0[:, None]
    h_last = xs[:, -1]
    if not need_out:
        return None, h_last
    c_mat = lax.complex(lp['s5_c_re'][d].astype(F32), lp['s5_c_im'][d].astype(F32))
    y = jnp.real(jnp.einsum('blgp,gcp->blgc', xs, c_mat)) + lp['s5_d'][d].astype(F32).reshape(S5_GROUPS, S5_GC) * ug
    if reverse:
        y = jnp.flip(y, axis=1)
    return y.reshape(b_, l_, S5_WIDTH).astype(u.dtype), h_last


def s5_glu(y, w, b):
    g = jax.nn.gelu(y) @ w + b
    a, gt = jnp.split(g, 2, axis=-1)
    return a * jax.nn.sigmoid(gt)


def token_mixer(u, lp, ctx_side, rows, cols, need_out):
    b_, l_, _ = u.shape
    proj = u @ lp['w_in']
    p_in = proj[..., OFF_POOL:OFF_Q]
    q = rmsnorm(proj[..., OFF_Q:OFF_K].reshape(b_, l_, N_Q_HEADS, HEAD_DIM), lp['q_norm'])
    k = rmsnorm(proj[..., OFF_K:OFF_V].reshape(b_, l_, N_KV_HEADS, HEAD_DIM), lp['k_norm'])
    v = proj[..., OFF_V:OFF_HY].reshape(b_, l_, N_KV_HEADS, HEAD_DIM)
    hy_in = proj[..., OFF_HY:OFF_S5]
    s5_in = proj[..., OFF_S5:IN_WIDTH]
    if rows is not None:
        q = rope_2d(q, rows, cols)
        k = rope_2d(k, rows, cols)
    h0_f = None if ctx_side is None else ctx_side[2]
    h0_b = None if ctx_side is None else ctx_side[3]
    y_f, h_f = s5_direction(s5_in, lp, 0, h0_f, False, need_out)
    y_b, h_b = s5_direction(s5_in, lp, 1, h0_b, True, need_out)
    if not need_out:
        return None, (k, v, h_f, h_b)
    if ctx_side is None:
        k_all, v_all = k, v
    else:
        k_all = jnp.concatenate([ctx_side[0], k], axis=1)
        v_all = jnp.concatenate([ctx_side[1], v], axis=1)
    y_attn = block_attention(q, k_all, v_all)
    y_pool = pool_mix(p_in, lp['pool_w'], lp['pool_scale'])
    y_hy = hyena_mix(hy_in, lp)
    y_s5 = s5_glu(y_f + y_b, lp['s5_glu_w'], lp['s5_glu_b'])
    gates = jax.nn.sigmoid((u @ lp['w_gate'] + lp['b_gate']).astype(F32)).astype(u.dtype)
    gates = gates.reshape(b_, l_, N_BRANCH, D_MODEL)
    branches = jnp.stack([y_pool, y_attn, y_hy, y_s5], axis=2)
    proj_b = jnp.einsum('blnc,ncd->blnd', branches, lp['w_branch'])
    merged = jnp.sum(gates * proj_b, axis=2)
    return merged @ lp['w_out'], (k, v, h_f, h_b)


def setup_inputs(seed: int = 0) -> dict:
    key = jax.random.key(seed)
    it = iter(list(jax.random.split(key, 48)))

    def nrm(shape, scale):
        return jax.random.normal(next(it), shape, F32) * scale

    def gain(shape):
        return 1.0 + nrm(shape, 0.01)

    dm = D_MODEL
    a_im = math.pi * jnp.arange(S5_STATE, dtype=F32)
    return {
        'x': nrm((BATCH, SEQ, dm), 1.0),
        'c': nrm((BATCH, dm), 1.0),
        'ctx': nrm((BATCH, CTX_LEN, dm), 1.0),
        'c_ctx': nrm((dm,), 1.0),
        'w_ada': nrm((DEPTH, dm, N_MOD * dm), 0.5 * dm ** -0.5),
        'b_ada': nrm((DEPTH, N_MOD * dm), 0.01),
        'norm_ffn1': gain((DEPTH, dm)),
        'norm_mix': gain((DEPTH, dm)),
        'norm_ffn2': gain((DEPTH, dm)),
        'norm_final': gain((dm,)),
        'ffn1_wi': nrm((DEPTH, dm, 2 * D_FF), dm ** -0.5),
        'ffn1_wo': nrm((DEPTH, D_FF, dm), D_FF ** -0.5),
        'ffn2_wi': nrm((DEPTH, dm, 2 * D_FF), dm ** -0.5),
        'ffn2_wo': nrm((DEPTH, D_FF, dm), D_FF ** -0.5),
        'w_in': nrm((DEPTH, dm, IN_WIDTH), dm ** -0.5),
        'w_gate': nrm((DEPTH, dm, N_BRANCH * dm), dm ** -0.5),
        'b_gate': nrm((DEPTH, N_BRANCH * dm), 0.01),
        'w_branch': nrm((DEPTH, N_BRANCH, BRANCH_WIDTH, dm), BRANCH_WIDTH ** -0.5),
        'w_out': nrm((DEPTH, dm, dm), dm ** -0.5),
        'pool_w': nrm((DEPTH, len(POOL_WINDOWS), POOL_GC, POOL_GC), POOL_GC ** -0.5),
        'pool_scale': gain((DEPTH, POOL_WIDTH)),
        'q_norm': gain((DEPTH, HEAD_DIM)),
        'k_norm': gain((DEPTH, HEAD_DIM)),
        'hy_short_w': nrm((DEPTH, 3, 3 * HYENA_WIDTH), 3 ** -0.5),
        'hy_short_b': nrm((DEPTH, 3 * HYENA_WIDTH), 0.01),
        'hy_f1_w': nrm((DEPTH, HYENA_EMB, HYENA_HIDDEN), HYENA_EMB ** -0.5),
        'hy_f1_b': nrm((DEPTH, HYENA_HIDDEN), 0.01),
        'hy_f2_w': nrm((DEPTH, HYENA_HIDDEN, HYENA_HIDDEN), HYENA_HIDDEN ** -0.5),
        'hy_f2_b': nrm((DEPTH, HYENA_HIDDEN), 0.01),
        'hy_f3_w': nrm((DEPTH, HYENA_HIDDEN, 2 * HYENA_WIDTH), 0.2 * HYENA_HIDDEN ** -0.5),
        'hy_freq': gain((DEPTH, HYENA_HIDDEN)),
        'hy_bias': nrm((DEPTH, HYENA_WIDTH), 0.1),
        's5_a_re': -0.5 + nrm((DEPTH, 2, S5_GROUPS, S5_STATE), 0.01),
        's5_a_im': a_im + nrm((DEPTH, 2, S5_GROUPS, S5_STATE), 0.01),
        's5_log_dt': jax.random.uniform(next(it), (DEPTH, 2, S5_GROUPS), F32, math.log(1e-3), math.log(1e-1)),
        's5_b_re': nrm((DEPTH, 2, S5_GROUPS, S5_STATE, S5_GC), (2 * S5_GC) ** -0.5),
        's5_b_im': nrm((DEPTH, 2, S5_GROUPS, S5_STATE, S5_GC), (2 * S5_GC) ** -0.5),
        's5_c_re': nrm((DEPTH, 2, S5_GROUPS, S5_GC, S5_STATE), S5_STATE ** -0.5),
        's5_c_im': nrm((DEPTH, 2, S5_GROUPS, S5_GC, S5_STATE), S5_STATE ** -0.5),
        's5_d': nrm((DEPTH, 2, S5_WIDTH), 1.0),
        's5_glu_w': nrm((DEPTH, S5_WIDTH, 2 * S5_WIDTH), S5_WIDTH ** -0.5),
        's5_glu_b': nrm((DEPTH, 2 * S5_WIDTH), 0.01),
    }


def reference(x, c, ctx, c_ctx, w_ada, b_ada, norm_ffn1, norm_mix, norm_ffn2, norm_final,
              ffn1_wi, ffn1_wo, ffn2_wi, ffn2_wo, w_in, w_gate, b_gate, w_branch, w_out,
              pool_w, pool_scale, q_norm, k_norm,
              hy_short_w, hy_short_b, hy_f1_w, hy_f1_b, hy_f2_w, hy_f2_b, hy_f3_w, hy_freq, hy_bias,
              s5_a_re, s5_a_im, s5_log_dt, s5_b_re, s5_b_im, s5_c_re, s5_c_im, s5_d, s5_glu_w, s5_glu_b):
    n_tok = x.shape[1]
    ROWS = n_tok // GRID_W
    rows = jnp.repeat(jnp.arange(ROWS, dtype=F32), GRID_W)
    cols = jnp.tile(jnp.arange(GRID_W, dtype=F32), ROWS)
    xl, xc = x, ctx
    for l in range(DEPTH):
        last = l == DEPTH - 1
        lp = {
            'w_in': w_in[l], 'w_gate': w_gate[l], 'b_gate': b_gate[l], 'w_branch': w_branch[l], 'w_out': w_out[l],
            'pool_w': pool_w[l], 'pool_scale': pool_scale[l], 'q_norm': q_norm[l], 'k_norm': k_norm[l],
            'hy_short_w': hy_short_w[l], 'hy_short_b': hy_short_b[l], 'hy_f1_w': hy_f1_w[l], 'hy_f1_b': hy_f1_b[l],
            'hy_f2_w': hy_f2_w[l], 'hy_f2_b': hy_f2_b[l], 'hy_f3_w': hy_f3_w[l], 'hy_freq': hy_freq[l],
            'hy_bias': hy_bias[l],
            's5_a_re': s5_a_re[l], 's5_a_im': s5_a_im[l], 's5_log_dt': s5_log_dt[l], 's5_b_re': s5_b_re[l],
            's5_b_im': s5_b_im[l], 's5_c_re': s5_c_re[l], 's5_c_im': s5_c_im[l], 's5_d': s5_d[l],
            's5_glu_w': s5_glu_w[l], 's5_glu_b': s5_glu_b[l],
        }
        mod_l = (jax.nn.silu(c) @ w_ada[l] + b_ada[l]).reshape(-1, N_MOD, 1, D_MODEL)
        mod_c = (jax.nn.silu(c_ctx)[None] @ w_ada[l] + b_ada[l]).reshape(1, N_MOD, 1, D_MODEL)
        xl = ffn_sublayer(xl, mod_l, norm_ffn1[l], ffn1_wi[l], ffn1_wo[l], 0)
        xc = ffn_sublayer(xc, mod_c, norm_ffn1[l], ffn1_wi[l], ffn1_wo[l], 0)
        uc = modulate(xc, norm_mix[l], mod_c[:, 3], mod_c[:, 4])
        ul = modulate(xl, norm_mix[l], mod_l[:, 3], mod_l[:, 4])
        yc, ctx_side = token_mixer(uc, lp, None, None, None, not last)
        yl, _ = token_mixer(ul, lp, ctx_side, rows, cols, True)
        xl = xl + mod_l[:, 5] * yl
        xl = ffn_sublayer(xl, mod_l, norm_ffn2[l], ffn2_wi[l], ffn2_wo[l], 6)
        if not last:
            xc = xc + mod_c[:, 5] * yc
            xc = ffn_sublayer(xc, mod_c, norm_ffn2[l], ffn2_wi[l], ffn2_wo[l], 6)
    return rmsnorm(xl, norm_final)
```

```python
import functools
import math

import jax
import jax.numpy as jnp
from jax import lax
from jax.experimental import pallas as pl
from jax.experimental.pallas import tpu as pltpu

F32 = jnp.float32
BF16 = jnp.bfloat16

D_MODEL = 2048
BATCH = 4
SEQ = 2048
DEPTH = 4
GRID_W = 64
CTX_LEN = 256
D_FF = 5632
N_MOD = 9
EPS = 1e-6

POOL_WINDOWS = (2, 4, 8, 16)
HEAD_DIM = 128
N_Q_HEADS = 4
N_KV_HEADS = 2
Q_GROUP = N_Q_HEADS // N_KV_HEADS
ROPE_THETA = 10000.0
ROPE_FREQS = HEAD_DIM // 4

HYENA_WIDTH = 512
HYENA_EMB = 33
HYENA_BANDS = (HYENA_EMB - 1) // 2
HYENA_HIDDEN = 64
HYENA_TARGET = 1e-2
HYENA_FAST_PCT = 0.3
HYENA_SLOW_PCT = 1.5

S5_WIDTH = 512
S5_GC = 16
S5_GROUPS = S5_WIDTH // S5_GC
S5_STATE = 64
S5_LANES = 8
S5_CHUNK_GROUPS = 8
S5_CHUNK_CH = S5_CHUNK_GROUPS * S5_GC
S5_CHUNK_ST = S5_CHUNK_GROUPS * S5_STATE
S5_NCHUNK = S5_GROUPS // S5_CHUNK_GROUPS

N_BRANCH = 4
BRANCH_WIDTH = 512
LANE = 128
COL = 512

OFF_Q = 512
OFF_KV = 1024
OFF_HY = 1536
OFF_S5 = 3072
IN_WIDTH = 3584

N_LAT = BATCH * SEQ
N_CTX = BATCH * CTX_LEN
MT = N_LAT + N_CTX

TM = 1024
TM_EW = 512
VMEM_LIMIT = 56 * 1024 * 1024


def _cparams(*sem):
    return pltpu.CompilerParams(dimension_semantics=sem, vmem_limit_bytes=VMEM_LIMIT)


def _dot(a, b):
    return jnp.dot(a, b, preferred_element_type=F32)


def _mod_row(i, tm):
    return jnp.minimum((i * tm) // SEQ, BATCH)


def _mod_spec(tm, k):
    return pl.BlockSpec((None, None, 1, D_MODEL), lambda i, *_: (_mod_row(i, tm), k, 0, 0))


def _ada_body(c_ref, w_ref, b_ref, o_ref):
    c = c_ref[...]
    a = (c * jax.nn.sigmoid(c)).astype(BF16)
    o_ref[...] = _dot(a, w_ref[...].astype(BF16)) + b_ref[...]


def _ada(cc, w_ada, b_ada):
    tn = 1024
    nw = N_MOD * D_MODEL
    return pl.pallas_call(
        _ada_body,
        grid=(DEPTH, nw // tn),
        in_specs=[
            pl.BlockSpec((8, D_MODEL), lambda l, j: (0, 0)),
            pl.BlockSpec((None, D_MODEL, tn), lambda l, j: (l, 0, j)),
            pl.BlockSpec((None, 1, tn), lambda l, j: (l, 0, j)),
        ],
        out_specs=pl.BlockSpec((None, 8, tn), lambda l, j: (l, 0, j)),
        out_shape=jax.ShapeDtypeStruct((DEPTH, 8, nw), F32),
        compiler_params=_cparams("arbitrary", "arbitrary"),
        name="ada",
    )(cc, w_ada, b_ada.reshape(DEPTH, 1, nw))


def _normmod_body(x_ref, g_ref, sh_ref, sc_ref, o_ref):
    x = x_ref[...]
    y = x * lax.rsqrt(jnp.mean(x * x, axis=-1, keepdims=True) + EPS) * g_ref[...]
    o_ref[...] = (y * (1.0 + sc_ref[...]) + sh_ref[...]).astype(o_ref.dtype)


def _norm_body(x_ref, g_ref, o_ref):
    x = x_ref[...]
    y = x * lax.rsqrt(jnp.mean(x * x, axis=-1, keepdims=True) + EPS) * g_ref[...]
    o_ref[...] = y.astype(o_ref.dtype)


def _normmod(x, g, mods, base, n_rows):
    tm = TM_EW
    return pl.pallas_call(
        _normmod_body,
        grid=(n_rows // tm,),
        in_specs=[
            pl.BlockSpec((tm, D_MODEL), lambda i: (i, 0)),
            pl.BlockSpec((1, D_MODEL), lambda i: (0, 0)),
            _mod_spec(tm, base),
            _mod_spec(tm, base + 1),
        ],
        out_specs=pl.BlockSpec((tm, D_MODEL), lambda i: (i, 0)),
        out_shape=jax.ShapeDtypeStruct((n_rows, D_MODEL), BF16),
        compiler_params=_cparams("arbitrary"),
        name="normmod",
    )(x, g.reshape(1, D_MODEL), mods, mods)


def _final_norm(x, g, n_rows):
    tm = TM_EW
    return pl.pallas_call(
        _norm_body,
        grid=(n_rows // tm,),
        in_specs=[
            pl.BlockSpec((tm, D_MODEL), lambda i: (i, 0)),
            pl.BlockSpec((1, D_MODEL), lambda i: (0, 0)),
        ],
        out_specs=pl.BlockSpec((tm, D_MODEL), lambda i: (i, 0)),
        out_shape=jax.ShapeDtypeStruct((n_rows, D_MODEL), F32),
        compiler_params=_cparams("arbitrary"),
        name="final_norm",
    )(x, g.reshape(1, D_MODEL))


def _ffn_a_body(u_ref, wa_ref, wb_ref, h_ref):
    u = u_ref[...]
    a = _dot(u, wa_ref[...].astype(BF16))
    b = _dot(u, wb_ref[...].astype(BF16))
    h_ref[...] = (a * jax.nn.sigmoid(a) * b).astype(BF16)


def _ffn_a(u, wi, l, n_rows):
    tf = 512
    nf = D_FF // tf
    return pl.pallas_call(
        _ffn_a_body,
        grid=(n_rows // TM, nf),
        in_specs=[
            pl.BlockSpec((TM, D_MODEL), lambda i, j: (i, 0)),
            pl.BlockSpec((None, D_MODEL, tf), lambda i, j: (l, 0, j)),
            pl.BlockSpec((None, D_MODEL, tf), lambda i, j: (l, 0, j + nf)),
        ],
        out_specs=pl.BlockSpec((TM, tf), lambda i, j: (i, j)),
        out_shape=jax.ShapeDtypeStruct((n_rows, D_FF), BF16),
        compiler_params=_cparams("arbitrary", "arbitrary"),
        name="ffn_a",
    )(u, wi, wi)


def _ffn_b_body(h_ref, w_ref, x_ref, g_ref, o_ref):
    k = pl.program_id(2)

    @pl.when(k == 0)
    def _():
        o_ref[...] = _dot(h_ref[...], w_ref[...].astype(BF16))

    @pl.when(k > 0)
    def _():
        o_ref[...] += _dot(h_ref[...], w_ref[...].astype(BF16))

    @pl.when(k == pl.num_programs(2) - 1)
    def _():
        o_ref[...] = x_ref[...] + (0.5 * g_ref[...]) * o_ref[...]


def _ffn_b(h, wo, l, x, mods, gate_idx, n_rows):
    tk = 512
    tn = 1024
    gate = pl.BlockSpec((None, None, 1, tn), lambda i, j, k: (_mod_row(i, TM), gate_idx, 0, j))
    return pl.pallas_call(
        _ffn_b_body,
        grid=(n_rows // TM, D_MODEL // tn, D_FF // tk),
        in_specs=[
            pl.BlockSpec((TM, tk), lambda i, j, k: (i, k)),
            pl.BlockSpec((None, tk, tn), lambda i, j, k: (l, k, j)),
            pl.BlockSpec((TM, tn), lambda i, j, k: (i, j)),
            gate,
        ],
        out_specs=pl.BlockSpec((TM, tn), lambda i, j, k: (i, j)),
        out_shape=jax.ShapeDtypeStruct((n_rows, D_MODEL), F32),
        compiler_params=_cparams("arbitrary", "arbitrary", "arbitrary"),
        name="ffn_b",
    )(h, wo, x, mods)


def _ffn(x, mods, g, wi, wo, l, base, n_rows):
    u = _normmod(x, g, mods, base, n_rows)
    h = _ffn_a(u, wi, l, n_rows)
    return _ffn_b(h, wo, l, x, mods, base + 2, n_rows)


def _proj_body(u_ref, w_ref, o_ref):
    o_ref[...] = _dot(u_ref[...], w_ref[...].astype(BF16))


def _proj(u, w_in, l):
    tn = COL
    return pl.pallas_call(
        _proj_body,
        grid=(MT // TM, IN_WIDTH // tn),
        in_specs=[
            pl.BlockSpec((TM, D_MODEL), lambda i, j: (i, 0)),
            pl.BlockSpec((None, D_MODEL, tn), lambda i, j: (l, 0, j)),
        ],
        out_specs=pl.BlockSpec((TM, tn), lambda i, j: (i, j)),
        out_shape=jax.ShapeDtypeStruct((MT, IN_WIDTH), F32),
        compiler_params=_cparams("arbitrary", "arbitrary"),
        name="proj",
    )(u, w_in)


def _rope_tables():
    t = jnp.arange(SEQ)
    rows = (t // GRID_W).astype(F32)
    cols = (t % GRID_W).astype(F32)
    freqs = ROPE_THETA ** (-jnp.arange(ROPE_FREQS, dtype=F32) / ROPE_FREQS)
    ar = rows[:, None] * freqs[None, :]
    ac = cols[:, None] * freqs[None, :]
    cos = jnp.concatenate([jnp.cos(ar), jnp.cos(ar), jnp.cos(ac), jnp.cos(ac)], axis=-1)
    sin = jnp.concatenate([-jnp.sin(ar), jnp.sin(ar), -jnp.sin(ac), jnp.sin(ac)], axis=-1)
    cos = jnp.concatenate([cos, jnp.ones((TM_EW, HEAD_DIM), F32)], axis=0)
    sin = jnp.concatenate([sin, jnp.zeros((TM_EW, HEAD_DIM), F32)], axis=0)
    return cos, sin


def _qkv_body(q_ref, kv_ref, cos_ref, sin_ref, qn_ref, kn_ref, qo_ref, ko_ref, vo_ref):
    cos = cos_ref[...]
    sin = sin_ref[...]
    lane = lax.broadcasted_iota(jnp.int32, cos.shape, 1)
    first = (lane % (2 * ROPE_FREQS)) < ROPE_FREQS

    def norm_rope(xh, g):
        y = xh * lax.rsqrt(jnp.mean(xh * xh, axis=-1, keepdims=True) + EPS) * g
        swapped = jnp.where(first, pltpu.roll(y, HEAD_DIM - ROPE_FREQS, 1), pltpu.roll(y, ROPE_FREQS, 1))
        return y * cos + swapped * sin

    for h in range(N_Q_HEADS):
        sl = slice(h * HEAD_DIM, (h + 1) * HEAD_DIM)
        qo_ref[:, sl] = norm_rope(q_ref[:, sl], qn_ref[...]).astype(BF16)
    for h in range(N_KV_HEADS):
        sl = slice(h * HEAD_DIM, (h + 1) * HEAD_DIM)
        ko_ref[:, sl] = norm_rope(kv_ref[:, sl], kn_ref[...]).astype(BF16)
    kvw = N_KV_HEADS * HEAD_DIM
    vo_ref[...] = kv_ref[:, kvw:].astype(BF16)


def _qkv(proj, cos, sin, q_norm, k_norm):
    tm = TM_EW
    n_lat_tiles = N_LAT // tm
    per_seq = SEQ // tm
    kvw = N_KV_HEADS * HEAD_DIM

    def tab(i):
        return (jnp.where(i < n_lat_tiles, i % per_seq, per_seq), 0)

    return pl.pallas_call(
        _qkv_body,
        grid=(MT // tm,),
        in_specs=[
            pl.BlockSpec((tm, COL), lambda i: (i, OFF_Q // COL)),
            pl.BlockSpec((tm, COL), lambda i: (i, OFF_KV // COL)),
            pl.BlockSpec((tm, HEAD_DIM), tab),
            pl.BlockSpec((tm, HEAD_DIM), tab),
            pl.BlockSpec((1, HEAD_DIM), lambda i: (0, 0)),
            pl.BlockSpec((1, HEAD_DIM), lambda i: (0, 0)),
        ],
        out_specs=[
            pl.BlockSpec((tm, COL), lambda i: (i, 0)),
            pl.BlockSpec((tm, kvw), lambda i: (i, 0)),
            pl.BlockSpec((tm, kvw), lambda i: (i, 0)),
        ],
        out_shape=[
            jax.ShapeDtypeStruct((MT, COL), BF16),
            jax.ShapeDtypeStruct((MT, kvw), BF16),
            jax.ShapeDtypeStruct((MT, kvw), BF16),
        ],
        compiler_params=_cparams("arbitrary"),
        name="qkv",
    )(proj, proj, cos, sin, q_norm.reshape(1, HEAD_DIM), k_norm.reshape(1, HEAD_DIM))


def _attn_body(*refs, with_lat):
    if with_lat:
        q_ref, kl_ref, vl_ref, kc_ref, vc_ref, o_ref = refs
    else:
        q_ref, kc_ref, vc_ref, o_ref = refs
    scale = 1.0 / math.sqrt(HEAD_DIM)
    nt = (((1,), (1,)), ((), ()))
    for g in range(Q_GROUP):
        sl = slice(g * HEAD_DIM, (g + 1) * HEAD_DIM)
        q = q_ref[:, sl]
        sc = lax.dot_general(q, kc_ref[...], nt, preferred_element_type=F32) * scale
        m = jnp.max(sc, axis=-1, keepdims=True)
        if with_lat:
            s_lat = lax.dot_general(q, kl_ref[...], nt, preferred_element_type=F32) * scale
            m = jnp.maximum(m, jnp.max(s_lat, axis=-1, keepdims=True))
            e_lat = jnp.exp(s_lat - m)
        ec = jnp.exp(sc - m)
        den = jnp.sum(ec, axis=-1, keepdims=True)
        o = _dot(ec.astype(BF16), vc_ref[...])
        if with_lat:
            den = den + jnp.sum(e_lat, axis=-1, keepdims=True)
            o = o + _dot(e_lat.astype(BF16), vl_ref[...])
        o_ref[:, sl] = (o / den).astype(BF16)


def _attn_lat(q, k, v):
    tq = 512
    nq = SEQ // tq
    ctx_blk = N_LAT // CTX_LEN
    gw = Q_GROUP * HEAD_DIM
    return pl.pallas_call(
        functools.partial(_attn_body, with_lat=True),
        grid=(BATCH, N_KV_HEADS, nq),
        in_specs=[
            pl.BlockSpec((tq, gw), lambda b, h, i: (b * nq + i, h)),
            pl.BlockSpec((SEQ, HEAD_DIM), lambda b, h, i: (b, h)),
            pl.BlockSpec((SEQ, HEAD_DIM), lambda b, h, i: (b, h)),
            pl.BlockSpec((CTX_LEN, HEAD_DIM), lambda b, h, i: (ctx_blk + b, h)),
            pl.BlockSpec((CTX_LEN, HEAD_DIM), lambda b, h, i: (ctx_blk + b, h)),
        ],
        out_specs=pl.BlockSpec((tq, gw), lambda b, h, i: (b * nq + i, h)),
        out_shape=jax.ShapeDtypeStruct((N_LAT, N_Q_HEADS * HEAD_DIM), BF16),
        compiler_params=_cparams("arbitrary", "arbitrary", "arbitrary"),
        name="attn_lat",
    )(q, k, v, k, v)


def _attn_ctx(q, k, v):
    ctx_blk = N_LAT // CTX_LEN
    gw = Q_GROUP * HEAD_DIM
    return pl.pallas_call(
        functools.partial(_attn_body, with_lat=False),
        grid=(BATCH, N_KV_HEADS),
        in_specs=[
            pl.BlockSpec((CTX_LEN, gw), lambda b, h: (ctx_blk + b, h)),
            pl.BlockSpec((CTX_LEN, HEAD_DIM), lambda b, h: (ctx_blk + b, h)),
            pl.BlockSpec((CTX_LEN, HEAD_DIM), lambda b, h: (ctx_blk + b, h)),
        ],
        out_specs=pl.BlockSpec((CTX_LEN, gw), lambda b, h: (b, h)),
        out_shape=jax.ShapeDtypeStruct((N_CTX, N_Q_HEADS * HEAD_DIM), BF16),
        compiler_params=_cparams("arbitrary", "arbitrary"),
        name="attn_ctx",
    )(q, k, v)


POOL_PAD = 8


def _pool_body(a_ref, w_ref, s_ref, o_ref, *, seq):
    lp = seq + 2 * POOL_PAD
    t = lax.broadcasted_iota(jnp.int32, (seq, LANE), 0)
    zpad = jnp.zeros((POOL_PAD, LANE), F32)
    for gi, win in enumerate(POOL_WINDOWS):
        sl = slice(gi * LANE, (gi + 1) * LANE)
        a = a_ref[:, sl]
        s = jnp.concatenate([zpad, a, zpad], axis=0)
        s = s + pltpu.roll(s, 1, 0)
        half = 1
        while 2 * half < win:
            s = pltpu.roll(s, half, 0) + pltpu.roll(s, lp - half, 0)
            half *= 2
        s = s[POOL_PAD:POOL_PAD + seq]
        lo = jnp.maximum(t - win // 2, 0)
        hi = jnp.minimum(t + win // 2, seq)
        pooled = s / (hi - lo).astype(F32) - a
        y = _dot(pooled.astype(BF16), w_ref[gi].astype(BF16))
        o_ref[:, sl] = (y * s_ref[:, sl]).astype(BF16)


def _pool(proj, pool_w, pool_scale, seq, row_blk0):
    width = len(POOL_WINDOWS) * LANE
    return pl.pallas_call(
        functools.partial(_pool_body, seq=seq),
        grid=(BATCH,),
        in_specs=[
            pl.BlockSpec((seq, width), lambda b: (row_blk0 + b, 0)),
            pl.BlockSpec((len(POOL_WINDOWS), LANE, LANE), lambda b: (0, 0, 0)),
            pl.BlockSpec((1, width), lambda b: (0, 0)),
        ],
        out_specs=pl.BlockSpec((seq, width), lambda b: (b, 0)),
        out_shape=jax.ShapeDtypeStruct((BATCH * seq, width), BF16),
        compiler_params=_cparams("arbitrary"),
        name="pool",
    )(proj, pool_w, pool_scale.reshape(1, width))


def _split_bf16(x):
    hi = x.astype(BF16)
    lo = (x - hi.astype(F32)).astype(BF16)
    return hi, lo


def _hy_prep_body(x0_ref, x1_ref, v_ref, w0_ref, w1_ref, wv_ref, b0_ref, b1_ref, bv_ref,
                  x0o_ref, vh_ref, vl_ref, *, seq):
    t = lax.broadcasted_iota(jnp.int32, (seq, LANE), 0)

    def conv(x_ref, w_ref, b_ref):
        x = x_ref[...]
        prev = jnp.where(t >= 1, pltpu.roll(x, 1, 0), 0.0)
        nxt = jnp.where(t <= seq - 2, pltpu.roll(x, seq - 1, 0), 0.0)
        return prev * w_ref[0:1, :] + x * w_ref[1:2, :] + nxt * w_ref[2:3, :] + b_ref[...]

    x0o_ref[...] = conv(x0_ref, w0_ref, b0_ref)
    vx = conv(v_ref, wv_ref, bv_ref) * conv(x1_ref, w1_ref, b1_ref)
    hi, lo = _split_bf16(vx)
    vh_ref[...] = hi
    vl_ref[...] = lo


def _hy_prep(proj, short_w, short_b, seq, row_blk0):
    nc = HYENA_WIDTH // LANE
    c0 = OFF_HY // LANE
    short_b = short_b.reshape(1, 3 * HYENA_WIDTH)

    def xspec(part):
        return pl.BlockSpec((seq, LANE), lambda b, c: (row_blk0 + b, c0 + part * nc + c))

    def wspec(part, rows):
        return pl.BlockSpec((rows, LANE), lambda b, c: (0, part * nc + c))

    return pl.pallas_call(
        functools.partial(_hy_prep_body, seq=seq),
        grid=(BATCH, nc),
        in_specs=[xspec(0), xspec(1), xspec(2), wspec(0, 3), wspec(1, 3), wspec(2, 3),
                  wspec(0, 1), wspec(1, 1), wspec(2, 1)],
        out_specs=[
            pl.BlockSpec((seq, LANE), lambda b, c: (b, c)),
            pl.BlockSpec((seq, LANE), lambda b, c: (0, b * nc + c)),
            pl.BlockSpec((seq, LANE), lambda b, c: (0, b * nc + c)),
        ],
        out_shape=[
            jax.ShapeDtypeStruct((BATCH * seq, HYENA_WIDTH), F32),
            jax.ShapeDtypeStruct((seq, BATCH * HYENA_WIDTH), BF16),
            jax.ShapeDtypeStruct((seq, BATCH * HYENA_WIDTH), BF16),
        ],
        compiler_params=_cparams("arbitrary", "arbitrary"),
        name="hy_prep",
    )(proj, proj, proj, short_w, short_w, short_w, short_b, short_b, short_b)


def _hy_feats(seq):
    t = jnp.linspace(0.0, 1.0, seq, dtype=F32)[:, None]
    f = jnp.linspace(1e-4, HYENA_BANDS - 1, HYENA_BANDS, dtype=F32)
    w = 2.0 * math.pi * jnp.arange(seq, dtype=F32) / seq
    fw = w[:, None] * f[None, :]
    z = jnp.concatenate([t, jnp.cos(fw), -jnp.sin(fw)], axis=-1)
    return jnp.pad(z, ((0, 0), (0, LANE - HYENA_EMB)))


def _hy_deltas():
    max_decay = math.log(HYENA_TARGET) / HYENA_FAST_PCT
    min_decay = math.log(HYENA_TARGET) / HYENA_SLOW_PCT
    return jnp.abs(jnp.linspace(min_decay, max_decay, HYENA_WIDTH, dtype=F32)).reshape(1, HYENA_WIDTH)


def _hy_filter_body(z_ref, w1_ref, b1_ref, w2_ref, b2_ref, w3_ref, fr_ref, dl_ref, kh_ref, kl_ref, *, seq):
    hp = lax.Precision.HIGHEST
    freq = fr_ref[...]
    h = jnp.sin(freq * (jnp.dot(z_ref[...], w1_ref[...], precision=hp, preferred_element_type=F32) + b1_ref[...]))
    h = jnp.sin(freq * (jnp.dot(h, w2_ref[...], precision=hp, preferred_element_type=F32) + b2_ref[...]))
    h = jnp.dot(h, w3_ref[...], precision=hp, preferred_element_type=F32)
    ti = lax.broadcasted_iota(jnp.int32, (seq, HYENA_WIDTH), 0)
    decay = jnp.exp(-(ti.astype(F32) * (1.0 / (seq - 1))) * dl_ref[...])
    hf = h[:, :HYENA_WIDTH] * decay
    hb = jnp.where(ti == 0, 0.0, h[:, HYENA_WIDTH:] * decay)
    for part, val in enumerate((hf + hb, hf - hb)):
        hi, lo = _split_bf16(val)
        sl = slice(part * HYENA_WIDTH, (part + 1) * HYENA_WIDTH)
        kh_ref[:, sl] = hi
        kl_ref[:, sl] = lo


def _hy_filter(lp, z, deltas, seq):
    w1 = jnp.pad(lp['hy_f1_w'], ((0, LANE - HYENA_EMB), (0, 0)))
    args = (z, w1, lp['hy_f1_b'].reshape(1, -1), lp['hy_f2_w'], lp['hy_f2_b'].reshape(1, -1), lp['hy_f3_w'],
            lp['hy_freq'].reshape(1, -1), deltas)
    full = lambda a: pl.BlockSpec(a.shape, lambda i: (0,) * a.ndim)
    out = jax.ShapeDtypeStruct((seq, 2 * HYENA_WIDTH), BF16)
    return pl.pallas_call(
        functools.partial(_hy_filter_body, seq=seq),
        grid=(1,),
        in_specs=[full(a) for a in args],
        out_specs=[pl.BlockSpec((seq, 2 * HYENA_WIDTH), lambda i: (0, 0))] * 2,
        out_shape=[out, out],
        compiler_params=_cparams("arbitrary"),
        name="hy_filter",
    )(*args)


def _dft_matrices(seq):
    n = 2 * seq
    r = jnp.arange(n, dtype=jnp.int32)[:, None]
    s = jnp.arange(seq, dtype=jnp.int32)[None, :]
    f = r % seq
    ang = ((f * s) % n).astype(F32) * (2.0 * math.pi / n)
    nyq = jnp.where(s % 2 == 0, 1.0, -1.0).astype(F32)
    wf = jnp.where(r < seq, jnp.cos(ang), jnp.where(r == seq, nyq, -jnp.sin(ang)))
    colw = jnp.where((r == 0) | (r == seq), 1.0 / n, 2.0 / n).astype(F32)
    wi = (wf * colw).T
    return _split_bf16(wf) + _split_bf16(wi)


def _mm3(ah, al, bh, bl):
    return _dot(ah, bh) + _dot(al, bh) + _dot(ah, bl)


def _dft_fwd_body(wh_ref, wl_ref, xh_ref, xl_ref, o_ref):
    o_ref[...] = _mm3(wh_ref[...], wl_ref[...], xh_ref[...], xl_ref[...])


def _dft_fwd(wh, wl, xh, xl):
    n, seq = wh.shape
    ncol = xh.shape[1]
    tm = min(n, 1024)
    tn = min(ncol, 512)
    return pl.pallas_call(
        _dft_fwd_body,
        grid=(ncol // tn, n // tm),
        in_specs=[
            pl.BlockSpec((tm, seq), lambda j, i: (i, 0)),
            pl.BlockSpec((tm, seq), lambda j, i: (i, 0)),
            pl.BlockSpec((seq, tn), lambda j, i: (0, j)),
            pl.BlockSpec((seq, tn), lambda j, i: (0, j)),
        ],
        out_specs=pl.BlockSpec((tm, tn), lambda j, i: (i, j)),
        out_shape=jax.ShapeDtypeStruct((n, ncol), F32),
        compiler_params=_cparams("arbitrary", "arbitrary"),
        name="dft_fwd",
    )(wh, wl, xh, xl)


def _hy_prod_body(x_ref, hs_ref, hd_ref, ph_ref, pl_ref, *, seq):
    xr = x_ref[0:seq, :]
    xi = x_ref[seq:, :]
    hr = hs_ref[0:seq, :]
    hi = hd_ref[seq:, :]
    hn = hs_ref[seq:seq + 1, :]
    first = lax.broadcasted_iota(jnp.int32, xr.shape, 0) == 0
    pr = jnp.where(first, xr * hr, xr * hr - xi * hi)
    pi = jnp.where(first, xi * hn, xr * hi + xi * hr)
    for sl, val in ((slice(0, seq), pr), (slice(seq, 2 * seq), pi)):
        h, l = _split_bf16(val)
        ph_ref[sl, :] = h
        pl_ref[sl, :] = l


def _hy_prod(xf, hf, seq):
    n = 2 * seq
    nc = HYENA_WIDTH // LANE
    out = jax.ShapeDtypeStruct((n, BATCH * HYENA_WIDTH), BF16)
    return pl.pallas_call(
        functools.partial(_hy_prod_body, seq=seq),
        grid=(BATCH, nc),
        in_specs=[
            pl.BlockSpec((n, LANE), lambda b, c: (0, b * nc + c)),
            pl.BlockSpec((n, LANE), lambda b, c: (0, c)),
            pl.BlockSpec((n, LANE), lambda b, c: (0, nc + c)),
        ],
        out_specs=[pl.BlockSpec((n, LANE), lambda b, c: (0, b * nc + c))] * 2,
        out_shape=[out, out],
        compiler_params=_cparams("arbitrary", "arbitrary"),
        name="hy_prod",
    )(xf, hf, hf)


def _dft_inv_body(wh_ref, wl_ref, ph_ref, pl_ref, vh_ref, vl_ref, bias_ref, x0_ref, o_ref):
    y = _mm3(wh_ref[...], wl_ref[...], ph_ref[...], pl_ref[...])
    vx = vh_ref[...].astype(F32) + vl_ref[...].astype(F32)
    o_ref[...] = ((y + vx * bias_ref[...]) * x0_ref[...]).astype(BF16)


def _dft_inv(wh, wl, p_hi, p_lo, vh, vl, bias, x0):
    seq, n = wh.shape
    tm = min(seq, 512)
    tn = HYENA_WIDTH
    nt = seq // tm
    return pl.pallas_call(
        _dft_inv_body,
        grid=(BATCH, nt),
        in_specs=[
            pl.BlockSpec((tm, n), lambda b, i: (i, 0)),
            pl.BlockSpec((tm, n), lambda b, i: (i, 0)),
            pl.BlockSpec((n, tn), lambda b, i: (0, b)),
            pl.BlockSpec((n, tn), lambda b, i: (0, b)),
            pl.BlockSpec((tm, tn), lambda b, i: (i, b)),
            pl.BlockSpec((tm, tn), lambda b, i: (i, b)),
            pl.BlockSpec((1, tn), lambda b, i: (0, 0)),
            pl.BlockSpec((tm, tn), lambda b, i: (b * nt + i, 0)),
        ],
        out_specs=pl.BlockSpec((tm, tn), lambda b, i: (b * nt + i, 0)),
        out_shape=jax.ShapeDtypeStruct((BATCH * seq, HYENA_WIDTH), BF16),
        compiler_params=_cparams("arbitrary", "arbitrary"),
        name="dft_inv",
    )(wh, wl, p_hi, p_lo, vh, vl, bias.reshape(1, tn), x0)


def _hyena(proj, lp, consts, seq, row_blk0):
    z, deltas, (wfh, wfl, wih, wil) = consts
    x0, vh, vl = _hy_prep(proj, lp['hy_short_w'], lp['hy_short_b'], seq, row_blk0)
    kh, kl = _hy_filter(lp, z, deltas, seq)
    xf = _dft_fwd(wfh, wfl, vh, vl)
    hf = _dft_fwd(wfh, wfl, kh, kl)
    p_hi, p_lo = _hy_prod(xf, hf, seq)
    return _dft_inv(wih, wil, p_hi, p_lo, vh, vl, lp['hy_bias'], x0)


def _s5_params(lp, n_seg_steps):
    a_re, a_im = lp['s5_a_re'], lp['s5_a_im']
    dt = jnp.exp(lp['s5_log_dt'])[..., None]
    mag = jnp.exp(a_re * dt)
    ab_re, ab_im = mag * jnp.cos(a_im * dt), mag * jnp.sin(a_im * dt)
    den = a_re * a_re + a_im * a_im
    nr, ni = ab_re - 1.0, ab_im
    cf_re = (nr * a_re + ni * a_im) / den
    cf_im = (ni * a_re - nr * a_im) / den
    b_re, b_im = lp['s5_b_re'], lp['s5_b_im']
    bb_re = cf_re[..., None] * b_re - cf_im[..., None] * b_im
    bb_im = cf_re[..., None] * b_im + cf_im[..., None] * b_re
    eye = jnp.eye(S5_CHUNK_GROUPS, dtype=F32)

    def bdiag_in(m):
        m = m.reshape(2, S5_NCHUNK, S5_CHUNK_GROUPS, S5_STATE, S5_GC)
        return jnp.einsum('dqgpc,gh->dqgchp', m, eye).reshape(2, S5_NCHUNK, S5_CHUNK_CH, S5_CHUNK_ST)

    def bdiag_out(m):
        m = m.reshape(2, S5_NCHUNK, S5_CHUNK_GROUPS, S5_GC, S5_STATE)
        return jnp.einsum('dqgcp,gh->dqhpgc', m, eye).reshape(2, S5_NCHUNK, S5_CHUNK_ST, S5_CHUNK_CH)

    bbd = jnp.concatenate([bdiag_in(bb_re), bdiag_in(bb_im)], axis=-1).astype(BF16)
    cbd = jnp.concatenate([bdiag_out(lp['s5_c_re']), -bdiag_out(lp['s5_c_im'])], axis=-2).astype(BF16)
    a = jnp.stack([ab_re.reshape(2, -1), ab_im.reshape(2, -1)], axis=1)
    pr, pi = ab_re, ab_im
    for _ in range(int(math.log2(n_seg_steps))):
        pr, pi = pr * pr - pi * pi, 2.0 * pr * pi
    ak = jnp.stack([pr.reshape(2, -1), pi.reshape(2, -1)], axis=1)
    return bbd, cbd, a, ak


def _s5_scan_body(u_ref, bbd_ref, cbd_ref, a_ref, ak_ref, dsk_ref, h0_ref, y_ref, hl_ref, xr_ref, xi_ref, *, seq):
    nk = seq // S5_LANES
    shape = (S5_LANES, S5_CHUNK_ST)
    u = u_ref[...]
    ub = u.astype(BF16)
    row = lax.broadcasted_iota(jnp.int32, shape, 0)
    y = jnp.zeros((seq, S5_CHUNK_CH), F32)
    for d in range(2):
        xr_ref[...] = _dot(ub, bbd_ref[d, :, :S5_CHUNK_ST])
        xi_ref[...] = _dot(ub, bbd_ref[d, :, S5_CHUNK_ST:])
        ar = jnp.broadcast_to(a_ref[d, 0:1, :], shape)
        ai = jnp.broadcast_to(a_ref[d, 1:2, :], shape)

        def tile(s, d=d):
            k = s if d == 0 else nk - 1 - s
            return pl.ds(pl.multiple_of(k * S5_LANES, S5_LANES), S5_LANES)

        def scan_step(s, carry, ar=ar, ai=ai, tile=tile):
            xr, xi = carry
            rows = tile(s)
            nr = ar * xr - ai * xi + xr_ref[rows, :]
            ni = ar * xi + ai * xr + xi_ref[rows, :]
            xr_ref[rows, :] = nr
            xi_ref[rows, :] = ni
            return nr, ni

        zero = jnp.zeros(shape, F32)
        er, ei = lax.fori_loop(0, nk, scan_step, (zero, zero), unroll=8)

        akr, aki = ak_ref[d, 0:1, :], ak_ref[d, 1:2, :]
        hr, hi = h0_ref[2 * d:2 * d + 1, :], h0_ref[2 * d + 1:2 * d + 2, :]
        in_r, in_i = zero, zero
        for j in (range(S5_LANES) if d == 0 else range(S5_LANES - 1, -1, -1)):
            in_r = jnp.where(row == j, hr, in_r)
            in_i = jnp.where(row == j, hi, in_i)
            hr, hi = (akr * hr - aki * hi + er[j:j + 1, :], akr * hi + aki * hr + ei[j:j + 1, :])
        hl_ref[2 * d:2 * d + 1, :] = hr
        hl_ref[2 * d + 1:2 * d + 2, :] = hi

        def fix_step(s, carry, ar=ar, ai=ai, tile=tile):
            gr, gi = carry
            gr, gi = ar * gr - ai * gi, ar * gi + ai * gr
            rows = tile(s)
            xr_ref[rows, :] += gr
            xi_ref[rows, :] += gi
            return gr, gi

        lax.fori_loop(0, nk, fix_step, (in_r, in_i), unroll=8)
        y = y + _dot(xr_ref[...].astype(BF16), cbd_ref[d, :S5_CHUNK_ST, :])
        y = y + _dot(xi_ref[...].astype(BF16), cbd_ref[d, S5_CHUNK_ST:, :])
        y = y + u * dsk_ref[d]
    y_ref[...] = y


def _s5_scan(u_perm, params, dskip, h0, seq):
    bbd, cbd, a, ak = params
    nq = S5_NCHUNK
    return pl.pallas_call(
        functools.partial(_s5_scan_body, seq=seq),
        grid=(BATCH, nq),
        in_specs=[
            pl.BlockSpec((seq, S5_CHUNK_CH), lambda b, q: (b, q)),
            pl.BlockSpec((2, None, S5_CHUNK_CH, 2 * S5_CHUNK_ST), lambda b, q: (0, q, 0, 0)),
            pl.BlockSpec((2, None, 2 * S5_CHUNK_ST, S5_CHUNK_CH), lambda b, q: (0, q, 0, 0)),
            pl.BlockSpec((2, 2, S5_CHUNK_ST), lambda b, q: (0, 0, q)),
            pl.BlockSpec((2, 2, S5_CHUNK_ST), lambda b, q: (0, 0, q)),
            pl.BlockSpec((2, 1, S5_CHUNK_CH), lambda b, q: (0, 0, q)),
            pl.BlockSpec((None, 4, S5_CHUNK_ST), lambda b, q: (b, 0, q)),
        ],
        out_specs=[
            pl.BlockSpec((seq, S5_CHUNK_CH), lambda b, q: (b, q)),
            pl.BlockSpec((None, 4, S5_CHUNK_ST), lambda b, q: (b, 0, q)),
        ],
        out_shape=[
            jax.ShapeDtypeStruct((BATCH * seq, S5_WIDTH), F32),
            jax.ShapeDtypeStruct((BATCH, 4, S5_GROUPS * S5_STATE), F32),
        ],
        scratch_shapes=[pltpu.VMEM((seq, S5_CHUNK_ST), F32), pltpu.VMEM((seq, S5_CHUNK_ST), F32)],
        compiler_params=_cparams("arbitrary", "arbitrary"),
        name="s5_scan",
    )(u_perm, bbd, cbd, a, ak, dskip.reshape(2, 1, S5_WIDTH), h0)


def _s5_glu_body(y_ref, w_ref, b_ref, o_ref):
    g = _dot(jax.nn.gelu(y_ref[...]).astype(BF16), w_ref[...].astype(BF16)) + b_ref[...]
    o_ref[...] = (g[:, :S5_WIDTH] * jax.nn.sigmoid(g[:, S5_WIDTH:])).astype(BF16)


def _s5_glu(y, w, b):
    n_rows = y.shape[0]
    tm = TM_EW
    return pl.pallas_call(
        _s5_glu_body,
        grid=(n_rows // tm,),
        in_specs=[
            pl.BlockSpec((tm, S5_WIDTH), lambda i: (i, 0)),
            pl.BlockSpec((S5_WIDTH, 2 * S5_WIDTH), lambda i: (0, 0)),
            pl.BlockSpec((1, 2 * S5_WIDTH), lambda i: (0, 0)),
        ],
        out_specs=pl.BlockSpec((tm, S5_WIDTH), lambda i: (i, 0)),
        out_shape=jax.ShapeDtypeStruct((n_rows, S5_WIDTH), BF16),
        compiler_params=_cparams("arbitrary"),
        name="s5_glu",
    )(y, w, b.reshape(1, -1))


def _seg_perm(x, seq):
    nk = seq // S5_LANES
    return x.reshape(BATCH, S5_LANES, nk, -1).transpose(0, 2, 1, 3).reshape(BATCH * seq, -1)


def _seg_unperm(x, seq):
    nk = seq // S5_LANES
    return x.reshape(BATCH, nk, S5_LANES, -1).transpose(0, 2, 1, 3).reshape(BATCH * seq, -1)


def _merge_body(u_ref, wg_ref, bg_ref, y_ref, wb_ref, o_ref, acc_ref):
    n = pl.program_id(2)
    gate = jax.nn.sigmoid(_dot(u_ref[...], wg_ref[...].astype(BF16)) + bg_ref[...])
    contrib = gate * _dot(y_ref[...], wb_ref[...].astype(BF16))

    @pl.when(n == 0)
    def _():
        acc_ref[...] = contrib

    @pl.when(n > 0)
    def _():
        acc_ref[...] += contrib

    @pl.when(n == N_BRANCH - 1)
    def _():
        o_ref[...] = acc_ref[...].astype(BF16)


def _merge(u, w_gate, b_gate, branches, w_branch, l, n_rows):
    tc = 512
    ncol = D_MODEL // tc
    return pl.pallas_call(
        _merge_body,
        grid=(n_rows // TM, ncol, N_BRANCH),
        in_specs=[
            pl.BlockSpec((TM, D_MODEL), lambda i, c, n: (i, 0)),
            pl.BlockSpec((None, D_MODEL, tc), lambda i, c, n: (l, 0, n * ncol + c)),
            pl.BlockSpec((1, tc), lambda i, c, n: (0, n * ncol + c)),
            pl.BlockSpec((None, TM, BRANCH_WIDTH), lambda i, c, n: (n, i, 0)),
            pl.BlockSpec((None, None, BRANCH_WIDTH, tc), lambda i, c, n: (l, n, 0, c)),
        ],
        out_specs=pl.BlockSpec((TM, tc), lambda i, c, n: (i, c)),
        out_shape=jax.ShapeDtypeStruct((n_rows, D_MODEL), BF16),
        scratch_shapes=[pltpu.VMEM((TM, tc), F32)],
        compiler_params=_cparams("arbitrary", "arbitrary", "arbitrary"),
        name="merge",
    )(u, w_gate, b_gate.reshape(1, -1), branches, w_branch)


def _out_body(m_ref, w_ref, x_ref, g_ref, o_ref):
    o_ref[...] = x_ref[...] + g_ref[...] * _dot(m_ref[...], w_ref[...].astype(BF16))


def _out_proj(merged, w_out, l, x, mods, n_rows):
    tn = 512
    gate = pl.BlockSpec((None, None, 1, tn), lambda i, j: (_mod_row(i, TM), 5, 0, j))
    return pl.pallas_call(
        _out_body,
        grid=(n_rows // TM, D_MODEL // tn),
        in_specs=[
            pl.BlockSpec((TM, D_MODEL), lambda i, j: (i, 0)),
            pl.BlockSpec((None, D_MODEL, tn), lambda i, j: (l, 0, j)),
            pl.BlockSpec((TM, tn), lambda i, j: (i, j)),
            gate,
        ],
        out_specs=pl.BlockSpec((TM, tn), lambda i, j: (i, j)),
        out_shape=jax.ShapeDtypeStruct((n_rows, D_MODEL), F32),
        compiler_params=_cparams("arbitrary", "arbitrary"),
        name="out_proj",
    )(merged, w_out, x, mods)


def _mixer(x, u, lp, big, l, mods, consts, last):
    cos, sin, hy_lat, hy_ctx = consts
    proj = _proj(u, big['w_in'], l)
    q, k, v = _qkv(proj, cos, sin, lp['q_norm'], lp['k_norm'])

    s5_in = proj[:, OFF_S5:]
    nk_ctx = CTX_LEN // S5_LANES
    nk_lat = SEQ // S5_LANES
    zero_h = jnp.zeros((BATCH, 4, S5_GROUPS * S5_STATE), F32)
    ys_ctx, h_ctx = _s5_scan(_seg_perm(s5_in[N_LAT:], CTX_LEN), _s5_params(lp, nk_ctx), lp['s5_d'], zero_h, CTX_LEN)
    ys_lat, _ = _s5_scan(_seg_perm(s5_in[:N_LAT], SEQ), _s5_params(lp, nk_lat), lp['s5_d'], h_ctx, SEQ)

    def cat(lat, ctx):
        return lat if last else jnp.concatenate([lat, ctx], axis=0)

    ys = cat(_seg_unperm(ys_lat, SEQ), None if last else _seg_unperm(ys_ctx, CTX_LEN))
    y_s5 = _s5_glu(ys, lp['s5_glu_w'], lp['s5_glu_b'])
    ctx_blk = N_LAT // CTX_LEN
    y_attn = cat(_attn_lat(q, k, v), None if last else _attn_ctx(q, k, v))
    y_pool = cat(_pool(proj, lp['pool_w'], lp['pool_scale'], SEQ, 0),
                 None if last else _pool(proj, lp['pool_w'], lp['pool_scale'], CTX_LEN, ctx_blk))
    y_hy = cat(_hyena(proj, lp, hy_lat, SEQ, 0), None if last else _hyena(proj, lp, hy_ctx, CTX_LEN, ctx_blk))

    n_rows = N_LAT if last else MT
    branches = jnp.stack([y_pool, y_attn, y_hy, y_s5], axis=0)
    merged = _merge(u, big['w_gate'], lp['b_gate'], branches, big['w_branch'], l, n_rows)
    return _out_proj(merged, big['w_out'], l, x, mods, n_rows)


def kernel(x, c, ctx, c_ctx, w_ada, b_ada, norm_ffn1, norm_mix, norm_ffn2, norm_final, ffn1_wi, ffn1_wo, ffn2_wi, ffn2_wo, w_in, w_gate, b_gate, w_branch, w_out, pool_w, pool_scale, q_norm, k_norm, hy_short_w, hy_short_b, hy_f1_w, hy_f1_b, hy_f2_w, hy_f2_b, hy_f3_w, hy_freq, hy_bias, s5_a_re, s5_a_im, s5_log_dt, s5_b_re, s5_b_im, s5_c_re, s5_c_im, s5_d, s5_glu_w, s5_glu_b):
    big = dict(w_in=w_in, w_gate=w_gate, w_branch=w_branch, w_out=w_out)
    per_layer = dict(
        b_gate=b_gate, pool_w=pool_w,
        pool_scale=pool_scale, q_norm=q_norm, k_norm=k_norm, hy_short_w=hy_short_w, hy_short_b=hy_short_b,
        hy_f1_w=hy_f1_w, hy_f1_b=hy_f1_b, hy_f2_w=hy_f2_w, hy_f2_b=hy_f2_b, hy_f3_w=hy_f3_w, hy_freq=hy_freq,
        hy_bias=hy_bias, s5_a_re=s5_a_re, s5_a_im=s5_a_im, s5_log_dt=s5_log_dt, s5_b_re=s5_b_re, s5_b_im=s5_b_im,
        s5_c_re=s5_c_re, s5_c_im=s5_c_im, s5_d=s5_d, s5_glu_w=s5_glu_w, s5_glu_b=s5_glu_b)

    cos, sin = _rope_tables()
    deltas = _hy_deltas()
    consts = (cos, sin,
              (_hy_feats(SEQ), deltas, _dft_matrices(SEQ)),
              (_hy_feats(CTX_LEN), deltas, _dft_matrices(CTX_LEN)))

    cc = jnp.concatenate([c, c_ctx[None], jnp.zeros((8 - BATCH - 1, D_MODEL), F32)], axis=0)
    mods_all = _ada(cc, w_ada, b_ada).reshape(DEPTH, 8, N_MOD, 1, D_MODEL)

    xs = jnp.concatenate([x.reshape(N_LAT, D_MODEL), ctx.reshape(N_CTX, D_MODEL)], axis=0)
    for l in range(DEPTH):
        last = l == DEPTH - 1
        lp = {name: w[l] for name, w in per_layer.items()}
        mods = mods_all[l]
        xs = _ffn(xs, mods, norm_ffn1[l], ffn1_wi, ffn1_wo, l, 0, MT)
        u = _normmod(xs, norm_mix[l], mods, 3, MT)
        xs = _mixer(xs, u, lp, big, l, mods, consts, last)
        xs = _ffn(xs, mods, norm_ffn2[l], ffn2_wi, ffn2_wo, l, 6, N_LAT if last else MT)
    return _final_norm(xs, norm_final, N_LAT).reshape(BATCH, SEQ, D_MODEL)
```

```python
import functools
import math

import jax
import jax.numpy as jnp
from jax import lax
from jax.experimental import pallas as pl
from jax.experimental.pallas import tpu as pltpu

F32 = jnp.float32
BF16 = jnp.bfloat16

D_MODEL = 2048
BATCH = 4
SEQ = 2048
DEPTH = 4
GRID_W = 64
CTX_LEN = 256
D_FF = 5632
N_MOD = 9
EPS = 1e-6

POOL_WINDOWS = (2, 4, 8, 16)
HEAD_DIM = 128
N_Q_HEADS = 4
N_KV_HEADS = 2
Q_GROUP = N_Q_HEADS // N_KV_HEADS
ROPE_THETA = 10000.0
ROPE_FREQS = HEAD_DIM // 4

HYENA_WIDTH = 512
HYENA_EMB = 33
HYENA_BANDS = (HYENA_EMB - 1) // 2
HYENA_HIDDEN = 64
HYENA_TARGET = 1e-2
HYENA_FAST_PCT = 0.3
HYENA_SLOW_PCT = 1.5

S5_WIDTH = 512
S5_GC = 16
S5_GROUPS = S5_WIDTH // S5_GC
S5_STATE = 64
S5_LANES = 8
S5_CHUNK_GROUPS = 8
S5_CHUNK_CH = S5_CHUNK_GROUPS * S5_GC
S5_CHUNK_ST = S5_CHUNK_GROUPS * S5_STATE
S5_NCHUNK = S5_GROUPS // S5_CHUNK_GROUPS

N_BRANCH = 4
BRANCH_WIDTH = 512
LANE = 128
COL = 512

OFF_Q = 512
OFF_KV = 1024
OFF_HY = 1536
OFF_S5 = 3072
IN_WIDTH = 3584

N_LAT = BATCH * SEQ
N_CTX = BATCH * CTX_LEN
MT = N_LAT + N_CTX

TM = 1024
TM_EW = 512
VMEM_LIMIT = 56 * 1024 * 1024


def _cparams(*sem):
    return pltpu.CompilerParams(dimension_semantics=sem, vmem_limit_bytes=VMEM_LIMIT)


def _dot(a, b):
    return jnp.dot(a, b, preferred_element_type=F32)


SLOT_POOL, SLOT_ATTN, SLOT_HYENA, SLOT_S5 = range(4)
_ANY = pl.BlockSpec(memory_space=pl.ANY)


def _keep_branch_buffer(body, n_in):
    def wrapped(*refs):
        return body(*refs[:n_in], *refs[n_in + 1:])
    return wrapped


def _mod_row(i, tm):
    return jnp.minimum((i * tm) // SEQ, BATCH)


def _mod_spec(tm, k):
    return pl.BlockSpec((None, None, 1, D_MODEL), lambda i, *_: (_mod_row(i, tm), k, 0, 0))


def _ada_body(c_ref, w_ref, b_ref, o_ref):
    c = c_ref[...]
    a = (c * jax.nn.sigmoid(c)).astype(BF16)
    o_ref[...] = _dot(a, w_ref[...].astype(BF16)) + b_ref[...]


def _ada(cc, w_ada, b_ada):
    tn = 1024
    nw = N_MOD * D_MODEL
    return pl.pallas_call(
        _ada_body,
        grid=(DEPTH, nw // tn),
        in_specs=[
            pl.BlockSpec((8, D_MODEL), lambda l, j: (0, 0)),
            pl.BlockSpec((None, D_MODEL, tn), lambda l, j: (l, 0, j)),
            pl.BlockSpec((None, 1, tn), lambda l, j: (l, 0, j)),
        ],
        out_specs=pl.BlockSpec((None, 8, tn), lambda l, j: (l, 0, j)),
        out_shape=jax.ShapeDtypeStruct((DEPTH, 8, nw), F32),
        compiler_params=_cparams("arbitrary", "arbitrary"),
        name="ada",
    )(cc, w_ada, b_ada.reshape(DEPTH, 1, nw))


def _normmod_body(x_ref, g_ref, sh_ref, sc_ref, o_ref):
    x = x_ref[...]
    y = x * lax.rsqrt(jnp.mean(x * x, axis=-1, keepdims=True) + EPS) * g_ref[...]
    o_ref[...] = (y * (1.0 + sc_ref[...]) + sh_ref[...]).astype(o_ref.dtype)


def _norm_body(x_ref, g_ref, o_ref):
    x = x_ref[...]
    y = x * lax.rsqrt(jnp.mean(x * x, axis=-1, keepdims=True) + EPS) * g_ref[...]
    o_ref[...] = y.astype(o_ref.dtype)


def _normmod(x, g, mods, base, n_rows):
    tm = TM_EW
    return pl.pallas_call(
        _normmod_body,
        grid=(n_rows // tm,),
        in_specs=[
            pl.BlockSpec((tm, D_MODEL), lambda i: (i, 0)),
            pl.BlockSpec((1, D_MODEL), lambda i: (0, 0)),
            _mod_spec(tm, base),
            _mod_spec(tm, base + 1),
        ],
        out_specs=pl.BlockSpec((tm, D_MODEL), lambda i: (i, 0)),
        out_shape=jax.ShapeDtypeStruct((n_rows, D_MODEL), BF16),
        compiler_params=_cparams("arbitrary"),
        name="normmod",
    )(x, g.reshape(1, D_MODEL), mods, mods)


def _final_norm(x, g, n_rows):
    tm = TM_EW
    return pl.pallas_call(
        _norm_body,
        grid=(n_rows // tm,),
        in_specs=[
            pl.BlockSpec((tm, D_MODEL), lambda i: (i, 0)),
            pl.BlockSpec((1, D_MODEL), lambda i: (0, 0)),
        ],
        out_specs=pl.BlockSpec((tm, D_MODEL), lambda i: (i, 0)),
        out_shape=jax.ShapeDtypeStruct((n_rows, D_MODEL), F32),
        compiler_params=_cparams("arbitrary"),
        name="final_norm",
    )(x, g.reshape(1, D_MODEL))


def _ffn_a_body(u_ref, wa_ref, wb_ref, h_ref):
    u = u_ref[...]
    a = _dot(u, wa_ref[...].astype(BF16))
    b = _dot(u, wb_ref[...].astype(BF16))
    h_ref[...] = (a * jax.nn.sigmoid(a) * b).astype(BF16)


def _ffn_a(u, wi, l, n_rows):
    tf = 512
    nf = D_FF // tf
    return pl.pallas_call(
        _ffn_a_body,
        grid=(n_rows // TM, nf),
        in_specs=[
            pl.BlockSpec((TM, D_MODEL), lambda i, j: (i, 0)),
            pl.BlockSpec((None, D_MODEL, tf), lambda i, j: (l, 0, j)),
            pl.BlockSpec((None, D_MODEL, tf), lambda i, j: (l, 0, j + nf)),
        ],
        out_specs=pl.BlockSpec((TM, tf), lambda i, j: (i, j)),
        out_shape=jax.ShapeDtypeStruct((n_rows, D_FF), BF16),
        compiler_params=_cparams("arbitrary", "arbitrary"),
        name="ffn_a",
    )(u, wi, wi)


def _ffn_b_body(h_ref, w_ref, x_ref, g_ref, o_ref):
    o_ref[...] = x_ref[...] + (0.5 * g_ref[...]) * _dot(h_ref[...], w_ref[...].astype(BF16))


def _ffn_b(h, wo, l, x, mods, gate_idx, n_rows):
    tn = 256
    gate = pl.BlockSpec((None, None, 1, tn), lambda i, j: (_mod_row(i, TM), gate_idx, 0, j))
    return pl.pallas_call(
        _ffn_b_body,
        grid=(n_rows // TM, D_MODEL // tn),
        in_specs=[
            pl.BlockSpec((TM, D_FF), lambda i, j: (i, 0)),
            pl.BlockSpec((None, D_FF, tn), lambda i, j: (l, 0, j)),
            pl.BlockSpec((TM, tn), lambda i, j: (i, j)),
            gate,
        ],
        out_specs=pl.BlockSpec((TM, tn), lambda i, j: (i, j)),
        out_shape=jax.ShapeDtypeStruct((n_rows, D_MODEL), F32),
        compiler_params=_cparams("arbitrary", "arbitrary"),
        name="ffn_b",
    )(h, wo, x, mods)


def _ffn(x, mods, g, wi, wo, l, base, n_rows):
    u = _normmod(x, g, mods, base, n_rows)
    h = _ffn_a(u, wi, l, n_rows)
    return _ffn_b(h, wo, l, x, mods, base + 2, n_rows)


def _proj_body(u_ref, w_ref, o_ref):
    o_ref[...] = _dot(u_ref[...], w_ref[...].astype(BF16))


def _proj(u, w_in, l):
    tn = COL
    return pl.pallas_call(
        _proj_body,
        grid=(MT // TM, IN_WIDTH // tn),
        in_specs=[
            pl.BlockSpec((TM, D_MODEL), lambda i, j: (i, 0)),
            pl.BlockSpec((None, D_MODEL, tn), lambda i, j: (l, 0, j)),
        ],
        out_specs=pl.BlockSpec((TM, tn), lambda i, j: (i, j)),
        out_shape=jax.ShapeDtypeStruct((MT, IN_WIDTH), F32),
        compiler_params=_cparams("arbitrary", "arbitrary"),
        name="proj",
    )(u, w_in)


def _rope_tables():
    t = jnp.arange(SEQ)
    rows = (t // GRID_W).astype(F32)
    cols = (t % GRID_W).astype(F32)
    freqs = ROPE_THETA ** (-jnp.arange(ROPE_FREQS, dtype=F32) / ROPE_FREQS)
    ar = rows[:, None] * freqs[None, :]
    ac = cols[:, None] * freqs[None, :]
    cos = jnp.concatenate([jnp.cos(ar), jnp.cos(ar), jnp.cos(ac), jnp.cos(ac)], axis=-1)
    sin = jnp.concatenate([-jnp.sin(ar), jnp.sin(ar), -jnp.sin(ac), jnp.sin(ac)], axis=-1)
    cos = jnp.concatenate([cos, jnp.ones((TM_EW, HEAD_DIM), F32)], axis=0)
    sin = jnp.concatenate([sin, jnp.zeros((TM_EW, HEAD_DIM), F32)], axis=0)
    return cos, sin


def _qkv_body(q_ref, kv_ref, cos_ref, sin_ref, qn_ref, kn_ref, qo_ref, ko_ref, vo_ref):
    cos = cos_ref[...]
    sin = sin_ref[...]
    lane = lax.broadcasted_iota(jnp.int32, cos.shape, 1)
    first = (lane % (2 * ROPE_FREQS)) < ROPE_FREQS

    def norm_rope(xh, g):
        y = xh * lax.rsqrt(jnp.mean(xh * xh, axis=-1, keepdims=True) + EPS) * g
        swapped = jnp.where(first, pltpu.roll(y, HEAD_DIM - ROPE_FREQS, 1), pltpu.roll(y, ROPE_FREQS, 1))
        return y * cos + swapped * sin

    scale = 1.0 / math.sqrt(HEAD_DIM)
    for h in range(N_Q_HEADS):
        sl = slice(h * HEAD_DIM, (h + 1) * HEAD_DIM)
        qo_ref[:, sl] = (norm_rope(q_ref[:, sl], qn_ref[...]) * scale).astype(BF16)
    for h in range(N_KV_HEADS):
        sl = slice(h * HEAD_DIM, (h + 1) * HEAD_DIM)
        ko_ref[:, sl] = norm_rope(kv_ref[:, sl], kn_ref[...]).astype(BF16)
    kvw = N_KV_HEADS * HEAD_DIM
    vo_ref[...] = kv_ref[:, kvw:].astype(BF16)


def _qkv(proj, cos, sin, q_norm, k_norm):
    tm = TM_EW
    n_lat_tiles = N_LAT // tm
    per_seq = SEQ // tm
    kvw = N_KV_HEADS * HEAD_DIM

    def tab(i):
        return (jnp.where(i < n_lat_tiles, i % per_seq, per_seq), 0)

    return pl.pallas_call(
        _qkv_body,
        grid=(MT // tm,),
        in_specs=[
            pl.BlockSpec((tm, COL), lambda i: (i, OFF_Q // COL)),
            pl.BlockSpec((tm, COL), lambda i: (i, OFF_KV // COL)),
            pl.BlockSpec((tm, HEAD_DIM), tab),
            pl.BlockSpec((tm, HEAD_DIM), tab),
            pl.BlockSpec((1, HEAD_DIM), lambda i: (0, 0)),
            pl.BlockSpec((1, HEAD_DIM), lambda i: (0, 0)),
        ],
        out_specs=[
            pl.BlockSpec((tm, COL), lambda i: (i, 0)),
            pl.BlockSpec((tm, kvw), lambda i: (i, 0)),
            pl.BlockSpec((tm, kvw), lambda i: (i, 0)),
        ],
        out_shape=[
            jax.ShapeDtypeStruct((MT, COL), BF16),
            jax.ShapeDtypeStruct((MT, kvw), BF16),
            jax.ShapeDtypeStruct((MT, kvw), BF16),
        ],
        compiler_params=_cparams("arbitrary"),
        name="qkv",
    )(proj, proj, cos, sin, q_norm.reshape(1, HEAD_DIM), k_norm.reshape(1, HEAD_DIM))


def _attn_body(*refs, with_lat):
    if with_lat:
        q_ref, kl_ref, vl_ref, kc_ref, vc_ref, o_ref = refs
    else:
        q_ref, kc_ref, vc_ref, o_ref = refs
    nt = (((1,), (1,)), ((), ()))
    for g in range(Q_GROUP):
        sl = slice(g * HEAD_DIM, (g + 1) * HEAD_DIM)
        q = q_ref[:, sl]
        sc = lax.dot_general(q, kc_ref[...], nt, preferred_element_type=F32)
        m = jnp.max(sc, axis=-1, keepdims=True)
        if with_lat:
            s_lat = lax.dot_general(q, kl_ref[...], nt, preferred_element_type=F32)
            m = jnp.maximum(m, jnp.max(s_lat, axis=-1, keepdims=True))
            e_lat = jnp.exp(s_lat - m)
        ec = jnp.exp(sc - m)
        den = jnp.sum(ec, axis=-1, keepdims=True)
        o = _dot(ec.astype(BF16), vc_ref[...])
        if with_lat:
            den = den + jnp.sum(e_lat, axis=-1, keepdims=True)
            o = o + _dot(e_lat.astype(BF16), vl_ref[...])
        o_ref[:, sl] = (o / den).astype(BF16)


def _attn_lat(q, k, v, bb):
    tq = 512
    nq = SEQ // tq
    ctx_blk = N_LAT // CTX_LEN
    gw = Q_GROUP * HEAD_DIM
    return pl.pallas_call(
        _keep_branch_buffer(functools.partial(_attn_body, with_lat=True), 5),
        grid=(BATCH, N_KV_HEADS, nq),
        in_specs=[
            pl.BlockSpec((tq, gw), lambda b, h, i: (b * nq + i, h)),
            pl.BlockSpec((SEQ, HEAD_DIM), lambda b, h, i: (b, h)),
            pl.BlockSpec((SEQ, HEAD_DIM), lambda b, h, i: (b, h)),
            pl.BlockSpec((CTX_LEN, HEAD_DIM), lambda b, h, i: (ctx_blk + b, h)),
            pl.BlockSpec((CTX_LEN, HEAD_DIM), lambda b, h, i: (ctx_blk + b, h)),
            _ANY,
        ],
        out_specs=pl.BlockSpec((None, tq, gw), lambda b, h, i: (SLOT_ATTN, b * nq + i, h)),
        out_shape=jax.ShapeDtypeStruct(bb.shape, bb.dtype),
        input_output_aliases={5: 0},
        compiler_params=_cparams("arbitrary", "arbitrary", "arbitrary"),
        name="attn_lat",
    )(q, k, v, k, v, bb)


def _attn_ctx(q, k, v, bb):
    ctx_blk = N_LAT // CTX_LEN
    gw = Q_GROUP * HEAD_DIM
    return pl.pallas_call(
        _keep_branch_buffer(functools.partial(_attn_body, with_lat=False), 3),
        grid=(BATCH, N_KV_HEADS),
        in_specs=[
            pl.BlockSpec((CTX_LEN, gw), lambda b, h: (ctx_blk + b, h)),
            pl.BlockSpec((CTX_LEN, HEAD_DIM), lambda b, h: (ctx_blk + b, h)),
            pl.BlockSpec((CTX_LEN, HEAD_DIM), lambda b, h: (ctx_blk + b, h)),
            _ANY,
        ],
        out_specs=pl.BlockSpec((None, CTX_LEN, gw), lambda b, h: (SLOT_ATTN, ctx_blk + b, h)),
        out_shape=jax.ShapeDtypeStruct(bb.shape, bb.dtype),
        input_output_aliases={3: 0},
        compiler_params=_cparams("arbitrary", "arbitrary"),
        name="attn_ctx",
    )(q, k, v, bb)


POOL_PAD = 8


def _pool_body(a_ref, w_ref, s_ref, o_ref, *, seq):
    lp = seq + 2 * POOL_PAD
    t = lax.broadcasted_iota(jnp.int32, (seq, LANE), 0)
    zpad = jnp.zeros((POOL_PAD, LANE), F32)
    for gi, win in enumerate(POOL_WINDOWS):
        sl = slice(gi * LANE, (gi + 1) * LANE)
        a = a_ref[:, sl]
        s = jnp.concatenate([zpad, a, zpad], axis=0)
        s = s + pltpu.roll(s, 1, 0)
        half = 1
        while 2 * half < win:
            s = pltpu.roll(s, half, 0) + pltpu.roll(s, lp - half, 0)
            half *= 2
        s = s[POOL_PAD:POOL_PAD + seq]
        lo = jnp.maximum(t - win // 2, 0)
        hi = jnp.minimum(t + win // 2, seq)
        pooled = s / (hi - lo).astype(F32) - a
        y = _dot(pooled.astype(BF16), w_ref[gi].astype(BF16))
        o_ref[:, sl] = (y * s_ref[:, sl]).astype(BF16)


def _pool(proj, pool_w, pool_scale, bb, seq, row_blk0):
    width = len(POOL_WINDOWS) * LANE
    return pl.pallas_call(
        _keep_branch_buffer(functools.partial(_pool_body, seq=seq), 3),
        grid=(BATCH,),
        in_specs=[
            pl.BlockSpec((seq, width), lambda b: (row_blk0 + b, 0)),
            pl.BlockSpec((len(POOL_WINDOWS), LANE, LANE), lambda b: (0, 0, 0)),
            pl.BlockSpec((1, width), lambda b: (0, 0)),
            _ANY,
        ],
        out_specs=pl.BlockSpec((None, seq, width), lambda b: (SLOT_POOL, row_blk0 + b, 0)),
        out_shape=jax.ShapeDtypeStruct(bb.shape, bb.dtype),
        input_output_aliases={3: 0},
        compiler_params=_cparams("arbitrary"),
        name="pool",
    )(proj, pool_w, pool_scale.reshape(1, width), bb)


def _hy_prep_body(x0_ref, x1_ref, v_ref, w0_ref, w1_ref, wv_ref, b0_ref, b1_ref, bv_ref,
                  x0o_ref, vx_ref, vb_ref, *, seq):
    t = lax.broadcasted_iota(jnp.int32, (seq, LANE), 0)

    def conv(x_ref, w_ref, b_ref):
        x = x_ref[...]
        prev = jnp.where(t >= 1, pltpu.roll(x, 1, 0), 0.0)
        nxt = jnp.where(t <= seq - 2, pltpu.roll(x, seq - 1, 0), 0.0)
        return prev * w_ref[0:1, :] + x * w_ref[1:2, :] + nxt * w_ref[2:3, :] + b_ref[...]

    x0o_ref[...] = conv(x0_ref, w0_ref, b0_ref)
    vx = conv(v_ref, wv_ref, bv_ref) * conv(x1_ref, w1_ref, b1_ref)
    vx_ref[...] = vx
    vb_ref[...] = vx.astype(BF16)


def _hy_prep(proj, short_w, short_b, seq, row_blk0):
    nc = HYENA_WIDTH // LANE
    c0 = OFF_HY // LANE
    short_b = short_b.reshape(1, 3 * HYENA_WIDTH)

    def xspec(part):
        return pl.BlockSpec((seq, LANE), lambda b, c: (row_blk0 + b, c0 + part * nc + c))

    def wspec(part, rows):
        return pl.BlockSpec((rows, LANE), lambda b, c: (0, part * nc + c))

    return pl.pallas_call(
        functools.partial(_hy_prep_body, seq=seq),
        grid=(BATCH, nc),
        in_specs=[xspec(0), xspec(1), xspec(2), wspec(0, 3), wspec(1, 3), wspec(2, 3),
                  wspec(0, 1), wspec(1, 1), wspec(2, 1)],
        out_specs=[
            pl.BlockSpec((seq, LANE), lambda b, c: (b, c)),
            pl.BlockSpec((seq, LANE), lambda b, c: (b, c)),
            pl.BlockSpec((seq, LANE), lambda b, c: (0, b * nc + c)),
        ],
        out_shape=[
            jax.ShapeDtypeStruct((BATCH * seq, HYENA_WIDTH), F32),
            jax.ShapeDtypeStruct((BATCH * seq, HYENA_WIDTH), F32),
            jax.ShapeDtypeStruct((seq, BATCH * HYENA_WIDTH), BF16),
        ],
        compiler_params=_cparams("arbitrary", "arbitrary"),
        name="hy_prep",
    )(proj, proj, proj, short_w, short_w, short_w, short_b, short_b, short_b)


def _hy_feats(seq):
    t = jnp.linspace(0.0, 1.0, seq, dtype=F32)[:, None]
    f = jnp.linspace(1e-4, HYENA_BANDS - 1, HYENA_BANDS, dtype=F32)
    w = 2.0 * math.pi * jnp.arange(seq, dtype=F32) / seq
    fw = w[:, None] * f[None, :]
    z = jnp.concatenate([t, jnp.cos(fw), -jnp.sin(fw)], axis=-1)
    return jnp.pad(z, ((0, 0), (0, LANE - HYENA_EMB)))


def _hy_deltas():
    max_decay = math.log(HYENA_TARGET) / HYENA_FAST_PCT
    min_decay = math.log(HYENA_TARGET) / HYENA_SLOW_PCT
    return jnp.abs(jnp.linspace(min_decay, max_decay, HYENA_WIDTH, dtype=F32)).reshape(1, HYENA_WIDTH)


def _hy_filter_body(z_ref, w1_ref, b1_ref, w2_ref, b2_ref, w3_ref, fr_ref, dl_ref, k_ref, nyq_ref, *, seq):
    hp = lax.Precision.HIGHEST
    freq = fr_ref[...]
    h = jnp.sin(freq * (jnp.dot(z_ref[...], w1_ref[...], precision=hp, preferred_element_type=F32) + b1_ref[...]))
    h = jnp.sin(freq * (jnp.dot(h, w2_ref[...], precision=hp, preferred_element_type=F32) + b2_ref[...]))
    h = jnp.dot(h, w3_ref[...], precision=hp, preferred_element_type=F32)
    ti = lax.broadcasted_iota(jnp.int32, (seq, HYENA_WIDTH), 0)
    decay = jnp.exp(-(ti.astype(F32) * (1.0 / (seq - 1))) * dl_ref[...])
    hf = h[:, :HYENA_WIDTH] * decay
    hb = jnp.where(ti == 0, 0.0, h[:, HYENA_WIDTH:] * decay)
    ks = hf + hb
    k_ref[:, :HYENA_WIDTH] = ks.astype(BF16)
    k_ref[:, HYENA_WIDTH:] = (hf - hb).astype(BF16)
    nyq = jnp.sum(jnp.where(ti % 2 == 0, ks, -ks), axis=0, keepdims=True)
    nyq_ref[...] = jnp.broadcast_to(nyq, nyq_ref.shape)


def _hy_filter(lp, z, deltas, seq):
    w1 = jnp.pad(lp['hy_f1_w'], ((0, LANE - HYENA_EMB), (0, 0)))
    args = (z, w1, lp['hy_f1_b'].reshape(1, -1), lp['hy_f2_w'], lp['hy_f2_b'].reshape(1, -1), lp['hy_f3_w'],
            lp['hy_freq'].reshape(1, -1), deltas)
    full = lambda a: pl.BlockSpec(a.shape, lambda i: (0,) * a.ndim)
    return pl.pallas_call(
        functools.partial(_hy_filter_body, seq=seq),
        grid=(1,),
        in_specs=[full(a) for a in args],
        out_specs=[pl.BlockSpec((seq, 2 * HYENA_WIDTH), lambda i: (0, 0)),
                   pl.BlockSpec((8, HYENA_WIDTH), lambda i: (0, 0))],
        out_shape=[jax.ShapeDtypeStruct((seq, 2 * HYENA_WIDTH), BF16),
                   jax.ShapeDtypeStruct((8, HYENA_WIDTH), F32)],
        compiler_params=_cparams("arbitrary"),
        name="hy_filter",
    )(*args)


DFT_SPLIT = 64


def _dft_matrices(seq):
    n = 2 * seq
    f = jnp.arange(seq, dtype=jnp.int32)[:, None]

    def table(step, count):
        idx = (f * (jnp.arange(count, dtype=jnp.int32)[None, :] * step)) % n
        ang = idx.astype(F32) * (2.0 * math.pi / n)
        return jnp.cos(ang), jnp.sin(ang)

    hc, hs = table(DFT_SPLIT, seq // DFT_SPLIT)
    lc, ls = table(1, DFT_SPLIT)
    cos = (hc[:, :, None] * lc[:, None, :] - hs[:, :, None] * ls[:, None, :]).reshape(seq, seq)
    sin = (hs[:, :, None] * lc[:, None, :] + hc[:, :, None] * ls[:, None, :]).reshape(seq, seq)
    s = jnp.arange(seq, dtype=jnp.int32)[None, :]
    nyq = jnp.where(s % 2 == 0, 1.0, -1.0).astype(F32)
    wf = jnp.concatenate([cos, jnp.where(f == 0, nyq, -sin)], axis=0).astype(BF16)
    return wf, wf.T


def _mm_body(a_ref, b_ref, o_ref):
    o_ref[...] = _dot(a_ref[...], b_ref[...])


def _dft_fwd(wf, x):
    n, seq = wf.shape
    ncol = x.shape[1]
    tm = min(n, 1024)
    tn = min(ncol, 512)
    return pl.pallas_call(
        _mm_body,
        grid=(ncol // tn, n // tm),
        in_specs=[
            pl.BlockSpec((tm, seq), lambda j, i: (i, 0)),
            pl.BlockSpec((seq, tn), lambda j, i: (0, j)),
        ],
        out_specs=pl.BlockSpec((tm, tn), lambda j, i: (i, j)),
        out_shape=jax.ShapeDtypeStruct((n, ncol), F32),
        compiler_params=_cparams("arbitrary", "arbitrary"),
        name="dft_fwd",
    )(wf, x)


def _dft_filter(wf, k):
    n, seq = wf.shape
    tm = min(seq, 1024)
    return pl.pallas_call(
        _mm_body,
        grid=(n // tm,),
        in_specs=[
            pl.BlockSpec((tm, seq), lambda i: (i, 0)),
            pl.BlockSpec((seq, HYENA_WIDTH), lambda i: (0, (i * tm) // seq)),
        ],
        out_specs=pl.BlockSpec((tm, HYENA_WIDTH), lambda i: (i, 0)),
        out_shape=jax.ShapeDtypeStruct((n, HYENA_WIDTH), F32),
        compiler_params=_cparams("arbitrary"),
        name="dft_filter",
    )(wf, k)


def _hy_prod_body(x_ref, h_ref, nyq_ref, p_ref, *, seq):
    n = 2 * seq
    xr = x_ref[0:seq, :]
    xi = x_ref[seq:, :]
    hr = h_ref[0:seq, :]
    hi = h_ref[seq:, :]
    first = lax.broadcasted_iota(jnp.int32, xr.shape, 0) == 0
    w = jnp.where(first, 1.0 / n, 2.0 / n)
    p_ref[0:seq, :] = (jnp.where(first, xr * hr, xr * hr - xi * hi) * w).astype(BF16)
    p_ref[seq:, :] = (jnp.where(first, xi * nyq_ref[0:1, :], xr * hi + xi * hr) * w).astype(BF16)


def _hy_prod(xf, hf, nyq, seq):
    n = 2 * seq
    nc = HYENA_WIDTH // LANE
    return pl.pallas_call(
        functools.partial(_hy_prod_body, seq=seq),
        grid=(BATCH, nc),
        in_specs=[
            pl.BlockSpec((n, LANE), lambda b, c: (0, b * nc + c)),
            pl.BlockSpec((n, LANE), lambda b, c: (0, c)),
            pl.BlockSpec((8, LANE), lambda b, c: (0, c)),
        ],
        out_specs=pl.BlockSpec((n, LANE), lambda b, c: (0, b * nc + c)),
        out_shape=jax.ShapeDtypeStruct((n, BATCH * HYENA_WIDTH), BF16),
        compiler_params=_cparams("arbitrary", "arbitrary"),
        name="hy_prod",
    )(xf, hf, nyq)


def _dft_inv_body(w_ref, p_ref, vx_ref, bias_ref, x0_ref, o_ref):
    y = _dot(w_ref[...], p_ref[...])
    o_ref[...] = ((y + vx_ref[...] * bias_ref[...]) * x0_ref[...]).astype(BF16)


def _dft_inv(wi, p, vx, bias, x0, bb, row0):
    seq, n = wi.shape
    tm = min(seq, 512)
    tn = HYENA_WIDTH
    nt = seq // tm
    return pl.pallas_call(
        _keep_branch_buffer(_dft_inv_body, 5),
        grid=(BATCH, nt),
        in_specs=[
            pl.BlockSpec((tm, n), lambda b, i: (i, 0)),
            pl.BlockSpec((n, tn), lambda b, i: (0, b)),
            pl.BlockSpec((tm, tn), lambda b, i: (b * nt + i, 0)),
            pl.BlockSpec((1, tn), lambda b, i: (0, 0)),
            pl.BlockSpec((tm, tn), lambda b, i: (b * nt + i, 0)),
            _ANY,
        ],
        out_specs=pl.BlockSpec((None, tm, tn), lambda b, i: (SLOT_HYENA, row0 // tm + b * nt + i, 0)),
        out_shape=jax.ShapeDtypeStruct(bb.shape, bb.dtype),
        input_output_aliases={5: 0},
        compiler_params=_cparams("arbitrary", "arbitrary"),
        name="dft_inv",
    )(wi, p, vx, bias.reshape(1, tn), x0, bb)


def _hyena(proj, lp, consts, bb, seq, row0):
    z, deltas, (wf, wi) = consts
    x0, vx, vb = _hy_prep(proj, lp['hy_short_w'], lp['hy_short_b'], seq, row0 // seq)
    k, nyq = _hy_filter(lp, z, deltas, seq)
    xf = _dft_fwd(wf, vb)
    hf = _dft_filter(wf, k)
    p = _hy_prod(xf, hf, nyq, seq)
    return _dft_inv(wi, p, vx, lp['hy_bias'], x0, bb, row0)


def _s5_params(lp, n_seg_steps):
    a_re, a_im = lp['s5_a_re'], lp['s5_a_im']
    dt = jnp.exp(lp['s5_log_dt'])[..., None]
    mag = jnp.exp(a_re * dt)
    ab_re, ab_im = mag * jnp.cos(a_im * dt), mag * jnp.sin(a_im * dt)
    den = a_re * a_re + a_im * a_im
    nr, ni = ab_re - 1.0, ab_im
    cf_re = (nr * a_re + ni * a_im) / den
    cf_im = (ni * a_re - nr * a_im) / den
    b_re, b_im = lp['s5_b_re'], lp['s5_b_im']
    bb_re = cf_re[..., None] * b_re - cf_im[..., None] * b_im
    bb_im = cf_re[..., None] * b_im + cf_im[..., None] * b_re
    eye = jnp.eye(S5_CHUNK_GROUPS, dtype=F32)

    def bdiag_in(m):
        m = m.reshape(2, S5_NCHUNK, S5_CHUNK_GROUPS, S5_STATE, S5_GC)
        return jnp.einsum('dqgpc,gh->dqgchp', m, eye).reshape(2, S5_NCHUNK, S5_CHUNK_CH, S5_CHUNK_ST)

    def bdiag_out(m):
        m = m.reshape(2, S5_NCHUNK, S5_CHUNK_GROUPS, S5_GC, S5_STATE)
        return jnp.einsum('dqgcp,gh->dqhpgc', m, eye).reshape(2, S5_NCHUNK, S5_CHUNK_ST, S5_CHUNK_CH)

    bbd = jnp.concatenate([bdiag_in(bb_re), bdiag_in(bb_im)], axis=-1).astype(BF16)
    cbd = jnp.concatenate([bdiag_out(lp['s5_c_re']), -bdiag_out(lp['s5_c_im'])], axis=-2).astype(BF16)
    a = jnp.stack([ab_re.reshape(2, -1), ab_im.reshape(2, -1)], axis=1)
    pr, pi = ab_re, ab_im
    for _ in range(int(math.log2(n_seg_steps))):
        pr, pi = pr * pr - pi * pi, 2.0 * pr * pi
    ak = jnp.stack([pr.reshape(2, -1), pi.reshape(2, -1)], axis=1)
    return bbd, cbd, a, ak


def _s5_scan_body(u_ref, bbd_ref, cbd_ref, a_ref, ak_ref, dsk_ref, h0_ref, y_ref, hl_ref, xr_ref, xi_ref, up_ref,
                  *, seq):
    nk = seq // S5_LANES
    shape = (S5_LANES, S5_CHUNK_ST)
    for j in range(S5_LANES):
        up_ref[pl.ds(j, nk, stride=S5_LANES), :] = u_ref[pl.ds(j * nk, nk), :]
    u = up_ref[...]
    ub = u.astype(BF16)
    row = lax.broadcasted_iota(jnp.int32, shape, 0)
    y = jnp.zeros((seq, S5_CHUNK_CH), F32)
    for d in range(2):
        xr_ref[...] = _dot(ub, bbd_ref[d, :, :S5_CHUNK_ST])
        xi_ref[...] = _dot(ub, bbd_ref[d, :, S5_CHUNK_ST:])
        ar = jnp.broadcast_to(a_ref[d, 0:1, :], shape)
        ai = jnp.broadcast_to(a_ref[d, 1:2, :], shape)

        def tile(s, d=d):
            k = s if d == 0 else nk - 1 - s
            return pl.ds(pl.multiple_of(k * S5_LANES, S5_LANES), S5_LANES)

        def scan_step(s, carry, ar=ar, ai=ai, tile=tile):
            xr, xi = carry
            rows = tile(s)
            nr = ar * xr - ai * xi + xr_ref[rows, :]
            ni = ar * xi + ai * xr + xi_ref[rows, :]
            xr_ref[rows, :] = nr
            xi_ref[rows, :] = ni
            return nr, ni

        zero = jnp.zeros(shape, F32)
        er, ei = lax.fori_loop(0, nk, scan_step, (zero, zero), unroll=8)

        akr, aki = ak_ref[d, 0:1, :], ak_ref[d, 1:2, :]
        hr, hi = h0_ref[2 * d:2 * d + 1, :], h0_ref[2 * d + 1:2 * d + 2, :]
        in_r, in_i = zero, zero
        for j in (range(S5_LANES) if d == 0 else range(S5_LANES - 1, -1, -1)):
            in_r = jnp.where(row == j, hr, in_r)
            in_i = jnp.where(row == j, hi, in_i)
            hr, hi = (akr * hr - aki * hi + er[j:j + 1, :], akr * hi + aki * hr + ei[j:j + 1, :])
        hl_ref[2 * d:2 * d + 1, :] = hr
        hl_ref[2 * d + 1:2 * d + 2, :] = hi

        def fix_step(s, carry, ar=ar, ai=ai, tile=tile):
            gr, gi = carry
            gr, gi = ar * gr - ai * gi, ar * gi + ai * gr
            rows = tile(s)
            xr_ref[rows, :] += gr
            xi_ref[rows, :] += gi
            return gr, gi

        lax.fori_loop(0, nk, fix_step, (in_r, in_i), unroll=8)
        y = y + _dot(xr_ref[...].astype(BF16), cbd_ref[d, :S5_CHUNK_ST, :])
        y = y + _dot(xi_ref[...].astype(BF16), cbd_ref[d, S5_CHUNK_ST:, :])
        y = y + u * dsk_ref[d]
    up_ref[...] = y
    for j in range(S5_LANES):
        y_ref[pl.ds(j * nk, nk), :] = up_ref[pl.ds(j, nk, stride=S5_LANES), :]


def _s5_scan(proj, params, dskip, h0, seq, row_blk0):
    bbd, cbd, a, ak = params
    nq = S5_NCHUNK
    col0 = OFF_S5 // S5_CHUNK_CH
    return pl.pallas_call(
        functools.partial(_s5_scan_body, seq=seq),
        grid=(BATCH, nq),
        in_specs=[
            pl.BlockSpec((seq, S5_CHUNK_CH), lambda b, q: (row_blk0 + b, col0 + q)),
            pl.BlockSpec((2, None, S5_CHUNK_CH, 2 * S5_CHUNK_ST), lambda b, q: (0, q, 0, 0)),
            pl.BlockSpec((2, None, 2 * S5_CHUNK_ST, S5_CHUNK_CH), lambda b, q: (0, q, 0, 0)),
            pl.BlockSpec((2, 2, S5_CHUNK_ST), lambda b, q: (0, 0, q)),
            pl.BlockSpec((2, 2, S5_CHUNK_ST), lambda b, q: (0, 0, q)),
            pl.BlockSpec((2, 1, S5_CHUNK_CH), lambda b, q: (0, 0, q)),
            pl.BlockSpec((None, 4, S5_CHUNK_ST), lambda b, q: (b, 0, q)),
        ],
        out_specs=[
            pl.BlockSpec((seq, S5_CHUNK_CH), lambda b, q: (b, q)),
            pl.BlockSpec((None, 4, S5_CHUNK_ST), lambda b, q: (b, 0, q)),
        ],
        out_shape=[
            jax.ShapeDtypeStruct((BATCH * seq, S5_WIDTH), F32),
            jax.ShapeDtypeStruct((BATCH, 4, S5_GROUPS * S5_STATE), F32),
        ],
        scratch_shapes=[pltpu.VMEM((seq, S5_CHUNK_ST), F32), pltpu.VMEM((seq, S5_CHUNK_ST), F32),
                        pltpu.VMEM((seq, S5_CHUNK_CH), F32)],
        compiler_params=_cparams("arbitrary", "arbitrary"),
        name="s5_scan",
    )(proj, bbd, cbd, a, ak, dskip.reshape(2, 1, S5_WIDTH), h0)


def _s5_glu_body(y_ref, w_ref, b_ref, o_ref):
    g = _dot(jax.nn.gelu(y_ref[...]).astype(BF16), w_ref[...].astype(BF16)) + b_ref[...]
    o_ref[...] = (g[:, :S5_WIDTH] * jax.nn.sigmoid(g[:, S5_WIDTH:])).astype(BF16)


def _s5_glu(y, w, b, bb, row0):
    n_rows = y.shape[0]
    tm = TM_EW
    return pl.pallas_call(
        _keep_branch_buffer(_s5_glu_body, 3),
        grid=(n_rows // tm,),
        in_specs=[
            pl.BlockSpec((tm, S5_WIDTH), lambda i: (i, 0)),
            pl.BlockSpec((S5_WIDTH, 2 * S5_WIDTH), lambda i: (0, 0)),
            pl.BlockSpec((1, 2 * S5_WIDTH), lambda i: (0, 0)),
            _ANY,
        ],
        out_specs=pl.BlockSpec((None, tm, S5_WIDTH), lambda i: (SLOT_S5, row0 // tm + i, 0)),
        out_shape=jax.ShapeDtypeStruct(bb.shape, bb.dtype),
        input_output_aliases={3: 0},
        compiler_params=_cparams("arbitrary"),
        name="s5_glu",
    )(y, w, b.reshape(1, -1), bb)


def _merge_body(u_ref, wg_ref, bg_ref, y_ref, wb_ref, o_ref, acc_ref):
    n = pl.program_id(2)
    gate = jax.nn.sigmoid(_dot(u_ref[...], wg_ref[...].astype(BF16)) + bg_ref[...])
    contrib = gate * _dot(y_ref[...], wb_ref[...].astype(BF16))

    @pl.when(n == 0)
    def _():
        acc_ref[...] = contrib

    @pl.when(n > 0)
    def _():
        acc_ref[...] += contrib

    @pl.when(n == N_BRANCH - 1)
    def _():
        o_ref[...] = acc_ref[...].astype(BF16)


def _merge(u, w_gate, b_gate, branches, w_branch, l, n_rows):
    tc = 512
    ncol = D_MODEL // tc
    return pl.pallas_call(
        _merge_body,
        grid=(n_rows // TM, ncol, N_BRANCH),
        in_specs=[
            pl.BlockSpec((TM, D_MODEL), lambda i, c, n: (i, 0)),
            pl.BlockSpec((None, D_MODEL, tc), lambda i, c, n: (l, 0, n * ncol + c)),
            pl.BlockSpec((1, tc), lambda i, c, n: (0, n * ncol + c)),
            pl.BlockSpec((None, TM, BRANCH_WIDTH), lambda i, c, n: (n, i, 0)),
            pl.BlockSpec((None, None, BRANCH_WIDTH, tc), lambda i, c, n: (l, n, 0, c)),
        ],
        out_specs=pl.BlockSpec((TM, tc), lambda i, c, n: (i, c)),
        out_shape=jax.ShapeDtypeStruct((n_rows, D_MODEL), BF16),
        scratch_shapes=[pltpu.VMEM((TM, tc), F32)],
        compiler_params=_cparams("arbitrary", "arbitrary", "arbitrary"),
        name="merge",
    )(u, w_gate, b_gate.reshape(1, -1), branches, w_branch)


def _out_body(m_ref, w_ref, x_ref, g_ref, o_ref):
    o_ref[...] = x_ref[...] + g_ref[...] * _dot(m_ref[...], w_ref[...].astype(BF16))


def _out_proj(merged, w_out, l, x, mods, n_rows):
    tn = 512
    gate = pl.BlockSpec((None, None, 1, tn), lambda i, j: (_mod_row(i, TM), 5, 0, j))
    return pl.pallas_call(
        _out_body,
        grid=(n_rows // TM, D_MODEL // tn),
        in_specs=[
            pl.BlockSpec((TM, D_MODEL), lambda i, j: (i, 0)),
            pl.BlockSpec((None, D_MODEL, tn), lambda i, j: (l, 0, j)),
            pl.BlockSpec((TM, tn), lambda i, j: (i, j)),
            gate,
        ],
        out_specs=pl.BlockSpec((TM, tn), lambda i, j: (i, j)),
        out_shape=jax.ShapeDtypeStruct((n_rows, D_MODEL), F32),
        compiler_params=_cparams("arbitrary", "arbitrary"),
        name="out_proj",
    )(merged, w_out, x, mods)


def _mixer(x, u, lp, big, l, mods, consts, bb, last):
    cos, sin, hy_lat, hy_ctx = consts
    proj = _proj(u, big['w_in'], l)
    q, k, v = _qkv(proj, cos, sin, lp['q_norm'], lp['k_norm'])

    ctx_blk = N_LAT // CTX_LEN
    zero_h = jnp.zeros((BATCH, 4, S5_GROUPS * S5_STATE), F32)
    ys_ctx, h_ctx = _s5_scan(proj, _s5_params(lp, CTX_LEN // S5_LANES), lp['s5_d'], zero_h, CTX_LEN, ctx_blk)
    ys_lat, _ = _s5_scan(proj, _s5_params(lp, SEQ // S5_LANES), lp['s5_d'], h_ctx, SEQ, 0)

    bb = _s5_glu(ys_lat, lp['s5_glu_w'], lp['s5_glu_b'], bb, 0)
    bb = _attn_lat(q, k, v, bb)
    bb = _pool(proj, lp['pool_w'], lp['pool_scale'], bb, SEQ, 0)
    bb = _hyena(proj, lp, hy_lat, bb, SEQ, 0)
    if not last:
        bb = _s5_glu(ys_ctx, lp['s5_glu_w'], lp['s5_glu_b'], bb, N_LAT)
        bb = _attn_ctx(q, k, v, bb)
        bb = _pool(proj, lp['pool_w'], lp['pool_scale'], bb, CTX_LEN, ctx_blk)
        bb = _hyena(proj, lp, hy_ctx, bb, CTX_LEN, N_LAT)

    n_rows = N_LAT if last else MT
    merged = _merge(u, big['w_gate'], lp['b_gate'], bb, big['w_branch'], l, n_rows)
    return _out_proj(merged, big['w_out'], l, x, mods, n_rows), bb


def kernel(x, c, ctx, c_ctx, w_ada, b_ada, norm_ffn1, norm_mix, norm_ffn2, norm_final, ffn1_wi, ffn1_wo, ffn2_wi, ffn2_wo, w_in, w_gate, b_gate, w_branch, w_out, pool_w, pool_scale, q_norm, k_norm, hy_short_w, hy_short_b, hy_f1_w, hy_f1_b, hy_f2_w, hy_f2_b, hy_f3_w, hy_freq, hy_bias, s5_a_re, s5_a_im, s5_log_dt, s5_b_re, s5_b_im, s5_c_re, s5_c_im, s5_d, s5_glu_w, s5_glu_b):
    big = dict(w_in=w_in, w_gate=w_gate, w_branch=w_branch, w_out=w_out)
    per_layer = dict(
        b_gate=b_gate, pool_w=pool_w,
        pool_scale=pool_scale, q_norm=q_norm, k_norm=k_norm, hy_short_w=hy_short_w, hy_short_b=hy_short_b,
        hy_f1_w=hy_f1_w, hy_f1_b=hy_f1_b, hy_f2_w=hy_f2_w, hy_f2_b=hy_f2_b, hy_f3_w=hy_f3_w, hy_freq=hy_freq,
        hy_bias=hy_bias, s5_a_re=s5_a_re, s5_a_im=s5_a_im, s5_log_dt=s5_log_dt, s5_b_re=s5_b_re, s5_b_im=s5_b_im,
        s5_c_re=s5_c_re, s5_c_im=s5_c_im, s5_d=s5_d, s5_glu_w=s5_glu_w, s5_glu_b=s5_glu_b)

    cos, sin = _rope_tables()
    deltas = _hy_deltas()
    consts = (cos, sin,
              (_hy_feats(SEQ), deltas, _dft_matrices(SEQ)),
              (_hy_feats(CTX_LEN), deltas, _dft_matrices(CTX_LEN)))

    cc = jnp.concatenate([c, c_ctx[None], jnp.zeros((8 - BATCH - 1, D_MODEL), F32)], axis=0)
    mods_all = _ada(cc, w_ada, b_ada).reshape(DEPTH, 8, N_MOD, 1, D_MODEL)

    xs = jnp.concatenate([x.reshape(N_LAT, D_MODEL), ctx.reshape(N_CTX, D_MODEL)], axis=0)
    bb = jnp.zeros((N_BRANCH, MT, BRANCH_WIDTH), BF16)
    for l in range(DEPTH):
        last = l == DEPTH - 1
        lp = {name: w[l] for name, w in per_layer.items()}
        mods = mods_all[l]
        xs = _ffn(xs, mods, norm_ffn1[l], ffn1_wi, ffn1_wo, l, 0, MT)
        u = _normmod(xs, norm_mix[l], mods, 3, MT)
        xs, bb = _mixer(xs, u, lp, big, l, mods, consts, bb, last)
        xs = _ffn(xs, mods, norm_ffn2[l], ffn2_wi, ffn2_wo, l, 6, N_LAT if last else MT)
    return _final_norm(xs, norm_final, N_LAT).reshape(BATCH, SEQ, D_MODEL)
```

```python
import functools
import math

import jax
import jax.numpy as jnp
from jax import lax
from jax.experimental import pallas as pl
from jax.experimental.pallas import tpu as pltpu

F32 = jnp.float32
BF16 = jnp.bfloat16

D_MODEL = 2048
BATCH = 4
SEQ = 2048
DEPTH = 4
GRID_W = 64
CTX_LEN = 256
D_FF = 5632
N_MOD = 9
EPS = 1e-6

POOL_WINDOWS = (2, 4, 8, 16)
HEAD_DIM = 128
N_Q_HEADS = 4
N_KV_HEADS = 2
Q_GROUP = N_Q_HEADS // N_KV_HEADS
ROPE_THETA = 10000.0
ROPE_FREQS = HEAD_DIM // 4

HYENA_WIDTH = 512
HYENA_EMB = 33
HYENA_BANDS = (HYENA_EMB - 1) // 2
HYENA_HIDDEN = 64
HYENA_TARGET = 1e-2
HYENA_FAST_PCT = 0.3
HYENA_SLOW_PCT = 1.5

S5_WIDTH = 512
S5_GC = 16
S5_GROUPS = S5_WIDTH // S5_GC
S5_STATE = 64
S5_LANES = 8
S5_CHUNK_GROUPS = 8
S5_CHUNK_CH = S5_CHUNK_GROUPS * S5_GC
S5_CHUNK_ST = S5_CHUNK_GROUPS * S5_STATE
S5_NCHUNK = S5_GROUPS // S5_CHUNK_GROUPS

N_BRANCH = 4
BRANCH_WIDTH = 512
LANE = 128
COL = 512

OFF_Q = 512
OFF_KV = 1024
OFF_HY = 1536
OFF_S5 = 3072
IN_WIDTH = 3584

N_LAT = BATCH * SEQ
N_CTX = BATCH * CTX_LEN
MT = N_LAT + N_CTX

TM = 1024
TM_EW = 512
VMEM_LIMIT = 56 * 1024 * 1024


def _cparams(*sem):
    return pltpu.CompilerParams(dimension_semantics=sem, vmem_limit_bytes=VMEM_LIMIT)


def _dot(a, b):
    return jnp.dot(a, b, preferred_element_type=F32)


SLOT_POOL, SLOT_ATTN, SLOT_HYENA, SLOT_S5 = range(4)
_ANY = pl.BlockSpec(memory_space=pl.ANY)


def _keep_branch_buffer(body, n_in):
    def wrapped(*refs):
        return body(*refs[:n_in], *refs[n_in + 1:])
    return wrapped


def _mod_row(i, tm):
    return jnp.minimum((i * tm) // SEQ, BATCH)


def _mod_spec(tm, k):
    return pl.BlockSpec((None, None, 1, D_MODEL), lambda i, *_: (_mod_row(i, tm), k, 0, 0))


def _ada_body(c_ref, w_ref, b_ref, o_ref):
    c = c_ref[...]
    a = (c * jax.nn.sigmoid(c)).astype(BF16)
    o_ref[...] = _dot(a, w_ref[...].astype(BF16)) + b_ref[...]


def _ada(cc, w_ada, b_ada):
    tn = 1024
    nw = N_MOD * D_MODEL
    return pl.pallas_call(
        _ada_body,
        grid=(DEPTH, nw // tn),
        in_specs=[
            pl.BlockSpec((8, D_MODEL), lambda l, j: (0, 0)),
            pl.BlockSpec((None, D_MODEL, tn), lambda l, j: (l, 0, j)),
            pl.BlockSpec((None, 1, tn), lambda l, j: (l, 0, j)),
        ],
        out_specs=pl.BlockSpec((None, 8, tn), lambda l, j: (l, 0, j)),
        out_shape=jax.ShapeDtypeStruct((DEPTH, 8, nw), F32),
        compiler_params=_cparams("arbitrary", "arbitrary"),
        name="ada",
    )(cc, w_ada, b_ada.reshape(DEPTH, 1, nw))


NORM_ROWS = 16


def _rows_loop(n_rows, step):
    for c in range(n_rows // NORM_ROWS):
        step(pl.ds(c * NORM_ROWS, NORM_ROWS))


def _rms_scale_rows(x_ref, r_ref, scale, shift, o_ref):
    def stats(rows):
        x = x_ref[rows, :]
        r_ref[rows, :] = lax.rsqrt(jnp.mean(x * x, axis=-1, keepdims=True) + EPS)

    def apply(rows):
        y = x_ref[rows, :] * r_ref[rows, :] * scale
        if shift is not None:
            y = y + shift
        o_ref[rows, :] = y.astype(o_ref.dtype)

    _rows_loop(x_ref.shape[0], stats)
    _rows_loop(x_ref.shape[0], apply)


def _normmod_body(x_ref, g_ref, sh_ref, sc_ref, o_ref, r_ref):
    _rms_scale_rows(x_ref, r_ref, g_ref[...] * (1.0 + sc_ref[...]), sh_ref[...], o_ref)


def _norm_body(x_ref, g_ref, o_ref, r_ref):
    _rms_scale_rows(x_ref, r_ref, g_ref[...], None, o_ref)


def _normmod(x, g, mods, base, n_rows):
    tm = TM_EW
    return pl.pallas_call(
        _normmod_body,
        grid=(n_rows // tm,),
        in_specs=[
            pl.BlockSpec((tm, D_MODEL), lambda i: (i, 0)),
            pl.BlockSpec((1, D_MODEL), lambda i: (0, 0)),
            _mod_spec(tm, base),
            _mod_spec(tm, base + 1),
        ],
        out_specs=pl.BlockSpec((tm, D_MODEL), lambda i: (i, 0)),
        out_shape=jax.ShapeDtypeStruct((n_rows, D_MODEL), BF16),
        scratch_shapes=[pltpu.VMEM((tm, 1), F32)],
        compiler_params=_cparams("arbitrary"),
        name="normmod",
    )(x, g.reshape(1, D_MODEL), mods, mods)


def _final_norm(x, g, n_rows):
    tm = TM_EW
    return pl.pallas_call(
        _norm_body,
        grid=(n_rows // tm,),
        in_specs=[
            pl.BlockSpec((tm, D_MODEL), lambda i: (i, 0)),
            pl.BlockSpec((1, D_MODEL), lambda i: (0, 0)),
        ],
        out_specs=pl.BlockSpec((tm, D_MODEL), lambda i: (i, 0)),
        out_shape=jax.ShapeDtypeStruct((n_rows, D_MODEL), F32),
        scratch_shapes=[pltpu.VMEM((tm, 1), F32)],
        compiler_params=_cparams("arbitrary"),
        name="final_norm",
    )(x, g.reshape(1, D_MODEL))


MXU_COLS = 256


def _ffn_a_body(u_ref, wa_ref, wb_ref, h_ref):
    u = u_ref[...]
    for c in range(0, h_ref.shape[1], MXU_COLS):
        sl = slice(c, c + MXU_COLS)
        a = _dot(u, wa_ref[:, sl].astype(BF16))
        b = _dot(u, wb_ref[:, sl].astype(BF16))
        h_ref[:, sl] = (a * jax.nn.sigmoid(a) * b).astype(BF16)


def _ffn_a(u, wi, l, n_rows):
    tf = 512
    nf = D_FF // tf
    return pl.pallas_call(
        _ffn_a_body,
        grid=(n_rows // TM, nf),
        in_specs=[
            pl.BlockSpec((TM, D_MODEL), lambda i, j: (i, 0)),
            pl.BlockSpec((None, D_MODEL, tf), lambda i, j: (l, 0, j)),
            pl.BlockSpec((None, D_MODEL, tf), lambda i, j: (l, 0, j + nf)),
        ],
        out_specs=pl.BlockSpec((TM, tf), lambda i, j: (i, j)),
        out_shape=jax.ShapeDtypeStruct((n_rows, D_FF), BF16),
        compiler_params=_cparams("arbitrary", "arbitrary"),
        name="ffn_a",
    )(u, wi, wi)


def _ffn_b_body(h_ref, w_ref, x_ref, g_ref, o_ref):
    o_ref[...] = x_ref[...] + (0.5 * g_ref[...]) * _dot(h_ref[...], w_ref[...].astype(BF16))


def _ffn_b(h, wo, l, x, mods, gate_idx, n_rows):
    tn = 256
    gate = pl.BlockSpec((None, None, 1, tn), lambda i, j: (_mod_row(i, TM), gate_idx, 0, j))
    return pl.pallas_call(
        _ffn_b_body,
        grid=(n_rows // TM, D_MODEL // tn),
        in_specs=[
            pl.BlockSpec((TM, D_FF), lambda i, j: (i, 0)),
            pl.BlockSpec((None, D_FF, tn), lambda i, j: (l, 0, j)),
            pl.BlockSpec((TM, tn), lambda i, j: (i, j)),
            gate,
        ],
        out_specs=pl.BlockSpec((TM, tn), lambda i, j: (i, j)),
        out_shape=jax.ShapeDtypeStruct((n_rows, D_MODEL), F32),
        compiler_params=_cparams("arbitrary", "arbitrary"),
        name="ffn_b",
    )(h, wo, x, mods)


def _ffn(x, mods, g, wi, wo, l, base, n_rows):
    u = _normmod(x, g, mods, base, n_rows)
    h = _ffn_a(u, wi, l, n_rows)
    return _ffn_b(h, wo, l, x, mods, base + 2, n_rows)


def _proj_body(u_ref, w_ref, o_ref):
    o_ref[...] = _dot(u_ref[...], w_ref[...].astype(BF16))


def _proj(u, w_in, l):
    tn = COL
    tm = 1536
    return pl.pallas_call(
        _proj_body,
        grid=(MT // tm, IN_WIDTH // tn),
        in_specs=[
            pl.BlockSpec((tm, D_MODEL), lambda i, j: (i, 0)),
            pl.BlockSpec((None, D_MODEL, tn), lambda i, j: (l, 0, j)),
        ],
        out_specs=pl.BlockSpec((tm, tn), lambda i, j: (i, j)),
        out_shape=jax.ShapeDtypeStruct((MT, IN_WIDTH), F32),
        compiler_params=_cparams("arbitrary", "arbitrary"),
        name="proj",
    )(u, w_in)


def _rope_tables():
    t = jnp.arange(SEQ)
    rows = (t // GRID_W).astype(F32)
    cols = (t % GRID_W).astype(F32)
    freqs = ROPE_THETA ** (-jnp.arange(ROPE_FREQS, dtype=F32) / ROPE_FREQS)
    ar = rows[:, None] * freqs[None, :]
    ac = cols[:, None] * freqs[None, :]
    cos = jnp.concatenate([jnp.cos(ar), jnp.cos(ar), jnp.cos(ac), jnp.cos(ac)], axis=-1)
    sin = jnp.concatenate([-jnp.sin(ar), jnp.sin(ar), -jnp.sin(ac), jnp.sin(ac)], axis=-1)
    cos = jnp.concatenate([cos, jnp.ones((TM_EW, HEAD_DIM), F32)], axis=0)
    sin = jnp.concatenate([sin, jnp.zeros((TM_EW, HEAD_DIM), F32)], axis=0)
    return cos, sin


def _qkv_body(q_ref, kv_ref, cos_ref, sin_ref, qn_ref, kn_ref, qo_ref, ko_ref, vo_ref):
    cos = cos_ref[...]
    sin = sin_ref[...]
    lane = lax.broadcasted_iota(jnp.int32, cos.shape, 1)
    first = (lane % (2 * ROPE_FREQS)) < ROPE_FREQS

    def norm_rope(xh, g):
        y = xh * lax.rsqrt(jnp.mean(xh * xh, axis=-1, keepdims=True) + EPS) * g
        swapped = jnp.where(first, pltpu.roll(y, HEAD_DIM - ROPE_FREQS, 1), pltpu.roll(y, ROPE_FREQS, 1))
        return y * cos + swapped * sin

    scale = 1.0 / math.sqrt(HEAD_DIM)
    for h in range(N_Q_HEADS):
        sl = slice(h * HEAD_DIM, (h + 1) * HEAD_DIM)
        qo_ref[:, sl] = (norm_rope(q_ref[:, sl], qn_ref[...]) * scale).astype(BF16)
    for h in range(N_KV_HEADS):
        sl = slice(h * HEAD_DIM, (h + 1) * HEAD_DIM)
        ko_ref[:, sl] = norm_rope(kv_ref[:, sl], kn_ref[...]).astype(BF16)
    kvw = N_KV_HEADS * HEAD_DIM
    vo_ref[...] = kv_ref[:, kvw:].astype(BF16)


def _qkv(proj, cos, sin, q_norm, k_norm):
    tm = TM_EW
    n_lat_tiles = N_LAT // tm
    per_seq = SEQ // tm
    kvw = N_KV_HEADS * HEAD_DIM

    def tab(i):
        return (jnp.where(i < n_lat_tiles, i % per_seq, per_seq), 0)

    return pl.pallas_call(
        _qkv_body,
        grid=(MT // tm,),
        in_specs=[
            pl.BlockSpec((tm, COL), lambda i: (i, OFF_Q // COL)),
            pl.BlockSpec((tm, COL), lambda i: (i, OFF_KV // COL)),
            pl.BlockSpec((tm, HEAD_DIM), tab),
            pl.BlockSpec((tm, HEAD_DIM), tab),
            pl.BlockSpec((1, HEAD_DIM), lambda i: (0, 0)),
            pl.BlockSpec((1, HEAD_DIM), lambda i: (0, 0)),
        ],
        out_specs=[
            pl.BlockSpec((tm, COL), lambda i: (i, 0)),
            pl.BlockSpec((tm, kvw), lambda i: (i, 0)),
            pl.BlockSpec((tm, kvw), lambda i: (i, 0)),
        ],
        out_shape=[
            jax.ShapeDtypeStruct((MT, COL), BF16),
            jax.ShapeDtypeStruct((MT, kvw), BF16),
            jax.ShapeDtypeStruct((MT, kvw), BF16),
        ],
        compiler_params=_cparams("arbitrary"),
        name="qkv",
    )(proj, proj, cos, sin, q_norm.reshape(1, HEAD_DIM), k_norm.reshape(1, HEAD_DIM))


def _attn_body(*refs, with_lat):
    if with_lat:
        q_ref, kl_ref, vl_ref, kc_ref, vc_ref, o_ref = refs
    else:
        q_ref, kc_ref, vc_ref, o_ref = refs
    nt = (((1,), (1,)), ((), ()))
    for g in range(Q_GROUP):
        sl = slice(g * HEAD_DIM, (g + 1) * HEAD_DIM)
        q = q_ref[:, sl]
        sc = lax.dot_general(q, kc_ref[...], nt, preferred_element_type=F32)
        m = jnp.max(sc, axis=-1, keepdims=True)
        if with_lat:
            s_lat = lax.dot_general(q, kl_ref[...], nt, preferred_element_type=F32)
            m = jnp.maximum(m, jnp.max(s_lat, axis=-1, keepdims=True))
            e_lat = jnp.exp(s_lat - m)
        ec = jnp.exp(sc - m)
        den = jnp.sum(ec, axis=-1, keepdims=True)
        o = _dot(ec.astype(BF16), vc_ref[...])
        if with_lat:
            den = den + jnp.sum(e_lat, axis=-1, keepdims=True)
            o = o + _dot(e_lat.astype(BF16), vl_ref[...])
        o_ref[:, sl] = (o / den).astype(BF16)


def _attn_lat(q, k, v, bb):
    tq = 512
    nq = SEQ // tq
    ctx_blk = N_LAT // CTX_LEN
    gw = Q_GROUP * HEAD_DIM
    return pl.pallas_call(
        _keep_branch_buffer(functools.partial(_attn_body, with_lat=True), 5),
        grid=(BATCH, N_KV_HEADS, nq),
        in_specs=[
            pl.BlockSpec((tq, gw), lambda b, h, i: (b * nq + i, h)),
            pl.BlockSpec((SEQ, HEAD_DIM), lambda b, h, i: (b, h)),
            pl.BlockSpec((SEQ, HEAD_DIM), lambda b, h, i: (b, h)),
            pl.BlockSpec((CTX_LEN, HEAD_DIM), lambda b, h, i: (ctx_blk + b, h)),
            pl.BlockSpec((CTX_LEN, HEAD_DIM), lambda b, h, i: (ctx_blk + b, h)),
            _ANY,
        ],
        out_specs=pl.BlockSpec((None, tq, gw), lambda b, h, i: (SLOT_ATTN, b * nq + i, h)),
        out_shape=jax.ShapeDtypeStruct(bb.shape, bb.dtype),
        input_output_aliases={5: 0},
        compiler_params=_cparams("arbitrary", "arbitrary", "arbitrary"),
        name="attn_lat",
    )(q, k, v, k, v, bb)


def _attn_ctx(q, k, v, bb):
    ctx_blk = N_LAT // CTX_LEN
    gw = Q_GROUP * HEAD_DIM
    return pl.pallas_call(
        _keep_branch_buffer(functools.partial(_attn_body, with_lat=False), 3),
        grid=(BATCH, N_KV_HEADS),
        in_specs=[
            pl.BlockSpec((CTX_LEN, gw), lambda b, h: (ctx_blk + b, h)),
            pl.BlockSpec((CTX_LEN, HEAD_DIM), lambda b, h: (ctx_blk + b, h)),
            pl.BlockSpec((CTX_LEN, HEAD_DIM), lambda b, h: (ctx_blk + b, h)),
            _ANY,
        ],
        out_specs=pl.BlockSpec((None, CTX_LEN, gw), lambda b, h: (SLOT_ATTN, ctx_blk + b, h)),
        out_shape=jax.ShapeDtypeStruct(bb.shape, bb.dtype),
        input_output_aliases={3: 0},
        compiler_params=_cparams("arbitrary", "arbitrary"),
        name="attn_ctx",
    )(q, k, v, bb)


POOL_PAD = 8


def _pool_body(a_ref, w_ref, s_ref, o_ref, *, seq):
    lp = seq + 2 * POOL_PAD
    t = lax.broadcasted_iota(jnp.int32, (seq, LANE), 0)
    zpad = jnp.zeros((POOL_PAD, LANE), F32)
    for gi, win in enumerate(POOL_WINDOWS):
        sl = slice(gi * LANE, (gi + 1) * LANE)
        a = a_ref[:, sl]
        s = jnp.concatenate([zpad, a, zpad], axis=0)
        s = s + pltpu.roll(s, 1, 0)
        half = 1
        while 2 * half < win:
            s = pltpu.roll(s, half, 0) + pltpu.roll(s, lp - half, 0)
            half *= 2
        s = s[POOL_PAD:POOL_PAD + seq]
        lo = jnp.maximum(t - win // 2, 0)
        hi = jnp.minimum(t + win // 2, seq)
        pooled = s / (hi - lo).astype(F32) - a
        y = _dot(pooled.astype(BF16), w_ref[gi].astype(BF16))
        o_ref[:, sl] = (y * s_ref[:, sl]).astype(BF16)


def _pool(proj, pool_w, pool_scale, bb, seq, row_blk0):
    width = len(POOL_WINDOWS) * LANE
    return pl.pallas_call(
        _keep_branch_buffer(functools.partial(_pool_body, seq=seq), 3),
        grid=(BATCH,),
        in_specs=[
            pl.BlockSpec((seq, width), lambda b: (row_blk0 + b, 0)),
            pl.BlockSpec((len(POOL_WINDOWS), LANE, LANE), lambda b: (0, 0, 0)),
            pl.BlockSpec((1, width), lambda b: (0, 0)),
            _ANY,
        ],
        out_specs=pl.BlockSpec((None, seq, width), lambda b: (SLOT_POOL, row_blk0 + b, 0)),
        out_shape=jax.ShapeDtypeStruct(bb.shape, bb.dtype),
        input_output_aliases={3: 0},
        compiler_params=_cparams("arbitrary"),
        name="pool",
    )(proj, pool_w, pool_scale.reshape(1, width), bb)


def _hy_prep_body(x0_ref, x1_ref, v_ref, w0_ref, w1_ref, wv_ref, b0_ref, b1_ref, bv_ref,
                  x0o_ref, vx_ref, vb_ref, *, seq):
    t = lax.broadcasted_iota(jnp.int32, (seq, LANE), 0)

    def conv(x_ref, w_ref, b_ref):
        x = x_ref[...]
        prev = jnp.where(t >= 1, pltpu.roll(x, 1, 0), 0.0)
        nxt = jnp.where(t <= seq - 2, pltpu.roll(x, seq - 1, 0), 0.0)
        return prev * w_ref[0:1, :] + x * w_ref[1:2, :] + nxt * w_ref[2:3, :] + b_ref[...]

    x0o_ref[...] = conv(x0_ref, w0_ref, b0_ref)
    vx = conv(v_ref, wv_ref, bv_ref) * conv(x1_ref, w1_ref, b1_ref)
    vx_ref[...] = vx
    vb_ref[...] = vx.astype(BF16)


def _hy_prep(proj, short_w, short_b, seq, row_blk0):
    nc = HYENA_WIDTH // LANE
    c0 = OFF_HY // LANE
    short_b = short_b.reshape(1, 3 * HYENA_WIDTH)

    def xspec(part):
        return pl.BlockSpec((seq, LANE), lambda b, c: (row_blk0 + b, c0 + part * nc + c))

    def wspec(part, rows):
        return pl.BlockSpec((rows, LANE), lambda b, c: (0, part * nc + c))

    return pl.pallas_call(
        functools.partial(_hy_prep_body, seq=seq),
        grid=(BATCH, nc),
        in_specs=[xspec(0), xspec(1), xspec(2), wspec(0, 3), wspec(1, 3), wspec(2, 3),
                  wspec(0, 1), wspec(1, 1), wspec(2, 1)],
        out_specs=[
            pl.BlockSpec((seq, LANE), lambda b, c: (b, c)),
            pl.BlockSpec((seq, LANE), lambda b, c: (b, c)),
            pl.BlockSpec((seq, LANE), lambda b, c: (0, b * nc + c)),
        ],
        out_shape=[
            jax.ShapeDtypeStruct((BATCH * seq, HYENA_WIDTH), F32),
            jax.ShapeDtypeStruct((BATCH * seq, HYENA_WIDTH), F32),
            jax.ShapeDtypeStruct((seq, BATCH * HYENA_WIDTH), BF16),
        ],
        compiler_params=_cparams("arbitrary", "arbitrary"),
        name="hy_prep",
    )(proj, proj, proj, short_w, short_w, short_w, short_b, short_b, short_b)


def _hy_feats(seq):
    t = jnp.linspace(0.0, 1.0, seq, dtype=F32)[:, None]
    f = jnp.linspace(1e-4, HYENA_BANDS - 1, HYENA_BANDS, dtype=F32)
    w = 2.0 * math.pi * jnp.arange(seq, dtype=F32) / seq
    fw = w[:, None] * f[None, :]
    z = jnp.concatenate([t, jnp.cos(fw), -jnp.sin(fw)], axis=-1)
    return jnp.pad(z, ((0, 0), (0, LANE - HYENA_EMB)))


def _hy_deltas():
    max_decay = math.log(HYENA_TARGET) / HYENA_FAST_PCT
    min_decay = math.log(HYENA_TARGET) / HYENA_SLOW_PCT
    return jnp.abs(jnp.linspace(min_decay, max_decay, HYENA_WIDTH, dtype=F32)).reshape(1, HYENA_WIDTH)


def _hy_filter_body(z_ref, w1_ref, b1_ref, w2_ref, b2_ref, w3_ref, fr_ref, dl_ref, k_ref, nyq_ref, *, seq):
    hp = lax.Precision.HIGHEST
    freq = fr_ref[...]
    h = jnp.sin(freq * (jnp.dot(z_ref[...], w1_ref[...], precision=hp, preferred_element_type=F32) + b1_ref[...]))
    h = jnp.sin(freq * (jnp.dot(h, w2_ref[...], precision=hp, preferred_element_type=F32) + b2_ref[...]))
    h = jnp.dot(h, w3_ref[...], precision=hp, preferred_element_type=F32)
    ti = lax.broadcasted_iota(jnp.int32, (seq, HYENA_WIDTH), 0)
    decay = jnp.exp(-(ti.astype(F32) * (1.0 / (seq - 1))) * dl_ref[...])
    hf = h[:, :HYENA_WIDTH] * decay
    hb = jnp.where(ti == 0, 0.0, h[:, HYENA_WIDTH:] * decay)
    ks = hf + hb
    k_ref[:, :HYENA_WIDTH] = ks.astype(BF16)
    k_ref[:, HYENA_WIDTH:] = (hf - hb).astype(BF16)
    nyq = jnp.sum(jnp.where(ti % 2 == 0, ks, -ks), axis=0, keepdims=True)
    nyq_ref[...] = jnp.broadcast_to(nyq, nyq_ref.shape)


def _hy_filter(lp, z, deltas, seq):
    w1 = jnp.pad(lp['hy_f1_w'], ((0, LANE - HYENA_EMB), (0, 0)))
    args = (z, w1, lp['hy_f1_b'].reshape(1, -1), lp['hy_f2_w'], lp['hy_f2_b'].reshape(1, -1), lp['hy_f3_w'],
            lp['hy_freq'].reshape(1, -1), deltas)
    full = lambda a: pl.BlockSpec(a.shape, lambda i: (0,) * a.ndim)
    return pl.pallas_call(
        functools.partial(_hy_filter_body, seq=seq),
        grid=(1,),
        in_specs=[full(a) for a in args],
        out_specs=[pl.BlockSpec((seq, 2 * HYENA_WIDTH), lambda i: (0, 0)),
                   pl.BlockSpec((8, HYENA_WIDTH), lambda i: (0, 0))],
        out_shape=[jax.ShapeDtypeStruct((seq, 2 * HYENA_WIDTH), BF16),
                   jax.ShapeDtypeStruct((8, HYENA_WIDTH), F32)],
        compiler_params=_cparams("arbitrary"),
        name="hy_filter",
    )(*args)


DFT_SPLIT = 64


def _dft_matrices(seq):
    n = 2 * seq
    f = jnp.arange(seq, dtype=jnp.int32)[:, None]

    def table(step, count):
        idx = (f * (jnp.arange(count, dtype=jnp.int32)[None, :] * step)) % n
        ang = idx.astype(F32) * (2.0 * math.pi / n)
        return jnp.cos(ang), jnp.sin(ang)

    hc, hs = table(DFT_SPLIT, seq // DFT_SPLIT)
    lc, ls = table(1, DFT_SPLIT)
    cos = (hc[:, :, None] * lc[:, None, :] - hs[:, :, None] * ls[:, None, :]).reshape(seq, seq)
    sin = (hs[:, :, None] * lc[:, None, :] + hc[:, :, None] * ls[:, None, :]).reshape(seq, seq)
    s = jnp.arange(seq, dtype=jnp.int32)[None, :]
    nyq = jnp.where(s % 2 == 0, 1.0, -1.0).astype(F32)
    wf = jnp.concatenate([cos, jnp.where(f == 0, nyq, -sin)], axis=0).astype(BF16)
    return wf, wf.T


def _mm_body(a_ref, b_ref, o_ref):
    o_ref[...] = _dot(a_ref[...], b_ref[...])


def _dft_fwd_body(wc_ref, ws_ref, x_ref, hr_ref, hi_ref, nyq_ref, pr_ref, pi_ref, *, n):
    x = x_ref[...]
    xr = _dot(wc_ref[...], x)
    xi = _dot(ws_ref[...], x)
    hr = hr_ref[...]
    hi = hi_ref[...]
    freq = lax.broadcasted_iota(jnp.int32, xr.shape, 0) + pl.program_id(1) * xr.shape[0]
    first = freq == 0
    w = jnp.where(first, 1.0 / n, 2.0 / n)
    pr_ref[...] = (jnp.where(first, xr * hr, xr * hr - xi * hi) * w).astype(BF16)
    pi_ref[...] = (jnp.where(first, xi * nyq_ref[0:1, :], xr * hi + xi * hr) * w).astype(BF16)


def _dft_fwd(wf, x, hf, nyq):
    n, seq = wf.shape
    tf = min(seq, 512)
    nf = seq // tf
    out = jax.ShapeDtypeStruct((seq, BATCH * HYENA_WIDTH), BF16)
    return pl.pallas_call(
        functools.partial(_dft_fwd_body, n=n),
        grid=(BATCH, nf),
        in_specs=[
            pl.BlockSpec((tf, seq), lambda b, f: (f, 0)),
            pl.BlockSpec((tf, seq), lambda b, f: (nf + f, 0)),
            pl.BlockSpec((seq, HYENA_WIDTH), lambda b, f: (0, b)),
            pl.BlockSpec((tf, HYENA_WIDTH), lambda b, f: (f, 0)),
            pl.BlockSpec((tf, HYENA_WIDTH), lambda b, f: (nf + f, 0)),
            pl.BlockSpec((8, HYENA_WIDTH), lambda b, f: (0, 0)),
        ],
        out_specs=[pl.BlockSpec((tf, HYENA_WIDTH), lambda b, f: (f, b))] * 2,
        out_shape=[out, out],
        compiler_params=_cparams("arbitrary", "arbitrary"),
        name="dft_fwd",
    )(wf, wf, x, hf, hf, nyq)


def _dft_filter(wf, k):
    n, seq = wf.shape
    tm = min(seq, 1024)
    return pl.pallas_call(
        _mm_body,
        grid=(n // tm,),
        in_specs=[
            pl.BlockSpec((tm, seq), lambda i: (i, 0)),
            pl.BlockSpec((seq, HYENA_WIDTH), lambda i: (0, (i * tm) // seq)),
        ],
        out_specs=pl.BlockSpec((tm, HYENA_WIDTH), lambda i: (i, 0)),
        out_shape=jax.ShapeDtypeStruct((n, HYENA_WIDTH), F32),
        compiler_params=_cparams("arbitrary"),
        name="dft_filter",
    )(wf, k)


def _dft_inv_body(wc_ref, ws_ref, pr_ref, pi_ref, vx_ref, bias_ref, x0_ref, o_ref):
    y = _dot(wc_ref[...], pr_ref[...]) + _dot(ws_ref[...], pi_ref[...])
    o_ref[...] = ((y + vx_ref[...] * bias_ref[...]) * x0_ref[...]).astype(BF16)


def _dft_inv(wi, p_re, p_im, vx, bias, x0, bb, row0):
    seq, n = wi.shape
    tm = min(seq, 512)
    tn = HYENA_WIDTH
    nt = seq // tm
    return pl.pallas_call(
        _keep_branch_buffer(_dft_inv_body, 7),
        grid=(BATCH, nt),
        in_specs=[
            pl.BlockSpec((tm, seq), lambda b, i: (i, 0)),
            pl.BlockSpec((tm, seq), lambda b, i: (i, 1)),
            pl.BlockSpec((seq, tn), lambda b, i: (0, b)),
            pl.BlockSpec((seq, tn), lambda b, i: (0, b)),
            pl.BlockSpec((tm, tn), lambda b, i: (b * nt + i, 0)),
            pl.BlockSpec((1, tn), lambda b, i: (0, 0)),
            pl.BlockSpec((tm, tn), lambda b, i: (b * nt + i, 0)),
            _ANY,
        ],
        out_specs=pl.BlockSpec((None, tm, tn), lambda b, i: (SLOT_HYENA, row0 // tm + b * nt + i, 0)),
        out_shape=jax.ShapeDtypeStruct(bb.shape, bb.dtype),
        input_output_aliases={7: 0},
        compiler_params=_cparams("arbitrary", "arbitrary"),
        name="dft_inv",
    )(wi, wi, p_re, p_im, vx, bias.reshape(1, tn), x0, bb)


def _hyena(proj, lp, consts, bb, seq, row0):
    z, deltas, (wf, wi) = consts
    x0, vx, vb = _hy_prep(proj, lp['hy_short_w'], lp['hy_short_b'], seq, row0 // seq)
    k, nyq = _hy_filter(lp, z, deltas, seq)
    hf = _dft_filter(wf, k)
    p_re, p_im = _dft_fwd(wf, vb, hf, nyq)
    return _dft_inv(wi, p_re, p_im, vx, lp['hy_bias'], x0, bb, row0)


def _s5_params(lp, n_seg_steps):
    a_re, a_im = lp['s5_a_re'], lp['s5_a_im']
    dt = jnp.exp(lp['s5_log_dt'])[..., None]
    mag = jnp.exp(a_re * dt)
    ab_re, ab_im = mag * jnp.cos(a_im * dt), mag * jnp.sin(a_im * dt)
    den = a_re * a_re + a_im * a_im
    nr, ni = ab_re - 1.0, ab_im
    cf_re = (nr * a_re + ni * a_im) / den
    cf_im = (ni * a_re - nr * a_im) / den
    b_re, b_im = lp['s5_b_re'], lp['s5_b_im']
    bb_re = cf_re[..., None] * b_re - cf_im[..., None] * b_im
    bb_im = cf_re[..., None] * b_im + cf_im[..., None] * b_re
    eye = jnp.eye(S5_CHUNK_GROUPS, dtype=F32)

    def bdiag_in(m):
        m = m.reshape(2, S5_NCHUNK, S5_CHUNK_GROUPS, S5_STATE, S5_GC)
        return jnp.einsum('dqgpc,gh->dqgchp', m, eye).reshape(2, S5_NCHUNK, S5_CHUNK_CH, S5_CHUNK_ST)

    def bdiag_out(m):
        m = m.reshape(2, S5_NCHUNK, S5_CHUNK_GROUPS, S5_GC, S5_STATE)
        return jnp.einsum('dqgcp,gh->dqhpgc', m, eye).reshape(2, S5_NCHUNK, S5_CHUNK_ST, S5_CHUNK_CH)

    bbd = jnp.concatenate([bdiag_in(bb_re), bdiag_in(bb_im)], axis=-1).astype(BF16)
    cbd = jnp.concatenate([bdiag_out(lp['s5_c_re']), -bdiag_out(lp['s5_c_im'])], axis=-2).astype(BF16)
    a = jnp.stack([ab_re.reshape(2, -1), ab_im.reshape(2, -1)], axis=1)
    pr, pi = ab_re, ab_im
    for _ in range(int(math.log2(n_seg_steps))):
        pr, pi = pr * pr - pi * pi, 2.0 * pr * pi
    ak = jnp.stack([pr.reshape(2, -1), pi.reshape(2, -1)], axis=1)
    return bbd, cbd, a, ak


def _s5_scan_body(u_ref, bbd_ref, cbd_ref, a_ref, ak_ref, dsk_ref, h0_ref, y_ref, hl_ref, xr_ref, xi_ref, up_ref,
                  *, seq):
    nk = seq // S5_LANES
    shape = (S5_LANES, S5_CHUNK_ST)
    for j in range(S5_LANES):
        up_ref[pl.ds(j, nk, stride=S5_LANES), :] = u_ref[pl.ds(j * nk, nk), :]
    u = up_ref[...]
    ub = u.astype(BF16)
    row = lax.broadcasted_iota(jnp.int32, shape, 0)
    y = jnp.zeros((seq, S5_CHUNK_CH), F32)
    for d in range(2):
        xr_ref[...] = _dot(ub, bbd_ref[d, :, :S5_CHUNK_ST])
        xi_ref[...] = _dot(ub, bbd_ref[d, :, S5_CHUNK_ST:])
        ar = jnp.broadcast_to(a_ref[d, 0:1, :], shape)
        ai = jnp.broadcast_to(a_ref[d, 1:2, :], shape)

        def tile(s, d=d):
            k = s if d == 0 else nk - 1 - s
            return pl.ds(pl.multiple_of(k * S5_LANES, S5_LANES), S5_LANES)

        def scan_step(s, carry, ar=ar, ai=ai, tile=tile):
            xr, xi = carry
            rows = tile(s)
            nr = ar * xr - ai * xi + xr_ref[rows, :]
            ni = ar * xi + ai * xr + xi_ref[rows, :]
            xr_ref[rows, :] = nr
            xi_ref[rows, :] = ni
            return nr, ni

        zero = jnp.zeros(shape, F32)
        er, ei = lax.fori_loop(0, nk, scan_step, (zero, zero), unroll=8)

        akr, aki = ak_ref[d, 0:1, :], ak_ref[d, 1:2, :]
        hr, hi = h0_ref[2 * d:2 * d + 1, :], h0_ref[2 * d + 1:2 * d + 2, :]
        in_r, in_i = zero, zero
        for j in (range(S5_LANES) if d == 0 else range(S5_LANES - 1, -1, -1)):
            in_r = jnp.where(row == j, hr, in_r)
            in_i = jnp.where(row == j, hi, in_i)
            hr, hi = (akr * hr - aki * hi + er[j:j + 1, :], akr * hi + aki * hr + ei[j:j + 1, :])
        hl_ref[2 * d:2 * d + 1, :] = hr
        hl_ref[2 * d + 1:2 * d + 2, :] = hi

        def fix_step(s, carry, ar=ar, ai=ai, tile=tile):
            gr, gi = carry
            gr, gi = ar * gr - ai * gi, ar * gi + ai * gr
            rows = tile(s)
            xr_ref[rows, :] += gr
            xi_ref[rows, :] += gi
            return gr, gi

        lax.fori_loop(0, nk, fix_step, (in_r, in_i), unroll=8)
        y = y + _dot(xr_ref[...].astype(BF16), cbd_ref[d, :S5_CHUNK_ST, :])
        y = y + _dot(xi_ref[...].astype(BF16), cbd_ref[d, S5_CHUNK_ST:, :])
        y = y + u * dsk_ref[d]
    up_ref[...] = y
    for j in range(S5_LANES):
        y_ref[pl.ds(j * nk, nk), :] = up_ref[pl.ds(j, nk, stride=S5_LANES), :]


def _s5_scan(proj, params, dskip, h0, seq, row_blk0):
    bbd, cbd, a, ak = params
    nq = S5_NCHUNK
    col0 = OFF_S5 // S5_CHUNK_CH
    return pl.pallas_call(
        functools.partial(_s5_scan_body, seq=seq),
        grid=(BATCH, nq),
        in_specs=[
            pl.BlockSpec((seq, S5_CHUNK_CH), lambda b, q: (row_blk0 + b, col0 + q)),
            pl.BlockSpec((2, None, S5_CHUNK_CH, 2 * S5_CHUNK_ST), lambda b, q: (0, q, 0, 0)),
            pl.BlockSpec((2, None, 2 * S5_CHUNK_ST, S5_CHUNK_CH), lambda b, q: (0, q, 0, 0)),
            pl.BlockSpec((2, 2, S5_CHUNK_ST), lambda b, q: (0, 0, q)),
            pl.BlockSpec((2, 2, S5_CHUNK_ST), lambda b, q: (0, 0, q)),
            pl.BlockSpec((2, 1, S5_CHUNK_CH), lambda b, q: (0, 0, q)),
            pl.BlockSpec((None, 4, S5_CHUNK_ST), lambda b, q: (b, 0, q)),
        ],
        out_specs=[
            pl.BlockSpec((seq, S5_CHUNK_CH), lambda b, q: (b, q)),
            pl.BlockSpec((None, 4, S5_CHUNK_ST), lambda b, q: (b, 0, q)),
        ],
        out_shape=[
            jax.ShapeDtypeStruct((BATCH * seq, S5_WIDTH), F32),
            jax.ShapeDtypeStruct((BATCH, 4, S5_GROUPS * S5_STATE), F32),
        ],
        scratch_shapes=[pltpu.VMEM((seq, S5_CHUNK_ST), F32), pltpu.VMEM((seq, S5_CHUNK_ST), F32),
                        pltpu.VMEM((seq, S5_CHUNK_CH), F32)],
        compiler_params=_cparams("arbitrary", "arbitrary"),
        name="s5_scan",
    )(proj, bbd, cbd, a, ak, dskip.reshape(2, 1, S5_WIDTH), h0)


def _s5_glu_body(y_ref, w_ref, b_ref, o_ref):
    g = _dot(jax.nn.gelu(y_ref[...]).astype(BF16), w_ref[...].astype(BF16)) + b_ref[...]
    o_ref[...] = (g[:, :S5_WIDTH] * jax.nn.sigmoid(g[:, S5_WIDTH:])).astype(BF16)


def _s5_glu(y, w, b, bb, row0):
    n_rows = y.shape[0]
    tm = TM_EW
    return pl.pallas_call(
        _keep_branch_buffer(_s5_glu_body, 3),
        grid=(n_rows // tm,),
        in_specs=[
            pl.BlockSpec((tm, S5_WIDTH), lambda i: (i, 0)),
            pl.BlockSpec((S5_WIDTH, 2 * S5_WIDTH), lambda i: (0, 0)),
            pl.BlockSpec((1, 2 * S5_WIDTH), lambda i: (0, 0)),
            _ANY,
        ],
        out_specs=pl.BlockSpec((None, tm, S5_WIDTH), lambda i: (SLOT_S5, row0 // tm + i, 0)),
        out_shape=jax.ShapeDtypeStruct(bb.shape, bb.dtype),
        input_output_aliases={3: 0},
        compiler_params=_cparams("arbitrary"),
        name="s5_glu",
    )(y, w, b.reshape(1, -1), bb)


def _merge_body(u_ref, *refs):
    wg_refs, bg_refs, y_refs, wb_refs = (refs[k * N_BRANCH:(k + 1) * N_BRANCH] for k in range(4))
    o_ref = refs[4 * N_BRANCH]
    u = u_ref[...]
    acc = None
    for n in range(N_BRANCH):
        gate = jax.nn.sigmoid(_dot(u, wg_refs[n][...].astype(BF16)) + bg_refs[n][...])
        contrib = gate * _dot(y_refs[n][...], wb_refs[n][...].astype(BF16))
        acc = contrib if acc is None else acc + contrib
    o_ref[...] = acc.astype(BF16)


def _merge(u, w_gate, b_gate, branches, w_branch, l, n_rows):
    tc = MXU_COLS
    ncol = D_MODEL // tc
    b_gate = b_gate.reshape(1, -1)
    per_branch = lambda make: [make(n) for n in range(N_BRANCH)]
    return pl.pallas_call(
        _merge_body,
        grid=(n_rows // TM, ncol),
        in_specs=[pl.BlockSpec((TM, D_MODEL), lambda i, c: (i, 0))]
        + per_branch(lambda n: pl.BlockSpec((None, D_MODEL, tc), lambda i, c: (l, 0, n * ncol + c)))
        + per_branch(lambda n: pl.BlockSpec((1, tc), lambda i, c: (0, n * ncol + c)))
        + per_branch(lambda n: pl.BlockSpec((None, TM, BRANCH_WIDTH), lambda i, c: (n, i, 0)))
        + per_branch(lambda n: pl.BlockSpec((None, None, BRANCH_WIDTH, tc), lambda i, c: (l, n, 0, c))),
        out_specs=pl.BlockSpec((TM, tc), lambda i, c: (i, c)),
        out_shape=jax.ShapeDtypeStruct((n_rows, D_MODEL), BF16),
        compiler_params=_cparams("arbitrary", "arbitrary"),
        name="merge",
    )(u, *([w_gate] * N_BRANCH), *([b_gate] * N_BRANCH), *([branches] * N_BRANCH), *([w_branch] * N_BRANCH))


def _out_body(m_ref, w_ref, x_ref, g_ref, o_ref):
    o_ref[...] = x_ref[...] + g_ref[...] * _dot(m_ref[...], w_ref[...].astype(BF16))


def _out_proj(merged, w_out, l, x, mods, n_rows):
    tn = 1024
    gate = pl.BlockSpec((None, None, 1, tn), lambda j, i: (_mod_row(i, TM), 5, 0, j))
    return pl.pallas_call(
        _out_body,
        grid=(D_MODEL // tn, n_rows // TM),
        in_specs=[
            pl.BlockSpec((TM, D_MODEL), lambda j, i: (i, 0)),
            pl.BlockSpec((None, D_MODEL, tn), lambda j, i: (l, 0, j)),
            pl.BlockSpec((TM, tn), lambda j, i: (i, j)),
            gate,
        ],
        out_specs=pl.BlockSpec((TM, tn), lambda j, i: (i, j)),
        out_shape=jax.ShapeDtypeStruct((n_rows, D_MODEL), F32),
        compiler_params=_cparams("arbitrary", "arbitrary"),
        name="out_proj",
    )(merged, w_out, x, mods)


def _mixer(x, u, lp, big, l, mods, consts, bb, last):
    cos, sin, hy_lat, hy_ctx = consts
    proj = _proj(u, big['w_in'], l)
    q, k, v = _qkv(proj, cos, sin, lp['q_norm'], lp['k_norm'])

    ctx_blk = N_LAT // CTX_LEN
    zero_h = jnp.zeros((BATCH, 4, S5_GROUPS * S5_STATE), F32)
    ys_ctx, h_ctx = _s5_scan(proj, _s5_params(lp, CTX_LEN // S5_LANES), lp['s5_d'], zero_h, CTX_LEN, ctx_blk)
    ys_lat, _ = _s5_scan(proj, _s5_params(lp, SEQ // S5_LANES), lp['s5_d'], h_ctx, SEQ, 0)

    bb = _s5_glu(ys_lat, lp['s5_glu_w'], lp['s5_glu_b'], bb, 0)
    bb = _attn_lat(q, k, v, bb)
    bb = _pool(proj, lp['pool_w'], lp['pool_scale'], bb, SEQ, 0)
    bb = _hyena(proj, lp, hy_lat, bb, SEQ, 0)
    if not last:
        bb = _s5_glu(ys_ctx, lp['s5_glu_w'], lp['s5_glu_b'], bb, N_LAT)
        bb = _attn_ctx(q, k, v, bb)
        bb = _pool(proj, lp['pool_w'], lp['pool_scale'], bb, CTX_LEN, ctx_blk)
        bb = _hyena(proj, lp, hy_ctx, bb, CTX_LEN, N_LAT)

    n_rows = N_LAT if last else MT
    merged = _merge(u, big['w_gate'], lp['b_gate'], bb, big['w_branch'], l, n_rows)
    return _out_proj(merged, big['w_out'], l, x, mods, n_rows), bb


def kernel(x, c, ctx, c_ctx, w_ada, b_ada, norm_ffn1, norm_mix, norm_ffn2, norm_final, ffn1_wi, ffn1_wo, ffn2_wi, ffn2_wo, w_in, w_gate, b_gate, w_branch, w_out, pool_w, pool_scale, q_norm, k_norm, hy_short_w, hy_short_b, hy_f1_w, hy_f1_b, hy_f2_w, hy_f2_b, hy_f3_w, hy_freq, hy_bias, s5_a_re, s5_a_im, s5_log_dt, s5_b_re, s5_b_im, s5_c_re, s5_c_im, s5_d, s5_glu_w, s5_glu_b):
    big = dict(w_in=w_in, w_gate=w_gate, w_branch=w_branch, w_out=w_out)
    per_layer = dict(
        b_gate=b_gate, pool_w=pool_w,
        pool_scale=pool_scale, q_norm=q_norm, k_norm=k_norm, hy_short_w=hy_short_w, hy_short_b=hy_short_b,
        hy_f1_w=hy_f1_w, hy_f1_b=hy_f1_b, hy_f2_w=hy_f2_w, hy_f2_b=hy_f2_b, hy_f3_w=hy_f3_w, hy_freq=hy_freq,
        hy_bias=hy_bias, s5_a_re=s5_a_re, s5_a_im=s5_a_im, s5_log_dt=s5_log_dt, s5_b_re=s5_b_re, s5_b_im=s5_b_im,
        s5_c_re=s5_c_re, s5_c_im=s5_c_im, s5_d=s5_d, s5_glu_w=s5_glu_w, s5_glu_b=s5_glu_b)

    cos, sin = _rope_tables()
    deltas = _hy_deltas()
    consts = (cos, sin,
              (_hy_feats(SEQ), deltas, _dft_matrices(SEQ)),
              (_hy_feats(CTX_LEN), deltas, _dft_matrices(CTX_LEN)))

    cc = jnp.concatenate([c, c_ctx[None], jnp.zeros((8 - BATCH - 1, D_MODEL), F32)], axis=0)
    mods_all = _ada(cc, w_ada, b_ada).reshape(DEPTH, 8, N_MOD, 1, D_MODEL)

    xs = jnp.concatenate([x.reshape(N_LAT, D_MODEL), ctx.reshape(N_CTX, D_MODEL)], axis=0)
    bb = jnp.zeros((N_BRANCH, MT, BRANCH_WIDTH), BF16)
    for l in range(DEPTH):
        last = l == DEPTH - 1
        lp = {name: w[l] for name, w in per_layer.items()}
        mods = mods_all[l]
        xs = _ffn(xs, mods, norm_ffn1[l], ffn1_wi, ffn1_wo, l, 0, MT)
        u = _normmod(xs, norm_mix[l], mods, 3, MT)
        xs, bb = _mixer(xs, u, lp, big, l, mods, consts, bb, last)
        xs = _ffn(xs, mods, norm_ffn2[l], ffn2_wi, ffn2_wo, l, 6, N_LAT if last else MT)
    return _final_norm(xs, norm_final, N_LAT).reshape(BATCH, SEQ, D_MODEL)
```

```python
import functools
import math

import jax
import jax.numpy as jnp
from jax import lax
from jax.experimental import pallas as pl
from jax.experimental.pallas import tpu as pltpu

F32 = jnp.float32
BF16 = jnp.bfloat16

D_MODEL = 2048
BATCH = 4
SEQ = 2048
DEPTH = 4
GRID_W = 64
CTX_LEN = 256
D_FF = 5632
N_MOD = 9
EPS = 1e-6

POOL_WINDOWS = (2, 4, 8, 16)
HEAD_DIM = 128
N_Q_HEADS = 4
N_KV_HEADS = 2
Q_GROUP = N_Q_HEADS // N_KV_HEADS
ROPE_THETA = 10000.0
ROPE_FREQS = HEAD_DIM // 4

HYENA_WIDTH = 512
HYENA_EMB = 33
HYENA_BANDS = (HYENA_EMB - 1) // 2
HYENA_HIDDEN = 64
HYENA_TARGET = 1e-2
HYENA_FAST_PCT = 0.3
HYENA_SLOW_PCT = 1.5

S5_WIDTH = 512
S5_GC = 16
S5_GROUPS = S5_WIDTH // S5_GC
S5_STATE = 64
S5_LANES = 8
S5_CHUNK_GROUPS = 8
S5_CHUNK_CH = S5_CHUNK_GROUPS * S5_GC
S5_CHUNK_ST = S5_CHUNK_GROUPS * S5_STATE
S5_NCHUNK = S5_GROUPS // S5_CHUNK_GROUPS

N_BRANCH = 4
BRANCH_WIDTH = 512
LANE = 128
COL = 512

OFF_Q = 512
OFF_KV = 1024
OFF_HY = 1536
OFF_S5 = 3072
IN_WIDTH = 3584

N_LAT = BATCH * SEQ
N_CTX = BATCH * CTX_LEN
MT = N_LAT + N_CTX

TM = 1024
TM_EW = 512
VMEM_LIMIT = 56 * 1024 * 1024


def _cparams(*sem):
    return pltpu.CompilerParams(dimension_semantics=sem, vmem_limit_bytes=VMEM_LIMIT)


def _dot(a, b):
    return jnp.dot(a, b, preferred_element_type=F32)


SLOT_POOL, SLOT_ATTN, SLOT_HYENA, SLOT_S5 = range(4)
_ANY = pl.BlockSpec(memory_space=pl.ANY)


def _keep_branch_buffer(body, n_in):
    def wrapped(*refs):
        return body(*refs[:n_in], *refs[n_in + 1:])
    return wrapped


def _mod_row(i, tm):
    return jnp.minimum((i * tm) // SEQ, BATCH)


def _mod_spec(tm, k):
    return pl.BlockSpec((None, None, 1, D_MODEL), lambda i, *_: (_mod_row(i, tm), k, 0, 0))


def _ada_body(c_ref, w_ref, b_ref, o_ref):
    c = c_ref[...]
    a = (c * jax.nn.sigmoid(c)).astype(BF16)
    o_ref[...] = _dot(a, w_ref[...].astype(BF16)) + b_ref[...]


def _ada(cc, w_ada, b_ada):
    tn = 1024
    nw = N_MOD * D_MODEL
    return pl.pallas_call(
        _ada_body,
        grid=(DEPTH, nw // tn),
        in_specs=[
            pl.BlockSpec((8, D_MODEL), lambda l, j: (0, 0)),
            pl.BlockSpec((None, D_MODEL, tn), lambda l, j: (l, 0, j)),
            pl.BlockSpec((None, 1, tn), lambda l, j: (l, 0, j)),
        ],
        out_specs=pl.BlockSpec((None, 8, tn), lambda l, j: (l, 0, j)),
        out_shape=jax.ShapeDtypeStruct((DEPTH, 8, nw), F32),
        compiler_params=_cparams("arbitrary", "arbitrary"),
        name="ada",
    )(cc, w_ada, b_ada.reshape(DEPTH, 1, nw))


NORM_ROWS = 16


def _rows_loop(n_rows, step):
    for c in range(n_rows // NORM_ROWS):
        step(pl.ds(c * NORM_ROWS, NORM_ROWS))


def _rms_scale_rows(x_ref, r_ref, scale, shift, o_ref):
    def stats(rows):
        x = x_ref[rows, :]
        r_ref[rows, :] = lax.rsqrt(jnp.mean(x * x, axis=-1, keepdims=True) + EPS)

    def apply(rows):
        y = x_ref[rows, :] * r_ref[rows, :] * scale
        if shift is not None:
            y = y + shift
        o_ref[rows, :] = y.astype(o_ref.dtype)

    _rows_loop(x_ref.shape[0], stats)
    _rows_loop(x_ref.shape[0], apply)


def _normmod_body(x_ref, g_ref, sh_ref, sc_ref, o_ref, r_ref):
    _rms_scale_rows(x_ref, r_ref, g_ref[...] * (1.0 + sc_ref[...]), sh_ref[...], o_ref)


def _norm_body(x_ref, g_ref, o_ref, r_ref):
    _rms_scale_rows(x_ref, r_ref, g_ref[...], None, o_ref)


def _normmod(x, g, mods, base, n_rows):
    tm = TM_EW
    return pl.pallas_call(
        _normmod_body,
        grid=(n_rows // tm,),
        in_specs=[
            pl.BlockSpec((tm, D_MODEL), lambda i: (i, 0)),
            pl.BlockSpec((1, D_MODEL), lambda i: (0, 0)),
            _mod_spec(tm, base),
            _mod_spec(tm, base + 1),
        ],
        out_specs=pl.BlockSpec((tm, D_MODEL), lambda i: (i, 0)),
        out_shape=jax.ShapeDtypeStruct((n_rows, D_MODEL), BF16),
        scratch_shapes=[pltpu.VMEM((tm, 1), F32)],
        compiler_params=_cparams("arbitrary"),
        name="normmod",
    )(x, g.reshape(1, D_MODEL), mods, mods)


def _final_norm(x, g, n_rows):
    tm = TM_EW
    return pl.pallas_call(
        _norm_body,
        grid=(n_rows // tm,),
        in_specs=[
            pl.BlockSpec((tm, D_MODEL), lambda i: (i, 0)),
            pl.BlockSpec((1, D_MODEL), lambda i: (0, 0)),
        ],
        out_specs=pl.BlockSpec((tm, D_MODEL), lambda i: (i, 0)),
        out_shape=jax.ShapeDtypeStruct((n_rows, D_MODEL), F32),
        scratch_shapes=[pltpu.VMEM((tm, 1), F32)],
        compiler_params=_cparams("arbitrary"),
        name="final_norm",
    )(x, g.reshape(1, D_MODEL))


MXU_COLS = 256


WO_SLAB = 64
WO_NSLAB = D_FF // WO_SLAB


def _ffn_a_body(u_ref, wa_ref, wb_ref, wo_ref, h_ref, wob_ref):
    u = u_ref[...]
    for c in range(0, h_ref.shape[1], MXU_COLS):
        sl = slice(c, c + MXU_COLS)
        a = _dot(u, wa_ref[:, sl].astype(BF16))
        b = _dot(u, wb_ref[:, sl].astype(BF16))
        h_ref[:, sl] = (a * jax.nn.sigmoid(a) * b).astype(BF16)

    @pl.when(pl.program_id(0) * pl.num_programs(1) + pl.program_id(1) < WO_NSLAB)
    def _():
        wob_ref[...] = wo_ref[...].astype(BF16)


def _ffn_a(u, wi, wo, l, n_rows):
    tf = 512
    nf = D_FF // tf
    assert (n_rows // TM) * nf >= WO_NSLAB

    def wo_blk(i, j):
        return jnp.minimum(i * nf + j, WO_NSLAB - 1)

    return pl.pallas_call(
        _ffn_a_body,
        grid=(n_rows // TM, nf),
        in_specs=[
            pl.BlockSpec((TM, D_MODEL), lambda i, j: (i, 0)),
            pl.BlockSpec((None, D_MODEL, tf), lambda i, j: (l, 0, j)),
            pl.BlockSpec((None, D_MODEL, tf), lambda i, j: (l, 0, j + nf)),
            pl.BlockSpec((None, WO_SLAB, D_MODEL), lambda i, j: (l, wo_blk(i, j), 0)),
        ],
        out_specs=[
            pl.BlockSpec((TM, tf), lambda i, j: (i, j)),
            pl.BlockSpec((WO_SLAB, D_MODEL), lambda i, j: (wo_blk(i, j), 0)),
        ],
        out_shape=[
            jax.ShapeDtypeStruct((n_rows, D_FF), BF16),
            jax.ShapeDtypeStruct((D_FF, D_MODEL), BF16),
        ],
        compiler_params=_cparams("arbitrary", "arbitrary"),
        name="ffn_a",
    )(u, wi, wi, wo)


def _ffn_b_body(h_ref, w_ref, x_ref, g_ref, o_ref):
    h = h_ref[...]
    for c in range(0, o_ref.shape[1], MXU_COLS):
        sl = slice(c, c + MXU_COLS)
        o_ref[:, sl] = x_ref[:, sl] + (0.5 * g_ref[:, sl]) * _dot(h, w_ref[:, sl])


def _ffn_b(h, wo_bf16, x, mods, gate_idx, n_rows):
    tn = 512
    gate = pl.BlockSpec((None, None, 1, tn), lambda i, j: (_mod_row(i, TM), gate_idx, 0, j))
    return pl.pallas_call(
        _ffn_b_body,
        grid=(n_rows // TM, D_MODEL // tn),
        in_specs=[
            pl.BlockSpec((TM, D_FF), lambda i, j: (i, 0)),
            pl.BlockSpec((D_FF, tn), lambda i, j: (0, j)),
            pl.BlockSpec((TM, tn), lambda i, j: (i, j)),
            gate,
        ],
        out_specs=pl.BlockSpec((TM, tn), lambda i, j: (i, j)),
        out_shape=jax.ShapeDtypeStruct((n_rows, D_MODEL), F32),
        compiler_params=_cparams("arbitrary", "arbitrary"),
        name="ffn_b",
    )(h, wo_bf16, x, mods)


def _ffn(x, mods, g, wi, wo, l, base, n_rows):
    u = _normmod(x, g, mods, base, n_rows)
    h, wo_bf16 = _ffn_a(u, wi, wo, l, n_rows)
    return _ffn_b(h, wo_bf16, x, mods, base + 2, n_rows)


def _proj_body(u_ref, w_ref, o_ref):
    o_ref[...] = _dot(u_ref[...], w_ref[...].astype(BF16))


def _proj(u, w_in, l):
    tn = COL
    tm = 1536
    return pl.pallas_call(
        _proj_body,
        grid=(MT // tm, IN_WIDTH // tn),
        in_specs=[
            pl.BlockSpec((tm, D_MODEL), lambda i, j: (i, 0)),
            pl.BlockSpec((None, D_MODEL, tn), lambda i, j: (l, 0, j)),
        ],
        out_specs=pl.BlockSpec((tm, tn), lambda i, j: (i, j)),
        out_shape=jax.ShapeDtypeStruct((MT, IN_WIDTH), F32),
        compiler_params=_cparams("arbitrary", "arbitrary"),
        name="proj",
    )(u, w_in)


def _rope_tables():
    t = jnp.arange(SEQ)
    rows = (t // GRID_W).astype(F32)
    cols = (t % GRID_W).astype(F32)
    freqs = ROPE_THETA ** (-jnp.arange(ROPE_FREQS, dtype=F32) / ROPE_FREQS)
    ar = rows[:, None] * freqs[None, :]
    ac = cols[:, None] * freqs[None, :]
    cos = jnp.concatenate([jnp.cos(ar), jnp.cos(ar), jnp.cos(ac), jnp.cos(ac)], axis=-1)
    sin = jnp.concatenate([-jnp.sin(ar), jnp.sin(ar), -jnp.sin(ac), jnp.sin(ac)], axis=-1)
    cos = jnp.concatenate([cos, jnp.ones((TM_EW, HEAD_DIM), F32)], axis=0)
    sin = jnp.concatenate([sin, jnp.zeros((TM_EW, HEAD_DIM), F32)], axis=0)
    return cos, sin


def _qkv_body(q_ref, kv_ref, cos_ref, sin_ref, qn_ref, kn_ref, qo_ref, ko_ref, vo_ref):
    cos = cos_ref[...]
    sin = sin_ref[...]
    lane = lax.broadcasted_iota(jnp.int32, cos.shape, 1)
    first = (lane % (2 * ROPE_FREQS)) < ROPE_FREQS

    def norm_rope(xh, g):
        y = xh * lax.rsqrt(jnp.mean(xh * xh, axis=-1, keepdims=True) + EPS) * g
        swapped = jnp.where(first, pltpu.roll(y, HEAD_DIM - ROPE_FREQS, 1), pltpu.roll(y, ROPE_FREQS, 1))
        return y * cos + swapped * sin

    scale = math.log2(math.e) / math.sqrt(HEAD_DIM)
    for h in range(N_Q_HEADS):
        sl = slice(h * HEAD_DIM, (h + 1) * HEAD_DIM)
        qo_ref[:, sl] = (norm_rope(q_ref[:, sl], qn_ref[...]) * scale).astype(BF16)
    for h in range(N_KV_HEADS):
        sl = slice(h * HEAD_DIM, (h + 1) * HEAD_DIM)
        ko_ref[:, sl] = norm_rope(kv_ref[:, sl], kn_ref[...]).astype(BF16)
    kvw = N_KV_HEADS * HEAD_DIM
    vo_ref[...] = kv_ref[:, kvw:].astype(BF16)


def _qkv(proj, cos, sin, q_norm, k_norm):
    tm = TM_EW
    n_lat_tiles = N_LAT // tm
    per_seq = SEQ // tm
    kvw = N_KV_HEADS * HEAD_DIM

    def tab(i):
        return (jnp.where(i < n_lat_tiles, i % per_seq, per_seq), 0)

    return pl.pallas_call(
        _qkv_body,
        grid=(MT // tm,),
        in_specs=[
            pl.BlockSpec((tm, COL), lambda i: (i, OFF_Q // COL)),
            pl.BlockSpec((tm, COL), lambda i: (i, OFF_KV // COL)),
            pl.BlockSpec((tm, HEAD_DIM), tab),
            pl.BlockSpec((tm, HEAD_DIM), tab),
            pl.BlockSpec((1, HEAD_DIM), lambda i: (0, 0)),
            pl.BlockSpec((1, HEAD_DIM), lambda i: (0, 0)),
        ],
        out_specs=[
            pl.BlockSpec((tm, COL), lambda i: (i, 0)),
            pl.BlockSpec((tm, kvw), lambda i: (i, 0)),
            pl.BlockSpec((tm, kvw), lambda i: (i, 0)),
        ],
        out_shape=[
            jax.ShapeDtypeStruct((MT, COL), BF16),
            jax.ShapeDtypeStruct((MT, kvw), BF16),
            jax.ShapeDtypeStruct((MT, kvw), BF16),
        ],
        compiler_params=_cparams("arbitrary"),
        name="qkv",
    )(proj, proj, cos, sin, q_norm.reshape(1, HEAD_DIM), k_norm.reshape(1, HEAD_DIM))


def _attn_body(*refs, with_lat):
    if with_lat:
        q_ref, kl_ref, vl_ref, kc_ref, vc_ref, o_ref = refs
    else:
        q_ref, kc_ref, vc_ref, o_ref = refs
    nt = (((1,), (1,)), ((), ()))
    for g in range(Q_GROUP):
        sl = slice(g * HEAD_DIM, (g + 1) * HEAD_DIM)
        q = q_ref[:, sl]
        sc = lax.dot_general(q, kc_ref[...], nt, preferred_element_type=F32)
        m = jnp.max(sc, axis=-1, keepdims=True)
        if with_lat:
            s_lat = lax.dot_general(q, kl_ref[...], nt, preferred_element_type=F32)
            m = jnp.maximum(m, jnp.max(s_lat, axis=-1, keepdims=True))
            e_lat = jnp.exp2(s_lat - m)
        ec = jnp.exp2(sc - m)
        den = jnp.sum(ec, axis=-1, keepdims=True)
        o = _dot(ec.astype(BF16), vc_ref[...])
        if with_lat:
            den = den + jnp.sum(e_lat, axis=-1, keepdims=True)
            o = o + _dot(e_lat.astype(BF16), vl_ref[...])
        o_ref[:, sl] = (o / den).astype(BF16)


def _attn_lat(q, k, v, bb):
    tq = 512
    nq = SEQ // tq
    ctx_blk = N_LAT // CTX_LEN
    gw = Q_GROUP * HEAD_DIM
    return pl.pallas_call(
        _keep_branch_buffer(functools.partial(_attn_body, with_lat=True), 5),
        grid=(BATCH, N_KV_HEADS, nq),
        in_specs=[
            pl.BlockSpec((tq, gw), lambda b, h, i: (b * nq + i, h)),
            pl.BlockSpec((SEQ, HEAD_DIM), lambda b, h, i: (b, h)),
            pl.BlockSpec((SEQ, HEAD_DIM), lambda b, h, i: (b, h)),
            pl.BlockSpec((CTX_LEN, HEAD_DIM), lambda b, h, i: (ctx_blk + b, h)),
            pl.BlockSpec((CTX_LEN, HEAD_DIM), lambda b, h, i: (ctx_blk + b, h)),
            _ANY,
        ],
        out_specs=pl.BlockSpec((None, tq, gw), lambda b, h, i: (SLOT_ATTN, b * nq + i, h)),
        out_shape=jax.ShapeDtypeStruct(bb.shape, bb.dtype),
        input_output_aliases={5: 0},
        compiler_params=_cparams("arbitrary", "arbitrary", "arbitrary"),
        name="attn_lat",
    )(q, k, v, k, v, bb)


def _attn_ctx(q, k, v, bb):
    ctx_blk = N_LAT // CTX_LEN
    gw = Q_GROUP * HEAD_DIM
    return pl.pallas_call(
        _keep_branch_buffer(functools.partial(_attn_body, with_lat=False), 3),
        grid=(BATCH, N_KV_HEADS),
        in_specs=[
            pl.BlockSpec((CTX_LEN, gw), lambda b, h: (ctx_blk + b, h)),
            pl.BlockSpec((CTX_LEN, HEAD_DIM), lambda b, h: (ctx_blk + b, h)),
            pl.BlockSpec((CTX_LEN, HEAD_DIM), lambda b, h: (ctx_blk + b, h)),
            _ANY,
        ],
        out_specs=pl.BlockSpec((None, CTX_LEN, gw), lambda b, h: (SLOT_ATTN, ctx_blk + b, h)),
        out_shape=jax.ShapeDtypeStruct(bb.shape, bb.dtype),
        input_output_aliases={3: 0},
        compiler_params=_cparams("arbitrary", "arbitrary"),
        name="attn_ctx",
    )(q, k, v, bb)


POOL_PAD = 8


def _pool_body(a_ref, w_ref, s_ref, o_ref, *, seq):
    lp = seq + 2 * POOL_PAD
    t = lax.broadcasted_iota(jnp.int32, (seq, LANE), 0)
    zpad = jnp.zeros((POOL_PAD, LANE), F32)
    for gi, win in enumerate(POOL_WINDOWS):
        sl = slice(gi * LANE, (gi + 1) * LANE)
        a = a_ref[:, sl]
        s = jnp.concatenate([zpad, a, zpad], axis=0)
        s = s + pltpu.roll(s, 1, 0)
        half = 1
        while 2 * half < win:
            s = pltpu.roll(s, half, 0) + pltpu.roll(s, lp - half, 0)
            half *= 2
        s = s[POOL_PAD:POOL_PAD + seq]
        lo = jnp.maximum(t - win // 2, 0)
        hi = jnp.minimum(t + win // 2, seq)
        pooled = s / (hi - lo).astype(F32) - a
        y = _dot(pooled.astype(BF16), w_ref[gi].astype(BF16))
        o_ref[:, sl] = (y * s_ref[:, sl]).astype(BF16)


def _pool(proj, pool_w, pool_scale, bb, seq, row_blk0):
    width = len(POOL_WINDOWS) * LANE
    return pl.pallas_call(
        _keep_branch_buffer(functools.partial(_pool_body, seq=seq), 3),
        grid=(BATCH,),
        in_specs=[
            pl.BlockSpec((seq, width), lambda b: (row_blk0 + b, 0)),
            pl.BlockSpec((len(POOL_WINDOWS), LANE, LANE), lambda b: (0, 0, 0)),
            pl.BlockSpec((1, width), lambda b: (0, 0)),
            _ANY,
        ],
        out_specs=pl.BlockSpec((None, seq, width), lambda b: (SLOT_POOL, row_blk0 + b, 0)),
        out_shape=jax.ShapeDtypeStruct(bb.shape, bb.dtype),
        input_output_aliases={3: 0},
        compiler_params=_cparams("arbitrary"),
        name="pool",
    )(proj, pool_w, pool_scale.reshape(1, width), bb)


def _hy_prep_body(x0_ref, x1_ref, v_ref, w0_ref, w1_ref, wv_ref, b0_ref, b1_ref, bv_ref,
                  x0o_ref, vx_ref, vb_ref, *, seq):
    t = lax.broadcasted_iota(jnp.int32, x0_ref.shape, 0)

    def conv(x_ref, w_ref, b_ref):
        x = x_ref[...]
        prev = jnp.where(t >= 1, pltpu.roll(x, 1, 0), 0.0)
        nxt = jnp.where(t <= seq - 2, pltpu.roll(x, seq - 1, 0), 0.0)
        return prev * w_ref[0:1, :] + x * w_ref[1:2, :] + nxt * w_ref[2:3, :] + b_ref[...]

    x0o_ref[...] = conv(x0_ref, w0_ref, b0_ref)
    vx = conv(v_ref, wv_ref, bv_ref) * conv(x1_ref, w1_ref, b1_ref)
    vx_ref[...] = vx
    vb_ref[...] = vx.astype(BF16)


def _hy_prep(proj, short_w, short_b, seq, row_blk0):
    tc = 2 * LANE
    nc = HYENA_WIDTH // tc
    c0 = OFF_HY // tc
    short_b = short_b.reshape(1, 3 * HYENA_WIDTH)

    def xspec(part):
        return pl.BlockSpec((seq, tc), lambda b, c: (row_blk0 + b, c0 + part * nc + c))

    def wspec(part, rows):
        return pl.BlockSpec((rows, tc), lambda b, c: (0, part * nc + c))

    return pl.pallas_call(
        functools.partial(_hy_prep_body, seq=seq),
        grid=(BATCH, nc),
        in_specs=[xspec(0), xspec(1), xspec(2), wspec(0, 3), wspec(1, 3), wspec(2, 3),
                  wspec(0, 1), wspec(1, 1), wspec(2, 1)],
        out_specs=[
            pl.BlockSpec((seq, tc), lambda b, c: (b, c)),
            pl.BlockSpec((seq, tc), lambda b, c: (b, c)),
            pl.BlockSpec((seq, tc), lambda b, c: (0, b * nc + c)),
        ],
        out_shape=[
            jax.ShapeDtypeStruct((BATCH * seq, HYENA_WIDTH), F32),
            jax.ShapeDtypeStruct((BATCH * seq, HYENA_WIDTH), F32),
            jax.ShapeDtypeStruct((seq, BATCH * HYENA_WIDTH), BF16),
        ],
        compiler_params=_cparams("arbitrary", "arbitrary"),
        name="hy_prep",
    )(proj, proj, proj, short_w, short_w, short_w, short_b, short_b, short_b)


def _hy_feats(seq):
    t = jnp.linspace(0.0, 1.0, seq, dtype=F32)[:, None]
    f = jnp.linspace(1e-4, HYENA_BANDS - 1, HYENA_BANDS, dtype=F32)
    w = 2.0 * math.pi * jnp.arange(seq, dtype=F32) / seq
    fw = w[:, None] * f[None, :]
    z = jnp.concatenate([t, jnp.cos(fw), -jnp.sin(fw)], axis=-1)
    return jnp.pad(z, ((0, 0), (0, LANE - HYENA_EMB)))


def _hy_deltas():
    max_decay = math.log(HYENA_TARGET) / HYENA_FAST_PCT
    min_decay = math.log(HYENA_TARGET) / HYENA_SLOW_PCT
    return jnp.abs(jnp.linspace(min_decay, max_decay, HYENA_WIDTH, dtype=F32)).reshape(1, HYENA_WIDTH)


def _hy_filter_body(z_ref, w1_ref, b1_ref, w2_ref, b2_ref, w3_ref, fr_ref, dl_ref, k_ref, nyq_ref, *, seq):
    hp = lax.Precision.HIGHEST
    freq = fr_ref[...]
    h = jnp.sin(freq * (jnp.dot(z_ref[...], w1_ref[...], precision=hp, preferred_element_type=F32) + b1_ref[...]))
    h = jnp.sin(freq * (jnp.dot(h, w2_ref[...], precision=hp, preferred_element_type=F32) + b2_ref[...]))
    h = jnp.dot(h, w3_ref[...], precision=hp, preferred_element_type=F32)
    ti = lax.broadcasted_iota(jnp.int32, (seq, HYENA_WIDTH), 0)
    decay = jnp.exp(-(ti.astype(F32) * (1.0 / (seq - 1))) * dl_ref[...])
    hf = h[:, :HYENA_WIDTH] * decay
    hb = jnp.where(ti == 0, 0.0, h[:, HYENA_WIDTH:] * decay)
    ks = hf + hb
    k_ref[:, :HYENA_WIDTH] = ks.astype(BF16)
    k_ref[:, HYENA_WIDTH:] = (hf - hb).astype(BF16)
    nyq = jnp.sum(jnp.where(ti % 2 == 0, ks, -ks), axis=0, keepdims=True)
    nyq_ref[...] = jnp.broadcast_to(nyq, nyq_ref.shape)


def _hy_filter(lp, z, deltas, seq):
    w1 = jnp.pad(lp['hy_f1_w'], ((0, LANE - HYENA_EMB), (0, 0)))
    args = (z, w1, lp['hy_f1_b'].reshape(1, -1), lp['hy_f2_w'], lp['hy_f2_b'].reshape(1, -1), lp['hy_f3_w'],
            lp['hy_freq'].reshape(1, -1), deltas)
    full = lambda a: pl.BlockSpec(a.shape, lambda i: (0,) * a.ndim)
    return pl.pallas_call(
        functools.partial(_hy_filter_body, seq=seq),
        grid=(1,),
        in_specs=[full(a) for a in args],
        out_specs=[pl.BlockSpec((seq, 2 * HYENA_WIDTH), lambda i: (0, 0)),
                   pl.BlockSpec((8, HYENA_WIDTH), lambda i: (0, 0))],
        out_shape=[jax.ShapeDtypeStruct((seq, 2 * HYENA_WIDTH), BF16),
                   jax.ShapeDtypeStruct((8, HYENA_WIDTH), F32)],
        compiler_params=_cparams("arbitrary"),
        name="hy_filter",
    )(*args)


DFT_SPLIT = 64


def _dft_matrices(seq):
    n = 2 * seq
    f = jnp.arange(seq, dtype=jnp.int32)[:, None]

    def table(step, count):
        idx = (f * (jnp.arange(count, dtype=jnp.int32)[None, :] * step)) % n
        ang = idx.astype(F32) * (2.0 * math.pi / n)
        return jnp.cos(ang), jnp.sin(ang)

    hc, hs = table(DFT_SPLIT, seq // DFT_SPLIT)
    lc, ls = table(1, DFT_SPLIT)
    cos = (hc[:, :, None] * lc[:, None, :] - hs[:, :, None] * ls[:, None, :]).reshape(seq, seq)
    sin = (hs[:, :, None] * lc[:, None, :] + hc[:, :, None] * ls[:, None, :]).reshape(seq, seq)
    s = jnp.arange(seq, dtype=jnp.int32)[None, :]
    nyq = jnp.where(s % 2 == 0, 1.0, -1.0).astype(F32)
    wf = jnp.concatenate([cos, jnp.where(f == 0, nyq, -sin)], axis=0).astype(BF16)
    return wf, wf.T


def _mm_body(a_ref, b_ref, o_ref):
    o_ref[...] = _dot(a_ref[...], b_ref[...])


def _dft_fwd_body(wc_ref, ws_ref, x_ref, hr_ref, hi_ref, nyq_ref, pr_ref, pi_ref, *, n):
    x = x_ref[...]
    xr = _dot(wc_ref[...], x)
    xi = _dot(ws_ref[...], x)
    hr = hr_ref[...]
    hi = hi_ref[...]
    freq = lax.broadcasted_iota(jnp.int32, xr.shape, 0) + pl.program_id(0) * xr.shape[0]
    first = freq == 0
    w = jnp.where(first, 1.0 / n, 2.0 / n)
    pr_ref[...] = (jnp.where(first, xr * hr, xr * hr - xi * hi) * w).astype(BF16)
    pi_ref[...] = (jnp.where(first, xi * nyq_ref[0:1, :], xr * hi + xi * hr) * w).astype(BF16)


def _dft_fwd(wf, x, hf, nyq):
    n, seq = wf.shape
    tf = min(seq, 512)
    nf = seq // tf
    out = jax.ShapeDtypeStruct((seq, BATCH * HYENA_WIDTH), BF16)
    return pl.pallas_call(
        functools.partial(_dft_fwd_body, n=n),
        grid=(nf, BATCH),
        in_specs=[
            pl.BlockSpec((tf, seq), lambda f, b: (f, 0)),
            pl.BlockSpec((tf, seq), lambda f, b: (nf + f, 0)),
            pl.BlockSpec((seq, HYENA_WIDTH), lambda f, b: (0, b)),
            pl.BlockSpec((tf, HYENA_WIDTH), lambda f, b: (f, 0)),
            pl.BlockSpec((tf, HYENA_WIDTH), lambda f, b: (nf + f, 0)),
            pl.BlockSpec((8, HYENA_WIDTH), lambda f, b: (0, 0)),
        ],
        out_specs=[pl.BlockSpec((tf, HYENA_WIDTH), lambda f, b: (f, b))] * 2,
        out_shape=[out, out],
        compiler_params=_cparams("arbitrary", "arbitrary"),
        name="dft_fwd",
    )(wf, wf, x, hf, hf, nyq)


def _dft_filter(wf, k):
    n, seq = wf.shape
    tm = min(seq, 1024)
    return pl.pallas_call(
        _mm_body,
        grid=(n // tm,),
        in_specs=[
            pl.BlockSpec((tm, seq), lambda i: (i, 0)),
            pl.BlockSpec((seq, HYENA_WIDTH), lambda i: (0, (i * tm) // seq)),
        ],
        out_specs=pl.BlockSpec((tm, HYENA_WIDTH), lambda i: (i, 0)),
        out_shape=jax.ShapeDtypeStruct((n, HYENA_WIDTH), F32),
        compiler_params=_cparams("arbitrary"),
        name="dft_filter",
    )(wf, k)


def _dft_inv_body(wc_ref, ws_ref, pr_ref, pi_ref, vx_ref, bias_ref, x0_ref, o_ref):
    y = _dot(wc_ref[...], pr_ref[...]) + _dot(ws_ref[...], pi_ref[...])
    o_ref[...] = ((y + vx_ref[...] * bias_ref[...]) * x0_ref[...]).astype(BF16)


def _dft_inv(wi, p_re, p_im, vx, bias, x0, bb, row0):
    seq, n = wi.shape
    tm = min(seq, 1024)
    tn = HYENA_WIDTH
    nt = seq // tm
    return pl.pallas_call(
        _keep_branch_buffer(_dft_inv_body, 7),
        grid=(nt, BATCH),
        in_specs=[
            pl.BlockSpec((tm, seq), lambda i, b: (i, 0)),
            pl.BlockSpec((tm, seq), lambda i, b: (i, 1)),
            pl.BlockSpec((seq, tn), lambda i, b: (0, b)),
            pl.BlockSpec((seq, tn), lambda i, b: (0, b)),
            pl.BlockSpec((tm, tn), lambda i, b: (b * nt + i, 0)),
            pl.BlockSpec((1, tn), lambda i, b: (0, 0)),
            pl.BlockSpec((tm, tn), lambda i, b: (b * nt + i, 0)),
            _ANY,
        ],
        out_specs=pl.BlockSpec((None, tm, tn), lambda i, b: (SLOT_HYENA, row0 // tm + b * nt + i, 0)),
        out_shape=jax.ShapeDtypeStruct(bb.shape, bb.dtype),
        input_output_aliases={7: 0},
        compiler_params=_cparams("arbitrary", "arbitrary"),
        name="dft_inv",
    )(wi, wi, p_re, p_im, vx, bias.reshape(1, tn), x0, bb)


def _hyena(proj, lp, consts, bb, seq, row0):
    z, deltas, (wf, wi) = consts
    x0, vx, vb = _hy_prep(proj, lp['hy_short_w'], lp['hy_short_b'], seq, row0 // seq)
    k, nyq = _hy_filter(lp, z, deltas, seq)
    hf = _dft_filter(wf, k)
    p_re, p_im = _dft_fwd(wf, vb, hf, nyq)
    return _dft_inv(wi, p_re, p_im, vx, lp['hy_bias'], x0, bb, row0)


def _s5_params(lp, n_seg_steps):
    a_re, a_im = lp['s5_a_re'], lp['s5_a_im']
    dt = jnp.exp(lp['s5_log_dt'])[..., None]
    mag = jnp.exp(a_re * dt)
    ab_re, ab_im = mag * jnp.cos(a_im * dt), mag * jnp.sin(a_im * dt)
    den = a_re * a_re + a_im * a_im
    nr, ni = ab_re - 1.0, ab_im
    cf_re = (nr * a_re + ni * a_im) / den
    cf_im = (ni * a_re - nr * a_im) / den
    b_re, b_im = lp['s5_b_re'], lp['s5_b_im']
    bb_re = cf_re[..., None] * b_re - cf_im[..., None] * b_im
    bb_im = cf_re[..., None] * b_im + cf_im[..., None] * b_re
    eye = jnp.eye(S5_CHUNK_GROUPS, dtype=F32)

    def bdiag_in(m):
        m = m.reshape(2, S5_NCHUNK, S5_CHUNK_GROUPS, S5_STATE, S5_GC)
        return jnp.einsum('dqgpc,gh->dqgchp', m, eye).reshape(2, S5_NCHUNK, S5_CHUNK_CH, S5_CHUNK_ST)

    def bdiag_out(m):
        m = m.reshape(2, S5_NCHUNK, S5_CHUNK_GROUPS, S5_GC, S5_STATE)
        return jnp.einsum('dqgcp,gh->dqhpgc', m, eye).reshape(2, S5_NCHUNK, S5_CHUNK_ST, S5_CHUNK_CH)

    bbd = jnp.concatenate([bdiag_in(bb_re), bdiag_in(bb_im)], axis=-1).astype(BF16)
    cbd = jnp.concatenate([bdiag_out(lp['s5_c_re']), -bdiag_out(lp['s5_c_im'])], axis=-2).astype(BF16)
    a = jnp.stack([ab_re.reshape(2, -1), ab_im.reshape(2, -1)], axis=1)
    aks = []
    for steps in n_seg_steps:
        pr, pi = ab_re, ab_im
        for _ in range(int(math.log2(steps))):
            pr, pi = pr * pr - pi * pi, 2.0 * pr * pi
        aks.append(jnp.stack([pr.reshape(2, -1), pi.reshape(2, -1)], axis=1))
    return [(bbd, cbd, a, ak) for ak in aks]


S5_BLOCK = 256


def _s5_scan_body(u_ref, bbd_ref, cbd_ref, a_ref, ak_ref, dsk_ref, h0_ref, y_ref, hl_ref,
                  xr0_ref, xr1_ref, xi0_ref, xi1_ref, yd0_ref, yd1_ref, up_ref, ub_ref, *, seq):
    nk = seq // S5_LANES
    nblk = seq // S5_BLOCK
    tiles = S5_BLOCK // S5_LANES
    shape = (S5_LANES, S5_CHUNK_ST)
    xr_ref, xi_ref, yd_ref = (xr0_ref, xr1_ref), (xi0_ref, xi1_ref), (yd0_ref, yd1_ref)
    for j in range(S5_LANES):
        up_ref[pl.ds(j, nk, stride=S5_LANES), :] = u_ref[pl.ds(j * nk, nk), :]
    ub_ref[...] = up_ref[...].astype(BF16)
    row = lax.broadcasted_iota(jnp.int32, shape, 0)
    zero = jnp.zeros(shape, F32)

    def block(r):
        return pl.ds(pl.multiple_of(r * S5_BLOCK, S5_BLOCK), S5_BLOCK)

    def in_proj(d, rows):
        xr_ref[d][rows, :] = _dot(ub_ref[rows, :], bbd_ref[d, :, :S5_CHUNK_ST])
        xi_ref[d][rows, :] = _dot(ub_ref[rows, :], bbd_ref[d, :, S5_CHUNK_ST:])

    def out_proj(d, rows):
        yd_ref[d][rows, :] = (_dot(xr_ref[d][rows, :].astype(BF16), cbd_ref[d, :S5_CHUNK_ST, :])
                              + _dot(xi_ref[d][rows, :].astype(BF16), cbd_ref[d, S5_CHUNK_ST:, :]))

    def coeffs(d):
        return jnp.broadcast_to(a_ref[d, 0:1, :], shape), jnp.broadcast_to(a_ref[d, 1:2, :], shape)

    def block_tiles(d, r):
        base = pl.multiple_of(r * S5_BLOCK, S5_BLOCK)
        order = range(tiles) if d == 0 else range(tiles - 1, -1, -1)
        return [pl.ds(base + t * S5_LANES, S5_LANES) for t in order]

    def scan_block(d, r, carry):
        ar, ai = coeffs(d)
        xr, xi = carry
        for rows in block_tiles(d, r):
            xr, xi = (ar * xr - ai * xi + xr_ref[d][rows, :], ar * xi + ai * xr + xi_ref[d][rows, :])
            xr_ref[d][rows, :] = xr
            xi_ref[d][rows, :] = xi
        return xr, xi

    def fix_block(d, r, carry):
        ar, ai = coeffs(d)
        gr, gi = carry
        for rows in block_tiles(d, r):
            gr, gi = ar * gr - ai * gi, ar * gi + ai * gr
            xr_ref[d][rows, :] += gr
            xi_ref[d][rows, :] += gi
        return gr, gi

    def visit(d, s):
        return s if d == 0 else nblk - 1 - s

    def entering_states(d, er, ei):
        akr, aki = ak_ref[d, 0:1, :], ak_ref[d, 1:2, :]
        hr, hi = h0_ref[2 * d:2 * d + 1, :], h0_ref[2 * d + 1:2 * d + 2, :]
        in_r, in_i = zero, zero
        for j in (range(S5_LANES) if d == 0 else range(S5_LANES - 1, -1, -1)):
            in_r = jnp.where(row == j, hr, in_r)
            in_i = jnp.where(row == j, hi, in_i)
            hr, hi = (akr * hr - aki * hi + er[j:j + 1, :], akr * hi + aki * hr + ei[j:j + 1, :])
        hl_ref[2 * d:2 * d + 1, :] = hr
        hl_ref[2 * d + 1:2 * d + 2, :] = hi
        return in_r, in_i

    def fix_all(d, ins):
        lax.fori_loop(0, nblk, lambda s, carry: fix_block(d, visit(d, s), carry), ins)

    in_proj(0, slice(None))

    def scan0_body(s, carry):
        in_proj(1, block(s))
        return scan_block(0, s, carry)
    ends0 = lax.fori_loop(0, nblk, scan0_body, (zero, zero))
    fix_all(0, entering_states(0, *ends0))

    def scan1_body(s, carry):
        out_proj(0, block(s))
        return scan_block(1, visit(1, s), carry)
    ends1 = lax.fori_loop(0, nblk, scan1_body, (zero, zero))
    fix_all(1, entering_states(1, *ends1))
    out_proj(1, slice(None))

    up_ref[...] = yd_ref[0][...] + yd_ref[1][...] + up_ref[...] * (dsk_ref[0] + dsk_ref[1])
    for j in range(S5_LANES):
        y_ref[pl.ds(j * nk, nk), :] = up_ref[pl.ds(j, nk, stride=S5_LANES), :]


def _s5_scan(proj, params, dskip, h0, seq, row_blk0):
    bbd, cbd, a, ak = params
    nq = S5_NCHUNK
    col0 = OFF_S5 // S5_CHUNK_CH
    return pl.pallas_call(
        functools.partial(_s5_scan_body, seq=seq),
        grid=(BATCH, nq),
        in_specs=[
            pl.BlockSpec((seq, S5_CHUNK_CH), lambda b, q: (row_blk0 + b, col0 + q)),
            pl.BlockSpec((2, None, S5_CHUNK_CH, 2 * S5_CHUNK_ST), lambda b, q: (0, q, 0, 0)),
            pl.BlockSpec((2, None, 2 * S5_CHUNK_ST, S5_CHUNK_CH), lambda b, q: (0, q, 0, 0)),
            pl.BlockSpec((2, 2, S5_CHUNK_ST), lambda b, q: (0, 0, q)),
            pl.BlockSpec((2, 2, S5_CHUNK_ST), lambda b, q: (0, 0, q)),
            pl.BlockSpec((2, 1, S5_CHUNK_CH), lambda b, q: (0, 0, q)),
            pl.BlockSpec((None, 4, S5_CHUNK_ST), lambda b, q: (b, 0, q)),
        ],
        out_specs=[
            pl.BlockSpec((seq, S5_CHUNK_CH), lambda b, q: (b, q)),
            pl.BlockSpec((None, 4, S5_CHUNK_ST), lambda b, q: (b, 0, q)),
        ],
        out_shape=[
            jax.ShapeDtypeStruct((BATCH * seq, S5_WIDTH), F32),
            jax.ShapeDtypeStruct((BATCH, 4, S5_GROUPS * S5_STATE), F32),
        ],
        scratch_shapes=[pltpu.VMEM((seq, S5_CHUNK_ST), F32)] * 4 + [pltpu.VMEM((seq, S5_CHUNK_CH), F32)] * 3
        + [pltpu.VMEM((seq, S5_CHUNK_CH), BF16)],
        compiler_params=_cparams("arbitrary", "arbitrary"),
        name="s5_scan",
    )(proj, bbd, cbd, a, ak, dskip.reshape(2, 1, S5_WIDTH), h0)


def _s5_glu_body(y_ref, w_ref, b_ref, o_ref):
    g = _dot(jax.nn.gelu(y_ref[...]).astype(BF16), w_ref[...].astype(BF16)) + b_ref[...]
    o_ref[...] = (g[:, :S5_WIDTH] * jax.nn.sigmoid(g[:, S5_WIDTH:])).astype(BF16)


def _s5_glu(y, w, b, bb, row0):
    n_rows = y.shape[0]
    tm = TM_EW
    return pl.pallas_call(
        _keep_branch_buffer(_s5_glu_body, 3),
        grid=(n_rows // tm,),
        in_specs=[
            pl.BlockSpec((tm, S5_WIDTH), lambda i: (i, 0)),
            pl.BlockSpec((S5_WIDTH, 2 * S5_WIDTH), lambda i: (0, 0)),
            pl.BlockSpec((1, 2 * S5_WIDTH), lambda i: (0, 0)),
            _ANY,
        ],
        out_specs=pl.BlockSpec((None, tm, S5_WIDTH), lambda i: (SLOT_S5, row0 // tm + i, 0)),
        out_shape=jax.ShapeDtypeStruct(bb.shape, bb.dtype),
        input_output_aliases={3: 0},
        compiler_params=_cparams("arbitrary"),
        name="s5_glu",
    )(y, w, b.reshape(1, -1), bb)


def _merge_body(u_ref, *refs):
    wg_refs, bg_refs, y_refs, wb_refs = (refs[k * N_BRANCH:(k + 1) * N_BRANCH] for k in range(4))
    o_ref = refs[4 * N_BRANCH]
    u = u_ref[...]
    acc = None
    for n in range(N_BRANCH):
        gate = jax.nn.sigmoid(_dot(u, wg_refs[n][...].astype(BF16)) + bg_refs[n][...])
        contrib = gate * _dot(y_refs[n][...], wb_refs[n][...].astype(BF16))
        acc = contrib if acc is None else acc + contrib
    o_ref[...] = acc.astype(BF16)


def _merge(u, w_gate, b_gate, branches, w_branch, l, n_rows):
    tc = MXU_COLS
    ncol = D_MODEL // tc
    b_gate = b_gate.reshape(1, -1)
    per_branch = lambda make: [make(n) for n in range(N_BRANCH)]
    return pl.pallas_call(
        _merge_body,
        grid=(n_rows // TM, ncol),
        in_specs=[pl.BlockSpec((TM, D_MODEL), lambda i, c: (i, 0))]
        + per_branch(lambda n: pl.BlockSpec((None, D_MODEL, tc), lambda i, c: (l, 0, n * ncol + c)))
        + per_branch(lambda n: pl.BlockSpec((1, tc), lambda i, c: (0, n * ncol + c)))
        + per_branch(lambda n: pl.BlockSpec((None, TM, BRANCH_WIDTH), lambda i, c: (n, i, 0)))
        + per_branch(lambda n: pl.BlockSpec((None, None, BRANCH_WIDTH, tc), lambda i, c: (l, n, 0, c))),
        out_specs=pl.BlockSpec((TM, tc), lambda i, c: (i, c)),
        out_shape=jax.ShapeDtypeStruct((n_rows, D_MODEL), BF16),
        compiler_params=_cparams("arbitrary", "arbitrary"),
        name="merge",
    )(u, *([w_gate] * N_BRANCH), *([b_gate] * N_BRANCH), *([branches] * N_BRANCH), *([w_branch] * N_BRANCH))


def _out_body(m_ref, w_ref, x_ref, g_ref, o_ref):
    o_ref[...] = x_ref[...] + g_ref[...] * _dot(m_ref[...], w_ref[...].astype(BF16))


def _out_proj(merged, w_out, l, x, mods, n_rows):
    tn = 1024
    gate = pl.BlockSpec((None, None, 1, tn), lambda j, i: (_mod_row(i, TM), 5, 0, j))
    return pl.pallas_call(
        _out_body,
        grid=(D_MODEL // tn, n_rows // TM),
        in_specs=[
            pl.BlockSpec((TM, D_MODEL), lambda j, i: (i, 0)),
            pl.BlockSpec((None, D_MODEL, tn), lambda j, i: (l, 0, j)),
            pl.BlockSpec((TM, tn), lambda j, i: (i, j)),
            gate,
        ],
        out_specs=pl.BlockSpec((TM, tn), lambda j, i: (i, j)),
        out_shape=jax.ShapeDtypeStruct((n_rows, D_MODEL), F32),
        compiler_params=_cparams("arbitrary", "arbitrary"),
        name="out_proj",
    )(merged, w_out, x, mods)


def _mixer(x, g, lp, big, l, mods, consts, bb, last):
    cos, sin, hy_lat, hy_ctx = consts
    u = _normmod(x, g, mods, 3, MT)
    proj = _proj(u, big['w_in'], l)
    q, k, v = _qkv(proj, cos, sin, lp['q_norm'], lp['k_norm'])

    ctx_blk = N_LAT // CTX_LEN
    zero_h = jnp.zeros((BATCH, 4, S5_GROUPS * S5_STATE), F32)
    par_ctx, par_lat = _s5_params(lp, (CTX_LEN // S5_LANES, SEQ // S5_LANES))
    ys_ctx, h_ctx = _s5_scan(proj, par_ctx, lp['s5_d'], zero_h, CTX_LEN, ctx_blk)
    ys_lat, _ = _s5_scan(proj, par_lat, lp['s5_d'], h_ctx, SEQ, 0)

    bb = _s5_glu(ys_lat, lp['s5_glu_w'], lp['s5_glu_b'], bb, 0)
    bb = _attn_lat(q, k, v, bb)
    bb = _pool(proj, lp['pool_w'], lp['pool_scale'], bb, SEQ, 0)
    bb = _hyena(proj, lp, hy_lat, bb, SEQ, 0)
    if not last:
        bb = _s5_glu(ys_ctx, lp['s5_glu_w'], lp['s5_glu_b'], bb, N_LAT)
        bb = _attn_ctx(q, k, v, bb)
        bb = _pool(proj, lp['pool_w'], lp['pool_scale'], bb, CTX_LEN, ctx_blk)
        bb = _hyena(proj, lp, hy_ctx, bb, CTX_LEN, N_LAT)

    n_rows = N_LAT if last else MT
    merged = _merge(u, big['w_gate'], lp['b_gate'], bb, big['w_branch'], l, n_rows)
    return _out_proj(merged, big['w_out'], l, x, mods, n_rows), bb


def kernel(x, c, ctx, c_ctx, w_ada, b_ada, norm_ffn1, norm_mix, norm_ffn2, norm_final, ffn1_wi, ffn1_wo, ffn2_wi, ffn2_wo, w_in, w_gate, b_gate, w_branch, w_out, pool_w, pool_scale, q_norm, k_norm, hy_short_w, hy_short_b, hy_f1_w, hy_f1_b, hy_f2_w, hy_f2_b, hy_f3_w, hy_freq, hy_bias, s5_a_re, s5_a_im, s5_log_dt, s5_b_re, s5_b_im, s5_c_re, s5_c_im, s5_d, s5_glu_w, s5_glu_b):
    big = dict(w_in=w_in, w_gate=w_gate, w_branch=w_branch, w_out=w_out)
    per_layer = dict(
        b_gate=b_gate, pool_w=pool_w,
        pool_scale=pool_scale, q_norm=q_norm, k_norm=k_norm, hy_short_w=hy_short_w, hy_short_b=hy_short_b,
        hy_f1_w=hy_f1_w, hy_f1_b=hy_f1_b, hy_f2_w=hy_f2_w, hy_f2_b=hy_f2_b, hy_f3_w=hy_f3_w, hy_freq=hy_freq,
        hy_bias=hy_bias, s5_a_re=s5_a_re, s5_a_im=s5_a_im, s5_log_dt=s5_log_dt, s5_b_re=s5_b_re, s5_b_im=s5_b_im,
        s5_c_re=s5_c_re, s5_c_im=s5_c_im, s5_d=s5_d, s5_glu_w=s5_glu_w, s5_glu_b=s5_glu_b)

    cos, sin = _rope_tables()
    deltas = _hy_deltas()
    consts = (cos, sin,
              (_hy_feats(SEQ), deltas, _dft_matrices(SEQ)),
              (_hy_feats(CTX_LEN), deltas, _dft_matrices(CTX_LEN)))

    cc = jnp.concatenate([c, c_ctx[None], jnp.zeros((8 - BATCH - 1, D_MODEL), F32)], axis=0)
    mods_all = _ada(cc, w_ada, b_ada).reshape(DEPTH, 8, N_MOD, 1, D_MODEL)

    xs = jnp.concatenate([x.reshape(N_LAT, D_MODEL), ctx.reshape(N_CTX, D_MODEL)], axis=0)
    bb = jnp.zeros((N_BRANCH, MT, BRANCH_WIDTH), BF16)
    for l in range(DEPTH):
        last = l == DEPTH - 1
        lp = {name: w[l] for name, w in per_layer.items()}
        mods = mods_all[l]
        xs = _ffn(xs, mods, norm_ffn1[l], ffn1_wi, ffn1_wo, l, 0, MT)
        xs, bb = _mixer(xs, norm_mix[l], lp, big, l, mods, consts, bb, last)
        xs = _ffn(xs, mods, norm_ffn2[l], ffn2_wi, ffn2_wo, l, 6, N_LAT if last else MT)
    return _final_norm(xs, norm_final, N_LAT).reshape(BATCH, SEQ, D_MODEL)
```

```python
import functools
import math

import jax
import jax.numpy as jnp
from jax import lax
from jax.experimental import pallas as pl
from jax.experimental.pallas import tpu as pltpu

F32 = jnp.float32
BF16 = jnp.bfloat16

D_MODEL = 2048
BATCH = 4
SEQ = 2048
DEPTH = 4
GRID_W = 64
CTX_LEN = 256
D_FF = 5632
N_MOD = 9
EPS = 1e-6

POOL_WINDOWS = (2, 4, 8, 16)
HEAD_DIM = 128
N_Q_HEADS = 4
N_KV_HEADS = 2
Q_GROUP = N_Q_HEADS // N_KV_HEADS
ROPE_THETA = 10000.0
ROPE_FREQS = HEAD_DIM // 4

HYENA_WIDTH = 512
HYENA_EMB = 33
HYENA_BANDS = (HYENA_EMB - 1) // 2
HYENA_HIDDEN = 64
HYENA_TARGET = 1e-2
HYENA_FAST_PCT = 0.3
HYENA_SLOW_PCT = 1.5

S5_WIDTH = 512
S5_GC = 16
S5_GROUPS = S5_WIDTH // S5_GC
S5_STATE = 64
S5_LANES = 8
S5_CHUNK_GROUPS = 8
S5_CHUNK_CH = S5_CHUNK_GROUPS * S5_GC
S5_CHUNK_ST = S5_CHUNK_GROUPS * S5_STATE
S5_NCHUNK = S5_GROUPS // S5_CHUNK_GROUPS

N_BRANCH = 4
BRANCH_WIDTH = 512
LANE = 128
COL = 512

OFF_Q = 512
OFF_KV = 1024
OFF_HY = 1536
OFF_S5 = 3072
IN_WIDTH = 3584

N_LAT = BATCH * SEQ
N_CTX = BATCH * CTX_LEN
MT = N_LAT + N_CTX

TM = 1024
TM_EW = 512
VMEM_LIMIT = 56 * 1024 * 1024


def _cparams(*sem):
    return pltpu.CompilerParams(dimension_semantics=sem, vmem_limit_bytes=VMEM_LIMIT)


def _dot(a, b):
    return jnp.dot(a, b, preferred_element_type=F32)


SLOT_POOL, SLOT_ATTN, SLOT_HYENA, SLOT_S5 = range(4)
_ANY = pl.BlockSpec(memory_space=pl.ANY)


def _keep_branch_buffer(body, n_in):
    def wrapped(*refs):
        return body(*refs[:n_in], *refs[n_in + 1:])
    return wrapped


def _mod_row(i, tm):
    return jnp.minimum((i * tm) // SEQ, BATCH)


def _mod_spec(tm, k):
    return pl.BlockSpec((None, None, 1, D_MODEL), lambda i, *_: (_mod_row(i, tm), k, 0, 0))


def _ada_body(c_ref, w_ref, b_ref, o_ref):
    c = c_ref[...]
    a = (c * jax.nn.sigmoid(c)).astype(BF16)
    o_ref[...] = _dot(a, w_ref[...].astype(BF16)) + b_ref[...]


def _ada(cc, w_ada, b_ada):
    tn = 1024
    nw = N_MOD * D_MODEL
    return pl.pallas_call(
        _ada_body,
        grid=(DEPTH, nw // tn),
        in_specs=[
            pl.BlockSpec((8, D_MODEL), lambda l, j: (0, 0)),
            pl.BlockSpec((None, D_MODEL, tn), lambda l, j: (l, 0, j)),
            pl.BlockSpec((None, 1, tn), lambda l, j: (l, 0, j)),
        ],
        out_specs=pl.BlockSpec((None, 8, tn), lambda l, j: (l, 0, j)),
        out_shape=jax.ShapeDtypeStruct((DEPTH, 8, nw), F32),
        compiler_params=_cparams("arbitrary", "arbitrary"),
        name="ada",
    )(cc, w_ada, b_ada.reshape(DEPTH, 1, nw))


NORM_ROWS = 16


def _rows_loop(n_rows, step):
    for c in range(n_rows // NORM_ROWS):
        step(pl.ds(c * NORM_ROWS, NORM_ROWS))


def _rms_scale_rows(x_ref, r_ref, scale, shift, o_ref):
    def stats(rows):
        x = x_ref[rows, :]
        r_ref[rows, :] = lax.rsqrt(jnp.mean(x * x, axis=-1, keepdims=True) + EPS)

    def apply(rows):
        y = x_ref[rows, :] * r_ref[rows, :] * scale
        if shift is not None:
            y = y + shift
        o_ref[rows, :] = y.astype(o_ref.dtype)

    _rows_loop(x_ref.shape[0], stats)
    _rows_loop(x_ref.shape[0], apply)


def _normmod_body(x_ref, g_ref, sh_ref, sc_ref, o_ref, r_ref):
    _rms_scale_rows(x_ref, r_ref, g_ref[...] * (1.0 + sc_ref[...]), sh_ref[...], o_ref)


def _norm_body(x_ref, g_ref, o_ref, r_ref):
    _rms_scale_rows(x_ref, r_ref, g_ref[...], None, o_ref)


def _normmod(x, g, mods, base, n_rows):
    tm = TM_EW
    return pl.pallas_call(
        _normmod_body,
        grid=(n_rows // tm,),
        in_specs=[
            pl.BlockSpec((tm, D_MODEL), lambda i: (i, 0)),
            pl.BlockSpec((1, D_MODEL), lambda i: (0, 0)),
            _mod_spec(tm, base),
            _mod_spec(tm, base + 1),
        ],
        out_specs=pl.BlockSpec((tm, D_MODEL), lambda i: (i, 0)),
        out_shape=jax.ShapeDtypeStruct((n_rows, D_MODEL), BF16),
        scratch_shapes=[pltpu.VMEM((tm, 1), F32)],
        compiler_params=_cparams("arbitrary"),
        name="normmod",
    )(x, g.reshape(1, D_MODEL), mods, mods)


def _final_norm(x, g, n_rows):
    tm = TM_EW
    return pl.pallas_call(
        _norm_body,
        grid=(n_rows // tm,),
        in_specs=[
            pl.BlockSpec((tm, D_MODEL), lambda i: (i, 0)),
            pl.BlockSpec((1, D_MODEL), lambda i: (0, 0)),
        ],
        out_specs=pl.BlockSpec((tm, D_MODEL), lambda i: (i, 0)),
        out_shape=jax.ShapeDtypeStruct((n_rows, D_MODEL), F32),
        scratch_shapes=[pltpu.VMEM((tm, 1), F32)],
        compiler_params=_cparams("arbitrary"),
        name="final_norm",
    )(x, g.reshape(1, D_MODEL))


MXU_COLS = 256


WO_SLAB = 64
WO_NSLAB = D_FF // WO_SLAB


NEXT_ROWS = 128
NEXT_SLICES = TM // NEXT_ROWS


def _ffn_a_body(u0_ref, xn_ref, g_ref, sh_ref, sc_ref, wa_ref, wb_ref, wo_ref, h_ref, wob_ref, u2_ref, r_ref):
    i, j = pl.program_id(0), pl.program_id(1)
    slot = i % 2

    @pl.when((i == 0) & (j == 0))
    def _():
        u2_ref[0] = u0_ref[...]

    dst = pl.ds(pl.multiple_of(jnp.minimum(j, NEXT_SLICES - 1) * NEXT_ROWS, NEXT_ROWS), NEXT_ROWS)
    _rms_scale_rows(xn_ref, r_ref, g_ref[...] * (1.0 + sc_ref[...]), sh_ref[...], u2_ref.at[1 - slot, dst, :])

    u = u2_ref[slot]
    for c in range(0, h_ref.shape[1], MXU_COLS):
        sl = slice(c, c + MXU_COLS)
        a = _dot(u, wa_ref[:, sl].astype(BF16))
        b = _dot(u, wb_ref[:, sl].astype(BF16))
        h_ref[:, sl] = (a * jax.nn.sigmoid(a) * b).astype(BF16)

    @pl.when(pl.program_id(0) * pl.num_programs(1) + pl.program_id(1) < WO_NSLAB)
    def _():
        wob_ref[...] = wo_ref[...].astype(BF16)


def _ffn_a(x, g, mods, base, wi, wo, l, n_rows):
    tf = 512
    nf = D_FF // tf
    ni = n_rows // TM
    assert ni * nf >= WO_NSLAB and nf >= NEXT_SLICES

    def wo_blk(i, j):
        return jnp.minimum(i * nf + j, WO_NSLAB - 1)

    def nxt(i):
        return jnp.minimum(i + 1, ni - 1)

    def next_mod(k):
        return pl.BlockSpec((None, None, 1, D_MODEL), lambda i, j: (_mod_row(nxt(i), TM), k, 0, 0))

    u0 = _normmod(x, g, mods, base, TM)
    return pl.pallas_call(
        _ffn_a_body,
        grid=(ni, nf),
        in_specs=[
            pl.BlockSpec((TM, D_MODEL), lambda i, j: (0, 0)),
            pl.BlockSpec((NEXT_ROWS, D_MODEL),
                         lambda i, j: (nxt(i) * NEXT_SLICES + jnp.minimum(j, NEXT_SLICES - 1), 0)),
            pl.BlockSpec((1, D_MODEL), lambda i, j: (0, 0)),
            next_mod(base),
            next_mod(base + 1),
            pl.BlockSpec((None, D_MODEL, tf), lambda i, j: (l, 0, j)),
            pl.BlockSpec((None, D_MODEL, tf), lambda i, j: (l, 0, j + nf)),
            pl.BlockSpec((None, WO_SLAB, D_MODEL), lambda i, j: (l, wo_blk(i, j), 0)),
        ],
        out_specs=[
            pl.BlockSpec((TM, tf), lambda i, j: (i, j)),
            pl.BlockSpec((WO_SLAB, D_MODEL), lambda i, j: (wo_blk(i, j), 0)),
        ],
        out_shape=[
            jax.ShapeDtypeStruct((n_rows, D_FF), BF16),
            jax.ShapeDtypeStruct((D_FF, D_MODEL), BF16),
        ],
        scratch_shapes=[pltpu.VMEM((2, TM, D_MODEL), BF16), pltpu.VMEM((NEXT_ROWS, 1), F32)],
        compiler_params=_cparams("arbitrary", "arbitrary"),
        name="ffn_a",
    )(u0, x, g.reshape(1, D_MODEL), mods, mods, wi, wi, wo)


def _ffn_b_body(h_ref, w_ref, x_ref, g_ref, o_ref):
    h = h_ref[...]
    for c in range(0, o_ref.shape[1], MXU_COLS):
        sl = slice(c, c + MXU_COLS)
        o_ref[:, sl] = x_ref[:, sl] + (0.5 * g_ref[:, sl]) * _dot(h, w_ref[:, sl])


def _ffn_b(h, wo_bf16, x, mods, gate_idx, n_rows):
    tn = 512
    gate = pl.BlockSpec((None, None, 1, tn), lambda i, j: (_mod_row(i, TM), gate_idx, 0, j))
    return pl.pallas_call(
        _ffn_b_body,
        grid=(n_rows // TM, D_MODEL // tn),
        in_specs=[
            pl.BlockSpec((TM, D_FF), lambda i, j: (i, 0)),
            pl.BlockSpec((D_FF, tn), lambda i, j: (0, j)),
            pl.BlockSpec((TM, tn), lambda i, j: (i, j)),
            gate,
        ],
        out_specs=pl.BlockSpec((TM, tn), lambda i, j: (i, j)),
        out_shape=jax.ShapeDtypeStruct((n_rows, D_MODEL), F32),
        compiler_params=_cparams("arbitrary", "arbitrary"),
        name="ffn_b",
    )(h, wo_bf16, x, mods)


def _ffn(x, mods, g, wi, wo, l, base, n_rows):
    h, wo_bf16 = _ffn_a(x, g, mods, base, wi, wo, l, n_rows)
    return _ffn_b(h, wo_bf16, x, mods, base + 2, n_rows)


def _proj_body(u_ref, w_ref, o_ref):
    o_ref[...] = _dot(u_ref[...], w_ref[...].astype(BF16))


def _proj(u, w_in, l):
    tn = COL
    tm = 1536
    return pl.pallas_call(
        _proj_body,
        grid=(MT // tm, IN_WIDTH // tn),
        in_specs=[
            pl.BlockSpec((tm, D_MODEL), lambda i, j: (i, 0)),
            pl.BlockSpec((None, D_MODEL, tn), lambda i, j: (l, 0, j)),
        ],
        out_specs=pl.BlockSpec((tm, tn), lambda i, j: (i, j)),
        out_shape=jax.ShapeDtypeStruct((MT, IN_WIDTH), F32),
        compiler_params=_cparams("arbitrary", "arbitrary"),
        name="proj",
    )(u, w_in)


def _rope_tables():
    t = jnp.arange(SEQ)
    rows = (t // GRID_W).astype(F32)
    cols = (t % GRID_W).astype(F32)
    freqs = ROPE_THETA ** (-jnp.arange(ROPE_FREQS, dtype=F32) / ROPE_FREQS)
    ar = rows[:, None] * freqs[None, :]
    ac = cols[:, None] * freqs[None, :]
    cos = jnp.concatenate([jnp.cos(ar), jnp.cos(ar), jnp.cos(ac), jnp.cos(ac)], axis=-1)
    sin = jnp.concatenate([-jnp.sin(ar), jnp.sin(ar), -jnp.sin(ac), jnp.sin(ac)], axis=-1)
    cos = jnp.concatenate([cos, jnp.ones((TM_EW, HEAD_DIM), F32)], axis=0)
    sin = jnp.concatenate([sin, jnp.zeros((TM_EW, HEAD_DIM), F32)], axis=0)
    return cos, sin


def _qkv_body(q_ref, kv_ref, cos_ref, sin_ref, qn_ref, kn_ref, qo_ref, ko_ref, vo_ref):
    cos = cos_ref[...]
    sin = sin_ref[...]
    lane = lax.broadcasted_iota(jnp.int32, cos.shape, 1)
    first = (lane % (2 * ROPE_FREQS)) < ROPE_FREQS

    def norm_rope(xh, g):
        y = xh * lax.rsqrt(jnp.mean(xh * xh, axis=-1, keepdims=True) + EPS) * g
        swapped = jnp.where(first, pltpu.roll(y, HEAD_DIM - ROPE_FREQS, 1), pltpu.roll(y, ROPE_FREQS, 1))
        return y * cos + swapped * sin

    scale = math.log2(math.e) / math.sqrt(HEAD_DIM)
    for h in range(N_Q_HEADS):
        sl = slice(h * HEAD_DIM, (h + 1) * HEAD_DIM)
        qo_ref[:, sl] = (norm_rope(q_ref[:, sl], qn_ref[...]) * scale).astype(BF16)
    for h in range(N_KV_HEADS):
        sl = slice(h * HEAD_DIM, (h + 1) * HEAD_DIM)
        ko_ref[:, sl] = norm_rope(kv_ref[:, sl], kn_ref[...]).astype(BF16)
    kvw = N_KV_HEADS * HEAD_DIM
    vo_ref[...] = kv_ref[:, kvw:].astype(BF16)


def _qkv(proj, cos, sin, q_norm, k_norm):
    tm = TM_EW
    n_lat_tiles = N_LAT // tm
    per_seq = SEQ // tm
    kvw = N_KV_HEADS * HEAD_DIM

    def tab(i):
        return (jnp.where(i < n_lat_tiles, i % per_seq, per_seq), 0)

    return pl.pallas_call(
        _qkv_body,
        grid=(MT // tm,),
        in_specs=[
            pl.BlockSpec((tm, COL), lambda i: (i, OFF_Q // COL)),
            pl.BlockSpec((tm, COL), lambda i: (i, OFF_KV // COL)),
            pl.BlockSpec((tm, HEAD_DIM), tab),
            pl.BlockSpec((tm, HEAD_DIM), tab),
            pl.BlockSpec((1, HEAD_DIM), lambda i: (0, 0)),
            pl.BlockSpec((1, HEAD_DIM), lambda i: (0, 0)),
        ],
        out_specs=[
            pl.BlockSpec((tm, COL), lambda i: (i, 0)),
            pl.BlockSpec((tm, kvw), lambda i: (i, 0)),
            pl.BlockSpec((tm, kvw), lambda i: (i, 0)),
        ],
        out_shape=[
            jax.ShapeDtypeStruct((MT, COL), BF16),
            jax.ShapeDtypeStruct((MT, kvw), BF16),
            jax.ShapeDtypeStruct((MT, kvw), BF16),
        ],
        compiler_params=_cparams("arbitrary"),
        name="qkv",
    )(proj, proj, cos, sin, q_norm.reshape(1, HEAD_DIM), k_norm.reshape(1, HEAD_DIM))


def _attn_body(*refs, with_lat):
    if with_lat:
        q_ref, kl_ref, vl_ref, kc_ref, vc_ref, o_ref = refs
    else:
        q_ref, kc_ref, vc_ref, o_ref = refs
    nt = (((1,), (1,)), ((), ()))
    for g in range(Q_GROUP):
        sl = slice(g * HEAD_DIM, (g + 1) * HEAD_DIM)
        q = q_ref[:, sl]
        sc = lax.dot_general(q, kc_ref[...], nt, preferred_element_type=F32)
        m = jnp.max(sc, axis=-1, keepdims=True)
        if with_lat:
            s_lat = lax.dot_general(q, kl_ref[...], nt, preferred_element_type=F32)
            m = jnp.maximum(m, jnp.max(s_lat, axis=-1, keepdims=True))
            e_lat = jnp.exp2(s_lat - m)
        ec = jnp.exp2(sc - m)
        den = jnp.sum(ec, axis=-1, keepdims=True)
        o = _dot(ec.astype(BF16), vc_ref[...])
        if with_lat:
            den = den + jnp.sum(e_lat, axis=-1, keepdims=True)
            o = o + _dot(e_lat.astype(BF16), vl_ref[...])
        o_ref[:, sl] = (o / den).astype(BF16)


def _attn_lat(q, k, v, bb):
    tq = 512
    nq = SEQ // tq
    ctx_blk = N_LAT // CTX_LEN
    gw = Q_GROUP * HEAD_DIM
    return pl.pallas_call(
        _keep_branch_buffer(functools.partial(_attn_body, with_lat=True), 5),
        grid=(BATCH, N_KV_HEADS, nq),
        in_specs=[
            pl.BlockSpec((tq, gw), lambda b, h, i: (b * nq + i, h)),
            pl.BlockSpec((SEQ, HEAD_DIM), lambda b, h, i: (b, h)),
            pl.BlockSpec((SEQ, HEAD_DIM), lambda b, h, i: (b, h)),
            pl.BlockSpec((CTX_LEN, HEAD_DIM), lambda b, h, i: (ctx_blk + b, h)),
            pl.BlockSpec((CTX_LEN, HEAD_DIM), lambda b, h, i: (ctx_blk + b, h)),
            _ANY,
        ],
        out_specs=pl.BlockSpec((None, tq, gw), lambda b, h, i: (SLOT_ATTN, b * nq + i, h)),
        out_shape=jax.ShapeDtypeStruct(bb.shape, bb.dtype),
        input_output_aliases={5: 0},
        compiler_params=_cparams("arbitrary", "arbitrary", "arbitrary"),
        name="attn_lat",
    )(q, k, v, k, v, bb)


def _attn_ctx(q, k, v, bb):
    ctx_blk = N_LAT // CTX_LEN
    gw = Q_GROUP * HEAD_DIM
    return pl.pallas_call(
        _keep_branch_buffer(functools.partial(_attn_body, with_lat=False), 3),
        grid=(BATCH, N_KV_HEADS),
        in_specs=[
            pl.BlockSpec((CTX_LEN, gw), lambda b, h: (ctx_blk + b, h)),
            pl.BlockSpec((CTX_LEN, HEAD_DIM), lambda b, h: (ctx_blk + b, h)),
            pl.BlockSpec((CTX_LEN, HEAD_DIM), lambda b, h: (ctx_blk + b, h)),
            _ANY,
        ],
        out_specs=pl.BlockSpec((None, CTX_LEN, gw), lambda b, h: (SLOT_ATTN, ctx_blk + b, h)),
        out_shape=jax.ShapeDtypeStruct(bb.shape, bb.dtype),
        input_output_aliases={3: 0},
        compiler_params=_cparams("arbitrary", "arbitrary"),
        name="attn_ctx",
    )(q, k, v, bb)


POOL_PAD = 8


def _pool_body(a_ref, w_ref, s_ref, o_ref, *, seq):
    lp = seq + 2 * POOL_PAD
    t = lax.broadcasted_iota(jnp.int32, (seq, LANE), 0)
    zpad = jnp.zeros((POOL_PAD, LANE), F32)
    for gi, win in enumerate(POOL_WINDOWS):
        sl = slice(gi * LANE, (gi + 1) * LANE)
        a = a_ref[:, sl]
        s = jnp.concatenate([zpad, a, zpad], axis=0)
        s = s + pltpu.roll(s, 1, 0)
        half = 1
        while 2 * half < win:
            s = pltpu.roll(s, half, 0) + pltpu.roll(s, lp - half, 0)
            half *= 2
        s = s[POOL_PAD:POOL_PAD + seq]
        lo = jnp.maximum(t - win // 2, 0)
        hi = jnp.minimum(t + win // 2, seq)
        pooled = s / (hi - lo).astype(F32) - a
        y = _dot(pooled.astype(BF16), w_ref[gi].astype(BF16))
        o_ref[:, sl] = (y * s_ref[:, sl]).astype(BF16)


def _pool(proj, pool_w, pool_scale, bb, seq, row_blk0):
    width = len(POOL_WINDOWS) * LANE
    return pl.pallas_call(
        _keep_branch_buffer(functools.partial(_pool_body, seq=seq), 3),
        grid=(BATCH,),
        in_specs=[
            pl.BlockSpec((seq, width), lambda b: (row_blk0 + b, 0)),
            pl.BlockSpec((len(POOL_WINDOWS), LANE, LANE), lambda b: (0, 0, 0)),
            pl.BlockSpec((1, width), lambda b: (0, 0)),
            _ANY,
        ],
        out_specs=pl.BlockSpec((None, seq, width), lambda b: (SLOT_POOL, row_blk0 + b, 0)),
        out_shape=jax.ShapeDtypeStruct(bb.shape, bb.dtype),
        input_output_aliases={3: 0},
        compiler_params=_cparams("arbitrary"),
        name="pool",
    )(proj, pool_w, pool_scale.reshape(1, width), bb)


def _hy_prep_body(x0_ref, x1_ref, v_ref, w0_ref, w1_ref, wv_ref, b0_ref, b1_ref, bv_ref,
                  x0o_ref, vx_ref, vb_ref, *, seq):
    t = lax.broadcasted_iota(jnp.int32, x0_ref.shape, 0)

    def conv(x_ref, w_ref, b_ref):
        x = x_ref[...]
        prev = jnp.where(t >= 1, pltpu.roll(x, 1, 0), 0.0)
        nxt = jnp.where(t <= seq - 2, pltpu.roll(x, seq - 1, 0), 0.0)
        return prev * w_ref[0:1, :] + x * w_ref[1:2, :] + nxt * w_ref[2:3, :] + b_ref[...]

    x0o_ref[...] = conv(x0_ref, w0_ref, b0_ref)
    vx = conv(v_ref, wv_ref, bv_ref) * conv(x1_ref, w1_ref, b1_ref)
    vx_ref[...] = vx
    vb_ref[...] = vx.astype(BF16)


def _hy_prep(proj, short_w, short_b, seq, row_blk0):
    tc = 2 * LANE
    nc = HYENA_WIDTH // tc
    c0 = OFF_HY // tc
    short_b = short_b.reshape(1, 3 * HYENA_WIDTH)

    def xspec(part):
        return pl.BlockSpec((seq, tc), lambda b, c: (row_blk0 + b, c0 + part * nc + c))

    def wspec(part, rows):
        return pl.BlockSpec((rows, tc), lambda b, c: (0, part * nc + c))

    return pl.pallas_call(
        functools.partial(_hy_prep_body, seq=seq),
        grid=(BATCH, nc),
        in_specs=[xspec(0), xspec(1), xspec(2), wspec(0, 3), wspec(1, 3), wspec(2, 3),
                  wspec(0, 1), wspec(1, 1), wspec(2, 1)],
        out_specs=[
            pl.BlockSpec((seq, tc), lambda b, c: (b, c)),
            pl.BlockSpec((seq, tc), lambda b, c: (b, c)),
            pl.BlockSpec((seq, tc), lambda b, c: (0, b * nc + c)),
        ],
        out_shape=[
            jax.ShapeDtypeStruct((BATCH * seq, HYENA_WIDTH), F32),
            jax.ShapeDtypeStruct((BATCH * seq, HYENA_WIDTH), F32),
            jax.ShapeDtypeStruct((seq, BATCH * HYENA_WIDTH), BF16),
        ],
        compiler_params=_cparams("arbitrary", "arbitrary"),
        name="hy_prep",
    )(proj, proj, proj, short_w, short_w, short_w, short_b, short_b, short_b)


def _hy_feats(seq):
    t = jnp.linspace(0.0, 1.0, seq, dtype=F32)[:, None]
    f = jnp.linspace(1e-4, HYENA_BANDS - 1, HYENA_BANDS, dtype=F32)
    w = 2.0 * math.pi * jnp.arange(seq, dtype=F32) / seq
    fw = w[:, None] * f[None, :]
    z = jnp.concatenate([t, jnp.cos(fw), -jnp.sin(fw)], axis=-1)
    return jnp.pad(z, ((0, 0), (0, LANE - HYENA_EMB)))


def _hy_deltas():
    max_decay = math.log(HYENA_TARGET) / HYENA_FAST_PCT
    min_decay = math.log(HYENA_TARGET) / HYENA_SLOW_PCT
    return jnp.abs(jnp.linspace(min_decay, max_decay, HYENA_WIDTH, dtype=F32)).reshape(1, HYENA_WIDTH)


def _hy_filter_body(z_ref, w1_ref, b1_ref, w2_ref, b2_ref, w3_ref, fr_ref, dl_ref, k_ref, nyq_ref, *, seq):
    hp = lax.Precision.HIGHEST
    freq = fr_ref[...]
    h = jnp.sin(freq * (jnp.dot(z_ref[...], w1_ref[...], precision=hp, preferred_element_type=F32) + b1_ref[...]))
    h = jnp.sin(freq * (jnp.dot(h, w2_ref[...], precision=hp, preferred_element_type=F32) + b2_ref[...]))
    h = jnp.dot(h, w3_ref[...], precision=hp, preferred_element_type=F32)
    ti = lax.broadcasted_iota(jnp.int32, (seq, HYENA_WIDTH), 0)
    decay = jnp.exp(-(ti.astype(F32) * (1.0 / (seq - 1))) * dl_ref[...])
    hf = h[:, :HYENA_WIDTH] * decay
    hb = jnp.where(ti == 0, 0.0, h[:, HYENA_WIDTH:] * decay)
    ks = hf + hb
    k_ref[:, :HYENA_WIDTH] = ks.astype(BF16)
    k_ref[:, HYENA_WIDTH:] = (hf - hb).astype(BF16)
    nyq = jnp.sum(jnp.where(ti % 2 == 0, ks, -ks), axis=0, keepdims=True)
    nyq_ref[...] = jnp.broadcast_to(nyq, nyq_ref.shape)


def _hy_filter(lp, z, deltas, seq):
    w1 = jnp.pad(lp['hy_f1_w'], ((0, LANE - HYENA_EMB), (0, 0)))
    args = (z, w1, lp['hy_f1_b'].reshape(1, -1), lp['hy_f2_w'], lp['hy_f2_b'].reshape(1, -1), lp['hy_f3_w'],
            lp['hy_freq'].reshape(1, -1), deltas)
    full = lambda a: pl.BlockSpec(a.shape, lambda i: (0,) * a.ndim)
    return pl.pallas_call(
        functools.partial(_hy_filter_body, seq=seq),
        grid=(1,),
        in_specs=[full(a) for a in args],
        out_specs=[pl.BlockSpec((seq, 2 * HYENA_WIDTH), lambda i: (0, 0)),
                   pl.BlockSpec((8, HYENA_WIDTH), lambda i: (0, 0))],
        out_shape=[jax.ShapeDtypeStruct((seq, 2 * HYENA_WIDTH), BF16),
                   jax.ShapeDtypeStruct((8, HYENA_WIDTH), F32)],
        compiler_params=_cparams("arbitrary"),
        name="hy_filter",
    )(*args)


DFT_SPLIT = 64


def _dft_matrices(seq):
    n = 2 * seq
    f = jnp.arange(seq, dtype=jnp.int32)[:, None]

    def table(step, count):
        idx = (f * (jnp.arange(count, dtype=jnp.int32)[None, :] * step)) % n
        ang = idx.astype(F32) * (2.0 * math.pi / n)
        return jnp.cos(ang), jnp.sin(ang)

    hc, hs = table(DFT_SPLIT, seq // DFT_SPLIT)
    lc, ls = table(1, DFT_SPLIT)
    cos = (hc[:, :, None] * lc[:, None, :] - hs[:, :, None] * ls[:, None, :]).reshape(seq, seq)
    sin = (hs[:, :, None] * lc[:, None, :] + hc[:, :, None] * ls[:, None, :]).reshape(seq, seq)
    s = jnp.arange(seq, dtype=jnp.int32)[None, :]
    nyq = jnp.where(s % 2 == 0, 1.0, -1.0).astype(F32)
    wf = jnp.concatenate([cos, jnp.where(f == 0, nyq, -sin)], axis=0).astype(BF16)
    return wf, wf.T


def _mm_body(a_ref, b_ref, o_ref):
    o_ref[...] = _dot(a_ref[...], b_ref[...])


def _dft_fwd_body(wc_ref, ws_ref, x_ref, hr_ref, hi_ref, nyq_ref, pr_ref, pi_ref, *, n):
    x = x_ref[...]
    xr = _dot(wc_ref[...], x)
    xi = _dot(ws_ref[...], x)
    hr = hr_ref[...]
    hi = hi_ref[...]
    freq = lax.broadcasted_iota(jnp.int32, xr.shape, 0) + pl.program_id(0) * xr.shape[0]
    first = freq == 0
    w = jnp.where(first, 1.0 / n, 2.0 / n)
    pr_ref[...] = (jnp.where(first, xr * hr, xr * hr - xi * hi) * w).astype(BF16)
    pi_ref[...] = (jnp.where(first, xi * nyq_ref[0:1, :], xr * hi + xi * hr) * w).astype(BF16)


def _dft_fwd(wf, x, hf, nyq):
    n, seq = wf.shape
    tf = min(seq, 512)
    nf = seq // tf
    out = jax.ShapeDtypeStruct((seq, BATCH * HYENA_WIDTH), BF16)
    return pl.pallas_call(
        functools.partial(_dft_fwd_body, n=n),
        grid=(nf, BATCH),
        in_specs=[
            pl.BlockSpec((tf, seq), lambda f, b: (f, 0)),
            pl.BlockSpec((tf, seq), lambda f, b: (nf + f, 0)),
            pl.BlockSpec((seq, HYENA_WIDTH), lambda f, b: (0, b)),
            pl.BlockSpec((tf, HYENA_WIDTH), lambda f, b: (f, 0)),
            pl.BlockSpec((tf, HYENA_WIDTH), lambda f, b: (nf + f, 0)),
            pl.BlockSpec((8, HYENA_WIDTH), lambda f, b: (0, 0)),
        ],
        out_specs=[pl.BlockSpec((tf, HYENA_WIDTH), lambda f, b: (f, b))] * 2,
        out_shape=[out, out],
        compiler_params=_cparams("arbitrary", "arbitrary"),
        name="dft_fwd",
    )(wf, wf, x, hf, hf, nyq)


def _dft_filter(wf, k):
    n, seq = wf.shape
    tm = min(seq, 1024)
    return pl.pallas_call(
        _mm_body,
        grid=(n // tm,),
        in_specs=[
            pl.BlockSpec((tm, seq), lambda i: (i, 0)),
            pl.BlockSpec((seq, HYENA_WIDTH), lambda i: (0, (i * tm) // seq)),
        ],
        out_specs=pl.BlockSpec((tm, HYENA_WIDTH), lambda i: (i, 0)),
        out_shape=jax.ShapeDtypeStruct((n, HYENA_WIDTH), F32),
        compiler_params=_cparams("arbitrary"),
        name="dft_filter",
    )(wf, k)


def _dft_inv_body(wc_ref, ws_ref, pr_ref, pi_ref, vx_ref, bias_ref, x0_ref, o_ref):
    y = _dot(wc_ref[...], pr_ref[...]) + _dot(ws_ref[...], pi_ref[...])
    o_ref[...] = ((y + vx_ref[...] * bias_ref[...]) * x0_ref[...]).astype(BF16)


def _dft_inv(wi, p_re, p_im, vx, bias, x0, bb, row0):
    seq, n = wi.shape
    tm = min(seq, 1024)
    tn = HYENA_WIDTH
    nt = seq // tm
    return pl.pallas_call(
        _keep_branch_buffer(_dft_inv_body, 7),
        grid=(nt, BATCH),
        in_specs=[
            pl.BlockSpec((tm, seq), lambda i, b: (i, 0)),
            pl.BlockSpec((tm, seq), lambda i, b: (i, 1)),
            pl.BlockSpec((seq, tn), lambda i, b: (0, b)),
            pl.BlockSpec((seq, tn), lambda i, b: (0, b)),
            pl.BlockSpec((tm, tn), lambda i, b: (b * nt + i, 0)),
            pl.BlockSpec((1, tn), lambda i, b: (0, 0)),
            pl.BlockSpec((tm, tn), lambda i, b: (b * nt + i, 0)),
            _ANY,
        ],
        out_specs=pl.BlockSpec((None, tm, tn), lambda i, b: (SLOT_HYENA, row0 // tm + b * nt + i, 0)),
        out_shape=jax.ShapeDtypeStruct(bb.shape, bb.dtype),
        input_output_aliases={7: 0},
        compiler_params=_cparams("arbitrary", "arbitrary"),
        name="dft_inv",
    )(wi, wi, p_re, p_im, vx, bias.reshape(1, tn), x0, bb)


def _hyena(proj, lp, consts, bb, seq, row0):
    z, deltas, (wf, wi) = consts
    x0, vx, vb = _hy_prep(proj, lp['hy_short_w'], lp['hy_short_b'], seq, row0 // seq)
    k, nyq = _hy_filter(lp, z, deltas, seq)
    hf = _dft_filter(wf, k)
    p_re, p_im = _dft_fwd(wf, vb, hf, nyq)
    return _dft_inv(wi, p_re, p_im, vx, lp['hy_bias'], x0, bb, row0)


def _s5_params(lp, n_seg_steps):
    a_re, a_im = lp['s5_a_re'], lp['s5_a_im']
    dt = jnp.exp(lp['s5_log_dt'])[..., None]
    mag = jnp.exp(a_re * dt)
    ab_re, ab_im = mag * jnp.cos(a_im * dt), mag * jnp.sin(a_im * dt)
    den = a_re * a_re + a_im * a_im
    nr, ni = ab_re - 1.0, ab_im
    cf_re = (nr * a_re + ni * a_im) / den
    cf_im = (ni * a_re - nr * a_im) / den
    b_re, b_im = lp['s5_b_re'], lp['s5_b_im']
    bb_re = cf_re[..., None] * b_re - cf_im[..., None] * b_im
    bb_im = cf_re[..., None] * b_im + cf_im[..., None] * b_re
    eye = jnp.eye(S5_CHUNK_GROUPS, dtype=F32)

    def bdiag_in(m):
        m = m.reshape(2, S5_NCHUNK, S5_CHUNK_GROUPS, S5_STATE, S5_GC)
        return jnp.einsum('dqgpc,gh->dqgchp', m, eye).reshape(2, S5_NCHUNK, S5_CHUNK_CH, S5_CHUNK_ST)

    def bdiag_out(m):
        m = m.reshape(2, S5_NCHUNK, S5_CHUNK_GROUPS, S5_GC, S5_STATE)
        return jnp.einsum('dqgcp,gh->dqhpgc', m, eye).reshape(2, S5_NCHUNK, S5_CHUNK_ST, S5_CHUNK_CH)

    bbd = jnp.concatenate([bdiag_in(bb_re), bdiag_in(bb_im)], axis=-1).astype(BF16)
    cbd = jnp.concatenate([bdiag_out(lp['s5_c_re']), -bdiag_out(lp['s5_c_im'])], axis=-2).astype(BF16)
    a = jnp.stack([ab_re.reshape(2, -1), ab_im.reshape(2, -1)], axis=1)
    aks = []
    for steps in n_seg_steps:
        pr, pi = ab_re, ab_im
        for _ in range(int(math.log2(steps))):
            pr, pi = pr * pr - pi * pi, 2.0 * pr * pi
        aks.append(jnp.stack([pr.reshape(2, -1), pi.reshape(2, -1)], axis=1))
    return [(bbd, cbd, a, ak) for ak in aks]


S5_BLOCK = 256


def _s5_scan_body(u_ref, bbd_ref, cbd_ref, a_ref, ak_ref, dsk_ref, h0_ref, y_ref, hl_ref,
                  xr0_ref, xr1_ref, xi0_ref, xi1_ref, yd0_ref, yd1_ref, up_ref, ub_ref, *, seq):
    nk = seq // S5_LANES
    nblk = seq // S5_BLOCK
    tiles = S5_BLOCK // S5_LANES
    shape = (S5_LANES, S5_CHUNK_ST)
    xr_ref, xi_ref, yd_ref = (xr0_ref, xr1_ref), (xi0_ref, xi1_ref), (yd0_ref, yd1_ref)
    for j in range(S5_LANES):
        up_ref[pl.ds(j, nk, stride=S5_LANES), :] = u_ref[pl.ds(j * nk, nk), :]
    ub_ref[...] = up_ref[...].astype(BF16)
    row = lax.broadcasted_iota(jnp.int32, shape, 0)
    zero = jnp.zeros(shape, F32)

    def block(r):
        return pl.ds(pl.multiple_of(r * S5_BLOCK, S5_BLOCK), S5_BLOCK)

    def in_proj(d, rows):
        xr_ref[d][rows, :] = _dot(ub_ref[rows, :], bbd_ref[d, :, :S5_CHUNK_ST])
        xi_ref[d][rows, :] = _dot(ub_ref[rows, :], bbd_ref[d, :, S5_CHUNK_ST:])

    def out_proj(d, rows):
        yd_ref[d][rows, :] = (_dot(xr_ref[d][rows, :].astype(BF16), cbd_ref[d, :S5_CHUNK_ST, :])
                              + _dot(xi_ref[d][rows, :].astype(BF16), cbd_ref[d, S5_CHUNK_ST:, :]))

    def coeffs(d):
        return jnp.broadcast_to(a_ref[d, 0:1, :], shape), jnp.broadcast_to(a_ref[d, 1:2, :], shape)

    def block_tiles(d, r):
        base = pl.multiple_of(r * S5_BLOCK, S5_BLOCK)
        order = range(tiles) if d == 0 else range(tiles - 1, -1, -1)
        return [pl.ds(base + t * S5_LANES, S5_LANES) for t in order]

    def scan_block(d, r, carry):
        ar, ai = coeffs(d)
        xr, xi = carry
        for rows in block_tiles(d, r):
            xr, xi = (ar * xr - ai * xi + xr_ref[d][rows, :], ar * xi + ai * xr + xi_ref[d][rows, :])
            xr_ref[d][rows, :] = xr
            xi_ref[d][rows, :] = xi
        return xr, xi

    def fix_block(d, r, carry):
        ar, ai = coeffs(d)
        gr, gi = carry
        for rows in block_tiles(d, r):
            gr, gi = ar * gr - ai * gi, ar * gi + ai * gr
            xr_ref[d][rows, :] += gr
            xi_ref[d][rows, :] += gi
        return gr, gi

    def visit(d, s):
        return s if d == 0 else nblk - 1 - s

    def entering_states(d, er, ei):
        akr, aki = ak_ref[d, 0:1, :], ak_ref[d, 1:2, :]
        hr, hi = h0_ref[2 * d:2 * d + 1, :], h0_ref[2 * d + 1:2 * d + 2, :]
        in_r, in_i = zero, zero
        for j in (range(S5_LANES) if d == 0 else range(S5_LANES - 1, -1, -1)):
            in_r = jnp.where(row == j, hr, in_r)
            in_i = jnp.where(row == j, hi, in_i)
            hr, hi = (akr * hr - aki * hi + er[j:j + 1, :], akr * hi + aki * hr + ei[j:j + 1, :])
        hl_ref[2 * d:2 * d + 1, :] = hr
        hl_ref[2 * d + 1:2 * d + 2, :] = hi
        return in_r, in_i

    def fix_all(d, ins):
        lax.fori_loop(0, nblk, lambda s, carry: fix_block(d, visit(d, s), carry), ins)

    in_proj(0, slice(None))

    def scan0_body(s, carry):
        in_proj(1, block(s))
        return scan_block(0, s, carry)
    ends0 = lax.fori_loop(0, nblk, scan0_body, (zero, zero))
    fix_all(0, entering_states(0, *ends0))

    def scan1_body(s, carry):
        out_proj(0, block(s))
        return scan_block(1, visit(1, s), carry)
    ends1 = lax.fori_loop(0, nblk, scan1_body, (zero, zero))
    fix_all(1, entering_states(1, *ends1))
    out_proj(1, slice(None))

    up_ref[...] = yd_ref[0][...] + yd_ref[1][...] + up_ref[...] * (dsk_ref[0] + dsk_ref[1])
    for j in range(S5_LANES):
        y_ref[pl.ds(j * nk, nk), :] = up_ref[pl.ds(j, nk, stride=S5_LANES), :]


def _s5_scan(proj, params, dskip, h0, seq, row_blk0):
    bbd, cbd, a, ak = params
    nq = S5_NCHUNK
    col0 = OFF_S5 // S5_CHUNK_CH
    return pl.pallas_call(
        functools.partial(_s5_scan_body, seq=seq),
        grid=(BATCH, nq),
        in_specs=[
            pl.BlockSpec((seq, S5_CHUNK_CH), lambda b, q: (row_blk0 + b, col0 + q)),
            pl.BlockSpec((2, None, S5_CHUNK_CH, 2 * S5_CHUNK_ST), lambda b, q: (0, q, 0, 0)),
            pl.BlockSpec((2, None, 2 * S5_CHUNK_ST, S5_CHUNK_CH), lambda b, q: (0, q, 0, 0)),
            pl.BlockSpec((2, 2, S5_CHUNK_ST), lambda b, q: (0, 0, q)),
            pl.BlockSpec((2, 2, S5_CHUNK_ST), lambda b, q: (0, 0, q)),
            pl.BlockSpec((2, 1, S5_CHUNK_CH), lambda b, q: (0, 0, q)),
            pl.BlockSpec((None, 4, S5_CHUNK_ST), lambda b, q: (b, 0, q)),
        ],
        out_specs=[
            pl.BlockSpec((seq, S5_CHUNK_CH), lambda b, q: (b, q)),
            pl.BlockSpec((None, 4, S5_CHUNK_ST), lambda b, q: (b, 0, q)),
        ],
        out_shape=[
            jax.ShapeDtypeStruct((BATCH * seq, S5_WIDTH), F32),
            jax.ShapeDtypeStruct((BATCH, 4, S5_GROUPS * S5_STATE), F32),
        ],
        scratch_shapes=[pltpu.VMEM((seq, S5_CHUNK_ST), F32)] * 4 + [pltpu.VMEM((seq, S5_CHUNK_CH), F32)] * 3
        + [pltpu.VMEM((seq, S5_CHUNK_CH), BF16)],
        compiler_params=_cparams("arbitrary", "arbitrary"),
        name="s5_scan",
    )(proj, bbd, cbd, a, ak, dskip.reshape(2, 1, S5_WIDTH), h0)


def _s5_glu_body(y_ref, w_ref, b_ref, o_ref):
    g = _dot(jax.nn.gelu(y_ref[...]).astype(BF16), w_ref[...].astype(BF16)) + b_ref[...]
    o_ref[...] = (g[:, :S5_WIDTH] * jax.nn.sigmoid(g[:, S5_WIDTH:])).astype(BF16)


def _s5_glu(y, w, b, bb, row0):
    n_rows = y.shape[0]
    tm = TM_EW
    return pl.pallas_call(
        _keep_branch_buffer(_s5_glu_body, 3),
        grid=(n_rows // tm,),
        in_specs=[
            pl.BlockSpec((tm, S5_WIDTH), lambda i: (i, 0)),
            pl.BlockSpec((S5_WIDTH, 2 * S5_WIDTH), lambda i: (0, 0)),
            pl.BlockSpec((1, 2 * S5_WIDTH), lambda i: (0, 0)),
            _ANY,
        ],
        out_specs=pl.BlockSpec((None, tm, S5_WIDTH), lambda i: (SLOT_S5, row0 // tm + i, 0)),
        out_shape=jax.ShapeDtypeStruct(bb.shape, bb.dtype),
        input_output_aliases={3: 0},
        compiler_params=_cparams("arbitrary"),
        name="s5_glu",
    )(y, w, b.reshape(1, -1), bb)


def _merge_body(u_ref, *refs):
    wg_refs, bg_refs, y_refs, wb_refs = (refs[k * N_BRANCH:(k + 1) * N_BRANCH] for k in range(4))
    o_ref = refs[4 * N_BRANCH]
    u = u_ref[...]
    acc = None
    for n in range(N_BRANCH):
        gate = jax.nn.sigmoid(_dot(u, wg_refs[n][...].astype(BF16)) + bg_refs[n][...])
        contrib = gate * _dot(y_refs[n][...], wb_refs[n][...].astype(BF16))
        acc = contrib if acc is None else acc + contrib
    o_ref[...] = acc.astype(BF16)


def _merge(u, w_gate, b_gate, branches, w_branch, l, n_rows):
    tc = MXU_COLS
    ncol = D_MODEL // tc
    b_gate = b_gate.reshape(1, -1)
    per_branch = lambda make: [make(n) for n in range(N_BRANCH)]
    return pl.pallas_call(
        _merge_body,
        grid=(n_rows // TM, ncol),
        in_specs=[pl.BlockSpec((TM, D_MODEL), lambda i, c: (i, 0))]
        + per_branch(lambda n: pl.BlockSpec((None, D_MODEL, tc), lambda i, c: (l, 0, n * ncol + c)))
        + per_branch(lambda n: pl.BlockSpec((1, tc), lambda i, c: (0, n * ncol + c)))
        + per_branch(lambda n: pl.BlockSpec((None, TM, BRANCH_WIDTH), lambda i, c: (n, i, 0)))
        + per_branch(lambda n: pl.BlockSpec((None, None, BRANCH_WIDTH, tc), lambda i, c: (l, n, 0, c))),
        out_specs=pl.BlockSpec((TM, tc), lambda i, c: (i, c)),
        out_shape=jax.ShapeDtypeStruct((n_rows, D_MODEL), BF16),
        compiler_params=_cparams("arbitrary", "arbitrary"),
        name="merge",
    )(u, *([w_gate] * N_BRANCH), *([b_gate] * N_BRANCH), *([branches] * N_BRANCH), *([w_branch] * N_BRANCH))


def _out_body(m_ref, w_ref, x_ref, g_ref, o_ref):
    o_ref[...] = x_ref[...] + g_ref[...] * _dot(m_ref[...], w_ref[...].astype(BF16))


def _out_proj(merged, w_out, l, x, mods, n_rows):
    tn = 1024
    gate = pl.BlockSpec((None, None, 1, tn), lambda j, i: (_mod_row(i, TM), 5, 0, j))
    return pl.pallas_call(
        _out_body,
        grid=(D_MODEL // tn, n_rows // TM),
        in_specs=[
            pl.BlockSpec((TM, D_MODEL), lambda j, i: (i, 0)),
            pl.BlockSpec((None, D_MODEL, tn), lambda j, i: (l, 0, j)),
            pl.BlockSpec((TM, tn), lambda j, i: (i, j)),
            gate,
        ],
        out_specs=pl.BlockSpec((TM, tn), lambda j, i: (i, j)),
        out_shape=jax.ShapeDtypeStruct((n_rows, D_MODEL), F32),
        compiler_params=_cparams("arbitrary", "arbitrary"),
        name="out_proj",
    )(merged, w_out, x, mods)


def _mixer(x, g, lp, big, l, mods, consts, bb, last):
    cos, sin, hy_lat, hy_ctx = consts
    u = _normmod(x, g, mods, 3, MT)
    proj = _proj(u, big['w_in'], l)
    q, k, v = _qkv(proj, cos, sin, lp['q_norm'], lp['k_norm'])

    ctx_blk = N_LAT // CTX_LEN
    zero_h = jnp.zeros((BATCH, 4, S5_GROUPS * S5_STATE), F32)
    par_ctx, par_lat = _s5_params(lp, (CTX_LEN // S5_LANES, SEQ // S5_LANES))
    ys_ctx, h_ctx = _s5_scan(proj, par_ctx, lp['s5_d'], zero_h, CTX_LEN, ctx_blk)
    ys_lat, _ = _s5_scan(proj, par_lat, lp['s5_d'], h_ctx, SEQ, 0)

    bb = _s5_glu(ys_lat, lp['s5_glu_w'], lp['s5_glu_b'], bb, 0)
    bb = _attn_lat(q, k, v, bb)
    bb = _pool(proj, lp['pool_w'], lp['pool_scale'], bb, SEQ, 0)
    bb = _hyena(proj, lp, hy_lat, bb, SEQ, 0)
    if not last:
        bb = _s5_glu(ys_ctx, lp['s5_glu_w'], lp['s5_glu_b'], bb, N_LAT)
        bb = _attn_ctx(q, k, v, bb)
        bb = _pool(proj, lp['pool_w'], lp['pool_scale'], bb, CTX_LEN, ctx_blk)
        bb = _hyena(proj, lp, hy_ctx, bb, CTX_LEN, N_LAT)

    n_rows = N_LAT if last else MT
    merged = _merge(u, big['w_gate'], lp['b_gate'], bb, big['w_branch'], l, n_rows)
    return _out_proj(merged, big['w_out'], l, x, mods, n_rows), bb


def kernel(x, c, ctx, c_ctx, w_ada, b_ada, norm_ffn1, norm_mix, norm_ffn2, norm_final, ffn1_wi, ffn1_wo, ffn2_wi, ffn2_wo, w_in, w_gate, b_gate, w_branch, w_out, pool_w, pool_scale, q_norm, k_norm, hy_short_w, hy_short_b, hy_f1_w, hy_f1_b, hy_f2_w, hy_f2_b, hy_f3_w, hy_freq, hy_bias, s5_a_re, s5_a_im, s5_log_dt, s5_b_re, s5_b_im, s5_c_re, s5_c_im, s5_d, s5_glu_w, s5_glu_b):
    big = dict(w_in=w_in, w_gate=w_gate, w_branch=w_branch, w_out=w_out)
    per_layer = dict(
        b_gate=b_gate, pool_w=pool_w,
        pool_scale=pool_scale, q_norm=q_norm, k_norm=k_norm, hy_short_w=hy_short_w, hy_short_b=hy_short_b,
        hy_f1_w=hy_f1_w, hy_f1_b=hy_f1_b, hy_f2_w=hy_f2_w, hy_f2_b=hy_f2_b, hy_f3_w=hy_f3_w, hy_freq=hy_freq,
        hy_bias=hy_bias, s5_a_re=s5_a_re, s5_a_im=s5_a_im, s5_log_dt=s5_log_dt, s5_b_re=s5_b_re, s5_b_im=s5_b_im,
        s5_c_re=s5_c_re, s5_c_im=s5_c_im, s5_d=s5_d, s5_glu_w=s5_glu_w, s5_glu_b=s5_glu_b)

    cos, sin = _rope_tables()
    deltas = _hy_deltas()
    consts = (cos, sin,
              (_hy_feats(SEQ), deltas, _dft_matrices(SEQ)),
              (_hy_feats(CTX_LEN), deltas, _dft_matrices(CTX_LEN)))

    cc = jnp.concatenate([c, c_ctx[None], jnp.zeros((8 - BATCH - 1, D_MODEL), F32)], axis=0)
    mods_all = _ada(cc, w_ada, b_ada).reshape(DEPTH, 8, N_MOD, 1, D_MODEL)

    xs = jnp.concatenate([x.reshape(N_LAT, D_MODEL), ctx.reshape(N_CTX, D_MODEL)], axis=0)
    bb = jnp.zeros((N_BRANCH, MT, BRANCH_WIDTH), BF16)
    for l in range(DEPTH):
        last = l == DEPTH - 1
        lp = {name: w[l] for name, w in per_layer.items()}
        mods = mods_all[l]
        xs = _ffn(xs, mods, norm_ffn1[l], ffn1_wi, ffn1_wo, l, 0, MT)
        xs, bb = _mixer(xs, norm_mix[l], lp, big, l, mods, consts, bb, last)
        xs = _ffn(xs, mods, norm_ffn2[l], ffn2_wi, ffn2_wo, l, 6, N_LAT if last else MT)
    return _final_norm(xs, norm_final, N_LAT).reshape(BATCH, SEQ, D_MODEL)
```

```python
import functools
import math

import jax
import jax.numpy as jnp
from jax import lax
from jax.experimental import pallas as pl
from jax.experimental.pallas import tpu as pltpu

F32 = jnp.float32
BF16 = jnp.bfloat16

D_MODEL = 2048
BATCH = 4
SEQ = 2048
DEPTH = 4
GRID_W = 64
CTX_LEN = 256
D_FF = 5632
N_MOD = 9
EPS = 1e-6

POOL_WINDOWS = (2, 4, 8, 16)
HEAD_DIM = 128
N_Q_HEADS = 4
N_KV_HEADS = 2
Q_GROUP = N_Q_HEADS // N_KV_HEADS
ROPE_THETA = 10000.0
ROPE_FREQS = HEAD_DIM // 4

HYENA_WIDTH = 512
HYENA_EMB = 33
HYENA_BANDS = (HYENA_EMB - 1) // 2
HYENA_HIDDEN = 64
HYENA_TARGET = 1e-2
HYENA_FAST_PCT = 0.3
HYENA_SLOW_PCT = 1.5

S5_WIDTH = 512
S5_GC = 16
S5_GROUPS = S5_WIDTH // S5_GC
S5_STATE = 64
S5_LANES = 8
S5_CHUNK_GROUPS = 8
S5_CHUNK_CH = S5_CHUNK_GROUPS * S5_GC
S5_CHUNK_ST = S5_CHUNK_GROUPS * S5_STATE
S5_NCHUNK = S5_GROUPS // S5_CHUNK_GROUPS

N_BRANCH = 4
BRANCH_WIDTH = 512
LANE = 128
COL = 512

OFF_Q = 512
OFF_KV = 1024
OFF_HY = 1536
OFF_S5 = 3072
IN_WIDTH = 3584

N_LAT = BATCH * SEQ
N_CTX = BATCH * CTX_LEN
MT = N_LAT + N_CTX

TM = 1024
TM_EW = 512
VMEM_LIMIT = 56 * 1024 * 1024


def _cparams(*sem):
    return pltpu.CompilerParams(dimension_semantics=sem, vmem_limit_bytes=VMEM_LIMIT)


def _dot(a, b):
    return jnp.dot(a, b, preferred_element_type=F32)


SLOT_POOL, SLOT_ATTN, SLOT_HYENA, SLOT_S5 = range(4)
_ANY = pl.BlockSpec(memory_space=pl.ANY)


def _keep_branch_buffer(body, n_in):
    def wrapped(*refs):
        return body(*refs[:n_in], *refs[n_in + 1:])
    return wrapped


def _mod_row(i, tm):
    return jnp.minimum((i * tm) // SEQ, BATCH)


def _mod_spec(tm, k):
    return pl.BlockSpec((None, None, 1, D_MODEL), lambda i, *_: (_mod_row(i, tm), k, 0, 0))


def _ada_body(c_ref, w_ref, b_ref, o_ref):
    c = c_ref[...]
    a = (c * jax.nn.sigmoid(c)).astype(BF16)
    o_ref[...] = _dot(a, w_ref[...].astype(BF16)) + b_ref[...]


def _ada(cc, w_ada, b_ada):
    tn = 1024
    nw = N_MOD * D_MODEL
    return pl.pallas_call(
        _ada_body,
        grid=(DEPTH, nw // tn),
        in_specs=[
            pl.BlockSpec((8, D_MODEL), lambda l, j: (0, 0)),
            pl.BlockSpec((None, D_MODEL, tn), lambda l, j: (l, 0, j)),
            pl.BlockSpec((None, 1, tn), lambda l, j: (l, 0, j)),
        ],
        out_specs=pl.BlockSpec((None, 8, tn), lambda l, j: (l, 0, j)),
        out_shape=jax.ShapeDtypeStruct((DEPTH, 8, nw), F32),
        compiler_params=_cparams("arbitrary", "arbitrary"),
        name="ada",
    )(cc, w_ada, b_ada.reshape(DEPTH, 1, nw))


NORM_ROWS = 16


def _rows_loop(n_rows, step):
    for c in range(n_rows // NORM_ROWS):
        step(pl.ds(c * NORM_ROWS, NORM_ROWS))


def _rms_scale_rows(x_ref, r_ref, scale, shift, o_ref):
    def stats(rows):
        x = x_ref[rows, :]
        r_ref[rows, :] = lax.rsqrt(jnp.mean(x * x, axis=-1, keepdims=True) + EPS)

    def apply(rows):
        y = x_ref[rows, :] * r_ref[rows, :] * scale
        if shift is not None:
            y = y + shift
        o_ref[rows, :] = y.astype(o_ref.dtype)

    _rows_loop(x_ref.shape[0], stats)
    _rows_loop(x_ref.shape[0], apply)


def _normmod_body(x_ref, g_ref, sh_ref, sc_ref, o_ref, r_ref):
    _rms_scale_rows(x_ref, r_ref, g_ref[...] * (1.0 + sc_ref[...]), sh_ref[...], o_ref)


def _norm_body(x_ref, g_ref, o_ref, r_ref):
    _rms_scale_rows(x_ref, r_ref, g_ref[...], None, o_ref)


def _normmod(x, g, mods, base, n_rows):
    tm = TM_EW
    return pl.pallas_call(
        _normmod_body,
        grid=(n_rows // tm,),
        in_specs=[
            pl.BlockSpec((tm, D_MODEL), lambda i: (i, 0)),
            pl.BlockSpec((1, D_MODEL), lambda i: (0, 0)),
            _mod_spec(tm, base),
            _mod_spec(tm, base + 1),
        ],
        out_specs=pl.BlockSpec((tm, D_MODEL), lambda i: (i, 0)),
        out_shape=jax.ShapeDtypeStruct((n_rows, D_MODEL), BF16),
        scratch_shapes=[pltpu.VMEM((tm, 1), F32)],
        compiler_params=_cparams("arbitrary"),
        name="normmod",
    )(x, g.reshape(1, D_MODEL), mods, mods)


def _final_norm(x, g, n_rows):
    tm = TM_EW
    return pl.pallas_call(
        _norm_body,
        grid=(n_rows // tm,),
        in_specs=[
            pl.BlockSpec((tm, D_MODEL), lambda i: (i, 0)),
            pl.BlockSpec((1, D_MODEL), lambda i: (0, 0)),
        ],
        out_specs=pl.BlockSpec((tm, D_MODEL), lambda i: (i, 0)),
        out_shape=jax.ShapeDtypeStruct((n_rows, D_MODEL), F32),
        scratch_shapes=[pltpu.VMEM((tm, 1), F32)],
        compiler_params=_cparams("arbitrary"),
        name="final_norm",
    )(x, g.reshape(1, D_MODEL))


MXU_COLS = 256


WO_SLAB = 64
WO_NSLAB = D_FF // WO_SLAB


NEXT_ROWS = 128
NEXT_SLICES = TM // NEXT_ROWS


def _ffn_a_body(u0_ref, xn_ref, g_ref, sh_ref, sc_ref, wa_ref, wb_ref, wo_ref, h_ref, wob_ref, u2_ref, r_ref):
    i, j = pl.program_id(0), pl.program_id(1)
    slot = i % 2

    @pl.when((i == 0) & (j == 0))
    def _():
        u2_ref[0] = u0_ref[...]

    dst = pl.ds(pl.multiple_of(jnp.minimum(j, NEXT_SLICES - 1) * NEXT_ROWS, NEXT_ROWS), NEXT_ROWS)
    _rms_scale_rows(xn_ref, r_ref, g_ref[...] * (1.0 + sc_ref[...]), sh_ref[...], u2_ref.at[1 - slot, dst, :])

    u = u2_ref[slot]
    for c in range(0, h_ref.shape[1], MXU_COLS):
        sl = slice(c, c + MXU_COLS)
        a = _dot(u, wa_ref[:, sl].astype(BF16))
        b = _dot(u, wb_ref[:, sl].astype(BF16))
        h_ref[:, sl] = (a * jax.nn.sigmoid(a) * b).astype(BF16)

    @pl.when(pl.program_id(0) * pl.num_programs(1) + pl.program_id(1) < WO_NSLAB)
    def _():
        wob_ref[...] = wo_ref[...].astype(BF16)


def _ffn_a(x, g, mods, base, wi, wo, l, n_rows):
    tf = 512
    nf = D_FF // tf
    ni = n_rows // TM
    assert ni * nf >= WO_NSLAB and nf >= NEXT_SLICES

    def wo_blk(i, j):
        return jnp.minimum(i * nf + j, WO_NSLAB - 1)

    def nxt(i):
        return jnp.minimum(i + 1, ni - 1)

    def next_mod(k):
        return pl.BlockSpec((None, None, 1, D_MODEL), lambda i, j: (_mod_row(nxt(i), TM), k, 0, 0))

    u0 = _normmod(x, g, mods, base, TM)
    return pl.pallas_call(
        _ffn_a_body,
        grid=(ni, nf),
        in_specs=[
            pl.BlockSpec((TM, D_MODEL), lambda i, j: (0, 0)),
            pl.BlockSpec((NEXT_ROWS, D_MODEL),
                         lambda i, j: (nxt(i) * NEXT_SLICES + jnp.minimum(j, NEXT_SLICES - 1), 0)),
            pl.BlockSpec((1, D_MODEL), lambda i, j: (0, 0)),
            next_mod(base),
            next_mod(base + 1),
            pl.BlockSpec((None, D_MODEL, tf), lambda i, j: (l, 0, j)),
            pl.BlockSpec((None, D_MODEL, tf), lambda i, j: (l, 0, j + nf)),
            pl.BlockSpec((None, WO_SLAB, D_MODEL), lambda i, j: (l, wo_blk(i, j), 0)),
        ],
        out_specs=[
            pl.BlockSpec((TM, tf), lambda i, j: (i, j)),
            pl.BlockSpec((WO_SLAB, D_MODEL), lambda i, j: (wo_blk(i, j), 0)),
        ],
        out_shape=[
            jax.ShapeDtypeStruct((n_rows, D_FF), BF16),
            jax.ShapeDtypeStruct((D_FF, D_MODEL), BF16),
        ],
        scratch_shapes=[pltpu.VMEM((2, TM, D_MODEL), BF16), pltpu.VMEM((NEXT_ROWS, 1), F32)],
        compiler_params=_cparams("arbitrary", "arbitrary"),
        name="ffn_a",
    )(u0, x, g.reshape(1, D_MODEL), mods, mods, wi, wi, wo)


SIDE_SLAB = 64


def _ffn_b_body(h_ref, w_ref, x_ref, g_ref, *rest, n_slab):
    o_ref = rest[-2] if n_slab else rest[-1]
    h = h_ref[...]
    for c in range(0, o_ref.shape[1], MXU_COLS):
        sl = slice(c, c + MXU_COLS)
        o_ref[:, sl] = x_ref[:, sl] + (0.5 * g_ref[:, sl]) * _dot(h, w_ref[:, sl])

    if n_slab:
        side_ref, _, side_out_ref = rest

        @pl.when(pl.program_id(0) * pl.num_programs(1) + pl.program_id(1) < n_slab)
        def _():
            side_out_ref[...] = side_ref[...].astype(BF16)


def _ffn_b(h, wo_bf16, x, mods, gate_idx, n_rows, side=None, l=None):
    tn = 512
    nj = D_MODEL // tn
    gate = pl.BlockSpec((None, None, 1, tn), lambda i, j: (_mod_row(i, TM), gate_idx, 0, j))
    in_specs = [
        pl.BlockSpec((TM, D_FF), lambda i, j: (i, 0)),
        pl.BlockSpec((D_FF, tn), lambda i, j: (0, j)),
        pl.BlockSpec((TM, tn), lambda i, j: (i, j)),
        gate,
    ]
    out_specs = [pl.BlockSpec((TM, tn), lambda i, j: (i, j))]
    out_shape = [jax.ShapeDtypeStruct((n_rows, D_MODEL), F32)]
    args = [h, wo_bf16, x, mods]
    n_slab = 0
    if side is not None:
        _, rows, cols = side.shape
        n_slab = rows // SIDE_SLAB
        assert (n_rows // TM) * nj >= n_slab

        def slab(i, j):
            return jnp.minimum(i * nj + j, n_slab - 1)

        in_specs.append(pl.BlockSpec((None, SIDE_SLAB, cols), lambda i, j: (l, slab(i, j), 0)))
        out_specs.append(pl.BlockSpec((SIDE_SLAB, cols), lambda i, j: (slab(i, j), 0)))
        out_shape.append(jax.ShapeDtypeStruct((rows, cols), BF16))
        args.append(side)
    res = pl.pallas_call(
        functools.partial(_ffn_b_body, n_slab=n_slab),
        grid=(n_rows // TM, nj),
        in_specs=in_specs,
        out_specs=out_specs,
        out_shape=out_shape,
        compiler_params=_cparams("arbitrary", "arbitrary"),
        name="ffn_b",
    )(*args)
    return res if side is not None else res[0]


def _ffn(x, mods, g, wi, wo, l, base, n_rows, side=None):
    h, wo_bf16 = _ffn_a(x, g, mods, base, wi, wo, l, n_rows)
    return _ffn_b(h, wo_bf16, x, mods, base + 2, n_rows, side, l)


def _proj_body(u_ref, w_ref, o_ref):
    o_ref[...] = _dot(u_ref[...], w_ref[...])


def _proj(u, w_in_bf16):
    tn = COL
    tm = 1536
    return pl.pallas_call(
        _proj_body,
        grid=(MT // tm, IN_WIDTH // tn),
        in_specs=[
            pl.BlockSpec((tm, D_MODEL), lambda i, j: (i, 0)),
            pl.BlockSpec((D_MODEL, tn), lambda i, j: (0, j)),
        ],
        out_specs=pl.BlockSpec((tm, tn), lambda i, j: (i, j)),
        out_shape=jax.ShapeDtypeStruct((MT, IN_WIDTH), F32),
        compiler_params=_cparams("arbitrary", "arbitrary"),
        name="proj",
    )(u, w_in_bf16)


def _rope_tables():
    t = jnp.arange(SEQ)
    rows = (t // GRID_W).astype(F32)
    cols = (t % GRID_W).astype(F32)
    freqs = ROPE_THETA ** (-jnp.arange(ROPE_FREQS, dtype=F32) / ROPE_FREQS)
    ar = rows[:, None] * freqs[None, :]
    ac = cols[:, None] * freqs[None, :]
    cos = jnp.concatenate([jnp.cos(ar), jnp.cos(ar), jnp.cos(ac), jnp.cos(ac)], axis=-1)
    sin = jnp.concatenate([-jnp.sin(ar), jnp.sin(ar), -jnp.sin(ac), jnp.sin(ac)], axis=-1)
    cos = jnp.concatenate([cos, jnp.ones((TM_EW, HEAD_DIM), F32)], axis=0)
    sin = jnp.concatenate([sin, jnp.zeros((TM_EW, HEAD_DIM), F32)], axis=0)
    return cos, sin


def _qkv_body(q_ref, kv_ref, cos_ref, sin_ref, qn_ref, kn_ref, qo_ref, ko_ref, vo_ref):
    cos = cos_ref[...]
    sin = sin_ref[...]
    lane = lax.broadcasted_iota(jnp.int32, cos.shape, 1)
    first = (lane % (2 * ROPE_FREQS)) < ROPE_FREQS

    def norm_rope(xh, g):
        y = xh * lax.rsqrt(jnp.mean(xh * xh, axis=-1, keepdims=True) + EPS) * g
        swapped = jnp.where(first, pltpu.roll(y, HEAD_DIM - ROPE_FREQS, 1), pltpu.roll(y, ROPE_FREQS, 1))
        return y * cos + swapped * sin

    scale = math.log2(math.e) / math.sqrt(HEAD_DIM)
    for h in range(N_Q_HEADS):
        sl = slice(h * HEAD_DIM, (h + 1) * HEAD_DIM)
        qo_ref[:, sl] = (norm_rope(q_ref[:, sl], qn_ref[...]) * scale).astype(BF16)
    for h in range(N_KV_HEADS):
        sl = slice(h * HEAD_DIM, (h + 1) * HEAD_DIM)
        ko_ref[:, sl] = norm_rope(kv_ref[:, sl], kn_ref[...]).astype(BF16)
    kvw = N_KV_HEADS * HEAD_DIM
    for h in range(N_KV_HEADS):
        vo_ref[:, 2 * h * HEAD_DIM:(2 * h + 1) * HEAD_DIM] = (
            kv_ref[:, kvw + h * HEAD_DIM:kvw + (h + 1) * HEAD_DIM].astype(BF16))
        vo_ref[:, (2 * h + 1) * HEAD_DIM:(2 * h + 2) * HEAD_DIM] = jnp.ones((vo_ref.shape[0], HEAD_DIM), BF16)


def _qkv(proj, cos, sin, q_norm, k_norm):
    tm = TM_EW
    n_lat_tiles = N_LAT // tm
    per_seq = SEQ // tm
    kvw = N_KV_HEADS * HEAD_DIM

    def tab(i):
        return (jnp.where(i < n_lat_tiles, i % per_seq, per_seq), 0)

    return pl.pallas_call(
        _qkv_body,
        grid=(MT // tm,),
        in_specs=[
            pl.BlockSpec((tm, COL), lambda i: (i, OFF_Q // COL)),
            pl.BlockSpec((tm, COL), lambda i: (i, OFF_KV // COL)),
            pl.BlockSpec((tm, HEAD_DIM), tab),
            pl.BlockSpec((tm, HEAD_DIM), tab),
            pl.BlockSpec((1, HEAD_DIM), lambda i: (0, 0)),
            pl.BlockSpec((1, HEAD_DIM), lambda i: (0, 0)),
        ],
        out_specs=[
            pl.BlockSpec((tm, COL), lambda i: (i, 0)),
            pl.BlockSpec((tm, kvw), lambda i: (i, 0)),
            pl.BlockSpec((tm, 2 * kvw), lambda i: (i, 0)),
        ],
        out_shape=[
            jax.ShapeDtypeStruct((MT, COL), BF16),
            jax.ShapeDtypeStruct((MT, kvw), BF16),
            jax.ShapeDtypeStruct((MT, 2 * kvw), BF16),
        ],
        compiler_params=_cparams("arbitrary"),
        name="qkv",
    )(proj, proj, cos, sin, q_norm.reshape(1, HEAD_DIM), k_norm.reshape(1, HEAD_DIM))


def _attn_body(*refs, with_lat):
    if with_lat:
        q_ref, kl_ref, vl_ref, kc_ref, vc_ref, o_ref = refs
    else:
        q_ref, kc_ref, vc_ref, o_ref = refs
    nt = (((1,), (1,)), ((), ()))
    for g in range(Q_GROUP):
        sl = slice(g * HEAD_DIM, (g + 1) * HEAD_DIM)
        q = q_ref[:, sl]
        sc = lax.dot_general(q, kc_ref[...], nt, preferred_element_type=F32)
        m = jnp.max(sc, axis=-1, keepdims=True)
        if with_lat:
            s_lat = lax.dot_general(q, kl_ref[...], nt, preferred_element_type=F32)
            m = jnp.maximum(m, jnp.max(s_lat, axis=-1, keepdims=True))
            e_lat = jnp.exp2(s_lat - m)
        ec = jnp.exp2(sc - m)
        o = _dot(ec.astype(BF16), vc_ref[...])
        if with_lat:
            o = o + _dot(e_lat.astype(BF16), vl_ref[...])
        o_ref[:, sl] = (o[:, :HEAD_DIM] / o[:, HEAD_DIM:]).astype(BF16)


def _attn_lat(q, k, v, bb):
    tq = 512
    nq = SEQ // tq
    ctx_blk = N_LAT // CTX_LEN
    gw = Q_GROUP * HEAD_DIM
    return pl.pallas_call(
        _keep_branch_buffer(functools.partial(_attn_body, with_lat=True), 5),
        grid=(BATCH, N_KV_HEADS, nq),
        in_specs=[
            pl.BlockSpec((tq, gw), lambda b, h, i: (b * nq + i, h)),
            pl.BlockSpec((SEQ, HEAD_DIM), lambda b, h, i: (b, h)),
            pl.BlockSpec((SEQ, 2 * HEAD_DIM), lambda b, h, i: (b, h)),
            pl.BlockSpec((CTX_LEN, HEAD_DIM), lambda b, h, i: (ctx_blk + b, h)),
            pl.BlockSpec((CTX_LEN, 2 * HEAD_DIM), lambda b, h, i: (ctx_blk + b, h)),
            _ANY,
        ],
        out_specs=pl.BlockSpec((None, tq, gw), lambda b, h, i: (SLOT_ATTN, b * nq + i, h)),
        out_shape=jax.ShapeDtypeStruct(bb.shape, bb.dtype),
        input_output_aliases={5: 0},
        compiler_params=_cparams("arbitrary", "arbitrary", "arbitrary"),
        name="attn_lat",
    )(q, k, v, k, v, bb)


def _attn_ctx(q, k, v, bb):
    ctx_blk = N_LAT // CTX_LEN
    gw = Q_GROUP * HEAD_DIM
    return pl.pallas_call(
        _keep_branch_buffer(functools.partial(_attn_body, with_lat=False), 3),
        grid=(BATCH, N_KV_HEADS),
        in_specs=[
            pl.BlockSpec((CTX_LEN, gw), lambda b, h: (ctx_blk + b, h)),
            pl.BlockSpec((CTX_LEN, HEAD_DIM), lambda b, h: (ctx_blk + b, h)),
            pl.BlockSpec((CTX_LEN, 2 * HEAD_DIM), lambda b, h: (ctx_blk + b, h)),
            _ANY,
        ],
        out_specs=pl.BlockSpec((None, CTX_LEN, gw), lambda b, h: (SLOT_ATTN, ctx_blk + b, h)),
        out_shape=jax.ShapeDtypeStruct(bb.shape, bb.dtype),
        input_output_aliases={3: 0},
        compiler_params=_cparams("arbitrary", "arbitrary"),
        name="attn_ctx",
    )(q, k, v, bb)


POOL_PAD = 8


def _pool_body(a_ref, w_ref, s_ref, o_ref, *, seq):
    lp = seq + 2 * POOL_PAD
    t = lax.broadcasted_iota(jnp.int32, (seq, LANE), 0)
    zpad = jnp.zeros((POOL_PAD, LANE), F32)
    for gi, win in enumerate(POOL_WINDOWS):
        sl = slice(gi * LANE, (gi + 1) * LANE)
        a = a_ref[:, sl]
        s = jnp.concatenate([zpad, a, zpad], axis=0)
        s = s + pltpu.roll(s, 1, 0)
        half = 1
        while 2 * half < win:
            s = pltpu.roll(s, half, 0) + pltpu.roll(s, lp - half, 0)
            half *= 2
        s = s[POOL_PAD:POOL_PAD + seq]
        lo = jnp.maximum(t - win // 2, 0)
        hi = jnp.minimum(t + win // 2, seq)
        pooled = s / (hi - lo).astype(F32) - a
        y = _dot(pooled.astype(BF16), w_ref[gi].astype(BF16))
        o_ref[:, sl] = (y * s_ref[:, sl]).astype(BF16)


def _pool(proj, pool_w, pool_scale, bb, seq, row_blk0):
    width = len(POOL_WINDOWS) * LANE
    return pl.pallas_call(
        _keep_branch_buffer(functools.partial(_pool_body, seq=seq), 3),
        grid=(BATCH,),
        in_specs=[
            pl.BlockSpec((seq, width), lambda b: (row_blk0 + b, 0)),
            pl.BlockSpec((len(POOL_WINDOWS), LANE, LANE), lambda b: (0, 0, 0)),
            pl.BlockSpec((1, width), lambda b: (0, 0)),
            _ANY,
        ],
        out_specs=pl.BlockSpec((None, seq, width), lambda b: (SLOT_POOL, row_blk0 + b, 0)),
        out_shape=jax.ShapeDtypeStruct(bb.shape, bb.dtype),
        input_output_aliases={3: 0},
        compiler_params=_cparams("arbitrary"),
        name="pool",
    )(proj, pool_w, pool_scale.reshape(1, width), bb)


def _hy_prep_body(x0_ref, x1_ref, v_ref, w0_ref, w1_ref, wv_ref, b0_ref, b1_ref, bv_ref,
                  x0o_ref, vx_ref, vb_ref, *, seq):
    t = lax.broadcasted_iota(jnp.int32, x0_ref.shape, 0)

    def conv(x_ref, w_ref, b_ref):
        x = x_ref[...]
        prev = jnp.where(t >= 1, pltpu.roll(x, 1, 0), 0.0)
        nxt = jnp.where(t <= seq - 2, pltpu.roll(x, seq - 1, 0), 0.0)
        return prev * w_ref[0:1, :] + x * w_ref[1:2, :] + nxt * w_ref[2:3, :] + b_ref[...]

    x0o_ref[...] = conv(x0_ref, w0_ref, b0_ref)
    vx = conv(v_ref, wv_ref, bv_ref) * conv(x1_ref, w1_ref, b1_ref)
    vx_ref[...] = vx
    vb_ref[...] = vx.astype(BF16)


def _hy_prep(proj, short_w, short_b, seq, row_blk0):
    tc = 2 * LANE
    nc = HYENA_WIDTH // tc
    c0 = OFF_HY // tc
    short_b = short_b.reshape(1, 3 * HYENA_WIDTH)

    def xspec(part):
        return pl.BlockSpec((seq, tc), lambda b, c: (row_blk0 + b, c0 + part * nc + c))

    def wspec(part, rows):
        return pl.BlockSpec((rows, tc), lambda b, c: (0, part * nc + c))

    return pl.pallas_call(
        functools.partial(_hy_prep_body, seq=seq),
        grid=(BATCH, nc),
        in_specs=[xspec(0), xspec(1), xspec(2), wspec(0, 3), wspec(1, 3), wspec(2, 3),
                  wspec(0, 1), wspec(1, 1), wspec(2, 1)],
        out_specs=[
            pl.BlockSpec((seq, tc), lambda b, c: (b, c)),
            pl.BlockSpec((seq, tc), lambda b, c: (b, c)),
            pl.BlockSpec((seq, tc), lambda b, c: (0, b * nc + c)),
        ],
        out_shape=[
            jax.ShapeDtypeStruct((BATCH * seq, HYENA_WIDTH), F32),
            jax.ShapeDtypeStruct((BATCH * seq, HYENA_WIDTH), F32),
            jax.ShapeDtypeStruct((seq, BATCH * HYENA_WIDTH), BF16),
        ],
        compiler_params=_cparams("arbitrary", "arbitrary"),
        name="hy_prep",
    )(proj, proj, proj, short_w, short_w, short_w, short_b, short_b, short_b)


def _hy_feats(seq):
    t = jnp.linspace(0.0, 1.0, seq, dtype=F32)[:, None]
    f = jnp.linspace(1e-4, HYENA_BANDS - 1, HYENA_BANDS, dtype=F32)
    w = 2.0 * math.pi * jnp.arange(seq, dtype=F32) / seq
    fw = w[:, None] * f[None, :]
    z = jnp.concatenate([t, jnp.cos(fw), -jnp.sin(fw)], axis=-1)
    return jnp.pad(z, ((0, 0), (0, LANE - HYENA_EMB)))


def _hy_deltas():
    max_decay = math.log(HYENA_TARGET) / HYENA_FAST_PCT
    min_decay = math.log(HYENA_TARGET) / HYENA_SLOW_PCT
    return jnp.abs(jnp.linspace(min_decay, max_decay, HYENA_WIDTH, dtype=F32)).reshape(1, HYENA_WIDTH)


def _hy_filter_body(z_ref, w1_ref, b1_ref, w2_ref, b2_ref, w3_ref, fr_ref, dl_ref, k_ref, nyq_ref, *, seq):
    hp = lax.Precision.HIGHEST
    freq = fr_ref[...]
    h = jnp.sin(freq * (jnp.dot(z_ref[...], w1_ref[...], precision=hp, preferred_element_type=F32) + b1_ref[...]))
    h = jnp.sin(freq * (jnp.dot(h, w2_ref[...], precision=hp, preferred_element_type=F32) + b2_ref[...]))
    h = jnp.dot(h, w3_ref[...], precision=hp, preferred_element_type=F32)
    ti = lax.broadcasted_iota(jnp.int32, (seq, HYENA_WIDTH), 0)
    decay = jnp.exp(-(ti.astype(F32) * (1.0 / (seq - 1))) * dl_ref[...])
    hf = h[:, :HYENA_WIDTH] * decay
    hb = jnp.where(ti == 0, 0.0, h[:, HYENA_WIDTH:] * decay)
    ks = hf + hb
    k_ref[:, :HYENA_WIDTH] = ks.astype(BF16)
    k_ref[:, HYENA_WIDTH:] = (hf - hb).astype(BF16)
    nyq = jnp.sum(jnp.where(ti % 2 == 0, ks, -ks), axis=0, keepdims=True)
    nyq_ref[...] = jnp.broadcast_to(nyq, nyq_ref.shape)


def _hy_filter(lp, z, deltas, seq):
    w1 = jnp.pad(lp['hy_f1_w'], ((0, LANE - HYENA_EMB), (0, 0)))
    args = (z, w1, lp['hy_f1_b'].reshape(1, -1), lp['hy_f2_w'], lp['hy_f2_b'].reshape(1, -1), lp['hy_f3_w'],
            lp['hy_freq'].reshape(1, -1), deltas)
    full = lambda a: pl.BlockSpec(a.shape, lambda i: (0,) * a.ndim)
    return pl.pallas_call(
        functools.partial(_hy_filter_body, seq=seq),
        grid=(1,),
        in_specs=[full(a) for a in args],
        out_specs=[pl.BlockSpec((seq, 2 * HYENA_WIDTH), lambda i: (0, 0)),
                   pl.BlockSpec((8, HYENA_WIDTH), lambda i: (0, 0))],
        out_shape=[jax.ShapeDtypeStruct((seq, 2 * HYENA_WIDTH), BF16),
                   jax.ShapeDtypeStruct((8, HYENA_WIDTH), F32)],
        compiler_params=_cparams("arbitrary"),
        name="hy_filter",
    )(*args)


DFT_SPLIT = 64


def _dft_matrices(seq):
    n = 2 * seq
    f = jnp.arange(seq, dtype=jnp.int32)[:, None]

    def table(step, count):
        idx = (f * (jnp.arange(count, dtype=jnp.int32)[None, :] * step)) % n
        ang = idx.astype(F32) * (2.0 * math.pi / n)
        return jnp.cos(ang), jnp.sin(ang)

    hc, hs = table(DFT_SPLIT, seq // DFT_SPLIT)
    lc, ls = table(1, DFT_SPLIT)
    cos = (hc[:, :, None] * lc[:, None, :] - hs[:, :, None] * ls[:, None, :]).reshape(seq, seq)
    sin = (hs[:, :, None] * lc[:, None, :] + hc[:, :, None] * ls[:, None, :]).reshape(seq, seq)
    s = jnp.arange(seq, dtype=jnp.int32)[None, :]
    alt = lambda idx: jnp.where(idx % 2 == 0, 1.0, -1.0).astype(F32)
    part = jnp.arange(2, dtype=jnp.int32)
    wf = jnp.where(part[:, None, None] == 0, cos[None], jnp.where(f == 0, alt(s), -sin)[None])
    wi = jnp.where(part[None, :, None] == 0, cos[:, None, :], jnp.where(s == 0, alt(f), -sin)[:, None, :])
    return wf.astype(BF16).reshape(n, seq), wi.astype(BF16).reshape(seq, n)


def _mm_body(a_ref, b_ref, o_ref):
    o_ref[...] = _dot(a_ref[...], b_ref[...])


def _dft_fwd_body(wc_ref, ws_ref, x_ref, hr_ref, hi_ref, nyq_ref, pr_ref, pi_ref, *, n):
    x = x_ref[...]
    xr = _dot(wc_ref[...], x)
    xi = _dot(ws_ref[...], x)
    hr = hr_ref[...]
    hi = hi_ref[...]
    freq = lax.broadcasted_iota(jnp.int32, xr.shape, 0) + pl.program_id(0) * xr.shape[0]
    first = freq == 0
    w = jnp.where(first, 1.0 / n, 2.0 / n)
    pr_ref[...] = (jnp.where(first, xr * hr, xr * hr - xi * hi) * w).astype(BF16)
    pi_ref[...] = (jnp.where(first, xi * nyq_ref[0:1, :], xr * hi + xi * hr) * w).astype(BF16)


def _dft_fwd(wf, x, hf, nyq):
    n, seq = wf.shape
    tf = min(seq, 512)
    nf = seq // tf
    out = jax.ShapeDtypeStruct((seq, BATCH * HYENA_WIDTH), BF16)
    return pl.pallas_call(
        functools.partial(_dft_fwd_body, n=n),
        grid=(nf, BATCH),
        in_specs=[
            pl.BlockSpec((tf, seq), lambda f, b: (f, 0)),
            pl.BlockSpec((tf, seq), lambda f, b: (nf + f, 0)),
            pl.BlockSpec((seq, HYENA_WIDTH), lambda f, b: (0, b)),
            pl.BlockSpec((tf, HYENA_WIDTH), lambda f, b: (f, 0)),
            pl.BlockSpec((tf, HYENA_WIDTH), lambda f, b: (nf + f, 0)),
            pl.BlockSpec((8, HYENA_WIDTH), lambda f, b: (0, 0)),
        ],
        out_specs=[pl.BlockSpec((tf, HYENA_WIDTH), lambda f, b: (f, b))] * 2,
        out_shape=[out, out],
        compiler_params=_cparams("arbitrary", "arbitrary"),
        name="dft_fwd",
    )(wf, wf, x, hf, hf, nyq)


def _dft_filter(wf, k):
    n, seq = wf.shape
    tm = min(seq, 1024)
    return pl.pallas_call(
        _mm_body,
        grid=(n // tm,),
        in_specs=[
            pl.BlockSpec((tm, seq), lambda i: (i, 0)),
            pl.BlockSpec((seq, HYENA_WIDTH), lambda i: (0, (i * tm) // seq)),
        ],
        out_specs=pl.BlockSpec((tm, HYENA_WIDTH), lambda i: (i, 0)),
        out_shape=jax.ShapeDtypeStruct((n, HYENA_WIDTH), F32),
        compiler_params=_cparams("arbitrary"),
        name="dft_filter",
    )(wf, k)


def _dft_inv_body(wc_ref, ws_ref, pr_ref, pi_ref, vx_ref, bias_ref, x0_ref, o_ref):
    y = _dot(wc_ref[...], pr_ref[...]) + _dot(ws_ref[...], pi_ref[...])
    o_ref[...] = ((y + vx_ref[...] * bias_ref[...]) * x0_ref[...]).astype(BF16)


def _dft_inv(wi, p_re, p_im, vx, bias, x0, bb, row0):
    seq, n = wi.shape
    tm = min(seq, 1024)
    tn = HYENA_WIDTH
    nt = seq // tm
    return pl.pallas_call(
        _keep_branch_buffer(_dft_inv_body, 7),
        grid=(nt, BATCH),
        in_specs=[
            pl.BlockSpec((tm, seq), lambda i, b: (i, 0)),
            pl.BlockSpec((tm, seq), lambda i, b: (i, 1)),
            pl.BlockSpec((seq, tn), lambda i, b: (0, b)),
            pl.BlockSpec((seq, tn), lambda i, b: (0, b)),
            pl.BlockSpec((tm, tn), lambda i, b: (b * nt + i, 0)),
            pl.BlockSpec((1, tn), lambda i, b: (0, 0)),
            pl.BlockSpec((tm, tn), lambda i, b: (b * nt + i, 0)),
            _ANY,
        ],
        out_specs=pl.BlockSpec((None, tm, tn), lambda i, b: (SLOT_HYENA, row0 // tm + b * nt + i, 0)),
        out_shape=jax.ShapeDtypeStruct(bb.shape, bb.dtype),
        input_output_aliases={7: 0},
        compiler_params=_cparams("arbitrary", "arbitrary"),
        name="dft_inv",
    )(wi, wi, p_re, p_im, vx, bias.reshape(1, tn), x0, bb)


def _hyena(proj, lp, consts, bb, seq, row0):
    z, deltas, (wf, wi) = consts
    x0, vx, vb = _hy_prep(proj, lp['hy_short_w'], lp['hy_short_b'], seq, row0 // seq)
    k, nyq = _hy_filter(lp, z, deltas, seq)
    hf = _dft_filter(wf, k)
    p_re, p_im = _dft_fwd(wf, vb, hf, nyq)
    return _dft_inv(wi, p_re, p_im, vx, lp['hy_bias'], x0, bb, row0)


def _s5_params(lp, n_seg_steps):
    a_re, a_im = lp['s5_a_re'], lp['s5_a_im']
    dt = jnp.exp(lp['s5_log_dt'])[..., None]
    mag = jnp.exp(a_re * dt)
    ab_re, ab_im = mag * jnp.cos(a_im * dt), mag * jnp.sin(a_im * dt)
    den = a_re * a_re + a_im * a_im
    nr, ni = ab_re - 1.0, ab_im
    cf_re = (nr * a_re + ni * a_im) / den
    cf_im = (ni * a_re - nr * a_im) / den
    b_re, b_im = lp['s5_b_re'], lp['s5_b_im']
    bb_re = cf_re[..., None] * b_re - cf_im[..., None] * b_im
    bb_im = cf_re[..., None] * b_im + cf_im[..., None] * b_re
    eye = jnp.eye(S5_CHUNK_GROUPS, dtype=F32)

    def bdiag_in(m):
        m = m.reshape(2, S5_NCHUNK, S5_CHUNK_GROUPS, S5_STATE, S5_GC)
        return jnp.einsum('dqgpc,gh->dqgchp', m, eye).reshape(2, S5_NCHUNK, S5_CHUNK_CH, S5_CHUNK_ST)

    def bdiag_out(m):
        m = m.reshape(2, S5_NCHUNK, S5_CHUNK_GROUPS, S5_GC, S5_STATE)
        return jnp.einsum('dqgcp,gh->dqhpgc', m, eye).reshape(2, S5_NCHUNK, S5_CHUNK_ST, S5_CHUNK_CH)

    bbd = jnp.concatenate([bdiag_in(bb_re), bdiag_in(bb_im)], axis=-1).astype(BF16)
    cbd = jnp.concatenate([bdiag_out(lp['s5_c_re']), -bdiag_out(lp['s5_c_im'])], axis=-2).astype(BF16)
    a = jnp.stack([ab_re.reshape(2, -1), ab_im.reshape(2, -1)], axis=1)
    aks = []
    for steps in n_seg_steps:
        pr, pi = ab_re, ab_im
        for _ in range(int(math.log2(steps))):
            pr, pi = pr * pr - pi * pi, 2.0 * pr * pi
        aks.append(jnp.stack([pr.reshape(2, -1), pi.reshape(2, -1)], axis=1))
    return [(bbd, cbd, a, ak) for ak in aks]


S5_BLOCK = 256


def _s5_scan_body(u_ref, bbd_ref, cbd_ref, a_ref, ak_ref, dsk_ref, h0_ref, y_ref, hl_ref,
                  xr0_ref, xr1_ref, xi0_ref, xi1_ref, yd0_ref, yd1_ref, up_ref, ub_ref, *, seq):
    nk = seq // S5_LANES
    nblk = seq // S5_BLOCK
    tiles = S5_BLOCK // S5_LANES
    shape = (S5_LANES, S5_CHUNK_ST)
    xr_ref, xi_ref, yd_ref = (xr0_ref, xr1_ref), (xi0_ref, xi1_ref), (yd0_ref, yd1_ref)
    for j in range(S5_LANES):
        up_ref[pl.ds(j, nk, stride=S5_LANES), :] = u_ref[pl.ds(j * nk, nk), :]
    ub_ref[...] = up_ref[...].astype(BF16)
    row = lax.broadcasted_iota(jnp.int32, shape, 0)
    zero = jnp.zeros(shape, F32)

    def block(r):
        return pl.ds(pl.multiple_of(r * S5_BLOCK, S5_BLOCK), S5_BLOCK)

    def in_proj(d, rows):
        xr_ref[d][rows, :] = _dot(ub_ref[rows, :], bbd_ref[d, :, :S5_CHUNK_ST])
        xi_ref[d][rows, :] = _dot(ub_ref[rows, :], bbd_ref[d, :, S5_CHUNK_ST:])

    def out_proj(d, rows):
        yd_ref[d][rows, :] = (_dot(xr_ref[d][rows, :].astype(BF16), cbd_ref[d, :S5_CHUNK_ST, :])
                              + _dot(xi_ref[d][rows, :].astype(BF16), cbd_ref[d, S5_CHUNK_ST:, :]))

    def coeffs(d):
        return jnp.broadcast_to(a_ref[d, 0:1, :], shape), jnp.broadcast_to(a_ref[d, 1:2, :], shape)

    def block_tiles(d, r):
        base = pl.multiple_of(r * S5_BLOCK, S5_BLOCK)
        order = range(tiles) if d == 0 else range(tiles - 1, -1, -1)
        return [pl.ds(base + t * S5_LANES, S5_LANES) for t in order]

    def scan_block(d, r, carry):
        ar, ai = coeffs(d)
        xr, xi = carry
        for rows in block_tiles(d, r):
            xr, xi = (ar * xr - ai * xi + xr_ref[d][rows, :], ar * xi + ai * xr + xi_ref[d][rows, :])
            xr_ref[d][rows, :] = xr
            xi_ref[d][rows, :] = xi
        return xr, xi

    def fix_block(d, r, carry):
        ar, ai = coeffs(d)
        gr, gi = carry
        for rows in block_tiles(d, r):
            gr, gi = ar * gr - ai * gi, ar * gi + ai * gr
            xr_ref[d][rows, :] += gr
            xi_ref[d][rows, :] += gi
        return gr, gi

    def visit(d, s):
        return s if d == 0 else nblk - 1 - s

    def entering_states(d, er, ei):
        akr, aki = ak_ref[d, 0:1, :], ak_ref[d, 1:2, :]
        hr, hi = h0_ref[2 * d:2 * d + 1, :], h0_ref[2 * d + 1:2 * d + 2, :]
        in_r, in_i = zero, zero
        for j in (range(S5_LANES) if d == 0 else range(S5_LANES - 1, -1, -1)):
            in_r = jnp.where(row == j, hr, in_r)
            in_i = jnp.where(row == j, hi, in_i)
            hr, hi = (akr * hr - aki * hi + er[j:j + 1, :], akr * hi + aki * hr + ei[j:j + 1, :])
        hl_ref[2 * d:2 * d + 1, :] = hr
        hl_ref[2 * d + 1:2 * d + 2, :] = hi
        return in_r, in_i

    def fix_all(d, ins):
        lax.fori_loop(0, nblk, lambda s, carry: fix_block(d, visit(d, s), carry), ins)

    in_proj(0, slice(None))

    def scan0_body(s, carry):
        in_proj(1, block(s))
        return scan_block(0, s, carry)
    ends0 = lax.fori_loop(0, nblk, scan0_body, (zero, zero))
    fix_all(0, entering_states(0, *ends0))

    def scan1_body(s, carry):
        out_proj(0, block(s))
        return scan_block(1, visit(1, s), carry)
    ends1 = lax.fori_loop(0, nblk, scan1_body, (zero, zero))
    fix_all(1, entering_states(1, *ends1))
    out_proj(1, slice(None))

    up_ref[...] = yd_ref[0][...] + yd_ref[1][...] + up_ref[...] * (dsk_ref[0] + dsk_ref[1])
    for j in range(S5_LANES):
        y_ref[pl.ds(j * nk, nk), :] = up_ref[pl.ds(j, nk, stride=S5_LANES), :]


def _s5_scan(proj, params, dskip, h0, seq, row_blk0):
    bbd, cbd, a, ak = params
    nq = S5_NCHUNK
    col0 = OFF_S5 // S5_CHUNK_CH
    return pl.pallas_call(
        functools.partial(_s5_scan_body, seq=seq),
        grid=(BATCH, nq),
        in_specs=[
            pl.BlockSpec((seq, S5_CHUNK_CH), lambda b, q: (row_blk0 + b, col0 + q)),
            pl.BlockSpec((2, None, S5_CHUNK_CH, 2 * S5_CHUNK_ST), lambda b, q: (0, q, 0, 0)),
            pl.BlockSpec((2, None, 2 * S5_CHUNK_ST, S5_CHUNK_CH), lambda b, q: (0, q, 0, 0)),
            pl.BlockSpec((2, 2, S5_CHUNK_ST), lambda b, q: (0, 0, q)),
            pl.BlockSpec((2, 2, S5_CHUNK_ST), lambda b, q: (0, 0, q)),
            pl.BlockSpec((2, 1, S5_CHUNK_CH), lambda b, q: (0, 0, q)),
            pl.BlockSpec((None, 4, S5_CHUNK_ST), lambda b, q: (b, 0, q)),
        ],
        out_specs=[
            pl.BlockSpec((seq, S5_CHUNK_CH), lambda b, q: (b, q)),
            pl.BlockSpec((None, 4, S5_CHUNK_ST), lambda b, q: (b, 0, q)),
        ],
        out_shape=[
            jax.ShapeDtypeStruct((BATCH * seq, S5_WIDTH), F32),
            jax.ShapeDtypeStruct((BATCH, 4, S5_GROUPS * S5_STATE), F32),
        ],
        scratch_shapes=[pltpu.VMEM((seq, S5_CHUNK_ST), F32)] * 4 + [pltpu.VMEM((seq, S5_CHUNK_CH), F32)] * 3
        + [pltpu.VMEM((seq, S5_CHUNK_CH), BF16)],
        compiler_params=_cparams("arbitrary", "arbitrary"),
        name="s5_scan",
    )(proj, bbd, cbd, a, ak, dskip.reshape(2, 1, S5_WIDTH), h0)


def _s5_glu_body(y_ref, w_ref, b_ref, o_ref):
    g = _dot(jax.nn.gelu(y_ref[...]).astype(BF16), w_ref[...].astype(BF16)) + b_ref[...]
    o_ref[...] = (g[:, :S5_WIDTH] * jax.nn.sigmoid(g[:, S5_WIDTH:])).astype(BF16)


def _s5_glu(y, w, b, bb, row0):
    n_rows = y.shape[0]
    tm = TM_EW
    return pl.pallas_call(
        _keep_branch_buffer(_s5_glu_body, 3),
        grid=(n_rows // tm,),
        in_specs=[
            pl.BlockSpec((tm, S5_WIDTH), lambda i: (i, 0)),
            pl.BlockSpec((S5_WIDTH, 2 * S5_WIDTH), lambda i: (0, 0)),
            pl.BlockSpec((1, 2 * S5_WIDTH), lambda i: (0, 0)),
            _ANY,
        ],
        out_specs=pl.BlockSpec((None, tm, S5_WIDTH), lambda i: (SLOT_S5, row0 // tm + i, 0)),
        out_shape=jax.ShapeDtypeStruct(bb.shape, bb.dtype),
        input_output_aliases={3: 0},
        compiler_params=_cparams("arbitrary"),
        name="s5_glu",
    )(y, w, b.reshape(1, -1), bb)


def _merge_body(u_ref, *refs):
    wg_refs, bg_refs, y_refs, wb_refs = (refs[k * N_BRANCH:(k + 1) * N_BRANCH] for k in range(4))
    o_ref = refs[4 * N_BRANCH]
    u = u_ref[...]
    acc = None
    for n in range(N_BRANCH):
        gate = jax.nn.sigmoid(_dot(u, wg_refs[n][...].astype(BF16)) + bg_refs[n][...])
        contrib = gate * _dot(y_refs[n][...], wb_refs[n][...].astype(BF16))
        acc = contrib if acc is None else acc + contrib
    o_ref[...] = acc.astype(BF16)


def _merge(u, w_gate, b_gate, branches, w_branch, l, n_rows):
    tc = MXU_COLS
    ncol = D_MODEL // tc
    b_gate = b_gate.reshape(1, -1)
    per_branch = lambda make: [make(n) for n in range(N_BRANCH)]
    return pl.pallas_call(
        _merge_body,
        grid=(n_rows // TM, ncol),
        in_specs=[pl.BlockSpec((TM, D_MODEL), lambda i, c: (i, 0))]
        + per_branch(lambda n: pl.BlockSpec((None, D_MODEL, tc), lambda i, c: (l, 0, n * ncol + c)))
        + per_branch(lambda n: pl.BlockSpec((1, tc), lambda i, c: (0, n * ncol + c)))
        + per_branch(lambda n: pl.BlockSpec((None, TM, BRANCH_WIDTH), lambda i, c: (n, i, 0)))
        + per_branch(lambda n: pl.BlockSpec((None, None, BRANCH_WIDTH, tc), lambda i, c: (l, n, 0, c))),
        out_specs=pl.BlockSpec((TM, tc), lambda i, c: (i, c)),
        out_shape=jax.ShapeDtypeStruct((n_rows, D_MODEL), BF16),
        compiler_params=_cparams("arbitrary", "arbitrary"),
        name="merge",
    )(u, *([w_gate] * N_BRANCH), *([b_gate] * N_BRANCH), *([branches] * N_BRANCH), *([w_branch] * N_BRANCH))


def _out_body(m_ref, w_ref, x_ref, g_ref, o_ref):
    o_ref[...] = x_ref[...] + g_ref[...] * _dot(m_ref[...], w_ref[...].astype(BF16))


def _out_proj(merged, w_out, l, x, mods, n_rows):
    tn = 1024
    gate = pl.BlockSpec((None, None, 1, tn), lambda j, i: (_mod_row(i, TM), 5, 0, j))
    return pl.pallas_call(
        _out_body,
        grid=(D_MODEL // tn, n_rows // TM),
        in_specs=[
            pl.BlockSpec((TM, D_MODEL), lambda j, i: (i, 0)),
            pl.BlockSpec((None, D_MODEL, tn), lambda j, i: (l, 0, j)),
            pl.BlockSpec((TM, tn), lambda j, i: (i, j)),
            gate,
        ],
        out_specs=pl.BlockSpec((TM, tn), lambda j, i: (i, j)),
        out_shape=jax.ShapeDtypeStruct((n_rows, D_MODEL), F32),
        compiler_params=_cparams("arbitrary", "arbitrary"),
        name="out_proj",
    )(merged, w_out, x, mods)


def _mixer(x, g, lp, big, w_in_bf16, l, mods, consts, bb, last):
    cos, sin, hy_lat, hy_ctx = consts
    u = _normmod(x, g, mods, 3, MT)
    proj = _proj(u, w_in_bf16)
    q, k, v = _qkv(proj, cos, sin, lp['q_norm'], lp['k_norm'])

    ctx_blk = N_LAT // CTX_LEN
    zero_h = jnp.zeros((BATCH, 4, S5_GROUPS * S5_STATE), F32)
    par_ctx, par_lat = _s5_params(lp, (CTX_LEN // S5_LANES, SEQ // S5_LANES))
    ys_ctx, h_ctx = _s5_scan(proj, par_ctx, lp['s5_d'], zero_h, CTX_LEN, ctx_blk)
    ys_lat, _ = _s5_scan(proj, par_lat, lp['s5_d'], h_ctx, SEQ, 0)

    bb = _s5_glu(ys_lat, lp['s5_glu_w'], lp['s5_glu_b'], bb, 0)
    bb = _attn_lat(q, k, v, bb)
    bb = _pool(proj, lp['pool_w'], lp['pool_scale'], bb, SEQ, 0)
    bb = _hyena(proj, lp, hy_lat, bb, SEQ, 0)
    if not last:
        bb = _s5_glu(ys_ctx, lp['s5_glu_w'], lp['s5_glu_b'], bb, N_LAT)
        bb = _attn_ctx(q, k, v, bb)
        bb = _pool(proj, lp['pool_w'], lp['pool_scale'], bb, CTX_LEN, ctx_blk)
        bb = _hyena(proj, lp, hy_ctx, bb, CTX_LEN, N_LAT)

    n_rows = N_LAT if last else MT
    merged = _merge(u, big['w_gate'], lp['b_gate'], bb, big['w_branch'], l, n_rows)
    return _out_proj(merged, big['w_out'], l, x, mods, n_rows), bb


def kernel(x, c, ctx, c_ctx, w_ada, b_ada, norm_ffn1, norm_mix, norm_ffn2, norm_final, ffn1_wi, ffn1_wo, ffn2_wi, ffn2_wo, w_in, w_gate, b_gate, w_branch, w_out, pool_w, pool_scale, q_norm, k_norm, hy_short_w, hy_short_b, hy_f1_w, hy_f1_b, hy_f2_w, hy_f2_b, hy_f3_w, hy_freq, hy_bias, s5_a_re, s5_a_im, s5_log_dt, s5_b_re, s5_b_im, s5_c_re, s5_c_im, s5_d, s5_glu_w, s5_glu_b):
    big = dict(w_in=w_in, w_gate=w_gate, w_branch=w_branch, w_out=w_out)
    per_layer = dict(
        b_gate=b_gate, pool_w=pool_w,
        pool_scale=pool_scale, q_norm=q_norm, k_norm=k_norm, hy_short_w=hy_short_w, hy_short_b=hy_short_b,
        hy_f1_w=hy_f1_w, hy_f1_b=hy_f1_b, hy_f2_w=hy_f2_w, hy_f2_b=hy_f2_b, hy_f3_w=hy_f3_w, hy_freq=hy_freq,
        hy_bias=hy_bias, s5_a_re=s5_a_re, s5_a_im=s5_a_im, s5_log_dt=s5_log_dt, s5_b_re=s5_b_re, s5_b_im=s5_b_im,
        s5_c_re=s5_c_re, s5_c_im=s5_c_im, s5_d=s5_d, s5_glu_w=s5_glu_w, s5_glu_b=s5_glu_b)

    cos, sin = _rope_tables()
    deltas = _hy_deltas()
    consts = (cos, sin,
              (_hy_feats(SEQ), deltas, _dft_matrices(SEQ)),
              (_hy_feats(CTX_LEN), deltas, _dft_matrices(CTX_LEN)))

    cc = jnp.concatenate([c, c_ctx[None], jnp.zeros((8 - BATCH - 1, D_MODEL), F32)], axis=0)
    mods_all = _ada(cc, w_ada, b_ada).reshape(DEPTH, 8, N_MOD, 1, D_MODEL)

    xs = jnp.concatenate([x.reshape(N_LAT, D_MODEL), ctx.reshape(N_CTX, D_MODEL)], axis=0)
    bb = jnp.zeros((N_BRANCH, MT, BRANCH_WIDTH), BF16)
    for l in range(DEPTH):
        last = l == DEPTH - 1
        lp = {name: w[l] for name, w in per_layer.items()}
        mods = mods_all[l]
        xs, w_in_bf16 = _ffn(xs, mods, norm_ffn1[l], ffn1_wi, ffn1_wo, l, 0, MT, side=w_in)
        xs, bb = _mixer(xs, norm_mix[l], lp, big, w_in_bf16, l, mods, consts, bb, last)
        xs = _ffn(xs, mods, norm_ffn2[l], ffn2_wi, ffn2_wo, l, 6, N_LAT if last else MT)
    return _final_norm(xs, norm_final, N_LAT).reshape(BATCH, SEQ, D_MODEL)
```

```python
import functools
import math

import jax
import jax.numpy as jnp
from jax import lax
from jax.experimental import pallas as pl
from jax.experimental.pallas import tpu as pltpu

F32 = jnp.float32
BF16 = jnp.bfloat16

D_MODEL = 2048
BATCH = 4
SEQ = 2048
DEPTH = 4
GRID_W = 64
CTX_LEN = 256
D_FF = 5632
N_MOD = 9
EPS = 1e-6

POOL_WINDOWS = (2, 4, 8, 16)
HEAD_DIM = 128
N_Q_HEADS = 4
N_KV_HEADS = 2
Q_GROUP = N_Q_HEADS // N_KV_HEADS
ROPE_THETA = 10000.0
ROPE_FREQS = HEAD_DIM // 4

HYENA_WIDTH = 512
HYENA_EMB = 33
HYENA_BANDS = (HYENA_EMB - 1) // 2
HYENA_HIDDEN = 64
HYENA_TARGET = 1e-2
HYENA_FAST_PCT = 0.3
HYENA_SLOW_PCT = 1.5

S5_WIDTH = 512
S5_GC = 16
S5_GROUPS = S5_WIDTH // S5_GC
S5_STATE = 64
S5_LANES = 8
S5_CHUNK_GROUPS = 8
S5_CHUNK_CH = S5_CHUNK_GROUPS * S5_GC
S5_CHUNK_ST = S5_CHUNK_GROUPS * S5_STATE
S5_NCHUNK = S5_GROUPS // S5_CHUNK_GROUPS

N_BRANCH = 4
BRANCH_WIDTH = 512
LANE = 128
COL = 512

OFF_Q = 512
OFF_KV = 1024
OFF_HY = 1536
OFF_S5 = 3072
IN_WIDTH = 3584

N_LAT = BATCH * SEQ
N_CTX = BATCH * CTX_LEN
MT = N_LAT + N_CTX

TM = 1024
TM_EW = 512
VMEM_LIMIT = 56 * 1024 * 1024


def _cparams(*sem):
    return pltpu.CompilerParams(dimension_semantics=sem, vmem_limit_bytes=VMEM_LIMIT)


def _dot(a, b):
    return jnp.dot(a, b, preferred_element_type=F32)


SLOT_POOL, SLOT_ATTN, SLOT_HYENA, SLOT_S5 = range(4)
_ANY = pl.BlockSpec(memory_space=pl.ANY)


def _keep_branch_buffer(body, n_in):
    def wrapped(*refs):
        return body(*refs[:n_in], *refs[n_in + 1:])
    return wrapped


def _mod_row(i, tm):
    return jnp.minimum((i * tm) // SEQ, BATCH)


def _mod_spec(tm, k):
    return pl.BlockSpec((None, None, 1, D_MODEL), lambda i, *_: (_mod_row(i, tm), k, 0, 0))


def _ada_body(c_ref, w_ref, b_ref, o_ref):
    c = c_ref[...]
    a = (c * jax.nn.sigmoid(c)).astype(BF16)
    o_ref[...] = _dot(a, w_ref[...].astype(BF16)) + b_ref[...]


def _ada(cc, w_ada, b_ada):
    tn = 1024
    nw = N_MOD * D_MODEL
    return pl.pallas_call(
        _ada_body,
        grid=(DEPTH, nw // tn),
        in_specs=[
            pl.BlockSpec((8, D_MODEL), lambda l, j: (0, 0)),
            pl.BlockSpec((None, D_MODEL, tn), lambda l, j: (l, 0, j)),
            pl.BlockSpec((None, 1, tn), lambda l, j: (l, 0, j)),
        ],
        out_specs=pl.BlockSpec((None, 8, tn), lambda l, j: (l, 0, j)),
        out_shape=jax.ShapeDtypeStruct((DEPTH, 8, nw), F32),
        compiler_params=_cparams("arbitrary", "arbitrary"),
        name="ada",
    )(cc, w_ada, b_ada.reshape(DEPTH, 1, nw))


NORM_ROWS = 16


def _rows_loop(n_rows, step):
    for c in range(n_rows // NORM_ROWS):
        step(pl.ds(c * NORM_ROWS, NORM_ROWS))


def _rms_scale_rows(x_ref, r_ref, scale, shift, o_ref):
    def stats(rows):
        x = x_ref[rows, :]
        r_ref[rows, :] = lax.rsqrt(jnp.mean(x * x, axis=-1, keepdims=True) + EPS)

    def apply(rows):
        y = x_ref[rows, :] * r_ref[rows, :] * scale
        if shift is not None:
            y = y + shift
        o_ref[rows, :] = y.astype(o_ref.dtype)

    _rows_loop(x_ref.shape[0], stats)
    _rows_loop(x_ref.shape[0], apply)


def _normmod_body(x_ref, g_ref, sh_ref, sc_ref, o_ref, r_ref):
    _rms_scale_rows(x_ref, r_ref, g_ref[...] * (1.0 + sc_ref[...]), sh_ref[...], o_ref)


def _norm_body(x_ref, g_ref, o_ref, r_ref):
    _rms_scale_rows(x_ref, r_ref, g_ref[...], None, o_ref)


def _normmod(x, g, mods, base, n_rows):
    tm = TM_EW
    return pl.pallas_call(
        _normmod_body,
        grid=(n_rows // tm,),
        in_specs=[
            pl.BlockSpec((tm, D_MODEL), lambda i: (i, 0)),
            pl.BlockSpec((1, D_MODEL), lambda i: (0, 0)),
            _mod_spec(tm, base),
            _mod_spec(tm, base + 1),
        ],
        out_specs=pl.BlockSpec((tm, D_MODEL), lambda i: (i, 0)),
        out_shape=jax.ShapeDtypeStruct((n_rows, D_MODEL), BF16),
        scratch_shapes=[pltpu.VMEM((tm, 1), F32)],
        compiler_params=_cparams("arbitrary"),
        name="normmod",
    )(x, g.reshape(1, D_MODEL), mods, mods)


def _final_norm(x, g, n_rows):
    tm = TM_EW
    return pl.pallas_call(
        _norm_body,
        grid=(n_rows // tm,),
        in_specs=[
            pl.BlockSpec((tm, D_MODEL), lambda i: (i, 0)),
            pl.BlockSpec((1, D_MODEL), lambda i: (0, 0)),
        ],
        out_specs=pl.BlockSpec((tm, D_MODEL), lambda i: (i, 0)),
        out_shape=jax.ShapeDtypeStruct((n_rows, D_MODEL), F32),
        scratch_shapes=[pltpu.VMEM((tm, 1), F32)],
        compiler_params=_cparams("arbitrary"),
        name="final_norm",
    )(x, g.reshape(1, D_MODEL))


MXU_COLS = 256


WO_SLAB = 64
WO_NSLAB = D_FF // WO_SLAB


NEXT_ROWS = 128
NEXT_SLICES = TM // NEXT_ROWS


def _ffn_a_body(u0_ref, xn_ref, g_ref, sh_ref, sc_ref, wa_ref, wb_ref, wo_ref, h_ref, wob_ref, u2_ref, r_ref):
    i, j = pl.program_id(0), pl.program_id(1)
    slot = i % 2

    @pl.when((i == 0) & (j == 0))
    def _():
        u2_ref[0] = u0_ref[...]

    dst = pl.ds(pl.multiple_of(jnp.minimum(j, NEXT_SLICES - 1) * NEXT_ROWS, NEXT_ROWS), NEXT_ROWS)
    _rms_scale_rows(xn_ref, r_ref, g_ref[...] * (1.0 + sc_ref[...]), sh_ref[...], u2_ref.at[1 - slot, dst, :])

    u = u2_ref[slot]
    for c in range(0, h_ref.shape[1], MXU_COLS):
        sl = slice(c, c + MXU_COLS)
        a = _dot(u, wa_ref[:, sl].astype(BF16))
        b = _dot(u, wb_ref[:, sl].astype(BF16))
        h_ref[:, sl] = (a * jax.nn.sigmoid(a) * b).astype(BF16)

    @pl.when(pl.program_id(0) * pl.num_programs(1) + pl.program_id(1) < WO_NSLAB)
    def _():
        wob_ref[...] = wo_ref[...].astype(BF16)


def _ffn_a(x, g, mods, base, wi, wo, l, n_rows):
    tf = 512
    nf = D_FF // tf
    ni = n_rows // TM
    assert ni * nf >= WO_NSLAB and nf >= NEXT_SLICES

    def wo_blk(i, j):
        return jnp.minimum(i * nf + j, WO_NSLAB - 1)

    def nxt(i):
        return jnp.minimum(i + 1, ni - 1)

    def next_mod(k):
        return pl.BlockSpec((None, None, 1, D_MODEL), lambda i, j: (_mod_row(nxt(i), TM), k, 0, 0))

    u0 = _normmod(x, g, mods, base, TM)
    return pl.pallas_call(
        _ffn_a_body,
        grid=(ni, nf),
        in_specs=[
            pl.BlockSpec((TM, D_MODEL), lambda i, j: (0, 0)),
            pl.BlockSpec((NEXT_ROWS, D_MODEL),
                         lambda i, j: (nxt(i) * NEXT_SLICES + jnp.minimum(j, NEXT_SLICES - 1), 0)),
            pl.BlockSpec((1, D_MODEL), lambda i, j: (0, 0)),
            next_mod(base),
            next_mod(base + 1),
            pl.BlockSpec((None, D_MODEL, tf), lambda i, j: (l, 0, j)),
            pl.BlockSpec((None, D_MODEL, tf), lambda i, j: (l, 0, j + nf)),
            pl.BlockSpec((None, WO_SLAB, D_MODEL), lambda i, j: (l, wo_blk(i, j), 0)),
        ],
        out_specs=[
            pl.BlockSpec((TM, tf), lambda i, j: (i, j)),
            pl.BlockSpec((WO_SLAB, D_MODEL), lambda i, j: (wo_blk(i, j), 0)),
        ],
        out_shape=[
            jax.ShapeDtypeStruct((n_rows, D_FF), BF16),
            jax.ShapeDtypeStruct((D_FF, D_MODEL), BF16),
        ],
        scratch_shapes=[pltpu.VMEM((2, TM, D_MODEL), BF16), pltpu.VMEM((NEXT_ROWS, 1), F32)],
        compiler_params=_cparams("arbitrary", "arbitrary"),
        name="ffn_a",
    )(u0, x, g.reshape(1, D_MODEL), mods, mods, wi, wi, wo)


SIDE_SLAB = 64


def _ffn_b_body(h_ref, w_ref, x_ref, g_ref, *rest, n_slab):
    o_ref = rest[-2] if n_slab else rest[-1]
    h = h_ref[...]
    for c in range(0, o_ref.shape[1], MXU_COLS):
        sl = slice(c, c + MXU_COLS)
        o_ref[:, sl] = x_ref[:, sl] + (0.5 * g_ref[:, sl]) * _dot(h, w_ref[:, sl])

    if n_slab:
        side_ref, _, side_out_ref = rest

        @pl.when(pl.program_id(0) * pl.num_programs(1) + pl.program_id(1) < n_slab)
        def _():
            side_out_ref[...] = side_ref[...].astype(BF16)


def _ffn_b(h, wo_bf16, x, mods, gate_idx, n_rows, side=None, l=None):
    tn = 512
    nj = D_MODEL // tn
    gate = pl.BlockSpec((None, None, 1, tn), lambda i, j: (_mod_row(i, TM), gate_idx, 0, j))
    in_specs = [
        pl.BlockSpec((TM, D_FF), lambda i, j: (i, 0)),
        pl.BlockSpec((D_FF, tn), lambda i, j: (0, j)),
        pl.BlockSpec((TM, tn), lambda i, j: (i, j)),
        gate,
    ]
    out_specs = [pl.BlockSpec((TM, tn), lambda i, j: (i, j))]
    out_shape = [jax.ShapeDtypeStruct((n_rows, D_MODEL), F32)]
    args = [h, wo_bf16, x, mods]
    n_slab = 0
    if side is not None:
        _, rows, cols = side.shape
        n_slab = rows // SIDE_SLAB
        assert (n_rows // TM) * nj >= n_slab

        def slab(i, j):
            return jnp.minimum(i * nj + j, n_slab - 1)

        in_specs.append(pl.BlockSpec((None, SIDE_SLAB, cols), lambda i, j: (l, slab(i, j), 0)))
        out_specs.append(pl.BlockSpec((SIDE_SLAB, cols), lambda i, j: (slab(i, j), 0)))
        out_shape.append(jax.ShapeDtypeStruct((rows, cols), BF16))
        args.append(side)
    res = pl.pallas_call(
        functools.partial(_ffn_b_body, n_slab=n_slab),
        grid=(n_rows // TM, nj),
        in_specs=in_specs,
        out_specs=out_specs,
        out_shape=out_shape,
        compiler_params=_cparams("arbitrary", "arbitrary"),
        name="ffn_b",
    )(*args)
    return res if side is not None else res[0]


def _ffn(x, mods, g, wi, wo, l, base, n_rows, side=None):
    h, wo_bf16 = _ffn_a(x, g, mods, base, wi, wo, l, n_rows)
    return _ffn_b(h, wo_bf16, x, mods, base + 2, n_rows, side, l)


def _proj_body(u_ref, w_ref, o_ref):
    o_ref[...] = _dot(u_ref[...], w_ref[...])


def _proj(u, w_in_bf16):
    tn = COL
    tm = 1536
    return pl.pallas_call(
        _proj_body,
        grid=(MT // tm, IN_WIDTH // tn),
        in_specs=[
            pl.BlockSpec((tm, D_MODEL), lambda i, j: (i, 0)),
            pl.BlockSpec((D_MODEL, tn), lambda i, j: (0, j)),
        ],
        out_specs=pl.BlockSpec((tm, tn), lambda i, j: (i, j)),
        out_shape=jax.ShapeDtypeStruct((MT, IN_WIDTH), F32),
        compiler_params=_cparams("arbitrary", "arbitrary"),
        name="proj",
    )(u, w_in_bf16)


def _rope_tables():
    t = jnp.arange(SEQ)
    rows = (t // GRID_W).astype(F32)
    cols = (t % GRID_W).astype(F32)
    freqs = ROPE_THETA ** (-jnp.arange(ROPE_FREQS, dtype=F32) / ROPE_FREQS)
    ar = rows[:, None] * freqs[None, :]
    ac = cols[:, None] * freqs[None, :]
    cos = jnp.concatenate([jnp.cos(ar), jnp.cos(ar), jnp.cos(ac), jnp.cos(ac)], axis=-1)
    sin = jnp.concatenate([-jnp.sin(ar), jnp.sin(ar), -jnp.sin(ac), jnp.sin(ac)], axis=-1)
    cos = jnp.concatenate([cos, jnp.ones((TM_EW, HEAD_DIM), F32)], axis=0)
    sin = jnp.concatenate([sin, jnp.zeros((TM_EW, HEAD_DIM), F32)], axis=0)
    return cos, sin


def _qkv_body(q_ref, kv_ref, cos_ref, sin_ref, qn_ref, kn_ref, qo_ref, ko_ref, vo_ref):
    cos = cos_ref[...]
    sin = sin_ref[...]
    lane = lax.broadcasted_iota(jnp.int32, cos.shape, 1)
    first = (lane % (2 * ROPE_FREQS)) < ROPE_FREQS

    def norm_rope(xh, g):
        y = xh * lax.rsqrt(jnp.mean(xh * xh, axis=-1, keepdims=True) + EPS) * g
        swapped = jnp.where(first, pltpu.roll(y, HEAD_DIM - ROPE_FREQS, 1), pltpu.roll(y, ROPE_FREQS, 1))
        return y * cos + swapped * sin

    scale = math.log2(math.e) / math.sqrt(HEAD_DIM)
    for h in range(N_Q_HEADS):
        sl = slice(h * HEAD_DIM, (h + 1) * HEAD_DIM)
        qo_ref[:, sl] = (norm_rope(q_ref[:, sl], qn_ref[...]) * scale).astype(BF16)
    for h in range(N_KV_HEADS):
        sl = slice(h * HEAD_DIM, (h + 1) * HEAD_DIM)
        ko_ref[:, sl] = norm_rope(kv_ref[:, sl], kn_ref[...]).astype(BF16)
    kvw = N_KV_HEADS * HEAD_DIM
    for h in range(N_KV_HEADS):
        vo_ref[:, 2 * h * HEAD_DIM:(2 * h + 1) * HEAD_DIM] = (
            kv_ref[:, kvw + h * HEAD_DIM:kvw + (h + 1) * HEAD_DIM].astype(BF16))
        vo_ref[:, (2 * h + 1) * HEAD_DIM:(2 * h + 2) * HEAD_DIM] = jnp.ones((vo_ref.shape[0], HEAD_DIM), BF16)


def _qkv(proj, cos, sin, q_norm, k_norm):
    tm = TM_EW
    n_lat_tiles = N_LAT // tm
    per_seq = SEQ // tm
    kvw = N_KV_HEADS * HEAD_DIM

    def tab(i):
        return (jnp.where(i < n_lat_tiles, i % per_seq, per_seq), 0)

    return pl.pallas_call(
        _qkv_body,
        grid=(MT // tm,),
        in_specs=[
            pl.BlockSpec((tm, COL), lambda i: (i, OFF_Q // COL)),
            pl.BlockSpec((tm, COL), lambda i: (i, OFF_KV // COL)),
            pl.BlockSpec((tm, HEAD_DIM), tab),
            pl.BlockSpec((tm, HEAD_DIM), tab),
            pl.BlockSpec((1, HEAD_DIM), lambda i: (0, 0)),
            pl.BlockSpec((1, HEAD_DIM), lambda i: (0, 0)),
        ],
        out_specs=[
            pl.BlockSpec((tm, COL), lambda i: (i, 0)),
            pl.BlockSpec((tm, kvw), lambda i: (i, 0)),
            pl.BlockSpec((tm, 2 * kvw), lambda i: (i, 0)),
        ],
        out_shape=[
            jax.ShapeDtypeStruct((MT, COL), BF16),
            jax.ShapeDtypeStruct((MT, kvw), BF16),
            jax.ShapeDtypeStruct((MT, 2 * kvw), BF16),
        ],
        compiler_params=_cparams("arbitrary"),
        name="qkv",
    )(proj, proj, cos, sin, q_norm.reshape(1, HEAD_DIM), k_norm.reshape(1, HEAD_DIM))


def _attn_body(*refs, with_lat):
    if with_lat:
        q_ref, kl_ref, vl_ref, kc_ref, vc_ref, o_ref = refs
    else:
        q_ref, kc_ref, vc_ref, o_ref = refs
    nt = (((1,), (1,)), ((), ()))
    for g in range(Q_GROUP):
        sl = slice(g * HEAD_DIM, (g + 1) * HEAD_DIM)
        q = q_ref[:, sl]
        sc = lax.dot_general(q, kc_ref[...], nt, preferred_element_type=F32)
        m = jnp.max(sc, axis=-1, keepdims=True)
        if with_lat:
            s_lat = lax.dot_general(q, kl_ref[...], nt, preferred_element_type=F32)
            m = jnp.maximum(m, jnp.max(s_lat, axis=-1, keepdims=True))
            e_lat = jnp.exp2(s_lat - m)
        ec = jnp.exp2(sc - m)
        o = _dot(ec.astype(BF16), vc_ref[...])
        if with_lat:
            o = o + _dot(e_lat.astype(BF16), vl_ref[...])
        o_ref[:, sl] = (o[:, :HEAD_DIM] / o[:, HEAD_DIM:]).astype(BF16)


def _attn_lat(q, k, v, bb):
    tq = 512
    nq = SEQ // tq
    ctx_blk = N_LAT // CTX_LEN
    gw = Q_GROUP * HEAD_DIM
    return pl.pallas_call(
        _keep_branch_buffer(functools.partial(_attn_body, with_lat=True), 5),
        grid=(BATCH, N_KV_HEADS, nq),
        in_specs=[
            pl.BlockSpec((tq, gw), lambda b, h, i: (b * nq + i, h)),
            pl.BlockSpec((SEQ, HEAD_DIM), lambda b, h, i: (b, h)),
            pl.BlockSpec((SEQ, 2 * HEAD_DIM), lambda b, h, i: (b, h)),
            pl.BlockSpec((CTX_LEN, HEAD_DIM), lambda b, h, i: (ctx_blk + b, h)),
            pl.BlockSpec((CTX_LEN, 2 * HEAD_DIM), lambda b, h, i: (ctx_blk + b, h)),
            _ANY,
        ],
        out_specs=pl.BlockSpec((None, tq, gw), lambda b, h, i: (SLOT_ATTN, b * nq + i, h)),
        out_shape=jax.ShapeDtypeStruct(bb.shape, bb.dtype),
        input_output_aliases={5: 0},
        compiler_params=_cparams("arbitrary", "arbitrary", "arbitrary"),
        name="attn_lat",
    )(q, k, v, k, v, bb)


def _attn_ctx(q, k, v, bb):
    ctx_blk = N_LAT // CTX_LEN
    gw = Q_GROUP * HEAD_DIM
    return pl.pallas_call(
        _keep_branch_buffer(functools.partial(_attn_body, with_lat=False), 3),
        grid=(BATCH, N_KV_HEADS),
        in_specs=[
            pl.BlockSpec((CTX_LEN, gw), lambda b, h: (ctx_blk + b, h)),
            pl.BlockSpec((CTX_LEN, HEAD_DIM), lambda b, h: (ctx_blk + b, h)),
            pl.BlockSpec((CTX_LEN, 2 * HEAD_DIM), lambda b, h: (ctx_blk + b, h)),
            _ANY,
        ],
        out_specs=pl.BlockSpec((None, CTX_LEN, gw), lambda b, h: (SLOT_ATTN, ctx_blk + b, h)),
        out_shape=jax.ShapeDtypeStruct(bb.shape, bb.dtype),
        input_output_aliases={3: 0},
        compiler_params=_cparams("arbitrary", "arbitrary"),
        name="attn_ctx",
    )(q, k, v, bb)


POOL_PAD = 8


def _pool_body(a_ref, w_ref, s_ref, o_ref, *, seq):
    lp = seq + 2 * POOL_PAD
    t = lax.broadcasted_iota(jnp.int32, (seq, LANE), 0)
    zpad = jnp.zeros((POOL_PAD, LANE), F32)
    for gi, win in enumerate(POOL_WINDOWS):
        sl = slice(gi * LANE, (gi + 1) * LANE)
        a = a_ref[:, sl]
        s = jnp.concatenate([zpad, a, zpad], axis=0)
        s = s + pltpu.roll(s, 1, 0)
        half = 1
        while 2 * half < win:
            s = pltpu.roll(s, half, 0) + pltpu.roll(s, lp - half, 0)
            half *= 2
        s = s[POOL_PAD:POOL_PAD + seq]
        lo = jnp.maximum(t - win // 2, 0)
        hi = jnp.minimum(t + win // 2, seq)
        pooled = s / (hi - lo).astype(F32) - a
        y = _dot(pooled.astype(BF16), w_ref[gi].astype(BF16))
        o_ref[:, sl] = (y * s_ref[:, sl]).astype(BF16)


def _pool(proj, pool_w, pool_scale, bb, seq, row_blk0):
    width = len(POOL_WINDOWS) * LANE
    return pl.pallas_call(
        _keep_branch_buffer(functools.partial(_pool_body, seq=seq), 3),
        grid=(BATCH,),
        in_specs=[
            pl.BlockSpec((seq, width), lambda b: (row_blk0 + b, 0)),
            pl.BlockSpec((len(POOL_WINDOWS), LANE, LANE), lambda b: (0, 0, 0)),
            pl.BlockSpec((1, width), lambda b: (0, 0)),
            _ANY,
        ],
        out_specs=pl.BlockSpec((None, seq, width), lambda b: (SLOT_POOL, row_blk0 + b, 0)),
        out_shape=jax.ShapeDtypeStruct(bb.shape, bb.dtype),
        input_output_aliases={3: 0},
        compiler_params=_cparams("arbitrary"),
        name="pool",
    )(proj, pool_w, pool_scale.reshape(1, width), bb)


def _hy_prep_body(x0_ref, x1_ref, v_ref, w0_ref, w1_ref, wv_ref, b0_ref, b1_ref, bv_ref,
                  x0o_ref, vx_ref, vb_ref, *, seq):
    t = lax.broadcasted_iota(jnp.int32, x0_ref.shape, 0)

    def conv(x_ref, w_ref, b_ref):
        x = x_ref[...]
        prev = jnp.where(t >= 1, pltpu.roll(x, 1, 0), 0.0)
        nxt = jnp.where(t <= seq - 2, pltpu.roll(x, seq - 1, 0), 0.0)
        return prev * w_ref[0:1, :] + x * w_ref[1:2, :] + nxt * w_ref[2:3, :] + b_ref[...]

    x0o_ref[...] = conv(x0_ref, w0_ref, b0_ref)
    vx = conv(v_ref, wv_ref, bv_ref) * conv(x1_ref, w1_ref, b1_ref)
    vx_ref[...] = vx
    vb_ref[...] = vx.astype(BF16)


def _hy_prep(proj, short_w, short_b, seq, row_blk0):
    tc = 2 * LANE
    nc = HYENA_WIDTH // tc
    c0 = OFF_HY // tc
    short_b = short_b.reshape(1, 3 * HYENA_WIDTH)

    def xspec(part):
        return pl.BlockSpec((seq, tc), lambda b, c: (row_blk0 + b, c0 + part * nc + c))

    def wspec(part, rows):
        return pl.BlockSpec((rows, tc), lambda b, c: (0, part * nc + c))

    return pl.pallas_call(
        functools.partial(_hy_prep_body, seq=seq),
        grid=(BATCH, nc),
        in_specs=[xspec(0), xspec(1), xspec(2), wspec(0, 3), wspec(1, 3), wspec(2, 3),
                  wspec(0, 1), wspec(1, 1), wspec(2, 1)],
        out_specs=[
            pl.BlockSpec((seq, tc), lambda b, c: (b, c)),
            pl.BlockSpec((seq, tc), lambda b, c: (b, c)),
            pl.BlockSpec((seq, tc), lambda b, c: (0, b * nc + c)),
        ],
        out_shape=[
            jax.ShapeDtypeStruct((BATCH * seq, HYENA_WIDTH), F32),
            jax.ShapeDtypeStruct((BATCH * seq, HYENA_WIDTH), F32),
            jax.ShapeDtypeStruct((seq, BATCH * HYENA_WIDTH), BF16),
        ],
        compiler_params=_cparams("arbitrary", "arbitrary"),
        name="hy_prep",
    )(proj, proj, proj, short_w, short_w, short_w, short_b, short_b, short_b)


def _hy_feats(seq):
    t = jnp.linspace(0.0, 1.0, seq, dtype=F32)[:, None]
    f = jnp.linspace(1e-4, HYENA_BANDS - 1, HYENA_BANDS, dtype=F32)
    w = 2.0 * math.pi * jnp.arange(seq, dtype=F32) / seq
    fw = w[:, None] * f[None, :]
    z = jnp.concatenate([t, jnp.cos(fw), -jnp.sin(fw)], axis=-1)
    return jnp.pad(z, ((0, 0), (0, LANE - HYENA_EMB)))


def _hy_deltas():
    max_decay = math.log(HYENA_TARGET) / HYENA_FAST_PCT
    min_decay = math.log(HYENA_TARGET) / HYENA_SLOW_PCT
    return jnp.abs(jnp.linspace(min_decay, max_decay, HYENA_WIDTH, dtype=F32)).reshape(1, HYENA_WIDTH)


def _hy_filter_body(z_ref, w1_ref, b1_ref, w2_ref, b2_ref, w3_ref, fr_ref, dl_ref, k_ref, nyq_ref, *, seq):
    hp = lax.Precision.HIGHEST
    freq = fr_ref[...]
    h = jnp.sin(freq * (jnp.dot(z_ref[...], w1_ref[...], precision=hp, preferred_element_type=F32) + b1_ref[...]))
    h = jnp.sin(freq * (jnp.dot(h, w2_ref[...], precision=hp, preferred_element_type=F32) + b2_ref[...]))
    h = jnp.dot(h, w3_ref[...], precision=hp, preferred_element_type=F32)
    ti = lax.broadcasted_iota(jnp.int32, (seq, HYENA_WIDTH), 0)
    decay = jnp.exp(-(ti.astype(F32) * (1.0 / (seq - 1))) * dl_ref[...])
    hf = h[:, :HYENA_WIDTH] * decay
    hb = jnp.where(ti == 0, 0.0, h[:, HYENA_WIDTH:] * decay)
    ks = hf + hb
    k_ref[:, :HYENA_WIDTH] = ks.astype(BF16)
    k_ref[:, HYENA_WIDTH:] = (hf - hb).astype(BF16)
    nyq = jnp.sum(jnp.where(ti % 2 == 0, ks, -ks), axis=0, keepdims=True)
    nyq_ref[...] = jnp.broadcast_to(nyq, nyq_ref.shape)


def _hy_filter(lp, z, deltas, seq):
    w1 = jnp.pad(lp['hy_f1_w'], ((0, LANE - HYENA_EMB), (0, 0)))
    args = (z, w1, lp['hy_f1_b'].reshape(1, -1), lp['hy_f2_w'], lp['hy_f2_b'].reshape(1, -1), lp['hy_f3_w'],
            lp['hy_freq'].reshape(1, -1), deltas)
    full = lambda a: pl.BlockSpec(a.shape, lambda i: (0,) * a.ndim)
    return pl.pallas_call(
        functools.partial(_hy_filter_body, seq=seq),
        grid=(1,),
        in_specs=[full(a) for a in args],
        out_specs=[pl.BlockSpec((seq, 2 * HYENA_WIDTH), lambda i: (0, 0)),
                   pl.BlockSpec((8, HYENA_WIDTH), lambda i: (0, 0))],
        out_shape=[jax.ShapeDtypeStruct((seq, 2 * HYENA_WIDTH), BF16),
                   jax.ShapeDtypeStruct((8, HYENA_WIDTH), F32)],
        compiler_params=_cparams("arbitrary"),
        name="hy_filter",
    )(*args)


DFT_SPLIT = 64


def _dft_matrices(seq):
    n = 2 * seq
    f = jnp.arange(seq, dtype=jnp.int32)[:, None]

    def table(step, count):
        idx = (f * (jnp.arange(count, dtype=jnp.int32)[None, :] * step)) % n
        ang = idx.astype(F32) * (2.0 * math.pi / n)
        return jnp.cos(ang), jnp.sin(ang)

    hc, hs = table(DFT_SPLIT, seq // DFT_SPLIT)
    lc, ls = table(1, DFT_SPLIT)
    cos = (hc[:, :, None] * lc[:, None, :] - hs[:, :, None] * ls[:, None, :]).reshape(seq, seq)
    sin = (hs[:, :, None] * lc[:, None, :] + hc[:, :, None] * ls[:, None, :]).reshape(seq, seq)
    s = jnp.arange(seq, dtype=jnp.int32)[None, :]
    nyq = jnp.where(s % 2 == 0, 1.0, -1.0).astype(F32)
    wf = jnp.concatenate([cos, jnp.where(f == 0, nyq, -sin)], axis=0).astype(BF16)
    return wf, wf.T


def _mm_body(a_ref, b_ref, o_ref):
    o_ref[...] = _dot(a_ref[...], b_ref[...])


def _dft_fwd_body(wc_ref, ws_ref, x_ref, hr_ref, hi_ref, nyq_ref, pr_ref, pi_ref, *, n):
    x = x_ref[...]
    xr = _dot(wc_ref[...], x)
    xi = _dot(ws_ref[...], x)
    hr = hr_ref[...]
    hi = hi_ref[...]
    freq = lax.broadcasted_iota(jnp.int32, xr.shape, 0) + pl.program_id(0) * xr.shape[0]
    first = freq == 0
    w = jnp.where(first, 1.0 / n, 2.0 / n)
    pr_ref[...] = (jnp.where(first, xr * hr, xr * hr - xi * hi) * w).astype(BF16)
    pi_ref[...] = (jnp.where(first, xi * nyq_ref[0:1, :], xr * hi + xi * hr) * w).astype(BF16)


def _dft_fwd(wf, x, hf, nyq):
    n, seq = wf.shape
    tf = min(seq, 512)
    nf = seq // tf
    out = jax.ShapeDtypeStruct((seq, BATCH * HYENA_WIDTH), BF16)
    return pl.pallas_call(
        functools.partial(_dft_fwd_body, n=n),
        grid=(nf, BATCH),
        in_specs=[
            pl.BlockSpec((tf, seq), lambda f, b: (f, 0)),
            pl.BlockSpec((tf, seq), lambda f, b: (nf + f, 0)),
            pl.BlockSpec((seq, HYENA_WIDTH), lambda f, b: (0, b)),
            pl.BlockSpec((tf, HYENA_WIDTH), lambda f, b: (f, 0)),
            pl.BlockSpec((tf, HYENA_WIDTH), lambda f, b: (nf + f, 0)),
            pl.BlockSpec((8, HYENA_WIDTH), lambda f, b: (0, 0)),
        ],
        out_specs=[pl.BlockSpec((tf, HYENA_WIDTH), lambda f, b: (f, b))] * 2,
        out_shape=[out, out],
        compiler_params=_cparams("arbitrary", "arbitrary"),
        name="dft_fwd",
    )(wf, wf, x, hf, hf, nyq)


def _dft_filter(wf, k):
    n, seq = wf.shape
    tm = min(seq, 1024)
    return pl.pallas_call(
        _mm_body,
        grid=(n // tm,),
        in_specs=[
            pl.BlockSpec((tm, seq), lambda i: (i, 0)),
            pl.BlockSpec((seq, HYENA_WIDTH), lambda i: (0, (i * tm) // seq)),
        ],
        out_specs=pl.BlockSpec((tm, HYENA_WIDTH), lambda i: (i, 0)),
        out_shape=jax.ShapeDtypeStruct((n, HYENA_WIDTH), F32),
        compiler_params=_cparams("arbitrary"),
        name="dft_filter",
    )(wf, k)


def _dft_inv_body(wc_ref, ws_ref, pr_ref, pi_ref, vx_ref, bias_ref, x0_ref, o_ref):
    y = _dot(wc_ref[...], pr_ref[...]) + _dot(ws_ref[...], pi_ref[...])
    o_ref[...] = ((y + vx_ref[...] * bias_ref[...]) * x0_ref[...]).astype(BF16)


def _dft_inv(wi, p_re, p_im, vx, bias, x0, bb, row0):
    seq, n = wi.shape
    tm = min(seq, 1024)
    tn = HYENA_WIDTH
    nt = seq // tm
    return pl.pallas_call(
        _keep_branch_buffer(_dft_inv_body, 7),
        grid=(nt, BATCH),
        in_specs=[
            pl.BlockSpec((tm, seq), lambda i, b: (i, 0)),
            pl.BlockSpec((tm, seq), lambda i, b: (i, 1)),
            pl.BlockSpec((seq, tn), lambda i, b: (0, b)),
            pl.BlockSpec((seq, tn), lambda i, b: (0, b)),
            pl.BlockSpec((tm, tn), lambda i, b: (b * nt + i, 0)),
            pl.BlockSpec((1, tn), lambda i, b: (0, 0)),
            pl.BlockSpec((tm, tn), lambda i, b: (b * nt + i, 0)),
            _ANY,
        ],
        out_specs=pl.BlockSpec((None, tm, tn), lambda i, b: (SLOT_HYENA, row0 // tm + b * nt + i, 0)),
        out_shape=jax.ShapeDtypeStruct(bb.shape, bb.dtype),
        input_output_aliases={7: 0},
        compiler_params=_cparams("arbitrary", "arbitrary"),
        name="dft_inv",
    )(wi, wi, p_re, p_im, vx, bias.reshape(1, tn), x0, bb)


def _hyena(proj, lp, consts, bb, seq, row0):
    z, deltas, (wf, wi) = consts
    x0, vx, vb = _hy_prep(proj, lp['hy_short_w'], lp['hy_short_b'], seq, row0 // seq)
    k, nyq = _hy_filter(lp, z, deltas, seq)
    hf = _dft_filter(wf, k)
    p_re, p_im = _dft_fwd(wf, vb, hf, nyq)
    return _dft_inv(wi, p_re, p_im, vx, lp['hy_bias'], x0, bb, row0)


def _s5_params(lp, n_seg_steps):
    a_re, a_im = lp['s5_a_re'], lp['s5_a_im']
    dt = jnp.exp(lp['s5_log_dt'])[..., None]
    mag = jnp.exp(a_re * dt)
    ab_re, ab_im = mag * jnp.cos(a_im * dt), mag * jnp.sin(a_im * dt)
    den = a_re * a_re + a_im * a_im
    nr, ni = ab_re - 1.0, ab_im
    cf_re = (nr * a_re + ni * a_im) / den
    cf_im = (ni * a_re - nr * a_im) / den
    b_re, b_im = lp['s5_b_re'], lp['s5_b_im']
    bb_re = cf_re[..., None] * b_re - cf_im[..., None] * b_im
    bb_im = cf_re[..., None] * b_im + cf_im[..., None] * b_re
    eye = jnp.eye(S5_CHUNK_GROUPS, dtype=F32)

    def bdiag_in(m):
        m = m.reshape(2, S5_NCHUNK, S5_CHUNK_GROUPS, S5_STATE, S5_GC)
        return jnp.einsum('dqgpc,gh->dqgchp', m, eye).reshape(2, S5_NCHUNK, S5_CHUNK_CH, S5_CHUNK_ST)

    def bdiag_out(m):
        m = m.reshape(2, S5_NCHUNK, S5_CHUNK_GROUPS, S5_GC, S5_STATE)
        return jnp.einsum('dqgcp,gh->dqhpgc', m, eye).reshape(2, S5_NCHUNK, S5_CHUNK_ST, S5_CHUNK_CH)

    bbd = jnp.concatenate([bdiag_in(bb_re), bdiag_in(bb_im)], axis=-1).astype(BF16)
    cbd = jnp.concatenate([bdiag_out(lp['s5_c_re']), -bdiag_out(lp['s5_c_im'])], axis=-2).astype(BF16)
    a = jnp.stack([ab_re.reshape(2, -1), ab_im.reshape(2, -1)], axis=1)
    aks = []
    for steps in n_seg_steps:
        pr, pi = ab_re, ab_im
        for _ in range(int(math.log2(steps))):
            pr, pi = pr * pr - pi * pi, 2.0 * pr * pi
        aks.append(jnp.stack([pr.reshape(2, -1), pi.reshape(2, -1)], axis=1))
    return [(bbd, cbd, a, ak) for ak in aks]


S5_BLOCK = 256


def _s5_scan_body(u_ref, bbd_ref, cbd_ref, a_ref, ak_ref, dsk_ref, h0_ref, y_ref, hl_ref,
                  xr0_ref, xr1_ref, xi0_ref, xi1_ref, yd0_ref, yd1_ref, up_ref, ub_ref, *, seq):
    nk = seq // S5_LANES
    nblk = seq // S5_BLOCK
    tiles = S5_BLOCK // S5_LANES
    shape = (S5_LANES, S5_CHUNK_ST)
    xr_ref, xi_ref, yd_ref = (xr0_ref, xr1_ref), (xi0_ref, xi1_ref), (yd0_ref, yd1_ref)
    for j in range(S5_LANES):
        up_ref[pl.ds(j, nk, stride=S5_LANES), :] = u_ref[pl.ds(j * nk, nk), :]
    ub_ref[...] = up_ref[...].astype(BF16)
    row = lax.broadcasted_iota(jnp.int32, shape, 0)
    zero = jnp.zeros(shape, F32)

    def block(r):
        return pl.ds(pl.multiple_of(r * S5_BLOCK, S5_BLOCK), S5_BLOCK)

    def in_proj(d, rows):
        xr_ref[d][rows, :] = _dot(ub_ref[rows, :], bbd_ref[d, :, :S5_CHUNK_ST])
        xi_ref[d][rows, :] = _dot(ub_ref[rows, :], bbd_ref[d, :, S5_CHUNK_ST:])

    def out_proj(d, rows):
        yd_ref[d][rows, :] = (_dot(xr_ref[d][rows, :].astype(BF16), cbd_ref[d, :S5_CHUNK_ST, :])
                              + _dot(xi_ref[d][rows, :].astype(BF16), cbd_ref[d, S5_CHUNK_ST:, :]))

    def coeffs(d):
        return jnp.broadcast_to(a_ref[d, 0:1, :], shape), jnp.broadcast_to(a_ref[d, 1:2, :], shape)

    def block_tiles(d, r):
        base = pl.multiple_of(r * S5_BLOCK, S5_BLOCK)
        order = range(tiles) if d == 0 else range(tiles - 1, -1, -1)
        return [pl.ds(base + t * S5_LANES, S5_LANES) for t in order]

    def scan_block(d, r, carry):
        ar, ai = coeffs(d)
        xr, xi = carry
        for rows in block_tiles(d, r):
            xr, xi = (ar * xr - ai * xi + xr_ref[d][rows, :], ar * xi + ai * xr + xi_ref[d][rows, :])
            xr_ref[d][rows, :] = xr
            xi_ref[d][rows, :] = xi
        return xr, xi

    def fix_block(d, r, carry):
        ar, ai = coeffs(d)
        gr, gi = carry
        for rows in block_tiles(d, r):
            gr, gi = ar * gr - ai * gi, ar * gi + ai * gr
            xr_ref[d][rows, :] += gr
            xi_ref[d][rows, :] += gi
        return gr, gi

    def visit(d, s):
        return s if d == 0 else nblk - 1 - s

    def entering_states(d, er, ei):
        akr, aki = ak_ref[d, 0:1, :], ak_ref[d, 1:2, :]
        hr, hi = h0_ref[2 * d:2 * d + 1, :], h0_ref[2 * d + 1:2 * d + 2, :]
        in_r, in_i = zero, zero
        for j in (range(S5_LANES) if d == 0 else range(S5_LANES - 1, -1, -1)):
            in_r = jnp.where(row == j, hr, in_r)
            in_i = jnp.where(row == j, hi, in_i)
            hr, hi = (akr * hr - aki * hi + er[j:j + 1, :], akr * hi + aki * hr + ei[j:j + 1, :])
        hl_ref[2 * d:2 * d + 1, :] = hr
        hl_ref[2 * d + 1:2 * d + 2, :] = hi
        return in_r, in_i

    def fix_all(d, ins):
        lax.fori_loop(0, nblk, lambda s, carry: fix_block(d, visit(d, s), carry), ins)

    in_proj(0, slice(None))

    def scan0_body(s, carry):
        in_proj(1, block(s))
        return scan_block(0, s, carry)
    ends0 = lax.fori_loop(0, nblk, scan0_body, (zero, zero))
    fix_all(0, entering_states(0, *ends0))

    def scan1_body(s, carry):
        out_proj(0, block(s))
        return scan_block(1, visit(1, s), carry)
    ends1 = lax.fori_loop(0, nblk, scan1_body, (zero, zero))
    fix_all(1, entering_states(1, *ends1))
    out_proj(1, slice(None))

    up_ref[...] = yd_ref[0][...] + yd_ref[1][...] + up_ref[...] * (dsk_ref[0] + dsk_ref[1])
    for j in range(S5_LANES):
        y_ref[pl.ds(j * nk, nk), :] = up_ref[pl.ds(j, nk, stride=S5_LANES), :]


def _s5_scan(proj, params, dskip, h0, seq, row_blk0):
    bbd, cbd, a, ak = params
    nq = S5_NCHUNK
    col0 = OFF_S5 // S5_CHUNK_CH
    return pl.pallas_call(
        functools.partial(_s5_scan_body, seq=seq),
        grid=(BATCH, nq),
        in_specs=[
            pl.BlockSpec((seq, S5_CHUNK_CH), lambda b, q: (row_blk0 + b, col0 + q)),
            pl.BlockSpec((2, None, S5_CHUNK_CH, 2 * S5_CHUNK_ST), lambda b, q: (0, q, 0, 0)),
            pl.BlockSpec((2, None, 2 * S5_CHUNK_ST, S5_CHUNK_CH), lambda b, q: (0, q, 0, 0)),
            pl.BlockSpec((2, 2, S5_CHUNK_ST), lambda b, q: (0, 0, q)),
            pl.BlockSpec((2, 2, S5_CHUNK_ST), lambda b, q: (0, 0, q)),
            pl.BlockSpec((2, 1, S5_CHUNK_CH), lambda b, q: (0, 0, q)),
            pl.BlockSpec((None, 4, S5_CHUNK_ST), lambda b, q: (b, 0, q)),
        ],
        out_specs=[
            pl.BlockSpec((seq, S5_CHUNK_CH), lambda b, q: (b, q)),
            pl.BlockSpec((None, 4, S5_CHUNK_ST), lambda b, q: (b, 0, q)),
        ],
        out_shape=[
            jax.ShapeDtypeStruct((BATCH * seq, S5_WIDTH), F32),
            jax.ShapeDtypeStruct((BATCH, 4, S5_GROUPS * S5_STATE), F32),
        ],
        scratch_shapes=[pltpu.VMEM((seq, S5_CHUNK_ST), F32)] * 4 + [pltpu.VMEM((seq, S5_CHUNK_CH), F32)] * 3
        + [pltpu.VMEM((seq, S5_CHUNK_CH), BF16)],
        compiler_params=_cparams("arbitrary", "arbitrary"),
        name="s5_scan",
    )(proj, bbd, cbd, a, ak, dskip.reshape(2, 1, S5_WIDTH), h0)


def _s5_glu_body(y_ref, w_ref, b_ref, o_ref):
    g = _dot(jax.nn.gelu(y_ref[...]).astype(BF16), w_ref[...].astype(BF16)) + b_ref[...]
    o_ref[...] = (g[:, :S5_WIDTH] * jax.nn.sigmoid(g[:, S5_WIDTH:])).astype(BF16)


def _s5_glu(y, w, b, bb, row0):
    n_rows = y.shape[0]
    tm = TM_EW
    return pl.pallas_call(
        _keep_branch_buffer(_s5_glu_body, 3),
        grid=(n_rows // tm,),
        in_specs=[
            pl.BlockSpec((tm, S5_WIDTH), lambda i: (i, 0)),
            pl.BlockSpec((S5_WIDTH, 2 * S5_WIDTH), lambda i: (0, 0)),
            pl.BlockSpec((1, 2 * S5_WIDTH), lambda i: (0, 0)),
            _ANY,
        ],
        out_specs=pl.BlockSpec((None, tm, S5_WIDTH), lambda i: (SLOT_S5, row0 // tm + i, 0)),
        out_shape=jax.ShapeDtypeStruct(bb.shape, bb.dtype),
        input_output_aliases={3: 0},
        compiler_params=_cparams("arbitrary"),
        name="s5_glu",
    )(y, w, b.reshape(1, -1), bb)


def _merge_body(u_ref, *refs):
    wg_refs, bg_refs, y_refs, wb_refs = (refs[k * N_BRANCH:(k + 1) * N_BRANCH] for k in range(4))
    o_ref = refs[4 * N_BRANCH]
    u = u_ref[...]
    acc = None
    for n in range(N_BRANCH):
        gate = jax.nn.sigmoid(_dot(u, wg_refs[n][...].astype(BF16)) + bg_refs[n][...])
        contrib = gate * _dot(y_refs[n][...], wb_refs[n][...].astype(BF16))
        acc = contrib if acc is None else acc + contrib
    o_ref[...] = acc.astype(BF16)


def _merge(u, w_gate, b_gate, branches, w_branch, l, n_rows):
    tc = MXU_COLS
    ncol = D_MODEL // tc
    b_gate = b_gate.reshape(1, -1)
    per_branch = lambda make: [make(n) for n in range(N_BRANCH)]
    return pl.pallas_call(
        _merge_body,
        grid=(n_rows // TM, ncol),
        in_specs=[pl.BlockSpec((TM, D_MODEL), lambda i, c: (i, 0))]
        + per_branch(lambda n: pl.BlockSpec((None, D_MODEL, tc), lambda i, c: (l, 0, n * ncol + c)))
        + per_branch(lambda n: pl.BlockSpec((1, tc), lambda i, c: (0, n * ncol + c)))
        + per_branch(lambda n: pl.BlockSpec((None, TM, BRANCH_WIDTH), lambda i, c: (n, i, 0)))
        + per_branch(lambda n: pl.BlockSpec((None, None, BRANCH_WIDTH, tc), lambda i, c: (l, n, 0, c))),
        out_specs=pl.BlockSpec((TM, tc), lambda i, c: (i, c)),
        out_shape=jax.ShapeDtypeStruct((n_rows, D_MODEL), BF16),
        compiler_params=_cparams("arbitrary", "arbitrary"),
        name="merge",
    )(u, *([w_gate] * N_BRANCH), *([b_gate] * N_BRANCH), *([branches] * N_BRANCH), *([w_branch] * N_BRANCH))


def _out_body(m_ref, w_ref, x_ref, g_ref, o_ref):
    o_ref[...] = x_ref[...] + g_ref[...] * _dot(m_ref[...], w_ref[...].astype(BF16))


def _out_proj(merged, w_out, l, x, mods, n_rows):
    tn = 1024
    gate = pl.BlockSpec((None, None, 1, tn), lambda j, i: (_mod_row(i, TM), 5, 0, j))
    return pl.pallas_call(
        _out_body,
        grid=(D_MODEL // tn, n_rows // TM),
        in_specs=[
            pl.BlockSpec((TM, D_MODEL), lambda j, i: (i, 0)),
            pl.BlockSpec((None, D_MODEL, tn), lambda j, i: (l, 0, j)),
            pl.BlockSpec((TM, tn), lambda j, i: (i, j)),
            gate,
        ],
        out_specs=pl.BlockSpec((TM, tn), lambda j, i: (i, j)),
        out_shape=jax.ShapeDtypeStruct((n_rows, D_MODEL), F32),
        compiler_params=_cparams("arbitrary", "arbitrary"),
        name="out_proj",
    )(merged, w_out, x, mods)


def _mixer(x, g, lp, big, w_in_bf16, l, mods, consts, bb, last):
    cos, sin, hy_lat, hy_ctx = consts
    u = _normmod(x, g, mods, 3, MT)
    proj = _proj(u, w_in_bf16)
    q, k, v = _qkv(proj, cos, sin, lp['q_norm'], lp['k_norm'])

    ctx_blk = N_LAT // CTX_LEN
    zero_h = jnp.zeros((BATCH, 4, S5_GROUPS * S5_STATE), F32)
    par_ctx, par_lat = _s5_params(lp, (CTX_LEN // S5_LANES, SEQ // S5_LANES))
    ys_ctx, h_ctx = _s5_scan(proj, par_ctx, lp['s5_d'], zero_h, CTX_LEN, ctx_blk)
    ys_lat, _ = _s5_scan(proj, par_lat, lp['s5_d'], h_ctx, SEQ, 0)

    bb = _s5_glu(ys_lat, lp['s5_glu_w'], lp['s5_glu_b'], bb, 0)
    bb = _attn_lat(q, k, v, bb)
    bb = _pool(proj, lp['pool_w'], lp['pool_scale'], bb, SEQ, 0)
    bb = _hyena(proj, lp, hy_lat, bb, SEQ, 0)
    if not last:
        bb = _s5_glu(ys_ctx, lp['s5_glu_w'], lp['s5_glu_b'], bb, N_LAT)
        bb = _attn_ctx(q, k, v, bb)
        bb = _pool(proj, lp['pool_w'], lp['pool_scale'], bb, CTX_LEN, ctx_blk)
        bb = _hyena(proj, lp, hy_ctx, bb, CTX_LEN, N_LAT)

    n_rows = N_LAT if last else MT
    merged = _merge(u, big['w_gate'], lp['b_gate'], bb, big['w_branch'], l, n_rows)
    return _out_proj(merged, big['w_out'], l, x, mods, n_rows), bb


def kernel(x, c, ctx, c_ctx, w_ada, b_ada, norm_ffn1, norm_mix, norm_ffn2, norm_final, ffn1_wi, ffn1_wo, ffn2_wi, ffn2_wo, w_in, w_gate, b_gate, w_branch, w_out, pool_w, pool_scale, q_norm, k_norm, hy_short_w, hy_short_b, hy_f1_w, hy_f1_b, hy_f2_w, hy_f2_b, hy_f3_w, hy_freq, hy_bias, s5_a_re, s5_a_im, s5_log_dt, s5_b_re, s5_b_im, s5_c_re, s5_c_im, s5_d, s5_glu_w, s5_glu_b):
    big = dict(w_in=w_in, w_gate=w_gate, w_branch=w_branch, w_out=w_out)
    per_layer = dict(
        b_gate=b_gate, pool_w=pool_w,
        pool_scale=pool_scale, q_norm=q_norm, k_norm=k_norm, hy_short_w=hy_short_w, hy_short_b=hy_short_b,
        hy_f1_w=hy_f1_w, hy_f1_b=hy_f1_b, hy_f2_w=hy_f2_w, hy_f2_b=hy_f2_b, hy_f3_w=hy_f3_w, hy_freq=hy_freq,
        hy_bias=hy_bias, s5_a_re=s5_a_re, s5_a_im=s5_a_im, s5_log_dt=s5_log_dt, s5_b_re=s5_b_re, s5_b_im=s5_b_im,
        s5_c_re=s5_c_re, s5_c_im=s5_c_im, s5_d=s5_d, s5_glu_w=s5_glu_w, s5_glu_b=s5_glu_b)

    cos, sin = _rope_tables()
    deltas = _hy_deltas()
    consts = (cos, sin,
              (_hy_feats(SEQ), deltas, _dft_matrices(SEQ)),
              (_hy_feats(CTX_LEN), deltas, _dft_matrices(CTX_LEN)))

    cc = jnp.concatenate([c, c_ctx[None], jnp.zeros((8 - BATCH - 1, D_MODEL), F32)], axis=0)
    mods_all = _ada(cc, w_ada, b_ada).reshape(DEPTH, 8, N_MOD, 1, D_MODEL)

    xs = jnp.concatenate([x.reshape(N_LAT, D_MODEL), ctx.reshape(N_CTX, D_MODEL)], axis=0)
    bb = jnp.zeros((N_BRANCH, MT, BRANCH_WIDTH), BF16)
    for l in range(DEPTH):
        last = l == DEPTH - 1
        lp = {name: w[l] for name, w in per_layer.items()}
        mods = mods_all[l]
        xs, w_in_bf16 = _ffn(xs, mods, norm_ffn1[l], ffn1_wi, ffn1_wo, l, 0, MT, side=w_in)
        xs, bb = _mixer(xs, norm_mix[l], lp, big, w_in_bf16, l, mods, consts, bb, last)
        xs = _ffn(xs, mods, norm_ffn2[l], ffn2_wi, ffn2_wo, l, 6, N_LAT if last else MT)
    return _final_norm(xs, norm_final, N_LAT).reshape(BATCH, SEQ, D_MODEL)
```

```python
import functools
import math

import jax
import jax.numpy as jnp
from jax import lax
from jax.experimental import pallas as pl
from jax.experimental.pallas import tpu as pltpu

F32 = jnp.float32
BF16 = jnp.bfloat16

D_MODEL = 2048
BATCH = 4
SEQ = 2048
DEPTH = 4
GRID_W = 64
CTX_LEN = 256
D_FF = 5632
N_MOD = 9
EPS = 1e-6

POOL_WINDOWS = (2, 4, 8, 16)
HEAD_DIM = 128
N_Q_HEADS = 4
N_KV_HEADS = 2
Q_GROUP = N_Q_HEADS // N_KV_HEADS
ROPE_THETA = 10000.0
ROPE_FREQS = HEAD_DIM // 4

HYENA_WIDTH = 512
HYENA_EMB = 33
HYENA_BANDS = (HYENA_EMB - 1) // 2
HYENA_HIDDEN = 64
HYENA_TARGET = 1e-2
HYENA_FAST_PCT = 0.3
HYENA_SLOW_PCT = 1.5

S5_WIDTH = 512
S5_GC = 16
S5_GROUPS = S5_WIDTH // S5_GC
S5_STATE = 64
S5_LANES = 8
S5_CHUNK_GROUPS = 8
S5_CHUNK_CH = S5_CHUNK_GROUPS * S5_GC
S5_CHUNK_ST = S5_CHUNK_GROUPS * S5_STATE
S5_NCHUNK = S5_GROUPS // S5_CHUNK_GROUPS

N_BRANCH = 4
BRANCH_WIDTH = 512
LANE = 128
COL = 512

OFF_Q = 512
OFF_KV = 1024
OFF_HY = 1536
OFF_S5 = 3072
IN_WIDTH = 3584

N_LAT = BATCH * SEQ
N_CTX = BATCH * CTX_LEN
MT = N_LAT + N_CTX

TM = 1024
TM_EW = 512
VMEM_LIMIT = 56 * 1024 * 1024


def _cparams(*sem):
    return pltpu.CompilerParams(dimension_semantics=sem, vmem_limit_bytes=VMEM_LIMIT)


def _dot(a, b):
    return jnp.dot(a, b, preferred_element_type=F32)


SLOT_POOL, SLOT_ATTN, SLOT_HYENA, SLOT_S5 = range(4)
_ANY = pl.BlockSpec(memory_space=pl.ANY)


def _keep_branch_buffer(body, n_in):
    def wrapped(*refs):
        return body(*refs[:n_in], *refs[n_in + 1:])
    return wrapped


def _mod_row(i, tm):
    return jnp.minimum((i * tm) // SEQ, BATCH)


def _mod_spec(tm, k):
    return pl.BlockSpec((None, None, 1, D_MODEL), lambda i, *_: (_mod_row(i, tm), k, 0, 0))


def _ada_body(c_ref, w_ref, b_ref, o_ref):
    c = c_ref[...]
    a = (c * jax.nn.sigmoid(c)).astype(BF16)
    o_ref[...] = _dot(a, w_ref[...].astype(BF16)) + b_ref[...]


def _ada(cc, w_ada, b_ada):
    tn = 1024
    nw = N_MOD * D_MODEL
    return pl.pallas_call(
        _ada_body,
        grid=(DEPTH, nw // tn),
        in_specs=[
            pl.BlockSpec((8, D_MODEL), lambda l, j: (0, 0)),
            pl.BlockSpec((None, D_MODEL, tn), lambda l, j: (l, 0, j)),
            pl.BlockSpec((None, 1, tn), lambda l, j: (l, 0, j)),
        ],
        out_specs=pl.BlockSpec((None, 8, tn), lambda l, j: (l, 0, j)),
        out_shape=jax.ShapeDtypeStruct((DEPTH, 8, nw), F32),
        compiler_params=_cparams("arbitrary", "arbitrary"),
        name="ada",
    )(cc, w_ada, b_ada.reshape(DEPTH, 1, nw))


NORM_ROWS = 16


def _rows_loop(n_rows, step):
    for c in range(n_rows // NORM_ROWS):
        step(pl.ds(c * NORM_ROWS, NORM_ROWS))


def _rms_scale_rows(x_ref, r_ref, scale, shift, o_ref):
    def stats(rows):
        x = x_ref[rows, :]
        r_ref[rows, :] = lax.rsqrt(jnp.mean(x * x, axis=-1, keepdims=True) + EPS)

    def apply(rows):
        y = x_ref[rows, :] * r_ref[rows, :] * scale
        if shift is not None:
            y = y + shift
        o_ref[rows, :] = y.astype(o_ref.dtype)

    _rows_loop(x_ref.shape[0], stats)
    _rows_loop(x_ref.shape[0], apply)


def _normmod_body(x_ref, g_ref, sh_ref, sc_ref, o_ref, r_ref):
    _rms_scale_rows(x_ref, r_ref, g_ref[...] * (1.0 + sc_ref[...]), sh_ref[...], o_ref)


def _norm_body(x_ref, g_ref, o_ref, r_ref):
    _rms_scale_rows(x_ref, r_ref, g_ref[...], None, o_ref)


def _normmod(x, g, mods, base, n_rows):
    tm = TM_EW
    return pl.pallas_call(
        _normmod_body,
        grid=(n_rows // tm,),
        in_specs=[
            pl.BlockSpec((tm, D_MODEL), lambda i: (i, 0)),
            pl.BlockSpec((1, D_MODEL), lambda i: (0, 0)),
            _mod_spec(tm, base),
            _mod_spec(tm, base + 1),
        ],
        out_specs=pl.BlockSpec((tm, D_MODEL), lambda i: (i, 0)),
        out_shape=jax.ShapeDtypeStruct((n_rows, D_MODEL), BF16),
        scratch_shapes=[pltpu.VMEM((tm, 1), F32)],
        compiler_params=_cparams("arbitrary"),
        name="normmod",
    )(x, g.reshape(1, D_MODEL), mods, mods)


def _final_norm(x, g, n_rows):
    tm = TM_EW
    return pl.pallas_call(
        _norm_body,
        grid=(n_rows // tm,),
        in_specs=[
            pl.BlockSpec((tm, D_MODEL), lambda i: (i, 0)),
            pl.BlockSpec((1, D_MODEL), lambda i: (0, 0)),
        ],
        out_specs=pl.BlockSpec((tm, D_MODEL), lambda i: (i, 0)),
        out_shape=jax.ShapeDtypeStruct((n_rows, D_MODEL), F32),
        scratch_shapes=[pltpu.VMEM((tm, 1), F32)],
        compiler_params=_cparams("arbitrary"),
        name="final_norm",
    )(x, g.reshape(1, D_MODEL))


MXU_COLS = 256


WO_SLAB = 64
WO_NSLAB = D_FF // WO_SLAB


NEXT_ROWS = 128
NEXT_SLICES = TM // NEXT_ROWS


def _ffn_a_body(u0_ref, xn_ref, g_ref, sh_ref, sc_ref, wa_ref, wb_ref, wo_ref, h_ref, wob_ref, u2_ref, r_ref):
    i, j = pl.program_id(0), pl.program_id(1)
    slot = i % 2

    @pl.when((i == 0) & (j == 0))
    def _():
        u2_ref[0] = u0_ref[...]

    dst = pl.ds(pl.multiple_of(jnp.minimum(j, NEXT_SLICES - 1) * NEXT_ROWS, NEXT_ROWS), NEXT_ROWS)
    _rms_scale_rows(xn_ref, r_ref, g_ref[...] * (1.0 + sc_ref[...]), sh_ref[...], u2_ref.at[1 - slot, dst, :])

    u = u2_ref[slot]
    for c in range(0, h_ref.shape[1], MXU_COLS):
        sl = slice(c, c + MXU_COLS)
        a = _dot(u, wa_ref[:, sl].astype(BF16))
        b = _dot(u, wb_ref[:, sl].astype(BF16))
        h_ref[:, sl] = (a * jax.nn.sigmoid(a) * b).astype(BF16)

    @pl.when(pl.program_id(0) * pl.num_programs(1) + pl.program_id(1) < WO_NSLAB)
    def _():
        wob_ref[...] = wo_ref[...].astype(BF16)


def _ffn_a(x, g, mods, base, wi, wo, l, n_rows):
    tf = 512
    nf = D_FF // tf
    ni = n_rows // TM
    assert ni * nf >= WO_NSLAB and nf >= NEXT_SLICES

    def wo_blk(i, j):
        return jnp.minimum(i * nf + j, WO_NSLAB - 1)

    def nxt(i):
        return jnp.minimum(i + 1, ni - 1)

    def next_mod(k):
        return pl.BlockSpec((None, None, 1, D_MODEL), lambda i, j: (_mod_row(nxt(i), TM), k, 0, 0))

    u0 = _normmod(x, g, mods, base, TM)
    return pl.pallas_call(
        _ffn_a_body,
        grid=(ni, nf),
        in_specs=[
            pl.BlockSpec((TM, D_MODEL), lambda i, j: (0, 0)),
            pl.BlockSpec((NEXT_ROWS, D_MODEL),
                         lambda i, j: (nxt(i) * NEXT_SLICES + jnp.minimum(j, NEXT_SLICES - 1), 0)),
            pl.BlockSpec((1, D_MODEL), lambda i, j: (0, 0)),
            next_mod(base),
            next_mod(base + 1),
            pl.BlockSpec((None, D_MODEL, tf), lambda i, j: (l, 0, j)),
            pl.BlockSpec((None, D_MODEL, tf), lambda i, j: (l, 0, j + nf)),
            pl.BlockSpec((None, WO_SLAB, D_MODEL), lambda i, j: (l, wo_blk(i, j), 0)),
        ],
        out_specs=[
            pl.BlockSpec((TM, tf), lambda i, j: (i, j)),
            pl.BlockSpec((WO_SLAB, D_MODEL), lambda i, j: (wo_blk(i, j), 0)),
        ],
        out_shape=[
            jax.ShapeDtypeStruct((n_rows, D_FF), BF16),
            jax.ShapeDtypeStruct((D_FF, D_MODEL), BF16),
        ],
        scratch_shapes=[pltpu.VMEM((2, TM, D_MODEL), BF16), pltpu.VMEM((NEXT_ROWS, 1), F32)],
        compiler_params=_cparams("arbitrary", "arbitrary"),
        name="ffn_a",
    )(u0, x, g.reshape(1, D_MODEL), mods, mods, wi, wi, wo)


SIDE_SLAB = 64


def _ffn_b_body(h_ref, w_ref, x_ref, g_ref, *rest, n_slab):
    o_ref = rest[-2] if n_slab else rest[-1]
    h = h_ref[...]
    for c in range(0, o_ref.shape[1], MXU_COLS):
        sl = slice(c, c + MXU_COLS)
        o_ref[:, sl] = x_ref[:, sl] + (0.5 * g_ref[:, sl]) * _dot(h, w_ref[:, sl])

    if n_slab:
        side_ref, _, side_out_ref = rest

        @pl.when(pl.program_id(0) * pl.num_programs(1) + pl.program_id(1) < n_slab)
        def _():
            side_out_ref[...] = side_ref[...].astype(BF16)


def _ffn_b(h, wo_bf16, x, mods, gate_idx, n_rows, side=None, l=None):
    tn = 512
    nj = D_MODEL // tn
    gate = pl.BlockSpec((None, None, 1, tn), lambda i, j: (_mod_row(i, TM), gate_idx, 0, j))
    in_specs = [
        pl.BlockSpec((TM, D_FF), lambda i, j: (i, 0)),
        pl.BlockSpec((D_FF, tn), lambda i, j: (0, j)),
        pl.BlockSpec((TM, tn), lambda i, j: (i, j)),
        gate,
    ]
    out_specs = [pl.BlockSpec((TM, tn), lambda i, j: (i, j))]
    out_shape = [jax.ShapeDtypeStruct((n_rows, D_MODEL), F32)]
    args = [h, wo_bf16, x, mods]
    n_slab = 0
    if side is not None:
        _, rows, cols = side.shape
        n_slab = rows // SIDE_SLAB
        assert (n_rows // TM) * nj >= n_slab

        def slab(i, j):
            return jnp.minimum(i * nj + j, n_slab - 1)

        in_specs.append(pl.BlockSpec((None, SIDE_SLAB, cols), lambda i, j: (l, slab(i, j), 0)))
        out_specs.append(pl.BlockSpec((SIDE_SLAB, cols), lambda i, j: (slab(i, j), 0)))
        out_shape.append(jax.ShapeDtypeStruct((rows, cols), BF16))
        args.append(side)
    res = pl.pallas_call(
        functools.partial(_ffn_b_body, n_slab=n_slab),
        grid=(n_rows // TM, nj),
        in_specs=in_specs,
        out_specs=out_specs,
        out_shape=out_shape,
        compiler_params=_cparams("arbitrary", "arbitrary"),
        name="ffn_b",
    )(*args)
    return res if side is not None else res[0]


def _ffn(x, mods, g, wi, wo, l, base, n_rows, side=None):
    h, wo_bf16 = _ffn_a(x, g, mods, base, wi, wo, l, n_rows)
    return _ffn_b(h, wo_bf16, x, mods, base + 2, n_rows, side, l)


def _proj_body(u_ref, w_ref, o_ref):
    o_ref[...] = _dot(u_ref[...], w_ref[...])


def _proj(u, w_in_bf16):
    tn = COL
    tm = 1536
    return pl.pallas_call(
        _proj_body,
        grid=(MT // tm, IN_WIDTH // tn),
        in_specs=[
            pl.BlockSpec((tm, D_MODEL), lambda i, j: (i, 0)),
            pl.BlockSpec((D_MODEL, tn), lambda i, j: (0, j)),
        ],
        out_specs=pl.BlockSpec((tm, tn), lambda i, j: (i, j)),
        out_shape=jax.ShapeDtypeStruct((MT, IN_WIDTH), F32),
        compiler_params=_cparams("arbitrary", "arbitrary"),
        name="proj",
    )(u, w_in_bf16)


def _rope_tables():
    t = jnp.arange(SEQ)
    rows = (t // GRID_W).astype(F32)
    cols = (t % GRID_W).astype(F32)
    freqs = ROPE_THETA ** (-jnp.arange(ROPE_FREQS, dtype=F32) / ROPE_FREQS)
    ar = rows[:, None] * freqs[None, :]
    ac = cols[:, None] * freqs[None, :]
    cos = jnp.concatenate([jnp.cos(ar), jnp.cos(ar), jnp.cos(ac), jnp.cos(ac)], axis=-1)
    sin = jnp.concatenate([-jnp.sin(ar), jnp.sin(ar), -jnp.sin(ac), jnp.sin(ac)], axis=-1)
    cos = jnp.concatenate([cos, jnp.ones((TM_EW, HEAD_DIM), F32)], axis=0)
    sin = jnp.concatenate([sin, jnp.zeros((TM_EW, HEAD_DIM), F32)], axis=0)
    return cos, sin


def _qkv_body(q_ref, kv_ref, cos_ref, sin_ref, qn_ref, kn_ref, qo_ref, ko_ref, vo_ref):
    cos = cos_ref[...]
    sin = sin_ref[...]
    lane = lax.broadcasted_iota(jnp.int32, cos.shape, 1)
    first = (lane % (2 * ROPE_FREQS)) < ROPE_FREQS

    def norm_rope(xh, g):
        y = xh * lax.rsqrt(jnp.mean(xh * xh, axis=-1, keepdims=True) + EPS) * g
        swapped = jnp.where(first, pltpu.roll(y, HEAD_DIM - ROPE_FREQS, 1), pltpu.roll(y, ROPE_FREQS, 1))
        return y * cos + swapped * sin

    scale = math.log2(math.e) / math.sqrt(HEAD_DIM)
    for h in range(N_Q_HEADS):
        sl = slice(h * HEAD_DIM, (h + 1) * HEAD_DIM)
        qo_ref[:, sl] = (norm_rope(q_ref[:, sl], qn_ref[...]) * scale).astype(BF16)
    for h in range(N_KV_HEADS):
        sl = slice(h * HEAD_DIM, (h + 1) * HEAD_DIM)
        ko_ref[:, sl] = norm_rope(kv_ref[:, sl], kn_ref[...]).astype(BF16)
    kvw = N_KV_HEADS * HEAD_DIM
    for h in range(N_KV_HEADS):
        vo_ref[:, 2 * h * HEAD_DIM:(2 * h + 1) * HEAD_DIM] = (
            kv_ref[:, kvw + h * HEAD_DIM:kvw + (h + 1) * HEAD_DIM].astype(BF16))
        vo_ref[:, (2 * h + 1) * HEAD_DIM:(2 * h + 2) * HEAD_DIM] = jnp.ones((vo_ref.shape[0], HEAD_DIM), BF16)


def _qkv(proj, cos, sin, q_norm, k_norm):
    tm = TM_EW
    n_lat_tiles = N_LAT // tm
    per_seq = SEQ // tm
    kvw = N_KV_HEADS * HEAD_DIM

    def tab(i):
        return (jnp.where(i < n_lat_tiles, i % per_seq, per_seq), 0)

    return pl.pallas_call(
        _qkv_body,
        grid=(MT // tm,),
        in_specs=[
            pl.BlockSpec((tm, COL), lambda i: (i, OFF_Q // COL)),
            pl.BlockSpec((tm, COL), lambda i: (i, OFF_KV // COL)),
            pl.BlockSpec((tm, HEAD_DIM), tab),
            pl.BlockSpec((tm, HEAD_DIM), tab),
            pl.BlockSpec((1, HEAD_DIM), lambda i: (0, 0)),
            pl.BlockSpec((1, HEAD_DIM), lambda i: (0, 0)),
        ],
        out_specs=[
            pl.BlockSpec((tm, COL), lambda i: (i, 0)),
            pl.BlockSpec((tm, kvw), lambda i: (i, 0)),
            pl.BlockSpec((tm, 2 * kvw), lambda i: (i, 0)),
        ],
        out_shape=[
            jax.ShapeDtypeStruct((MT, COL), BF16),
            jax.ShapeDtypeStruct((MT, kvw), BF16),
            jax.ShapeDtypeStruct((MT, 2 * kvw), BF16),
        ],
        compiler_params=_cparams("arbitrary"),
        name="qkv",
    )(proj, proj, cos, sin, q_norm.reshape(1, HEAD_DIM), k_norm.reshape(1, HEAD_DIM))


def _attn_body(*refs, with_lat):
    if with_lat:
        q_ref, kl_ref, vl_ref, kc_ref, vc_ref, o_ref = refs
    else:
        q_ref, kc_ref, vc_ref, o_ref = refs
    nt = (((1,), (1,)), ((), ()))
    half = q_ref.shape[0] // 2
    units = [(slice(r, r + half), slice(g * HEAD_DIM, (g + 1) * HEAD_DIM))
             for g in range(Q_GROUP) for r in (0, half)]
    scores = []
    for rows, sl in units:
        q = q_ref[rows, sl]
        sc = lax.dot_general(q, kc_ref[...], nt, preferred_element_type=F32)
        s_lat = lax.dot_general(q, kl_ref[...], nt, preferred_element_type=F32) if with_lat else None
        scores.append((sc, s_lat))
    for (rows, sl), (sc, s_lat) in zip(units, scores):
        m = jnp.max(sc, axis=-1, keepdims=True)
        if with_lat:
            m = jnp.maximum(m, jnp.max(s_lat, axis=-1, keepdims=True))
        o = _dot(jnp.exp2(sc - m).astype(BF16), vc_ref[...])
        if with_lat:
            o = o + _dot(jnp.exp2(s_lat - m).astype(BF16), vl_ref[...])
        o_ref[rows, sl] = (o[:, :HEAD_DIM] / o[:, HEAD_DIM:]).astype(BF16)


def _attn_lat(q, k, v, bb):
    tq = 512
    nq = SEQ // tq
    ctx_blk = N_LAT // CTX_LEN
    gw = Q_GROUP * HEAD_DIM
    return pl.pallas_call(
        _keep_branch_buffer(functools.partial(_attn_body, with_lat=True), 5),
        grid=(BATCH, N_KV_HEADS, nq),
        in_specs=[
            pl.BlockSpec((tq, gw), lambda b, h, i: (b * nq + i, h)),
            pl.BlockSpec((SEQ, HEAD_DIM), lambda b, h, i: (b, h)),
            pl.BlockSpec((SEQ, 2 * HEAD_DIM), lambda b, h, i: (b, h)),
            pl.BlockSpec((CTX_LEN, HEAD_DIM), lambda b, h, i: (ctx_blk + b, h)),
            pl.BlockSpec((CTX_LEN, 2 * HEAD_DIM), lambda b, h, i: (ctx_blk + b, h)),
            _ANY,
        ],
        out_specs=pl.BlockSpec((None, tq, gw), lambda b, h, i: (SLOT_ATTN, b * nq + i, h)),
        out_shape=jax.ShapeDtypeStruct(bb.shape, bb.dtype),
        input_output_aliases={5: 0},
        compiler_params=_cparams("arbitrary", "arbitrary", "arbitrary"),
        name="attn_lat",
    )(q, k, v, k, v, bb)


def _attn_ctx(q, k, v, bb):
    ctx_blk = N_LAT // CTX_LEN
    gw = Q_GROUP * HEAD_DIM
    return pl.pallas_call(
        _keep_branch_buffer(functools.partial(_attn_body, with_lat=False), 3),
        grid=(BATCH, N_KV_HEADS),
        in_specs=[
            pl.BlockSpec((CTX_LEN, gw), lambda b, h: (ctx_blk + b, h)),
            pl.BlockSpec((CTX_LEN, HEAD_DIM), lambda b, h: (ctx_blk + b, h)),
            pl.BlockSpec((CTX_LEN, 2 * HEAD_DIM), lambda b, h: (ctx_blk + b, h)),
            _ANY,
        ],
        out_specs=pl.BlockSpec((None, CTX_LEN, gw), lambda b, h: (SLOT_ATTN, ctx_blk + b, h)),
        out_shape=jax.ShapeDtypeStruct(bb.shape, bb.dtype),
        input_output_aliases={3: 0},
        compiler_params=_cparams("arbitrary", "arbitrary"),
        name="attn_ctx",
    )(q, k, v, bb)


POOL_PAD = 8


def _pool_body(a_ref, w_ref, s_ref, o_ref, *, seq):
    lp = seq + 2 * POOL_PAD
    t = lax.broadcasted_iota(jnp.int32, (seq, LANE), 0)
    zpad = jnp.zeros((POOL_PAD, LANE), F32)
    for gi, win in enumerate(POOL_WINDOWS):
        sl = slice(gi * LANE, (gi + 1) * LANE)
        a = a_ref[:, sl]
        s = jnp.concatenate([zpad, a, zpad], axis=0)
        s = s + pltpu.roll(s, 1, 0)
        half = 1
        while 2 * half < win:
            s = pltpu.roll(s, half, 0) + pltpu.roll(s, lp - half, 0)
            half *= 2
        s = s[POOL_PAD:POOL_PAD + seq]
        lo = jnp.maximum(t - win // 2, 0)
        hi = jnp.minimum(t + win // 2, seq)
        pooled = s / (hi - lo).astype(F32) - a
        y = _dot(pooled.astype(BF16), w_ref[gi].astype(BF16))
        o_ref[:, sl] = (y * s_ref[:, sl]).astype(BF16)


def _pool(proj, pool_w, pool_scale, bb, seq, row_blk0):
    width = len(POOL_WINDOWS) * LANE
    return pl.pallas_call(
        _keep_branch_buffer(functools.partial(_pool_body, seq=seq), 3),
        grid=(BATCH,),
        in_specs=[
            pl.BlockSpec((seq, width), lambda b: (row_blk0 + b, 0)),
            pl.BlockSpec((len(POOL_WINDOWS), LANE, LANE), lambda b: (0, 0, 0)),
            pl.BlockSpec((1, width), lambda b: (0, 0)),
            _ANY,
        ],
        out_specs=pl.BlockSpec((None, seq, width), lambda b: (SLOT_POOL, row_blk0 + b, 0)),
        out_shape=jax.ShapeDtypeStruct(bb.shape, bb.dtype),
        input_output_aliases={3: 0},
        compiler_params=_cparams("arbitrary"),
        name="pool",
    )(proj, pool_w, pool_scale.reshape(1, width), bb)


def _hy_prep_body(x0_ref, x1_ref, v_ref, w0_ref, w1_ref, wv_ref, b0_ref, b1_ref, bv_ref,
                  x0o_ref, vx_ref, vb_ref, *, seq):
    t = lax.broadcasted_iota(jnp.int32, x0_ref.shape, 0)

    def conv(x_ref, w_ref, b_ref):
        x = x_ref[...]
        prev = jnp.where(t >= 1, pltpu.roll(x, 1, 0), 0.0)
        nxt = jnp.where(t <= seq - 2, pltpu.roll(x, seq - 1, 0), 0.0)
        return prev * w_ref[0:1, :] + x * w_ref[1:2, :] + nxt * w_ref[2:3, :] + b_ref[...]

    x0o_ref[...] = conv(x0_ref, w0_ref, b0_ref)
    vx = conv(v_ref, wv_ref, bv_ref) * conv(x1_ref, w1_ref, b1_ref)
    vx_ref[...] = vx
    vb_ref[...] = vx.astype(BF16)


def _hy_prep(proj, short_w, short_b, seq, row_blk0):
    tc = 2 * LANE
    nc = HYENA_WIDTH // tc
    c0 = OFF_HY // tc
    short_b = short_b.reshape(1, 3 * HYENA_WIDTH)

    def xspec(part):
        return pl.BlockSpec((seq, tc), lambda b, c: (row_blk0 + b, c0 + part * nc + c))

    def wspec(part, rows):
        return pl.BlockSpec((rows, tc), lambda b, c: (0, part * nc + c))

    return pl.pallas_call(
        functools.partial(_hy_prep_body, seq=seq),
        grid=(BATCH, nc),
        in_specs=[xspec(0), xspec(1), xspec(2), wspec(0, 3), wspec(1, 3), wspec(2, 3),
                  wspec(0, 1), wspec(1, 1), wspec(2, 1)],
        out_specs=[
            pl.BlockSpec((seq, tc), lambda b, c: (b, c)),
            pl.BlockSpec((seq, tc), lambda b, c: (b, c)),
            pl.BlockSpec((seq, tc), lambda b, c: (0, b * nc + c)),
        ],
        out_shape=[
            jax.ShapeDtypeStruct((BATCH * seq, HYENA_WIDTH), F32),
            jax.ShapeDtypeStruct((BATCH * seq, HYENA_WIDTH), F32),
            jax.ShapeDtypeStruct((seq, BATCH * HYENA_WIDTH), BF16),
        ],
        compiler_params=_cparams("arbitrary", "arbitrary"),
        name="hy_prep",
    )(proj, proj, proj, short_w, short_w, short_w, short_b, short_b, short_b)


def _hy_feats(seq):
    t = jnp.linspace(0.0, 1.0, seq, dtype=F32)[:, None]
    f = jnp.linspace(1e-4, HYENA_BANDS - 1, HYENA_BANDS, dtype=F32)
    w = 2.0 * math.pi * jnp.arange(seq, dtype=F32) / seq
    fw = w[:, None] * f[None, :]
    z = jnp.concatenate([t, jnp.cos(fw), -jnp.sin(fw)], axis=-1)
    return jnp.pad(z, ((0, 0), (0, LANE - HYENA_EMB)))


def _hy_deltas():
    max_decay = math.log(HYENA_TARGET) / HYENA_FAST_PCT
    min_decay = math.log(HYENA_TARGET) / HYENA_SLOW_PCT
    return jnp.abs(jnp.linspace(min_decay, max_decay, HYENA_WIDTH, dtype=F32)).reshape(1, HYENA_WIDTH)


def _hy_filter_body(z_ref, w1_ref, b1_ref, w2_ref, b2_ref, w3_ref, fr_ref, dl_ref, k_ref, nyq_ref, *, seq):
    hp = lax.Precision.HIGHEST
    freq = fr_ref[...]
    h = jnp.sin(freq * (jnp.dot(z_ref[...], w1_ref[...], precision=hp, preferred_element_type=F32) + b1_ref[...]))
    h = jnp.sin(freq * (jnp.dot(h, w2_ref[...], precision=hp, preferred_element_type=F32) + b2_ref[...]))
    h = jnp.dot(h, w3_ref[...], precision=hp, preferred_element_type=F32)
    ti = lax.broadcasted_iota(jnp.int32, (seq, HYENA_WIDTH), 0)
    decay = jnp.exp(-(ti.astype(F32) * (1.0 / (seq - 1))) * dl_ref[...])
    hf = h[:, :HYENA_WIDTH] * decay
    hb = jnp.where(ti == 0, 0.0, h[:, HYENA_WIDTH:] * decay)
    ks = hf + hb
    k_ref[:, :HYENA_WIDTH] = ks.astype(BF16)
    k_ref[:, HYENA_WIDTH:] = (hf - hb).astype(BF16)
    nyq = jnp.sum(jnp.where(ti % 2 == 0, ks, -ks), axis=0, keepdims=True)
    nyq_ref[...] = jnp.broadcast_to(nyq, nyq_ref.shape)


def _hy_filter(lp, z, deltas, seq):
    w1 = jnp.pad(lp['hy_f1_w'], ((0, LANE - HYENA_EMB), (0, 0)))
    args = (z, w1, lp['hy_f1_b'].reshape(1, -1), lp['hy_f2_w'], lp['hy_f2_b'].reshape(1, -1), lp['hy_f3_w'],
            lp['hy_freq'].reshape(1, -1), deltas)
    full = lambda a: pl.BlockSpec(a.shape, lambda i: (0,) * a.ndim)
    return pl.pallas_call(
        functools.partial(_hy_filter_body, seq=seq),
        grid=(1,),
        in_specs=[full(a) for a in args],
        out_specs=[pl.BlockSpec((seq, 2 * HYENA_WIDTH), lambda i: (0, 0)),
                   pl.BlockSpec((8, HYENA_WIDTH), lambda i: (0, 0))],
        out_shape=[jax.ShapeDtypeStruct((seq, 2 * HYENA_WIDTH), BF16),
                   jax.ShapeDtypeStruct((8, HYENA_WIDTH), F32)],
        compiler_params=_cparams("arbitrary"),
        name="hy_filter",
    )(*args)


DFT_SPLIT = 64


def _dft_matrices(seq):
    n = 2 * seq
    f = jnp.arange(seq, dtype=jnp.int32)[:, None]

    def table(step, count):
        idx = (f * (jnp.arange(count, dtype=jnp.int32)[None, :] * step)) % n
        ang = idx.astype(F32) * (2.0 * math.pi / n)
        return jnp.cos(ang), jnp.sin(ang)

    hc, hs = table(DFT_SPLIT, seq // DFT_SPLIT)
    lc, ls = table(1, DFT_SPLIT)
    cos = (hc[:, :, None] * lc[:, None, :] - hs[:, :, None] * ls[:, None, :]).reshape(seq, seq)
    sin = (hs[:, :, None] * lc[:, None, :] + hc[:, :, None] * ls[:, None, :]).reshape(seq, seq)
    s = jnp.arange(seq, dtype=jnp.int32)[None, :]
    nyq = jnp.where(s % 2 == 0, 1.0, -1.0).astype(F32)
    wf = jnp.concatenate([cos, jnp.where(f == 0, nyq, -sin)], axis=0).astype(BF16)
    return wf, wf.T


def _mm_body(a_ref, b_ref, o_ref):
    o_ref[...] = _dot(a_ref[...], b_ref[...])


def _dft_fwd_body(wc_ref, ws_ref, x_ref, hr_ref, hi_ref, nyq_ref, pr_ref, pi_ref, *, n):
    tf = wc_ref.shape[0]
    freq = lax.broadcasted_iota(jnp.int32, (tf, MXU_COLS), 0) + pl.program_id(0) * tf
    first = freq == 0
    w = jnp.where(first, 1.0 / n, 2.0 / n)
    chunks = [slice(c, c + MXU_COLS) for c in range(0, x_ref.shape[1], MXU_COLS)]
    spectra = [(_dot(wc_ref[...], x_ref[:, sl]), _dot(ws_ref[...], x_ref[:, sl])) for sl in chunks]
    for sl, (xr, xi) in zip(chunks, spectra):
        hr = hr_ref[:, sl]
        hi = hi_ref[:, sl]
        pr_ref[:, sl] = (jnp.where(first, xr * hr, xr * hr - xi * hi) * w).astype(BF16)
        pi_ref[:, sl] = (jnp.where(first, xi * nyq_ref[0:1, sl], xr * hi + xi * hr) * w).astype(BF16)


def _dft_fwd(wf, x, hf, nyq):
    n, seq = wf.shape
    tf = min(seq, 512)
    nf = seq // tf
    out = jax.ShapeDtypeStruct((seq, BATCH * HYENA_WIDTH), BF16)
    return pl.pallas_call(
        functools.partial(_dft_fwd_body, n=n),
        grid=(nf, BATCH),
        in_specs=[
            pl.BlockSpec((tf, seq), lambda f, b: (f, 0)),
            pl.BlockSpec((tf, seq), lambda f, b: (nf + f, 0)),
            pl.BlockSpec((seq, HYENA_WIDTH), lambda f, b: (0, b)),
            pl.BlockSpec((tf, HYENA_WIDTH), lambda f, b: (f, 0)),
            pl.BlockSpec((tf, HYENA_WIDTH), lambda f, b: (nf + f, 0)),
            pl.BlockSpec((8, HYENA_WIDTH), lambda f, b: (0, 0)),
        ],
        out_specs=[pl.BlockSpec((tf, HYENA_WIDTH), lambda f, b: (f, b))] * 2,
        out_shape=[out, out],
        compiler_params=_cparams("arbitrary", "arbitrary"),
        name="dft_fwd",
    )(wf, wf, x, hf, hf, nyq)


def _dft_filter(wf, k):
    n, seq = wf.shape
    tm = min(seq, 1024)
    return pl.pallas_call(
        _mm_body,
        grid=(n // tm,),
        in_specs=[
            pl.BlockSpec((tm, seq), lambda i: (i, 0)),
            pl.BlockSpec((seq, HYENA_WIDTH), lambda i: (0, (i * tm) // seq)),
        ],
        out_specs=pl.BlockSpec((tm, HYENA_WIDTH), lambda i: (i, 0)),
        out_shape=jax.ShapeDtypeStruct((n, HYENA_WIDTH), F32),
        compiler_params=_cparams("arbitrary"),
        name="dft_filter",
    )(wf, k)


def _dft_inv_body(wc_ref, ws_ref, pr_ref, pi_ref, vx_ref, bias_ref, x0_ref, o_ref):
    y = _dot(wc_ref[...], pr_ref[...]) + _dot(ws_ref[...], pi_ref[...])
    o_ref[...] = ((y + vx_ref[...] * bias_ref[...]) * x0_ref[...]).astype(BF16)


def _dft_inv(wi, p_re, p_im, vx, bias, x0, bb, row0):
    seq, n = wi.shape
    tm = min(seq, 1024)
    tn = HYENA_WIDTH
    nt = seq // tm
    return pl.pallas_call(
        _keep_branch_buffer(_dft_inv_body, 7),
        grid=(nt, BATCH),
        in_specs=[
            pl.BlockSpec((tm, seq), lambda i, b: (i, 0)),
            pl.BlockSpec((tm, seq), lambda i, b: (i, 1)),
            pl.BlockSpec((seq, tn), lambda i, b: (0, b)),
            pl.BlockSpec((seq, tn), lambda i, b: (0, b)),
            pl.BlockSpec((tm, tn), lambda i, b: (b * nt + i, 0)),
            pl.BlockSpec((1, tn), lambda i, b: (0, 0)),
            pl.BlockSpec((tm, tn), lambda i, b: (b * nt + i, 0)),
            _ANY,
        ],
        out_specs=pl.BlockSpec((None, tm, tn), lambda i, b: (SLOT_HYENA, row0 // tm + b * nt + i, 0)),
        out_shape=jax.ShapeDtypeStruct(bb.shape, bb.dtype),
        input_output_aliases={7: 0},
        compiler_params=_cparams("arbitrary", "arbitrary"),
        name="dft_inv",
    )(wi, wi, p_re, p_im, vx, bias.reshape(1, tn), x0, bb)


def _hyena(proj, lp, consts, bb, seq, row0):
    z, deltas, (wf, wi) = consts
    x0, vx, vb = _hy_prep(proj, lp['hy_short_w'], lp['hy_short_b'], seq, row0 // seq)
    k, nyq = _hy_filter(lp, z, deltas, seq)
    hf = _dft_filter(wf, k)
    p_re, p_im = _dft_fwd(wf, vb, hf, nyq)
    return _dft_inv(wi, p_re, p_im, vx, lp['hy_bias'], x0, bb, row0)


def _s5_params(lp, n_seg_steps):
    a_re, a_im = lp['s5_a_re'], lp['s5_a_im']
    dt = jnp.exp(lp['s5_log_dt'])[..., None]
    mag = jnp.exp(a_re * dt)
    ab_re, ab_im = mag * jnp.cos(a_im * dt), mag * jnp.sin(a_im * dt)
    den = a_re * a_re + a_im * a_im
    nr, ni = ab_re - 1.0, ab_im
    cf_re = (nr * a_re + ni * a_im) / den
    cf_im = (ni * a_re - nr * a_im) / den
    b_re, b_im = lp['s5_b_re'], lp['s5_b_im']
    bb_re = cf_re[..., None] * b_re - cf_im[..., None] * b_im
    bb_im = cf_re[..., None] * b_im + cf_im[..., None] * b_re
    eye = jnp.eye(S5_CHUNK_GROUPS, dtype=F32)

    def bdiag_in(m):
        m = m.reshape(2, S5_NCHUNK, S5_CHUNK_GROUPS, S5_STATE, S5_GC)
        return jnp.einsum('dqgpc,gh->dqgchp', m, eye).reshape(2, S5_NCHUNK, S5_CHUNK_CH, S5_CHUNK_ST)

    def bdiag_out(m):
        m = m.reshape(2, S5_NCHUNK, S5_CHUNK_GROUPS, S5_GC, S5_STATE)
        return jnp.einsum('dqgcp,gh->dqhpgc', m, eye).reshape(2, S5_NCHUNK, S5_CHUNK_ST, S5_CHUNK_CH)

    bbd = jnp.concatenate([bdiag_in(bb_re), bdiag_in(bb_im)], axis=-1).astype(BF16)
    cbd = jnp.concatenate([bdiag_out(lp['s5_c_re']), -bdiag_out(lp['s5_c_im'])], axis=-2).astype(BF16)
    a = jnp.stack([ab_re.reshape(2, -1), ab_im.reshape(2, -1)], axis=1)
    aks = []
    for steps in n_seg_steps:
        pr, pi = ab_re, ab_im
        for _ in range(int(math.log2(steps))):
            pr, pi = pr * pr - pi * pi, 2.0 * pr * pi
        aks.append(jnp.stack([pr.reshape(2, -1), pi.reshape(2, -1)], axis=1))
    return [(bbd, cbd, a, ak) for ak in aks]


S5_BLOCK = 256


def _s5_scan_body(u_ref, bbd_ref, cbd_ref, a_ref, ak_ref, dsk_ref, h0_ref, y_ref, hl_ref,
                  xr0_ref, xr1_ref, xi0_ref, xi1_ref, yd0_ref, yd1_ref, up_ref, ub_ref, *, seq):
    nk = seq // S5_LANES
    nblk = seq // S5_BLOCK
    tiles = S5_BLOCK // S5_LANES
    shape = (S5_LANES, S5_CHUNK_ST)
    xr_ref, xi_ref, yd_ref = (xr0_ref, xr1_ref), (xi0_ref, xi1_ref), (yd0_ref, yd1_ref)
    for j in range(S5_LANES):
        up_ref[pl.ds(j, nk, stride=S5_LANES), :] = u_ref[pl.ds(j * nk, nk), :]
    ub_ref[...] = up_ref[...].astype(BF16)
    row = lax.broadcasted_iota(jnp.int32, shape, 0)
    zero = jnp.zeros(shape, F32)

    def block(r):
        return pl.ds(pl.multiple_of(r * S5_BLOCK, S5_BLOCK), S5_BLOCK)

    def in_proj(d, rows):
        xr_ref[d][rows, :] = _dot(ub_ref[rows, :], bbd_ref[d, :, :S5_CHUNK_ST])
        xi_ref[d][rows, :] = _dot(ub_ref[rows, :], bbd_ref[d, :, S5_CHUNK_ST:])

    def out_proj(d, rows):
        yd_ref[d][rows, :] = (_dot(xr_ref[d][rows, :].astype(BF16), cbd_ref[d, :S5_CHUNK_ST, :])
                              + _dot(xi_ref[d][rows, :].astype(BF16), cbd_ref[d, S5_CHUNK_ST:, :]))

    def coeffs(d):
        return jnp.broadcast_to(a_ref[d, 0:1, :], shape), jnp.broadcast_to(a_ref[d, 1:2, :], shape)

    def block_tiles(d, r):
        base = pl.multiple_of(r * S5_BLOCK, S5_BLOCK)
        order = range(tiles) if d == 0 else range(tiles - 1, -1, -1)
        return [pl.ds(base + t * S5_LANES, S5_LANES) for t in order]

    def scan_block(d, r, carry):
        ar, ai = coeffs(d)
        xr, xi = carry
        for rows in block_tiles(d, r):
            xr, xi = (ar * xr - ai * xi + xr_ref[d][rows, :], ar * xi + ai * xr + xi_ref[d][rows, :])
            xr_ref[d][rows, :] = xr
            xi_ref[d][rows, :] = xi
        return xr, xi

    def fix_block(d, r, carry):
        ar, ai = coeffs(d)
        gr, gi = carry
        for rows in block_tiles(d, r):
            gr, gi = ar * gr - ai * gi, ar * gi + ai * gr
            xr_ref[d][rows, :] += gr
            xi_ref[d][rows, :] += gi
        return gr, gi

    def visit(d, s):
        return s if d == 0 else nblk - 1 - s

    def entering_states(d, er, ei):
        akr, aki = ak_ref[d, 0:1, :], ak_ref[d, 1:2, :]
        hr, hi = h0_ref[2 * d:2 * d + 1, :], h0_ref[2 * d + 1:2 * d + 2, :]
        in_r, in_i = zero, zero
        for j in (range(S5_LANES) if d == 0 else range(S5_LANES - 1, -1, -1)):
            in_r = jnp.where(row == j, hr, in_r)
            in_i = jnp.where(row == j, hi, in_i)
            hr, hi = (akr * hr - aki * hi + er[j:j + 1, :], akr * hi + aki * hr + ei[j:j + 1, :])
        hl_ref[2 * d:2 * d + 1, :] = hr
        hl_ref[2 * d + 1:2 * d + 2, :] = hi
        return in_r, in_i

    def fix_all(d, ins):
        lax.fori_loop(0, nblk, lambda s, carry: fix_block(d, visit(d, s), carry), ins)

    in_proj(0, slice(None))

    def scan0_body(s, carry):
        in_proj(1, block(s))
        return scan_block(0, s, carry)
    ends0 = lax.fori_loop(0, nblk, scan0_body, (zero, zero))
    fix_all(0, entering_states(0, *ends0))

    def scan1_body(s, carry):
        out_proj(0, block(s))
        return scan_block(1, visit(1, s), carry)
    ends1 = lax.fori_loop(0, nblk, scan1_body, (zero, zero))
    fix_all(1, entering_states(1, *ends1))
    out_proj(1, slice(None))

    up_ref[...] = yd_ref[0][...] + yd_ref[1][...] + up_ref[...] * (dsk_ref[0] + dsk_ref[1])
    for j in range(S5_LANES):
        y_ref[pl.ds(j * nk, nk), :] = up_ref[pl.ds(j, nk, stride=S5_LANES), :]


def _s5_scan(proj, params, dskip, h0, seq, row_blk0):
    bbd, cbd, a, ak = params
    nq = S5_NCHUNK
    col0 = OFF_S5 // S5_CHUNK_CH
    return pl.pallas_call(
        functools.partial(_s5_scan_body, seq=seq),
        grid=(BATCH, nq),
        in_specs=[
            pl.BlockSpec((seq, S5_CHUNK_CH), lambda b, q: (row_blk0 + b, col0 + q)),
            pl.BlockSpec((2, None, S5_CHUNK_CH, 2 * S5_CHUNK_ST), lambda b, q: (0, q, 0, 0)),
            pl.BlockSpec((2, None, 2 * S5_CHUNK_ST, S5_CHUNK_CH), lambda b, q: (0, q, 0, 0)),
            pl.BlockSpec((2, 2, S5_CHUNK_ST), lambda b, q: (0, 0, q)),
            pl.BlockSpec((2, 2, S5_CHUNK_ST), lambda b, q: (0, 0, q)),
            pl.BlockSpec((2, 1, S5_CHUNK_CH), lambda b, q: (0, 0, q)),
            pl.BlockSpec((None, 4, S5_CHUNK_ST), lambda b, q: (b, 0, q)),
        ],
        out_specs=[
            pl.BlockSpec((seq, S5_CHUNK_CH), lambda b, q: (b, q)),
            pl.BlockSpec((None, 4, S5_CHUNK_ST), lambda b, q: (b, 0, q)),
        ],
        out_shape=[
            jax.ShapeDtypeStruct((BATCH * seq, S5_WIDTH), F32),
            jax.ShapeDtypeStruct((BATCH, 4, S5_GROUPS * S5_STATE), F32),
        ],
        scratch_shapes=[pltpu.VMEM((seq, S5_CHUNK_ST), F32)] * 4 + [pltpu.VMEM((seq, S5_CHUNK_CH), F32)] * 3
        + [pltpu.VMEM((seq, S5_CHUNK_CH), BF16)],
        compiler_params=_cparams("arbitrary", "arbitrary"),
        name="s5_scan",
    )(proj, bbd, cbd, a, ak, dskip.reshape(2, 1, S5_WIDTH), h0)


def _s5_glu_body(y_ref, w_ref, b_ref, o_ref):
    g = _dot(jax.nn.gelu(y_ref[...]).astype(BF16), w_ref[...].astype(BF16)) + b_ref[...]
    o_ref[...] = (g[:, :S5_WIDTH] * jax.nn.sigmoid(g[:, S5_WIDTH:])).astype(BF16)


def _s5_glu(y, w, b, bb, row0):
    n_rows = y.shape[0]
    tm = TM_EW
    return pl.pallas_call(
        _keep_branch_buffer(_s5_glu_body, 3),
        grid=(n_rows // tm,),
        in_specs=[
            pl.BlockSpec((tm, S5_WIDTH), lambda i: (i, 0)),
            pl.BlockSpec((S5_WIDTH, 2 * S5_WIDTH), lambda i: (0, 0)),
            pl.BlockSpec((1, 2 * S5_WIDTH), lambda i: (0, 0)),
            _ANY,
        ],
        out_specs=pl.BlockSpec((None, tm, S5_WIDTH), lambda i: (SLOT_S5, row0 // tm + i, 0)),
        out_shape=jax.ShapeDtypeStruct(bb.shape, bb.dtype),
        input_output_aliases={3: 0},
        compiler_params=_cparams("arbitrary"),
        name="s5_glu",
    )(y, w, b.reshape(1, -1), bb)


def _merge_body(u_ref, *refs):
    wg_refs, bg_refs, y_refs, wb_refs = (refs[k * N_BRANCH:(k + 1) * N_BRANCH] for k in range(4))
    o_ref = refs[4 * N_BRANCH]
    u = u_ref[...]
    acc = None
    for n in range(N_BRANCH):
        gate = jax.nn.sigmoid(_dot(u, wg_refs[n][...].astype(BF16)) + bg_refs[n][...])
        contrib = gate * _dot(y_refs[n][...], wb_refs[n][...].astype(BF16))
        acc = contrib if acc is None else acc + contrib
    o_ref[...] = acc.astype(BF16)


def _merge(u, w_gate, b_gate, branches, w_branch, l, n_rows):
    tc = MXU_COLS
    ncol = D_MODEL // tc
    b_gate = b_gate.reshape(1, -1)
    per_branch = lambda make: [make(n) for n in range(N_BRANCH)]
    return pl.pallas_call(
        _merge_body,
        grid=(n_rows // TM, ncol),
        in_specs=[pl.BlockSpec((TM, D_MODEL), lambda i, c: (i, 0))]
        + per_branch(lambda n: pl.BlockSpec((None, D_MODEL, tc), lambda i, c: (l, 0, n * ncol + c)))
        + per_branch(lambda n: pl.BlockSpec((1, tc), lambda i, c: (0, n * ncol + c)))
        + per_branch(lambda n: pl.BlockSpec((None, TM, BRANCH_WIDTH), lambda i, c: (n, i, 0)))
        + per_branch(lambda n: pl.BlockSpec((None, None, BRANCH_WIDTH, tc), lambda i, c: (l, n, 0, c))),
        out_specs=pl.BlockSpec((TM, tc), lambda i, c: (i, c)),
        out_shape=jax.ShapeDtypeStruct((n_rows, D_MODEL), BF16),
        compiler_params=_cparams("arbitrary", "arbitrary"),
        name="merge",
    )(u, *([w_gate] * N_BRANCH), *([b_gate] * N_BRANCH), *([branches] * N_BRANCH), *([w_branch] * N_BRANCH))


def _out_body(m_ref, w_ref, x_ref, g_ref, o_ref):
    o_ref[...] = x_ref[...] + g_ref[...] * _dot(m_ref[...], w_ref[...].astype(BF16))


def _out_proj(merged, w_out, l, x, mods, n_rows):
    tn = 1024
    gate = pl.BlockSpec((None, None, 1, tn), lambda j, i: (_mod_row(i, TM), 5, 0, j))
    return pl.pallas_call(
        _out_body,
        grid=(D_MODEL // tn, n_rows // TM),
        in_specs=[
            pl.BlockSpec((TM, D_MODEL), lambda j, i: (i, 0)),
            pl.BlockSpec((None, D_MODEL, tn), lambda j, i: (l, 0, j)),
            pl.BlockSpec((TM, tn), lambda j, i: (i, j)),
            gate,
        ],
        out_specs=pl.BlockSpec((TM, tn), lambda j, i: (i, j)),
        out_shape=jax.ShapeDtypeStruct((n_rows, D_MODEL), F32),
        compiler_params=_cparams("arbitrary", "arbitrary"),
        name="out_proj",
    )(merged, w_out, x, mods)


def _mixer(x, g, lp, big, w_in_bf16, l, mods, consts, bb, last):
    cos, sin, hy_lat, hy_ctx = consts
    u = _normmod(x, g, mods, 3, MT)
    proj = _proj(u, w_in_bf16)
    q, k, v = _qkv(proj, cos, sin, lp['q_norm'], lp['k_norm'])

    ctx_blk = N_LAT // CTX_LEN
    zero_h = jnp.zeros((BATCH, 4, S5_GROUPS * S5_STATE), F32)
    par_ctx, par_lat = _s5_params(lp, (CTX_LEN // S5_LANES, SEQ // S5_LANES))
    ys_ctx, h_ctx = _s5_scan(proj, par_ctx, lp['s5_d'], zero_h, CTX_LEN, ctx_blk)
    ys_lat, _ = _s5_scan(proj, par_lat, lp['s5_d'], h_ctx, SEQ, 0)

    bb = _s5_glu(ys_lat, lp['s5_glu_w'], lp['s5_glu_b'], bb, 0)
    bb = _attn_lat(q, k, v, bb)
    bb = _pool(proj, lp['pool_w'], lp['pool_scale'], bb, SEQ, 0)
    bb = _hyena(proj, lp, hy_lat, bb, SEQ, 0)
    if not last:
        bb = _s5_glu(ys_ctx, lp['s5_glu_w'], lp['s5_glu_b'], bb, N_LAT)
        bb = _attn_ctx(q, k, v, bb)
        bb = _pool(proj, lp['pool_w'], lp['pool_scale'], bb, CTX_LEN, ctx_blk)
        bb = _hyena(proj, lp, hy_ctx, bb, CTX_LEN, N_LAT)

    n_rows = N_LAT if last else MT
    merged = _merge(u, big['w_gate'], lp['b_gate'], bb, big['w_branch'], l, n_rows)
    return _out_proj(merged, big['w_out'], l, x, mods, n_rows), bb


def kernel(x, c, ctx, c_ctx, w_ada, b_ada, norm_ffn1, norm_mix, norm_ffn2, norm_final, ffn1_wi, ffn1_wo, ffn2_wi, ffn2_wo, w_in, w_gate, b_gate, w_branch, w_out, pool_w, pool_scale, q_norm, k_norm, hy_short_w, hy_short_b, hy_f1_w, hy_f1_b, hy_f2_w, hy_f2_b, hy_f3_w, hy_freq, hy_bias, s5_a_re, s5_a_im, s5_log_dt, s5_b_re, s5_b_im, s5_c_re, s5_c_im, s5_d, s5_glu_w, s5_glu_b):
    big = dict(w_in=w_in, w_gate=w_gate, w_branch=w_branch, w_out=w_out)
    per_layer = dict(
        b_gate=b_gate, pool_w=pool_w,
        pool_scale=pool_scale, q_norm=q_norm, k_norm=k_norm, hy_short_w=hy_short_w, hy_short_b=hy_short_b,
        hy_f1_w=hy_f1_w, hy_f1_b=hy_f1_b, hy_f2_w=hy_f2_w, hy_f2_b=hy_f2_b, hy_f3_w=hy_f3_w, hy_freq=hy_freq,
        hy_bias=hy_bias, s5_a_re=s5_a_re, s5_a_im=s5_a_im, s5_log_dt=s5_log_dt, s5_b_re=s5_b_re, s5_b_im=s5_b_im,
        s5_c_re=s5_c_re, s5_c_im=s5_c_im, s5_d=s5_d, s5_glu_w=s5_glu_w, s5_glu_b=s5_glu_b)

    cos, sin = _rope_tables()
    deltas = _hy_deltas()
    consts = (cos, sin,
              (_hy_feats(SEQ), deltas, _dft_matrices(SEQ)),
              (_hy_feats(CTX_LEN), deltas, _dft_matrices(CTX_LEN)))

    cc = jnp.concatenate([c, c_ctx[None], jnp.zeros((8 - BATCH - 1, D_MODEL), F32)], axis=0)
    mods_all = _ada(cc, w_ada, b_ada).reshape(DEPTH, 8, N_MOD, 1, D_MODEL)

    xs = jnp.concatenate([x.reshape(N_LAT, D_MODEL), ctx.reshape(N_CTX, D_MODEL)], axis=0)
    bb = jnp.zeros((N_BRANCH, MT, BRANCH_WIDTH), BF16)
    for l in range(DEPTH):
        last = l == DEPTH - 1
        lp = {name: w[l] for name, w in per_layer.items()}
        mods = mods_all[l]
        xs, w_in_bf16 = _ffn(xs, mods, norm_ffn1[l], ffn1_wi, ffn1_wo, l, 0, MT, side=w_in)
        xs, bb = _mixer(xs, norm_mix[l], lp, big, w_in_bf16, l, mods, consts, bb, last)
        xs = _ffn(xs, mods, norm_ffn2[l], ffn2_wi, ffn2_wo, l, 6, N_LAT if last else MT)
    return _final_norm(xs, norm_final, N_LAT).reshape(BATCH, SEQ, D_MODEL)
```

```python
import functools
import math

import jax
import jax.numpy as jnp
from jax import lax
from jax.experimental import pallas as pl
from jax.experimental.pallas import tpu as pltpu

F32 = jnp.float32
BF16 = jnp.bfloat16

D_MODEL = 2048
BATCH = 4
SEQ = 2048
DEPTH = 4
GRID_W = 64
CTX_LEN = 256
D_FF = 5632
N_MOD = 9
EPS = 1e-6

POOL_WINDOWS = (2, 4, 8, 16)
HEAD_DIM = 128
N_Q_HEADS = 4
N_KV_HEADS = 2
Q_GROUP = N_Q_HEADS // N_KV_HEADS
ROPE_THETA = 10000.0
ROPE_FREQS = HEAD_DIM // 4

HYENA_WIDTH = 512
HYENA_EMB = 33
HYENA_BANDS = (HYENA_EMB - 1) // 2
HYENA_HIDDEN = 64
HYENA_TARGET = 1e-2
HYENA_FAST_PCT = 0.3
HYENA_SLOW_PCT = 1.5

S5_WIDTH = 512
S5_GC = 16
S5_GROUPS = S5_WIDTH // S5_GC
S5_STATE = 64
S5_LANES = 8
S5_CHUNK_GROUPS = 8
S5_CHUNK_CH = S5_CHUNK_GROUPS * S5_GC
S5_CHUNK_ST = S5_CHUNK_GROUPS * S5_STATE
S5_NCHUNK = S5_GROUPS // S5_CHUNK_GROUPS

N_BRANCH = 4
BRANCH_WIDTH = 512
LANE = 128
COL = 512

OFF_Q = 512
OFF_KV = 1024
OFF_HY = 1536
OFF_S5 = 3072
IN_WIDTH = 3584

N_LAT = BATCH * SEQ
N_CTX = BATCH * CTX_LEN
MT = N_LAT + N_CTX

TM = 1024
TM_EW = 512
VMEM_LIMIT = 56 * 1024 * 1024


def _cparams(*sem):
    return pltpu.CompilerParams(dimension_semantics=sem, vmem_limit_bytes=VMEM_LIMIT)


def _dot(a, b):
    return jnp.dot(a, b, preferred_element_type=F32)


SLOT_POOL, SLOT_ATTN, SLOT_HYENA, SLOT_S5 = range(4)
_ANY = pl.BlockSpec(memory_space=pl.ANY)


def _keep_branch_buffer(body, n_in):
    def wrapped(*refs):
        return body(*refs[:n_in], *refs[n_in + 1:])
    return wrapped


def _mod_row(i, tm):
    return jnp.minimum((i * tm) // SEQ, BATCH)


def _mod_spec(tm, k):
    return pl.BlockSpec((None, None, 1, D_MODEL), lambda i, *_: (_mod_row(i, tm), k, 0, 0))


def _ada_body(c_ref, w_ref, b_ref, o_ref):
    c = c_ref[...]
    a = (c * jax.nn.sigmoid(c)).astype(BF16)
    o_ref[...] = _dot(a, w_ref[...].astype(BF16)) + b_ref[...]


def _ada(cc, w_ada, b_ada):
    tn = 1024
    nw = N_MOD * D_MODEL
    return pl.pallas_call(
        _ada_body,
        grid=(DEPTH, nw // tn),
        in_specs=[
            pl.BlockSpec((8, D_MODEL), lambda l, j: (0, 0)),
            pl.BlockSpec((None, D_MODEL, tn), lambda l, j: (l, 0, j)),
            pl.BlockSpec((None, 1, tn), lambda l, j: (l, 0, j)),
        ],
        out_specs=pl.BlockSpec((None, 8, tn), lambda l, j: (l, 0, j)),
        out_shape=jax.ShapeDtypeStruct((DEPTH, 8, nw), F32),
        compiler_params=_cparams("arbitrary", "arbitrary"),
        name="ada",
    )(cc, w_ada, b_ada.reshape(DEPTH, 1, nw))


NORM_ROWS = 16


def _rows_loop(n_rows, step):
    for c in range(n_rows // NORM_ROWS):
        step(pl.ds(c * NORM_ROWS, NORM_ROWS))


def _rms_scale_rows(x_ref, r_ref, scale, shift, o_ref):
    def stats(rows):
        x = x_ref[rows, :]
        r_ref[rows, :] = lax.rsqrt(jnp.mean(x * x, axis=-1, keepdims=True) + EPS)

    def apply(rows):
        y = x_ref[rows, :] * r_ref[rows, :] * scale
        if shift is not None:
            y = y + shift
        o_ref[rows, :] = y.astype(o_ref.dtype)

    _rows_loop(x_ref.shape[0], stats)
    _rows_loop(x_ref.shape[0], apply)


def _normmod_body(x_ref, g_ref, sh_ref, sc_ref, o_ref, r_ref):
    _rms_scale_rows(x_ref, r_ref, g_ref[...] * (1.0 + sc_ref[...]), sh_ref[...], o_ref)


def _norm_body(x_ref, g_ref, o_ref, r_ref):
    _rms_scale_rows(x_ref, r_ref, g_ref[...], None, o_ref)


def _normmod(x, g, mods, base, n_rows):
    tm = TM_EW
    return pl.pallas_call(
        _normmod_body,
        grid=(n_rows // tm,),
        in_specs=[
            pl.BlockSpec((tm, D_MODEL), lambda i: (i, 0)),
            pl.BlockSpec((1, D_MODEL), lambda i: (0, 0)),
            _mod_spec(tm, base),
            _mod_spec(tm, base + 1),
        ],
        out_specs=pl.BlockSpec((tm, D_MODEL), lambda i: (i, 0)),
        out_shape=jax.ShapeDtypeStruct((n_rows, D_MODEL), BF16),
        scratch_shapes=[pltpu.VMEM((tm, 1), F32)],
        compiler_params=_cparams("arbitrary"),
        name="normmod",
    )(x, g.reshape(1, D_MODEL), mods, mods)


def _final_norm(x, g, n_rows):
    tm = TM_EW
    return pl.pallas_call(
        _norm_body,
        grid=(n_rows // tm,),
        in_specs=[
            pl.BlockSpec((tm, D_MODEL), lambda i: (i, 0)),
            pl.BlockSpec((1, D_MODEL), lambda i: (0, 0)),
        ],
        out_specs=pl.BlockSpec((tm, D_MODEL), lambda i: (i, 0)),
        out_shape=jax.ShapeDtypeStruct((n_rows, D_MODEL), F32),
        scratch_shapes=[pltpu.VMEM((tm, 1), F32)],
        compiler_params=_cparams("arbitrary"),
        name="final_norm",
    )(x, g.reshape(1, D_MODEL))


MXU_COLS = 256


WO_SLAB = 64
WO_NSLAB = D_FF // WO_SLAB


NEXT_ROWS = 128
NEXT_SLICES = TM // NEXT_ROWS


def _ffn_a_body(u0_ref, xn_ref, g_ref, sh_ref, sc_ref, wa_ref, wb_ref, wo_ref, h_ref, wob_ref, u2_ref, r_ref):
    i, j = pl.program_id(0), pl.program_id(1)
    slot = i % 2

    @pl.when((i == 0) & (j == 0))
    def _():
        u2_ref[0] = u0_ref[...]

    u = u2_ref[slot]
    for c in range(0, h_ref.shape[1], MXU_COLS):
        sl = slice(c, c + MXU_COLS)
        a = _dot(u, wa_ref[:, sl].astype(BF16))
        b = _dot(u, wb_ref[:, sl].astype(BF16))
        h_ref[:, sl] = (a * jax.nn.sigmoid(a) * b).astype(BF16)

    dst = pl.ds(pl.multiple_of(jnp.minimum(j, NEXT_SLICES - 1) * NEXT_ROWS, NEXT_ROWS), NEXT_ROWS)
    _rms_scale_rows(xn_ref, r_ref, g_ref[...] * (1.0 + sc_ref[...]), sh_ref[...], u2_ref.at[1 - slot, dst, :])

    @pl.when(pl.program_id(0) * pl.num_programs(1) + pl.program_id(1) < WO_NSLAB)
    def _():
        wob_ref[...] = wo_ref[...].astype(BF16)


def _ffn_a(x, g, mods, base, wi, wo, l, n_rows):
    tf = 512
    nf = D_FF // tf
    ni = n_rows // TM
    assert ni * nf >= WO_NSLAB and nf >= NEXT_SLICES

    def wo_blk(i, j):
        return jnp.minimum(i * nf + j, WO_NSLAB - 1)

    def nxt(i):
        return jnp.minimum(i + 1, ni - 1)

    def next_mod(k):
        return pl.BlockSpec((None, None, 1, D_MODEL), lambda i, j: (_mod_row(nxt(i), TM), k, 0, 0))

    u0 = _normmod(x, g, mods, base, TM)
    return pl.pallas_call(
        _ffn_a_body,
        grid=(ni, nf),
        in_specs=[
            pl.BlockSpec((TM, D_MODEL), lambda i, j: (0, 0)),
            pl.BlockSpec((NEXT_ROWS, D_MODEL),
                         lambda i, j: (nxt(i) * NEXT_SLICES + jnp.minimum(j, NEXT_SLICES - 1), 0)),
            pl.BlockSpec((1, D_MODEL), lambda i, j: (0, 0)),
            next_mod(base),
            next_mod(base + 1),
            pl.BlockSpec((None, D_MODEL, tf), lambda i, j: (l, 0, j)),
            pl.BlockSpec((None, D_MODEL, tf), lambda i, j: (l, 0, j + nf)),
            pl.BlockSpec((None, WO_SLAB, D_MODEL), lambda i, j: (l, wo_blk(i, j), 0)),
        ],
        out_specs=[
            pl.BlockSpec((TM, tf), lambda i, j: (i, j)),
            pl.BlockSpec((WO_SLAB, D_MODEL), lambda i, j: (wo_blk(i, j), 0)),
        ],
        out_shape=[
            jax.ShapeDtypeStruct((n_rows, D_FF), BF16),
            jax.ShapeDtypeStruct((D_FF, D_MODEL), BF16),
        ],
        scratch_shapes=[pltpu.VMEM((2, TM, D_MODEL), BF16), pltpu.VMEM((NEXT_ROWS, 1), F32)],
        compiler_params=_cparams("arbitrary", "arbitrary"),
        name="ffn_a",
    )(u0, x, g.reshape(1, D_MODEL), mods, mods, wi, wi, wo)


SIDE_SLAB = 64


def _ffn_b_body(h_ref, w_ref, x_ref, g_ref, *rest, n_slab):
    o_ref = rest[-2] if n_slab else rest[-1]
    h = h_ref[...]
    for c in range(0, o_ref.shape[1], MXU_COLS):
        sl = slice(c, c + MXU_COLS)
        o_ref[:, sl] = x_ref[:, sl] + (0.5 * g_ref[:, sl]) * _dot(h, w_ref[:, sl])

    if n_slab:
        side_ref, _, side_out_ref = rest

        @pl.when(pl.program_id(0) * pl.num_programs(1) + pl.program_id(1) < n_slab)
        def _():
            side_out_ref[...] = side_ref[...].astype(BF16)


def _ffn_b(h, wo_bf16, x, mods, gate_idx, n_rows, side=None, l=None):
    tn = 512
    nj = D_MODEL // tn
    gate = pl.BlockSpec((None, None, 1, tn), lambda i, j: (_mod_row(i, TM), gate_idx, 0, j))
    in_specs = [
        pl.BlockSpec((TM, D_FF), lambda i, j: (i, 0)),
        pl.BlockSpec((D_FF, tn), lambda i, j: (0, j)),
        pl.BlockSpec((TM, tn), lambda i, j: (i, j)),
        gate,
    ]
    out_specs = [pl.BlockSpec((TM, tn), lambda i, j: (i, j))]
    out_shape = [jax.ShapeDtypeStruct((n_rows, D_MODEL), F32)]
    args = [h, wo_bf16, x, mods]
    n_slab = 0
    if side is not None:
        _, rows, cols = side.shape
        n_slab = rows // SIDE_SLAB
        assert (n_rows // TM) * nj >= n_slab

        def slab(i, j):
            return jnp.minimum(i * nj + j, n_slab - 1)

        in_specs.append(pl.BlockSpec((None, SIDE_SLAB, cols), lambda i, j: (l, slab(i, j), 0)))
        out_specs.append(pl.BlockSpec((SIDE_SLAB, cols), lambda i, j: (slab(i, j), 0)))
        out_shape.append(jax.ShapeDtypeStruct((rows, cols), BF16))
        args.append(side)
    res = pl.pallas_call(
        functools.partial(_ffn_b_body, n_slab=n_slab),
        grid=(n_rows // TM, nj),
        in_specs=in_specs,
        out_specs=out_specs,
        out_shape=out_shape,
        compiler_params=_cparams("arbitrary", "arbitrary"),
        name="ffn_b",
    )(*args)
    return res if side is not None else res[0]


def _ffn(x, mods, g, wi, wo, l, base, n_rows, side=None):
    h, wo_bf16 = _ffn_a(x, g, mods, base, wi, wo, l, n_rows)
    return _ffn_b(h, wo_bf16, x, mods, base + 2, n_rows, side, l)


def _proj_body(u_ref, w_ref, o_ref):
    o_ref[...] = _dot(u_ref[...], w_ref[...])


def _proj(u, w_in_bf16):
    tn = COL
    tm = 1536
    return pl.pallas_call(
        _proj_body,
        grid=(MT // tm, IN_WIDTH // tn),
        in_specs=[
            pl.BlockSpec((tm, D_MODEL), lambda i, j: (i, 0)),
            pl.BlockSpec((D_MODEL, tn), lambda i, j: (0, j)),
        ],
        out_specs=pl.BlockSpec((tm, tn), lambda i, j: (i, j)),
        out_shape=jax.ShapeDtypeStruct((MT, IN_WIDTH), F32),
        compiler_params=_cparams("arbitrary", "arbitrary"),
        name="proj",
    )(u, w_in_bf16)


def _rope_tables():
    t = jnp.arange(SEQ)
    rows = (t // GRID_W).astype(F32)
    cols = (t % GRID_W).astype(F32)
    freqs = ROPE_THETA ** (-jnp.arange(ROPE_FREQS, dtype=F32) / ROPE_FREQS)
    ar = rows[:, None] * freqs[None, :]
    ac = cols[:, None] * freqs[None, :]
    cos = jnp.concatenate([jnp.cos(ar), jnp.cos(ar), jnp.cos(ac), jnp.cos(ac)], axis=-1)
    sin = jnp.concatenate([-jnp.sin(ar), jnp.sin(ar), -jnp.sin(ac), jnp.sin(ac)], axis=-1)
    cos = jnp.concatenate([cos, jnp.ones((TM_EW, HEAD_DIM), F32)], axis=0)
    sin = jnp.concatenate([sin, jnp.zeros((TM_EW, HEAD_DIM), F32)], axis=0)
    return cos, sin


def _lane_matrices():
    lane = jnp.arange(HEAD_DIM)
    partner = jnp.where(lane % (2 * ROPE_FREQS) < ROPE_FREQS, lane + ROPE_FREQS, lane - ROPE_FREQS)
    swap = (lane[:, None] == partner[None, :]).astype(BF16)
    return jnp.ones((HEAD_DIM, HEAD_DIM), BF16), swap


def _dot_split(x, m_ref):
    hi = x.astype(BF16)
    lo = (x - hi.astype(F32)).astype(BF16)
    return _dot(hi, m_ref[...]) + _dot(lo, m_ref[...])


def _qkv_body(q_ref, kv_ref, cos_ref, sin_ref, qn_ref, kn_ref, ones_ref, swap_ref, qo_ref, ko_ref, vo_ref):
    cos = cos_ref[...]
    sin = sin_ref[...]

    def norm_rope(xh, g):
        ss = _dot_split(xh * xh, ones_ref)
        y = xh * lax.rsqrt(ss * (1.0 / HEAD_DIM) + EPS) * g
        return y * cos + _dot_split(y, swap_ref) * sin

    scale = math.log2(math.e) / math.sqrt(HEAD_DIM)
    for h in range(N_Q_HEADS):
        sl = slice(h * HEAD_DIM, (h + 1) * HEAD_DIM)
        qo_ref[:, sl] = (norm_rope(q_ref[:, sl], qn_ref[...]) * scale).astype(BF16)
    for h in range(N_KV_HEADS):
        sl = slice(h * HEAD_DIM, (h + 1) * HEAD_DIM)
        ko_ref[:, sl] = norm_rope(kv_ref[:, sl], kn_ref[...]).astype(BF16)
    kvw = N_KV_HEADS * HEAD_DIM
    for h in range(N_KV_HEADS):
        vo_ref[:, 2 * h * HEAD_DIM:(2 * h + 1) * HEAD_DIM] = (
            kv_ref[:, kvw + h * HEAD_DIM:kvw + (h + 1) * HEAD_DIM].astype(BF16))
        vo_ref[:, (2 * h + 1) * HEAD_DIM:(2 * h + 2) * HEAD_DIM] = jnp.ones((vo_ref.shape[0], HEAD_DIM), BF16)


def _qkv(proj, cos, sin, q_norm, k_norm):
    tm = TM_EW
    n_lat_tiles = N_LAT // tm
    per_seq = SEQ // tm
    kvw = N_KV_HEADS * HEAD_DIM

    def tab(i):
        return (jnp.where(i < n_lat_tiles, i % per_seq, per_seq), 0)

    return pl.pallas_call(
        _qkv_body,
        grid=(MT // tm,),
        in_specs=[
            pl.BlockSpec((tm, COL), lambda i: (i, OFF_Q // COL)),
            pl.BlockSpec((tm, COL), lambda i: (i, OFF_KV // COL)),
            pl.BlockSpec((tm, HEAD_DIM), tab),
            pl.BlockSpec((tm, HEAD_DIM), tab),
            pl.BlockSpec((1, HEAD_DIM), lambda i: (0, 0)),
            pl.BlockSpec((1, HEAD_DIM), lambda i: (0, 0)),
            pl.BlockSpec((HEAD_DIM, HEAD_DIM), lambda i: (0, 0)),
            pl.BlockSpec((HEAD_DIM, HEAD_DIM), lambda i: (0, 0)),
        ],
        out_specs=[
            pl.BlockSpec((tm, COL), lambda i: (i, 0)),
            pl.BlockSpec((tm, kvw), lambda i: (i, 0)),
            pl.BlockSpec((tm, 2 * kvw), lambda i: (i, 0)),
        ],
        out_shape=[
            jax.ShapeDtypeStruct((MT, COL), BF16),
            jax.ShapeDtypeStruct((MT, kvw), BF16),
            jax.ShapeDtypeStruct((MT, 2 * kvw), BF16),
        ],
        compiler_params=_cparams("arbitrary"),
        name="qkv",
    )(proj, proj, cos, sin, q_norm.reshape(1, HEAD_DIM), k_norm.reshape(1, HEAD_DIM), *_lane_matrices())


def _attn_body(*refs, with_lat):
    if with_lat:
        q_ref, kl_ref, vl_ref, kc_ref, vc_ref, o_ref = refs
    else:
        q_ref, kc_ref, vc_ref, o_ref = refs
    nt = (((1,), (1,)), ((), ()))
    half = q_ref.shape[0] // 2
    units = [(slice(r, r + half), slice(g * HEAD_DIM, (g + 1) * HEAD_DIM))
             for g in range(Q_GROUP) for r in (0, half)]
    scores = []
    for rows, sl in units:
        q = q_ref[rows, sl]
        sc = lax.dot_general(q, kc_ref[...], nt, preferred_element_type=F32)
        s_lat = lax.dot_general(q, kl_ref[...], nt, preferred_element_type=F32) if with_lat else None
        scores.append((sc, s_lat))
    for (rows, sl), (sc, s_lat) in zip(units, scores):
        m = jnp.max(sc, axis=-1, keepdims=True)
        if with_lat:
            m = jnp.maximum(m, jnp.max(s_lat, axis=-1, keepdims=True))
        o = _dot(jnp.exp2(sc - m).astype(BF16), vc_ref[...])
        if with_lat:
            o = o + _dot(jnp.exp2(s_lat - m).astype(BF16), vl_ref[...])
        o_ref[rows, sl] = (o[:, :HEAD_DIM] / o[:, HEAD_DIM:]).astype(BF16)


def _attn_lat(q, k, v, bb):
    tq = 512
    nq = SEQ // tq
    ctx_blk = N_LAT // CTX_LEN
    gw = Q_GROUP * HEAD_DIM
    return pl.pallas_call(
        _keep_branch_buffer(functools.partial(_attn_body, with_lat=True), 5),
        grid=(BATCH, N_KV_HEADS, nq),
        in_specs=[
            pl.BlockSpec((tq, gw), lambda b, h, i: (b * nq + i, h)),
            pl.BlockSpec((SEQ, HEAD_DIM), lambda b, h, i: (b, h)),
            pl.BlockSpec((SEQ, 2 * HEAD_DIM), lambda b, h, i: (b, h)),
            pl.BlockSpec((CTX_LEN, HEAD_DIM), lambda b, h, i: (ctx_blk + b, h)),
            pl.BlockSpec((CTX_LEN, 2 * HEAD_DIM), lambda b, h, i: (ctx_blk + b, h)),
            _ANY,
        ],
        out_specs=pl.BlockSpec((None, tq, gw), lambda b, h, i: (SLOT_ATTN, b * nq + i, h)),
        out_shape=jax.ShapeDtypeStruct(bb.shape, bb.dtype),
        input_output_aliases={5: 0},
        compiler_params=_cparams("arbitrary", "arbitrary", "arbitrary"),
        name="attn_lat",
    )(q, k, v, k, v, bb)


def _attn_ctx(q, k, v, bb):
    ctx_blk = N_LAT // CTX_LEN
    gw = Q_GROUP * HEAD_DIM
    return pl.pallas_call(
        _keep_branch_buffer(functools.partial(_attn_body, with_lat=False), 3),
        grid=(BATCH, N_KV_HEADS),
        in_specs=[
            pl.BlockSpec((CTX_LEN, gw), lambda b, h: (ctx_blk + b, h)),
            pl.BlockSpec((CTX_LEN, HEAD_DIM), lambda b, h: (ctx_blk + b, h)),
            pl.BlockSpec((CTX_LEN, 2 * HEAD_DIM), lambda b, h: (ctx_blk + b, h)),
            _ANY,
        ],
        out_specs=pl.BlockSpec((None, CTX_LEN, gw), lambda b, h: (SLOT_ATTN, ctx_blk + b, h)),
        out_shape=jax.ShapeDtypeStruct(bb.shape, bb.dtype),
        input_output_aliases={3: 0},
        compiler_params=_cparams("arbitrary", "arbitrary"),
        name="attn_ctx",
    )(q, k, v, bb)


POOL_PAD = 8


def _pool_body(a_ref, w_ref, s_ref, o_ref, *, seq):
    lp = seq + 2 * POOL_PAD
    t = lax.broadcasted_iota(jnp.int32, (seq, LANE), 0)
    zpad = jnp.zeros((POOL_PAD, LANE), F32)
    for gi, win in enumerate(POOL_WINDOWS):
        sl = slice(gi * LANE, (gi + 1) * LANE)
        a = a_ref[:, sl]
        s = jnp.concatenate([zpad, a, zpad], axis=0)
        s = s + pltpu.roll(s, 1, 0)
        half = 1
        while 2 * half < win:
            s = pltpu.roll(s, half, 0) + pltpu.roll(s, lp - half, 0)
            half *= 2
        s = s[POOL_PAD:POOL_PAD + seq]
        lo = jnp.maximum(t - win // 2, 0)
        hi = jnp.minimum(t + win // 2, seq)
        pooled = s / (hi - lo).astype(F32) - a
        y = _dot(pooled.astype(BF16), w_ref[gi].astype(BF16))
        o_ref[:, sl] = (y * s_ref[:, sl]).astype(BF16)


def _pool(proj, pool_w, pool_scale, bb, seq, row_blk0):
    width = len(POOL_WINDOWS) * LANE
    return pl.pallas_call(
        _keep_branch_buffer(functools.partial(_pool_body, seq=seq), 3),
        grid=(BATCH,),
        in_specs=[
            pl.BlockSpec((seq, width), lambda b: (row_blk0 + b, 0)),
            pl.BlockSpec((len(POOL_WINDOWS), LANE, LANE), lambda b: (0, 0, 0)),
            pl.BlockSpec((1, width), lambda b: (0, 0)),
            _ANY,
        ],
        out_specs=pl.BlockSpec((None, seq, width), lambda b: (SLOT_POOL, row_blk0 + b, 0)),
        out_shape=jax.ShapeDtypeStruct(bb.shape, bb.dtype),
        input_output_aliases={3: 0},
        compiler_params=_cparams("arbitrary"),
        name="pool",
    )(proj, pool_w, pool_scale.reshape(1, width), bb)


def _hy_prep_body(x0_ref, x1_ref, v_ref, w0_ref, w1_ref, wv_ref, b0_ref, b1_ref, bv_ref,
                  x0o_ref, vx_ref, vb_ref, *, seq):
    t = lax.broadcasted_iota(jnp.int32, x0_ref.shape, 0)

    def conv(x_ref, w_ref, b_ref):
        x = x_ref[...]
        prev = jnp.where(t >= 1, pltpu.roll(x, 1, 0), 0.0)
        nxt = jnp.where(t <= seq - 2, pltpu.roll(x, seq - 1, 0), 0.0)
        return prev * w_ref[0:1, :] + x * w_ref[1:2, :] + nxt * w_ref[2:3, :] + b_ref[...]

    x0o_ref[...] = conv(x0_ref, w0_ref, b0_ref)
    vx = conv(v_ref, wv_ref, bv_ref) * conv(x1_ref, w1_ref, b1_ref)
    vx_ref[...] = vx
    vb_ref[...] = vx.astype(BF16)


def _hy_prep(proj, short_w, short_b, seq, row_blk0):
    tc = 2 * LANE
    nc = HYENA_WIDTH // tc
    c0 = OFF_HY // tc
    short_b = short_b.reshape(1, 3 * HYENA_WIDTH)

    def xspec(part):
        return pl.BlockSpec((seq, tc), lambda b, c: (row_blk0 + b, c0 + part * nc + c))

    def wspec(part, rows):
        return pl.BlockSpec((rows, tc), lambda b, c: (0, part * nc + c))

    return pl.pallas_call(
        functools.partial(_hy_prep_body, seq=seq),
        grid=(BATCH, nc),
        in_specs=[xspec(0), xspec(1), xspec(2), wspec(0, 3), wspec(1, 3), wspec(2, 3),
                  wspec(0, 1), wspec(1, 1), wspec(2, 1)],
        out_specs=[
            pl.BlockSpec((seq, tc), lambda b, c: (b, c)),
            pl.BlockSpec((seq, tc), lambda b, c: (b, c)),
            pl.BlockSpec((seq, tc), lambda b, c: (0, b * nc + c)),
        ],
        out_shape=[
            jax.ShapeDtypeStruct((BATCH * seq, HYENA_WIDTH), F32),
            jax.ShapeDtypeStruct((BATCH * seq, HYENA_WIDTH), F32),
            jax.ShapeDtypeStruct((seq, BATCH * HYENA_WIDTH), BF16),
        ],
        compiler_params=_cparams("arbitrary", "arbitrary"),
        name="hy_prep",
    )(proj, proj, proj, short_w, short_w, short_w, short_b, short_b, short_b)


def _hy_feats(seq):
    t = jnp.linspace(0.0, 1.0, seq, dtype=F32)[:, None]
    f = jnp.linspace(1e-4, HYENA_BANDS - 1, HYENA_BANDS, dtype=F32)
    w = 2.0 * math.pi * jnp.arange(seq, dtype=F32) / seq
    fw = w[:, None] * f[None, :]
    z = jnp.concatenate([t, jnp.cos(fw), -jnp.sin(fw)], axis=-1)
    return jnp.pad(z, ((0, 0), (0, LANE - HYENA_EMB)))


def _hy_deltas():
    max_decay = math.log(HYENA_TARGET) / HYENA_FAST_PCT
    min_decay = math.log(HYENA_TARGET) / HYENA_SLOW_PCT
    return jnp.abs(jnp.linspace(min_decay, max_decay, HYENA_WIDTH, dtype=F32)).reshape(1, HYENA_WIDTH)


def _hy_filter_body(z_ref, w1_ref, b1_ref, w2_ref, b2_ref, w3_ref, fr_ref, dl_ref, k_ref, nyq_ref, *, seq):
    hp = lax.Precision.HIGHEST
    freq = fr_ref[...]
    h = jnp.sin(freq * (jnp.dot(z_ref[...], w1_ref[...], precision=hp, preferred_element_type=F32) + b1_ref[...]))
    h = jnp.sin(freq * (jnp.dot(h, w2_ref[...], precision=hp, preferred_element_type=F32) + b2_ref[...]))
    h = jnp.dot(h, w3_ref[...], precision=hp, preferred_element_type=F32)
    ti = lax.broadcasted_iota(jnp.int32, (seq, HYENA_WIDTH), 0)
    decay = jnp.exp(-(ti.astype(F32) * (1.0 / (seq - 1))) * dl_ref[...])
    hf = h[:, :HYENA_WIDTH] * decay
    hb = jnp.where(ti == 0, 0.0, h[:, HYENA_WIDTH:] * decay)
    ks = hf + hb
    k_ref[:, :HYENA_WIDTH] = ks.astype(BF16)
    k_ref[:, HYENA_WIDTH:] = (hf - hb).astype(BF16)
    nyq = jnp.sum(jnp.where(ti % 2 == 0, ks, -ks), axis=0, keepdims=True)
    nyq_ref[...] = jnp.broadcast_to(nyq, nyq_ref.shape)


def _hy_filter(lp, z, deltas, seq):
    w1 = jnp.pad(lp['hy_f1_w'], ((0, LANE - HYENA_EMB), (0, 0)))
    args = (z, w1, lp['hy_f1_b'].reshape(1, -1), lp['hy_f2_w'], lp['hy_f2_b'].reshape(1, -1), lp['hy_f3_w'],
            lp['hy_freq'].reshape(1, -1), deltas)
    full = lambda a: pl.BlockSpec(a.shape, lambda i: (0,) * a.ndim)
    return pl.pallas_call(
        functools.partial(_hy_filter_body, seq=seq),
        grid=(1,),
        in_specs=[full(a) for a in args],
        out_specs=[pl.BlockSpec((seq, 2 * HYENA_WIDTH), lambda i: (0, 0)),
                   pl.BlockSpec((8, HYENA_WIDTH), lambda i: (0, 0))],
        out_shape=[jax.ShapeDtypeStruct((seq, 2 * HYENA_WIDTH), BF16),
                   jax.ShapeDtypeStruct((8, HYENA_WIDTH), F32)],
        compiler_params=_cparams("arbitrary"),
        name="hy_filter",
    )(*args)


DFT_SPLIT = 64


def _dft_matrices(seq):
    n = 2 * seq
    f = jnp.arange(seq, dtype=jnp.int32)[:, None]

    def table(step, count):
        idx = (f * (jnp.arange(count, dtype=jnp.int32)[None, :] * step)) % n
        ang = idx.astype(F32) * (2.0 * math.pi / n)
        return jnp.cos(ang), jnp.sin(ang)

    hc, hs = table(DFT_SPLIT, seq // DFT_SPLIT)
    lc, ls = table(1, DFT_SPLIT)
    cos = (hc[:, :, None] * lc[:, None, :] - hs[:, :, None] * ls[:, None, :]).reshape(seq, seq)
    sin = (hs[:, :, None] * lc[:, None, :] + hc[:, :, None] * ls[:, None, :]).reshape(seq, seq)
    s = jnp.arange(seq, dtype=jnp.int32)[None, :]
    nyq = jnp.where(s % 2 == 0, 1.0, -1.0).astype(F32)
    wf = jnp.concatenate([cos, jnp.where(f == 0, nyq, -sin)], axis=0).astype(BF16)
    return wf, wf.T


def _mm_body(a_ref, b_ref, o_ref):
    o_ref[...] = _dot(a_ref[...], b_ref[...])


def _dft_fwd_body(wc_ref, ws_ref, x_ref, hr_ref, hi_ref, nyq_ref, pr_ref, pi_ref, *, n):
    tf = wc_ref.shape[0]
    freq = lax.broadcasted_iota(jnp.int32, (tf, MXU_COLS), 0) + pl.program_id(0) * tf
    first = freq == 0
    w = jnp.where(first, 1.0 / n, 2.0 / n)
    chunks = [slice(c, c + MXU_COLS) for c in range(0, x_ref.shape[1], MXU_COLS)]
    spectra = [(_dot(wc_ref[...], x_ref[:, sl]), _dot(ws_ref[...], x_ref[:, sl])) for sl in chunks]
    for sl, (xr, xi) in zip(chunks, spectra):
        hr = hr_ref[:, sl]
        hi = hi_ref[:, sl]
        pr_ref[:, sl] = (jnp.where(first, xr * hr, xr * hr - xi * hi) * w).astype(BF16)
        pi_ref[:, sl] = (jnp.where(first, xi * nyq_ref[0:1, sl], xr * hi + xi * hr) * w).astype(BF16)


def _dft_fwd(wf, x, hf, nyq):
    n, seq = wf.shape
    tf = min(seq, 512)
    nf = seq // tf
    out = jax.ShapeDtypeStruct((seq, BATCH * HYENA_WIDTH), BF16)
    return pl.pallas_call(
        functools.partial(_dft_fwd_body, n=n),
        grid=(nf, BATCH),
        in_specs=[
            pl.BlockSpec((tf, seq), lambda f, b: (f, 0)),
            pl.BlockSpec((tf, seq), lambda f, b: (nf + f, 0)),
            pl.BlockSpec((seq, HYENA_WIDTH), lambda f, b: (0, b)),
            pl.BlockSpec((tf, HYENA_WIDTH), lambda f, b: (f, 0)),
            pl.BlockSpec((tf, HYENA_WIDTH), lambda f, b: (nf + f, 0)),
            pl.BlockSpec((8, HYENA_WIDTH), lambda f, b: (0, 0)),
        ],
        out_specs=[pl.BlockSpec((tf, HYENA_WIDTH), lambda f, b: (f, b))] * 2,
        out_shape=[out, out],
        compiler_params=_cparams("arbitrary", "arbitrary"),
        name="dft_fwd",
    )(wf, wf, x, hf, hf, nyq)


def _dft_filter(wf, k):
    n, seq = wf.shape
    tm = min(seq, 1024)
    return pl.pallas_call(
        _mm_body,
        grid=(n // tm,),
        in_specs=[
            pl.BlockSpec((tm, seq), lambda i: (i, 0)),
            pl.BlockSpec((seq, HYENA_WIDTH), lambda i: (0, (i * tm) // seq)),
        ],
        out_specs=pl.BlockSpec((tm, HYENA_WIDTH), lambda i: (i, 0)),
        out_shape=jax.ShapeDtypeStruct((n, HYENA_WIDTH), F32),
        compiler_params=_cparams("arbitrary"),
        name="dft_filter",
    )(wf, k)


def _dft_inv_body(wc_ref, ws_ref, pr_ref, pi_ref, vx_ref, bias_ref, x0_ref, o_ref):
    y = _dot(wc_ref[...], pr_ref[...]) + _dot(ws_ref[...], pi_ref[...])
    o_ref[...] = ((y + vx_ref[...] * bias_ref[...]) * x0_ref[...]).astype(BF16)


def _dft_inv(wi, p_re, p_im, vx, bias, x0, bb, row0):
    seq, n = wi.shape
    tm = min(seq, 1024)
    tn = HYENA_WIDTH
    nt = seq // tm
    return pl.pallas_call(
        _keep_branch_buffer(_dft_inv_body, 7),
        grid=(nt, BATCH),
        in_specs=[
            pl.BlockSpec((tm, seq), lambda i, b: (i, 0)),
            pl.BlockSpec((tm, seq), lambda i, b: (i, 1)),
            pl.BlockSpec((seq, tn), lambda i, b: (0, b)),
            pl.BlockSpec((seq, tn), lambda i, b: (0, b)),
            pl.BlockSpec((tm, tn), lambda i, b: (b * nt + i, 0)),
            pl.BlockSpec((1, tn), lambda i, b: (0, 0)),
            pl.BlockSpec((tm, tn), lambda i, b: (b * nt + i, 0)),
            _ANY,
        ],
        out_specs=pl.BlockSpec((None, tm, tn), lambda i, b: (SLOT_HYENA, row0 // tm + b * nt + i, 0)),
        out_shape=jax.ShapeDtypeStruct(bb.shape, bb.dtype),
        input_output_aliases={7: 0},
        compiler_params=_cparams("arbitrary", "arbitrary"),
        name="dft_inv",
    )(wi, wi, p_re, p_im, vx, bias.reshape(1, tn), x0, bb)


def _hyena(proj, lp, consts, bb, seq, row0):
    z, deltas, (wf, wi) = consts
    x0, vx, vb = _hy_prep(proj, lp['hy_short_w'], lp['hy_short_b'], seq, row0 // seq)
    k, nyq = _hy_filter(lp, z, deltas, seq)
    hf = _dft_filter(wf, k)
    p_re, p_im = _dft_fwd(wf, vb, hf, nyq)
    return _dft_inv(wi, p_re, p_im, vx, lp['hy_bias'], x0, bb, row0)


def _s5_params(lp, n_seg_steps):
    a_re, a_im = lp['s5_a_re'], lp['s5_a_im']
    dt = jnp.exp(lp['s5_log_dt'])[..., None]
    mag = jnp.exp(a_re * dt)
    ab_re, ab_im = mag * jnp.cos(a_im * dt), mag * jnp.sin(a_im * dt)
    den = a_re * a_re + a_im * a_im
    nr, ni = ab_re - 1.0, ab_im
    cf_re = (nr * a_re + ni * a_im) / den
    cf_im = (ni * a_re - nr * a_im) / den
    b_re, b_im = lp['s5_b_re'], lp['s5_b_im']
    bb_re = cf_re[..., None] * b_re - cf_im[..., None] * b_im
    bb_im = cf_re[..., None] * b_im + cf_im[..., None] * b_re
    eye = jnp.eye(S5_CHUNK_GROUPS, dtype=F32)

    def bdiag_in(m):
        m = m.reshape(2, S5_NCHUNK, S5_CHUNK_GROUPS, S5_STATE, S5_GC)
        return jnp.einsum('dqgpc,gh->dqgchp', m, eye).reshape(2, S5_NCHUNK, S5_CHUNK_CH, S5_CHUNK_ST)

    def bdiag_out(m):
        m = m.reshape(2, S5_NCHUNK, S5_CHUNK_GROUPS, S5_GC, S5_STATE)
        return jnp.einsum('dqgcp,gh->dqhpgc', m, eye).reshape(2, S5_NCHUNK, S5_CHUNK_ST, S5_CHUNK_CH)

    bbd = jnp.concatenate([bdiag_in(bb_re), bdiag_in(bb_im)], axis=-1).astype(BF16)
    cbd = jnp.concatenate([bdiag_out(lp['s5_c_re']), -bdiag_out(lp['s5_c_im'])], axis=-2).astype(BF16)
    a = jnp.stack([ab_re.reshape(2, -1), ab_im.reshape(2, -1)], axis=1)
    aks = []
    for steps in n_seg_steps:
        pr, pi = ab_re, ab_im
        for _ in range(int(math.log2(steps))):
            pr, pi = pr * pr - pi * pi, 2.0 * pr * pi
        aks.append(jnp.stack([pr.reshape(2, -1), pi.reshape(2, -1)], axis=1))
    return [(bbd, cbd, a, ak) for ak in aks]


S5_BLOCK = 256


def _s5_scan_body(u_ref, bbd_ref, cbd_ref, a_ref, ak_ref, dsk_ref, h0_ref, y_ref, hl_ref,
                  xr0_ref, xr1_ref, xi0_ref, xi1_ref, yd0_ref, yd1_ref, up_ref, ub_ref, *, seq):
    nk = seq // S5_LANES
    nblk = seq // S5_BLOCK
    tiles = S5_BLOCK // S5_LANES
    shape = (S5_LANES, S5_CHUNK_ST)
    xr_ref, xi_ref, yd_ref = (xr0_ref, xr1_ref), (xi0_ref, xi1_ref), (yd0_ref, yd1_ref)
    for j in range(S5_LANES):
        up_ref[pl.ds(j, nk, stride=S5_LANES), :] = u_ref[pl.ds(j * nk, nk), :]
    ub_ref[...] = up_ref[...].astype(BF16)
    row = lax.broadcasted_iota(jnp.int32, shape, 0)
    zero = jnp.zeros(shape, F32)

    def block(r):
        return pl.ds(pl.multiple_of(r * S5_BLOCK, S5_BLOCK), S5_BLOCK)

    def in_proj(d, rows):
        xr_ref[d][rows, :] = _dot(ub_ref[rows, :], bbd_ref[d, :, :S5_CHUNK_ST])
        xi_ref[d][rows, :] = _dot(ub_ref[rows, :], bbd_ref[d, :, S5_CHUNK_ST:])

    def out_proj(d, rows):
        yd_ref[d][rows, :] = (_dot(xr_ref[d][rows, :].astype(BF16), cbd_ref[d, :S5_CHUNK_ST, :])
                              + _dot(xi_ref[d][rows, :].astype(BF16), cbd_ref[d, S5_CHUNK_ST:, :]))

    def coeffs(d):
        return jnp.broadcast_to(a_ref[d, 0:1, :], shape), jnp.broadcast_to(a_ref[d, 1:2, :], shape)

    def block_tiles(d, r):
        base = pl.multiple_of(r * S5_BLOCK, S5_BLOCK)
        order = range(tiles) if d == 0 else range(tiles - 1, -1, -1)
        return [pl.ds(base + t * S5_LANES, S5_LANES) for t in order]

    def scan_block(d, r, carry):
        ar, ai = coeffs(d)
        xr, xi = carry
        for rows in block_tiles(d, r):
            xr, xi = (ar * xr - ai * xi + xr_ref[d][rows, :], ar * xi + ai * xr + xi_ref[d][rows, :])
            xr_ref[d][rows, :] = xr
            xi_ref[d][rows, :] = xi
        return xr, xi

    def fix_block(d, r, carry):
        ar, ai = coeffs(d)
        gr, gi = carry
        for rows in block_tiles(d, r):
            gr, gi = ar * gr - ai * gi, ar * gi + ai * gr
            xr_ref[d][rows, :] += gr
            xi_ref[d][rows, :] += gi
        return gr, gi

    def visit(d, s):
        return s if d == 0 else nblk - 1 - s

    def entering_states(d, er, ei):
        akr, aki = ak_ref[d, 0:1, :], ak_ref[d, 1:2, :]
        hr, hi = h0_ref[2 * d:2 * d + 1, :], h0_ref[2 * d + 1:2 * d + 2, :]
        in_r, in_i = zero, zero
        for j in (range(S5_LANES) if d == 0 else range(S5_LANES - 1, -1, -1)):
            in_r = jnp.where(row == j, hr, in_r)
            in_i = jnp.where(row == j, hi, in_i)
            hr, hi = (akr * hr - aki * hi + er[j:j + 1, :], akr * hi + aki * hr + ei[j:j + 1, :])
        hl_ref[2 * d:2 * d + 1, :] = hr
        hl_ref[2 * d + 1:2 * d + 2, :] = hi
        return in_r, in_i

    def fix_all(d, ins):
        lax.fori_loop(0, nblk, lambda s, carry: fix_block(d, visit(d, s), carry), ins)

    in_proj(0, slice(None))

    def scan0_body(s, carry):
        in_proj(1, block(s))
        return scan_block(0, s, carry)
    ends0 = lax.fori_loop(0, nblk, scan0_body, (zero, zero))
    fix_all(0, entering_states(0, *ends0))

    def scan1_body(s, carry):
        out_proj(0, block(s))
        return scan_block(1, visit(1, s), carry)
    ends1 = lax.fori_loop(0, nblk, scan1_body, (zero, zero))
    fix_all(1, entering_states(1, *ends1))
    out_proj(1, slice(None))

    up_ref[...] = yd_ref[0][...] + yd_ref[1][...] + up_ref[...] * (dsk_ref[0] + dsk_ref[1])
    for j in range(S5_LANES):
        y_ref[pl.ds(j * nk, nk), :] = up_ref[pl.ds(j, nk, stride=S5_LANES), :]


def _s5_scan(proj, params, dskip, h0, seq, row_blk0):
    bbd, cbd, a, ak = params
    nq = S5_NCHUNK
    col0 = OFF_S5 // S5_CHUNK_CH
    return pl.pallas_call(
        functools.partial(_s5_scan_body, seq=seq),
        grid=(BATCH, nq),
        in_specs=[
            pl.BlockSpec((seq, S5_CHUNK_CH), lambda b, q: (row_blk0 + b, col0 + q)),
            pl.BlockSpec((2, None, S5_CHUNK_CH, 2 * S5_CHUNK_ST), lambda b, q: (0, q, 0, 0)),
            pl.BlockSpec((2, None, 2 * S5_CHUNK_ST, S5_CHUNK_CH), lambda b, q: (0, q, 0, 0)),
            pl.BlockSpec((2, 2, S5_CHUNK_ST), lambda b, q: (0, 0, q)),
            pl.BlockSpec((2, 2, S5_CHUNK_ST), lambda b, q: (0, 0, q)),
            pl.BlockSpec((2, 1, S5_CHUNK_CH), lambda b, q: (0, 0, q)),
            pl.BlockSpec((None, 4, S5_CHUNK_ST), lambda b, q: (b, 0, q)),
        ],
        out_specs=[
            pl.BlockSpec((seq, S5_CHUNK_CH), lambda b, q: (b, q)),
            pl.BlockSpec((None, 4, S5_CHUNK_ST), lambda b, q: (b, 0, q)),
        ],
        out_shape=[
            jax.ShapeDtypeStruct((BATCH * seq, S5_WIDTH), F32),
            jax.ShapeDtypeStruct((BATCH, 4, S5_GROUPS * S5_STATE), F32),
        ],
        scratch_shapes=[pltpu.VMEM((seq, S5_CHUNK_ST), F32)] * 4 + [pltpu.VMEM((seq, S5_CHUNK_CH), F32)] * 3
        + [pltpu.VMEM((seq, S5_CHUNK_CH), BF16)],
        compiler_params=_cparams("arbitrary", "arbitrary"),
        name="s5_scan",
    )(proj, bbd, cbd, a, ak, dskip.reshape(2, 1, S5_WIDTH), h0)


def _s5_glu_body(y_ref, w_ref, b_ref, o_ref):
    g = _dot(jax.nn.gelu(y_ref[...]).astype(BF16), w_ref[...].astype(BF16)) + b_ref[...]
    o_ref[...] = (g[:, :S5_WIDTH] * jax.nn.sigmoid(g[:, S5_WIDTH:])).astype(BF16)


def _s5_glu(y, w, b, bb, row0):
    n_rows = y.shape[0]
    tm = TM_EW
    return pl.pallas_call(
        _keep_branch_buffer(_s5_glu_body, 3),
        grid=(n_rows // tm,),
        in_specs=[
            pl.BlockSpec((tm, S5_WIDTH), lambda i: (i, 0)),
            pl.BlockSpec((S5_WIDTH, 2 * S5_WIDTH), lambda i: (0, 0)),
            pl.BlockSpec((1, 2 * S5_WIDTH), lambda i: (0, 0)),
            _ANY,
        ],
        out_specs=pl.BlockSpec((None, tm, S5_WIDTH), lambda i: (SLOT_S5, row0 // tm + i, 0)),
        out_shape=jax.ShapeDtypeStruct(bb.shape, bb.dtype),
        input_output_aliases={3: 0},
        compiler_params=_cparams("arbitrary"),
        name="s5_glu",
    )(y, w, b.reshape(1, -1), bb)


def _merge_body(u_ref, *refs):
    wg_refs, bg_refs, y_refs, wb_refs = (refs[k * N_BRANCH:(k + 1) * N_BRANCH] for k in range(4))
    o_ref = refs[4 * N_BRANCH]
    u = u_ref[...]
    acc = None
    for n in range(N_BRANCH):
        gate = jax.nn.sigmoid(_dot(u, wg_refs[n][...].astype(BF16)) + bg_refs[n][...])
        contrib = gate * _dot(y_refs[n][...], wb_refs[n][...].astype(BF16))
        acc = contrib if acc is None else acc + contrib
    o_ref[...] = acc.astype(BF16)


def _merge(u, w_gate, b_gate, branches, w_branch, l, n_rows):
    tc = MXU_COLS
    ncol = D_MODEL // tc
    b_gate = b_gate.reshape(1, -1)
    per_branch = lambda make: [make(n) for n in range(N_BRANCH)]
    return pl.pallas_call(
        _merge_body,
        grid=(n_rows // TM, ncol),
        in_specs=[pl.BlockSpec((TM, D_MODEL), lambda i, c: (i, 0))]
        + per_branch(lambda n: pl.BlockSpec((None, D_MODEL, tc), lambda i, c: (l, 0, n * ncol + c)))
        + per_branch(lambda n: pl.BlockSpec((1, tc), lambda i, c: (0, n * ncol + c)))
        + per_branch(lambda n: pl.BlockSpec((None, TM, BRANCH_WIDTH), lambda i, c: (n, i, 0)))
        + per_branch(lambda n: pl.BlockSpec((None, None, BRANCH_WIDTH, tc), lambda i, c: (l, n, 0, c))),
        out_specs=pl.BlockSpec((TM, tc), lambda i, c: (i, c)),
        out_shape=jax.ShapeDtypeStruct((n_rows, D_MODEL), BF16),
        compiler_params=_cparams("arbitrary", "arbitrary"),
        name="merge",
    )(u, *([w_gate] * N_BRANCH), *([b_gate] * N_BRANCH), *([branches] * N_BRANCH), *([w_branch] * N_BRANCH))


def _out_body(m_ref, w_ref, x_ref, g_ref, o_ref):
    o_ref[...] = x_ref[...] + g_ref[...] * _dot(m_ref[...], w_ref[...].astype(BF16))


def _out_proj(merged, w_out, l, x, mods, n_rows):
    tn = 1024
    gate = pl.BlockSpec((None, None, 1, tn), lambda j, i: (_mod_row(i, TM), 5, 0, j))
    return pl.pallas_call(
        _out_body,
        grid=(D_MODEL // tn, n_rows // TM),
        in_specs=[
            pl.BlockSpec((TM, D_MODEL), lambda j, i: (i, 0)),
            pl.BlockSpec((None, D_MODEL, tn), lambda j, i: (l, 0, j)),
            pl.BlockSpec((TM, tn), lambda j, i: (i, j)),
            gate,
        ],
        out_specs=pl.BlockSpec((TM, tn), lambda j, i: (i, j)),
        out_shape=jax.ShapeDtypeStruct((n_rows, D_MODEL), F32),
        compiler_params=_cparams("arbitrary", "arbitrary"),
        name="out_proj",
    )(merged, w_out, x, mods)


def _mixer(x, g, lp, big, w_in_bf16, l, mods, consts, bb, last):
    cos, sin, hy_lat, hy_ctx = consts
    u = _normmod(x, g, mods, 3, MT)
    proj = _proj(u, w_in_bf16)
    q, k, v = _qkv(proj, cos, sin, lp['q_norm'], lp['k_norm'])

    ctx_blk = N_LAT // CTX_LEN
    zero_h = jnp.zeros((BATCH, 4, S5_GROUPS * S5_STATE), F32)
    par_ctx, par_lat = _s5_params(lp, (CTX_LEN // S5_LANES, SEQ // S5_LANES))
    ys_ctx, h_ctx = _s5_scan(proj, par_ctx, lp['s5_d'], zero_h, CTX_LEN, ctx_blk)
    ys_lat, _ = _s5_scan(proj, par_lat, lp['s5_d'], h_ctx, SEQ, 0)

    bb = _s5_glu(ys_lat, lp['s5_glu_w'], lp['s5_glu_b'], bb, 0)
    bb = _attn_lat(q, k, v, bb)
    bb = _pool(proj, lp['pool_w'], lp['pool_scale'], bb, SEQ, 0)
    bb = _hyena(proj, lp, hy_lat, bb, SEQ, 0)
    if not last:
        bb = _s5_glu(ys_ctx, lp['s5_glu_w'], lp['s5_glu_b'], bb, N_LAT)
        bb = _attn_ctx(q, k, v, bb)
        bb = _pool(proj, lp['pool_w'], lp['pool_scale'], bb, CTX_LEN, ctx_blk)
        bb = _hyena(proj, lp, hy_ctx, bb, CTX_LEN, N_LAT)

    n_rows = N_LAT if last else MT
    merged = _merge(u, big['w_gate'], lp['b_gate'], bb, big['w_branch'], l, n_rows)
    return _out_proj(merged, big['w_out'], l, x, mods, n_rows), bb


def kernel(x, c, ctx, c_ctx, w_ada, b_ada, norm_ffn1, norm_mix, norm_ffn2, norm_final, ffn1_wi, ffn1_wo, ffn2_wi, ffn2_wo, w_in, w_gate, b_gate, w_branch, w_out, pool_w, pool_scale, q_norm, k_norm, hy_short_w, hy_short_b, hy_f1_w, hy_f1_b, hy_f2_w, hy_f2_b, hy_f3_w, hy_freq, hy_bias, s5_a_re, s5_a_im, s5_log_dt, s5_b_re, s5_b_im, s5_c_re, s5_c_im, s5_d, s5_glu_w, s5_glu_b):
    big = dict(w_in=w_in, w_gate=w_gate, w_branch=w_branch, w_out=w_out)
    per_layer = dict(
        b_gate=b_gate, pool_w=pool_w,
        pool_scale=pool_scale, q_norm=q_norm, k_norm=k_norm, hy_short_w=hy_short_w, hy_short_b=hy_short_b,
        hy_f1_w=hy_f1_w, hy_f1_b=hy_f1_b, hy_f2_w=hy_f2_w, hy_f2_b=hy_f2_b, hy_f3_w=hy_f3_w, hy_freq=hy_freq,
        hy_bias=hy_bias, s5_a_re=s5_a_re, s5_a_im=s5_a_im, s5_log_dt=s5_log_dt, s5_b_re=s5_b_re, s5_b_im=s5_b_im,
        s5_c_re=s5_c_re, s5_c_im=s5_c_im, s5_d=s5_d, s5_glu_w=s5_glu_w, s5_glu_b=s5_glu_b)

    cos, sin = _rope_tables()
    deltas = _hy_deltas()
    consts = (cos, sin,
              (_hy_feats(SEQ), deltas, _dft_matrices(SEQ)),
              (_hy_feats(CTX_LEN), deltas, _dft_matrices(CTX_LEN)))

    cc = jnp.concatenate([c, c_ctx[None], jnp.zeros((8 - BATCH - 1, D_MODEL), F32)], axis=0)
    mods_all = _ada(cc, w_ada, b_ada).reshape(DEPTH, 8, N_MOD, 1, D_MODEL)

    xs = jnp.concatenate([x.reshape(N_LAT, D_MODEL), ctx.reshape(N_CTX, D_MODEL)], axis=0)
    bb = jnp.zeros((N_BRANCH, MT, BRANCH_WIDTH), BF16)
    for l in range(DEPTH):
        last = l == DEPTH - 1
        lp = {name: w[l] for name, w in per_layer.items()}
        mods = mods_all[l]
        xs, w_in_bf16 = _ffn(xs, mods, norm_ffn1[l], ffn1_wi, ffn1_wo, l, 0, MT, side=w_in)
        xs, bb = _mixer(xs, norm_mix[l], lp, big, w_in_bf16, l, mods, consts, bb, last)
        xs = _ffn(xs, mods, norm_ffn2[l], ffn2_wi, ffn2_wo, l, 6, N_LAT if last else MT)
    return _final_norm(xs, norm_final, N_LAT).reshape(BATCH, SEQ, D_MODEL)
```

```python
import functools
import math

import jax
import jax.numpy as jnp
from jax import lax
from jax.experimental import pallas as pl
from jax.experimental.pallas import tpu as pltpu

F32 = jnp.float32
BF16 = jnp.bfloat16

D_MODEL = 2048
BATCH = 4
SEQ = 2048
DEPTH = 4
GRID_W = 64
CTX_LEN = 256
D_FF = 5632
N_MOD = 9
EPS = 1e-6

POOL_WINDOWS = (2, 4, 8, 16)
HEAD_DIM = 128
N_Q_HEADS = 4
N_KV_HEADS = 2
Q_GROUP = N_Q_HEADS // N_KV_HEADS
ROPE_THETA = 10000.0
ROPE_FREQS = HEAD_DIM // 4

HYENA_WIDTH = 512
HYENA_EMB = 33
HYENA_BANDS = (HYENA_EMB - 1) // 2
HYENA_HIDDEN = 64
HYENA_TARGET = 1e-2
HYENA_FAST_PCT = 0.3
HYENA_SLOW_PCT = 1.5

S5_WIDTH = 512
S5_GC = 16
S5_GROUPS = S5_WIDTH // S5_GC
S5_STATE = 64
S5_LANES = 8
S5_CHUNK_GROUPS = 8
S5_CHUNK_CH = S5_CHUNK_GROUPS * S5_GC
S5_CHUNK_ST = S5_CHUNK_GROUPS * S5_STATE
S5_NCHUNK = S5_GROUPS // S5_CHUNK_GROUPS

N_BRANCH = 4
BRANCH_WIDTH = 512
LANE = 128
COL = 512

OFF_Q = 512
OFF_KV = 1024
OFF_HY = 1536
OFF_S5 = 3072
IN_WIDTH = 3584

N_LAT = BATCH * SEQ
N_CTX = BATCH * CTX_LEN
MT = N_LAT + N_CTX

TM = 1024
TM_EW = 512
VMEM_LIMIT = 56 * 1024 * 1024


def _cparams(*sem):
    return pltpu.CompilerParams(dimension_semantics=sem, vmem_limit_bytes=VMEM_LIMIT)


def _dot(a, b):
    return jnp.dot(a, b, preferred_element_type=F32)


SLOT_POOL, SLOT_ATTN, SLOT_HYENA, SLOT_S5 = range(4)
_ANY = pl.BlockSpec(memory_space=pl.ANY)


def _keep_branch_buffer(body, n_in):
    def wrapped(*refs):
        return body(*refs[:n_in], *refs[n_in + 1:])
    return wrapped


def _mod_row(i, tm):
    return jnp.minimum((i * tm) // SEQ, BATCH)


def _mod_spec(tm, k):
    return pl.BlockSpec((None, None, 1, D_MODEL), lambda i, *_: (_mod_row(i, tm), k, 0, 0))


def _ada_body(c_ref, w_ref, b_ref, o_ref):
    c = c_ref[...]
    a = (c * jax.nn.sigmoid(c)).astype(BF16)
    o_ref[...] = _dot(a, w_ref[...].astype(BF16)) + b_ref[...]


def _ada(cc, w_ada, b_ada):
    tn = 1024
    nw = N_MOD * D_MODEL
    return pl.pallas_call(
        _ada_body,
        grid=(DEPTH, nw // tn),
        in_specs=[
            pl.BlockSpec((8, D_MODEL), lambda l, j: (0, 0)),
            pl.BlockSpec((None, D_MODEL, tn), lambda l, j: (l, 0, j)),
            pl.BlockSpec((None, 1, tn), lambda l, j: (l, 0, j)),
        ],
        out_specs=pl.BlockSpec((None, 8, tn), lambda l, j: (l, 0, j)),
        out_shape=jax.ShapeDtypeStruct((DEPTH, 8, nw), F32),
        compiler_params=_cparams("arbitrary", "arbitrary"),
        name="ada",
    )(cc, w_ada, b_ada.reshape(DEPTH, 1, nw))


NORM_ROWS = 16


def _rows_loop(n_rows, step):
    for c in range(n_rows // NORM_ROWS):
        step(pl.ds(c * NORM_ROWS, NORM_ROWS))


def _rms_scale_rows(x_ref, r_ref, scale, shift, o_ref):
    def stats(rows):
        x = x_ref[rows, :]
        r_ref[rows, :] = lax.rsqrt(jnp.mean(x * x, axis=-1, keepdims=True) + EPS)

    def apply(rows):
        y = x_ref[rows, :] * r_ref[rows, :] * scale
        if shift is not None:
            y = y + shift
        o_ref[rows, :] = y.astype(o_ref.dtype)

    _rows_loop(x_ref.shape[0], stats)
    _rows_loop(x_ref.shape[0], apply)


def _normmod_body(x_ref, g_ref, sh_ref, sc_ref, o_ref, r_ref):
    _rms_scale_rows(x_ref, r_ref, g_ref[...] * (1.0 + sc_ref[...]), sh_ref[...], o_ref)


def _norm_body(x_ref, g_ref, o_ref, r_ref):
    _rms_scale_rows(x_ref, r_ref, g_ref[...], None, o_ref)


def _normmod(x, g, mods, base, n_rows):
    tm = TM_EW
    return pl.pallas_call(
        _normmod_body,
        grid=(n_rows // tm,),
        in_specs=[
            pl.BlockSpec((tm, D_MODEL), lambda i: (i, 0)),
            pl.BlockSpec((1, D_MODEL), lambda i: (0, 0)),
            _mod_spec(tm, base),
            _mod_spec(tm, base + 1),
        ],
        out_specs=pl.BlockSpec((tm, D_MODEL), lambda i: (i, 0)),
        out_shape=jax.ShapeDtypeStruct((n_rows, D_MODEL), BF16),
        scratch_shapes=[pltpu.VMEM((tm, 1), F32)],
        compiler_params=_cparams("arbitrary"),
        name="normmod",
    )(x, g.reshape(1, D_MODEL), mods, mods)


def _final_norm(x, g, n_rows):
    tm = TM_EW
    return pl.pallas_call(
        _norm_body,
        grid=(n_rows // tm,),
        in_specs=[
            pl.BlockSpec((tm, D_MODEL), lambda i: (i, 0)),
            pl.BlockSpec((1, D_MODEL), lambda i: (0, 0)),
        ],
        out_specs=pl.BlockSpec((tm, D_MODEL), lambda i: (i, 0)),
        out_shape=jax.ShapeDtypeStruct((n_rows, D_MODEL), F32),
        scratch_shapes=[pltpu.VMEM((tm, 1), F32)],
        compiler_params=_cparams("arbitrary"),
        name="final_norm",
    )(x, g.reshape(1, D_MODEL))


MXU_COLS = 256


WO_SLAB = 64
WO_NSLAB = D_FF // WO_SLAB


NEXT_ROWS = 128
NEXT_SLICES = TM // NEXT_ROWS


def _ffn_a_body(u0_ref, xn_ref, g_ref, sh_ref, sc_ref, wa_ref, wb_ref, wo_ref, h_ref, wob_ref, u2_ref, r_ref):
    i, j = pl.program_id(0), pl.program_id(1)
    slot = i % 2

    @pl.when((i == 0) & (j == 0))
    def _():
        u2_ref[0] = u0_ref[...]

    u = u2_ref[slot]
    for c in range(0, h_ref.shape[1], MXU_COLS):
        sl = slice(c, c + MXU_COLS)
        a = _dot(u, wa_ref[:, sl].astype(BF16))
        b = _dot(u, wb_ref[:, sl].astype(BF16))
        h_ref[:, sl] = (a * jax.nn.sigmoid(a) * b).astype(BF16)

    dst = pl.ds(pl.multiple_of(jnp.minimum(j, NEXT_SLICES - 1) * NEXT_ROWS, NEXT_ROWS), NEXT_ROWS)
    _rms_scale_rows(xn_ref, r_ref, g_ref[...] * (1.0 + sc_ref[...]), sh_ref[...], u2_ref.at[1 - slot, dst, :])

    @pl.when(pl.program_id(0) * pl.num_programs(1) + pl.program_id(1) < WO_NSLAB)
    def _():
        wob_ref[...] = wo_ref[...].astype(BF16)


def _ffn_a(x, g, mods, base, wi, wo, l, n_rows):
    tf = 512
    nf = D_FF // tf
    ni = n_rows // TM
    assert ni * nf >= WO_NSLAB and nf >= NEXT_SLICES

    def wo_blk(i, j):
        return jnp.minimum(i * nf + j, WO_NSLAB - 1)

    def nxt(i):
        return jnp.minimum(i + 1, ni - 1)

    def next_mod(k):
        return pl.BlockSpec((None, None, 1, D_MODEL), lambda i, j: (_mod_row(nxt(i), TM), k, 0, 0))

    u0 = _normmod(x, g, mods, base, TM)
    return pl.pallas_call(
        _ffn_a_body,
        grid=(ni, nf),
        in_specs=[
            pl.BlockSpec((TM, D_MODEL), lambda i, j: (0, 0)),
            pl.BlockSpec((NEXT_ROWS, D_MODEL),
                         lambda i, j: (nxt(i) * NEXT_SLICES + jnp.minimum(j, NEXT_SLICES - 1), 0)),
            pl.BlockSpec((1, D_MODEL), lambda i, j: (0, 0)),
            next_mod(base),
            next_mod(base + 1),
            pl.BlockSpec((None, D_MODEL, tf), lambda i, j: (l, 0, j)),
            pl.BlockSpec((None, D_MODEL, tf), lambda i, j: (l, 0, j + nf)),
            pl.BlockSpec((None, WO_SLAB, D_MODEL), lambda i, j: (l, wo_blk(i, j), 0)),
        ],
        out_specs=[
            pl.BlockSpec((TM, tf), lambda i, j: (i, j)),
            pl.BlockSpec((WO_SLAB, D_MODEL), lambda i, j: (wo_blk(i, j), 0)),
        ],
        out_shape=[
            jax.ShapeDtypeStruct((n_rows, D_FF), BF16),
            jax.ShapeDtypeStruct((D_FF, D_MODEL), BF16),
        ],
        scratch_shapes=[pltpu.VMEM((2, TM, D_MODEL), BF16), pltpu.VMEM((NEXT_ROWS, 1), F32)],
        compiler_params=_cparams("arbitrary", "arbitrary"),
        name="ffn_a",
    )(u0, x, g.reshape(1, D_MODEL), mods, mods, wi, wi, wo)


SIDE_SLAB = 64


def _ffn_b_body(h_ref, w_ref, x_ref, g_ref, *rest, n_slab):
    o_ref = rest[-2] if n_slab else rest[-1]
    h = h_ref[...]
    for c in range(0, o_ref.shape[1], MXU_COLS):
        sl = slice(c, c + MXU_COLS)
        o_ref[:, sl] = x_ref[:, sl] + (0.5 * g_ref[:, sl]) * _dot(h, w_ref[:, sl])

    if n_slab:
        side_ref, _, side_out_ref = rest

        @pl.when(pl.program_id(0) * pl.num_programs(1) + pl.program_id(1) < n_slab)
        def _():
            side_out_ref[...] = side_ref[...].astype(BF16)


def _ffn_b(h, wo_bf16, x, mods, gate_idx, n_rows, side=None, l=None):
    tn = 512
    nj = D_MODEL // tn
    gate = pl.BlockSpec((None, None, 1, tn), lambda i, j: (_mod_row(i, TM), gate_idx, 0, j))
    in_specs = [
        pl.BlockSpec((TM, D_FF), lambda i, j: (i, 0)),
        pl.BlockSpec((D_FF, tn), lambda i, j: (0, j)),
        pl.BlockSpec((TM, tn), lambda i, j: (i, j)),
        gate,
    ]
    out_specs = [pl.BlockSpec((TM, tn), lambda i, j: (i, j))]
    out_shape = [jax.ShapeDtypeStruct((n_rows, D_MODEL), F32)]
    args = [h, wo_bf16, x, mods]
    n_slab = 0
    if side is not None:
        _, rows, cols = side.shape
        n_slab = rows // SIDE_SLAB
        assert (n_rows // TM) * nj >= n_slab

        def slab(i, j):
            return jnp.minimum(i * nj + j, n_slab - 1)

        in_specs.append(pl.BlockSpec((None, SIDE_SLAB, cols), lambda i, j: (l, slab(i, j), 0)))
        out_specs.append(pl.BlockSpec((SIDE_SLAB, cols), lambda i, j: (slab(i, j), 0)))
        out_shape.append(jax.ShapeDtypeStruct((rows, cols), BF16))
        args.append(side)
    res = pl.pallas_call(
        functools.partial(_ffn_b_body, n_slab=n_slab),
        grid=(n_rows // TM, nj),
        in_specs=in_specs,
        out_specs=out_specs,
        out_shape=out_shape,
        compiler_params=_cparams("arbitrary", "arbitrary"),
        name="ffn_b",
    )(*args)
    return res if side is not None else res[0]


def _ffn(x, mods, g, wi, wo, l, base, n_rows, side=None):
    h, wo_bf16 = _ffn_a(x, g, mods, base, wi, wo, l, n_rows)
    return _ffn_b(h, wo_bf16, x, mods, base + 2, n_rows, side, l)


PROJ_TM = 1536
PROJ_ROWS = 256
PROJ_SLICES = PROJ_TM // PROJ_ROWS


def _proj_body(u0_ref, xn_ref, g_ref, sh_ref, sc_ref, w_ref, o_ref, uo_ref, u2_ref, r_ref):
    i, j = pl.program_id(0), pl.program_id(1)
    slot = i % 2

    @pl.when((i == 0) & (j == 0))
    def _():
        u2_ref[0] = u0_ref[...]

    @pl.when(j == 0)
    def _():
        uo_ref[...] = u2_ref[slot]

    o_ref[...] = _dot(u2_ref[slot], w_ref[...])

    dst = pl.ds(pl.multiple_of(jnp.minimum(j, PROJ_SLICES - 1) * PROJ_ROWS, PROJ_ROWS), PROJ_ROWS)
    _rms_scale_rows(xn_ref, r_ref, g_ref[...] * (1.0 + sc_ref[...]), sh_ref[...], u2_ref.at[1 - slot, dst, :])


def _proj(x, g, mods, w_in_bf16):
    tn = COL
    ni = MT // PROJ_TM
    nj = IN_WIDTH // tn
    assert nj >= PROJ_SLICES and SEQ % PROJ_ROWS == 0 and N_LAT % PROJ_ROWS == 0

    def nxt_slice(i, j):
        return jnp.minimum(i + 1, ni - 1) * PROJ_SLICES + jnp.minimum(j, PROJ_SLICES - 1)

    def next_mod(k):
        return pl.BlockSpec((None, None, 1, D_MODEL), lambda i, j: (_mod_row(nxt_slice(i, j), PROJ_ROWS), k, 0, 0))

    u0 = _normmod(x, g, mods, 3, PROJ_TM)
    return pl.pallas_call(
        _proj_body,
        grid=(ni, nj),
        in_specs=[
            pl.BlockSpec((PROJ_TM, D_MODEL), lambda i, j: (0, 0)),
            pl.BlockSpec((PROJ_ROWS, D_MODEL), lambda i, j: (nxt_slice(i, j), 0)),
            pl.BlockSpec((1, D_MODEL), lambda i, j: (0, 0)),
            next_mod(3),
            next_mod(4),
            pl.BlockSpec((D_MODEL, tn), lambda i, j: (0, j)),
        ],
        out_specs=[
            pl.BlockSpec((PROJ_TM, tn), lambda i, j: (i, j)),
            pl.BlockSpec((PROJ_TM, D_MODEL), lambda i, j: (i, 0)),
        ],
        out_shape=[
            jax.ShapeDtypeStruct((MT, IN_WIDTH), F32),
            jax.ShapeDtypeStruct((MT, D_MODEL), BF16),
        ],
        scratch_shapes=[pltpu.VMEM((2, PROJ_TM, D_MODEL), BF16), pltpu.VMEM((PROJ_ROWS, 1), F32)],
        compiler_params=_cparams("arbitrary", "arbitrary"),
        name="proj",
    )(u0, x, g.reshape(1, D_MODEL), mods, mods, w_in_bf16)


def _rope_tables():
    t = jnp.arange(SEQ)
    rows = (t // GRID_W).astype(F32)
    cols = (t % GRID_W).astype(F32)
    freqs = ROPE_THETA ** (-jnp.arange(ROPE_FREQS, dtype=F32) / ROPE_FREQS)
    ar = rows[:, None] * freqs[None, :]
    ac = cols[:, None] * freqs[None, :]
    cos = jnp.concatenate([jnp.cos(ar), jnp.cos(ar), jnp.cos(ac), jnp.cos(ac)], axis=-1)
    sin = jnp.concatenate([-jnp.sin(ar), jnp.sin(ar), -jnp.sin(ac), jnp.sin(ac)], axis=-1)
    cos = jnp.concatenate([cos, jnp.ones((TM_EW, HEAD_DIM), F32)], axis=0)
    sin = jnp.concatenate([sin, jnp.zeros((TM_EW, HEAD_DIM), F32)], axis=0)
    return cos, sin


def _lane_matrices():
    lane = jnp.arange(HEAD_DIM)
    partner = jnp.where(lane % (2 * ROPE_FREQS) < ROPE_FREQS, lane + ROPE_FREQS, lane - ROPE_FREQS)
    swap = (lane[:, None] == partner[None, :]).astype(BF16)
    return jnp.ones((HEAD_DIM, HEAD_DIM), BF16), swap


def _dot_split(x, m_ref):
    hi = x.astype(BF16)
    lo = (x - hi.astype(F32)).astype(BF16)
    return _dot(hi, m_ref[...]) + _dot(lo, m_ref[...])


def _qkv_body(q_ref, kv_ref, cos_ref, sin_ref, qn_ref, kn_ref, ones_ref, swap_ref, qo_ref, ko_ref, vo_ref):
    cos = cos_ref[...]
    sin = sin_ref[...]

    def norm_rope(xh, g):
        ss = _dot_split(xh * xh, ones_ref)
        y = xh * lax.rsqrt(ss * (1.0 / HEAD_DIM) + EPS) * g
        return y * cos + _dot_split(y, swap_ref) * sin

    scale = math.log2(math.e) / math.sqrt(HEAD_DIM)
    for h in range(N_Q_HEADS):
        sl = slice(h * HEAD_DIM, (h + 1) * HEAD_DIM)
        qo_ref[:, sl] = (norm_rope(q_ref[:, sl], qn_ref[...]) * scale).astype(BF16)
    for h in range(N_KV_HEADS):
        sl = slice(h * HEAD_DIM, (h + 1) * HEAD_DIM)
        ko_ref[:, sl] = norm_rope(kv_ref[:, sl], kn_ref[...]).astype(BF16)
    kvw = N_KV_HEADS * HEAD_DIM
    for h in range(N_KV_HEADS):
        vo_ref[:, 2 * h * HEAD_DIM:(2 * h + 1) * HEAD_DIM] = (
            kv_ref[:, kvw + h * HEAD_DIM:kvw + (h + 1) * HEAD_DIM].astype(BF16))
        vo_ref[:, (2 * h + 1) * HEAD_DIM:(2 * h + 2) * HEAD_DIM] = jnp.ones((vo_ref.shape[0], HEAD_DIM), BF16)


def _qkv(proj, cos, sin, q_norm, k_norm):
    tm = TM_EW
    n_lat_tiles = N_LAT // tm
    per_seq = SEQ // tm
    kvw = N_KV_HEADS * HEAD_DIM

    def tab(i):
        return (jnp.where(i < n_lat_tiles, i % per_seq, per_seq), 0)

    return pl.pallas_call(
        _qkv_body,
        grid=(MT // tm,),
        in_specs=[
            pl.BlockSpec((tm, COL), lambda i: (i, OFF_Q // COL)),
            pl.BlockSpec((tm, COL), lambda i: (i, OFF_KV // COL)),
            pl.BlockSpec((tm, HEAD_DIM), tab),
            pl.BlockSpec((tm, HEAD_DIM), tab),
            pl.BlockSpec((1, HEAD_DIM), lambda i: (0, 0)),
            pl.BlockSpec((1, HEAD_DIM), lambda i: (0, 0)),
            pl.BlockSpec((HEAD_DIM, HEAD_DIM), lambda i: (0, 0)),
            pl.BlockSpec((HEAD_DIM, HEAD_DIM), lambda i: (0, 0)),
        ],
        out_specs=[
            pl.BlockSpec((tm, COL), lambda i: (i, 0)),
            pl.BlockSpec((tm, kvw), lambda i: (i, 0)),
            pl.BlockSpec((tm, 2 * kvw), lambda i: (i, 0)),
        ],
        out_shape=[
            jax.ShapeDtypeStruct((MT, COL), BF16),
            jax.ShapeDtypeStruct((MT, kvw), BF16),
            jax.ShapeDtypeStruct((MT, 2 * kvw), BF16),
        ],
        compiler_params=_cparams("arbitrary"),
        name="qkv",
    )(proj, proj, cos, sin, q_norm.reshape(1, HEAD_DIM), k_norm.reshape(1, HEAD_DIM), *_lane_matrices())


def _attn_body(*refs, with_lat):
    if with_lat:
        q_ref, kl_ref, vl_ref, kc_ref, vc_ref, o_ref = refs
    else:
        q_ref, kc_ref, vc_ref, o_ref = refs
    nt = (((1,), (1,)), ((), ()))
    half = q_ref.shape[0] // 2
    units = [(slice(r, r + half), slice(g * HEAD_DIM, (g + 1) * HEAD_DIM))
             for g in range(Q_GROUP) for r in (0, half)]
    scores = []
    for rows, sl in units:
        q = q_ref[rows, sl]
        sc = lax.dot_general(q, kc_ref[...], nt, preferred_element_type=F32)
        s_lat = lax.dot_general(q, kl_ref[...], nt, preferred_element_type=F32) if with_lat else None
        scores.append((sc, s_lat))
    for (rows, sl), (sc, s_lat) in zip(units, scores):
        m = jnp.max(sc, axis=-1, keepdims=True)
        if with_lat:
            m = jnp.maximum(m, jnp.max(s_lat, axis=-1, keepdims=True))
        o = _dot(jnp.exp2(sc - m).astype(BF16), vc_ref[...])
        if with_lat:
            o = o + _dot(jnp.exp2(s_lat - m).astype(BF16), vl_ref[...])
        o_ref[rows, sl] = (o[:, :HEAD_DIM] / o[:, HEAD_DIM:]).astype(BF16)


def _attn_lat(q, k, v, bb):
    tq = 512
    nq = SEQ // tq
    ctx_blk = N_LAT // CTX_LEN
    gw = Q_GROUP * HEAD_DIM
    return pl.pallas_call(
        _keep_branch_buffer(functools.partial(_attn_body, with_lat=True), 5),
        grid=(BATCH, N_KV_HEADS, nq),
        in_specs=[
            pl.BlockSpec((tq, gw), lambda b, h, i: (b * nq + i, h)),
            pl.BlockSpec((SEQ, HEAD_DIM), lambda b, h, i: (b, h)),
            pl.BlockSpec((SEQ, 2 * HEAD_DIM), lambda b, h, i: (b, h)),
            pl.BlockSpec((CTX_LEN, HEAD_DIM), lambda b, h, i: (ctx_blk + b, h)),
            pl.BlockSpec((CTX_LEN, 2 * HEAD_DIM), lambda b, h, i: (ctx_blk + b, h)),
            _ANY,
        ],
        out_specs=pl.BlockSpec((None, tq, gw), lambda b, h, i: (SLOT_ATTN, b * nq + i, h)),
        out_shape=jax.ShapeDtypeStruct(bb.shape, bb.dtype),
        input_output_aliases={5: 0},
        compiler_params=_cparams("arbitrary", "arbitrary", "arbitrary"),
        name="attn_lat",
    )(q, k, v, k, v, bb)


def _attn_ctx(q, k, v, bb):
    ctx_blk = N_LAT // CTX_LEN
    gw = Q_GROUP * HEAD_DIM
    return pl.pallas_call(
        _keep_branch_buffer(functools.partial(_attn_body, with_lat=False), 3),
        grid=(BATCH, N_KV_HEADS),
        in_specs=[
            pl.BlockSpec((CTX_LEN, gw), lambda b, h: (ctx_blk + b, h)),
            pl.BlockSpec((CTX_LEN, HEAD_DIM), lambda b, h: (ctx_blk + b, h)),
            pl.BlockSpec((CTX_LEN, 2 * HEAD_DIM), lambda b, h: (ctx_blk + b, h)),
            _ANY,
        ],
        out_specs=pl.BlockSpec((None, CTX_LEN, gw), lambda b, h: (SLOT_ATTN, ctx_blk + b, h)),
        out_shape=jax.ShapeDtypeStruct(bb.shape, bb.dtype),
        input_output_aliases={3: 0},
        compiler_params=_cparams("arbitrary", "arbitrary"),
        name="attn_ctx",
    )(q, k, v, bb)


POOL_PAD = 8


def _pool_body(a_ref, w_ref, s_ref, o_ref, *, seq):
    lp = seq + 2 * POOL_PAD
    t = lax.broadcasted_iota(jnp.int32, (seq, LANE), 0)
    zpad = jnp.zeros((POOL_PAD, LANE), F32)
    for gi, win in enumerate(POOL_WINDOWS):
        sl = slice(gi * LANE, (gi + 1) * LANE)
        a = a_ref[:, sl]
        s = jnp.concatenate([zpad, a, zpad], axis=0)
        s = s + pltpu.roll(s, 1, 0)
        half = 1
        while 2 * half < win:
            s = pltpu.roll(s, half, 0) + pltpu.roll(s, lp - half, 0)
            half *= 2
        s = s[POOL_PAD:POOL_PAD + seq]
        lo = jnp.maximum(t - win // 2, 0)
        hi = jnp.minimum(t + win // 2, seq)
        pooled = s / (hi - lo).astype(F32) - a
        y = _dot(pooled.astype(BF16), w_ref[gi].astype(BF16))
        o_ref[:, sl] = (y * s_ref[:, sl]).astype(BF16)


def _pool(proj, pool_w, pool_scale, bb, seq, row_blk0):
    width = len(POOL_WINDOWS) * LANE
    return pl.pallas_call(
        _keep_branch_buffer(functools.partial(_pool_body, seq=seq), 3),
        grid=(BATCH,),
        in_specs=[
            pl.BlockSpec((seq, width), lambda b: (row_blk0 + b, 0)),
            pl.BlockSpec((len(POOL_WINDOWS), LANE, LANE), lambda b: (0, 0, 0)),
            pl.BlockSpec((1, width), lambda b: (0, 0)),
            _ANY,
        ],
        out_specs=pl.BlockSpec((None, seq, width), lambda b: (SLOT_POOL, row_blk0 + b, 0)),
        out_shape=jax.ShapeDtypeStruct(bb.shape, bb.dtype),
        input_output_aliases={3: 0},
        compiler_params=_cparams("arbitrary"),
        name="pool",
    )(proj, pool_w, pool_scale.reshape(1, width), bb)


def _hy_prep_body(x0_ref, x1_ref, v_ref, w0_ref, w1_ref, wv_ref, b0_ref, b1_ref, bv_ref,
                  x0o_ref, vx_ref, vb_ref, *, seq):
    t = lax.broadcasted_iota(jnp.int32, x0_ref.shape, 0)

    def conv(x_ref, w_ref, b_ref):
        x = x_ref[...]
        prev = jnp.where(t >= 1, pltpu.roll(x, 1, 0), 0.0)
        nxt = jnp.where(t <= seq - 2, pltpu.roll(x, seq - 1, 0), 0.0)
        return prev * w_ref[0:1, :] + x * w_ref[1:2, :] + nxt * w_ref[2:3, :] + b_ref[...]

    x0o_ref[...] = conv(x0_ref, w0_ref, b0_ref)
    vx = conv(v_ref, wv_ref, bv_ref) * conv(x1_ref, w1_ref, b1_ref)
    vx_ref[...] = vx
    vb_ref[...] = vx.astype(BF16)


def _hy_prep(proj, short_w, short_b, seq, row_blk0):
    tc = 2 * LANE
    nc = HYENA_WIDTH // tc
    c0 = OFF_HY // tc
    short_b = short_b.reshape(1, 3 * HYENA_WIDTH)

    def xspec(part):
        return pl.BlockSpec((seq, tc), lambda b, c: (row_blk0 + b, c0 + part * nc + c))

    def wspec(part, rows):
        return pl.BlockSpec((rows, tc), lambda b, c: (0, part * nc + c))

    return pl.pallas_call(
        functools.partial(_hy_prep_body, seq=seq),
        grid=(BATCH, nc),
        in_specs=[xspec(0), xspec(1), xspec(2), wspec(0, 3), wspec(1, 3), wspec(2, 3),
                  wspec(0, 1), wspec(1, 1), wspec(2, 1)],
        out_specs=[
            pl.BlockSpec((seq, tc), lambda b, c: (b, c)),
            pl.BlockSpec((seq, tc), lambda b, c: (b, c)),
            pl.BlockSpec((seq, tc), lambda b, c: (0, b * nc + c)),
        ],
        out_shape=[
            jax.ShapeDtypeStruct((BATCH * seq, HYENA_WIDTH), F32),
            jax.ShapeDtypeStruct((BATCH * seq, HYENA_WIDTH), F32),
            jax.ShapeDtypeStruct((seq, BATCH * HYENA_WIDTH), BF16),
        ],
        compiler_params=_cparams("arbitrary", "arbitrary"),
        name="hy_prep",
    )(proj, proj, proj, short_w, short_w, short_w, short_b, short_b, short_b)


def _hy_feats(seq):
    t = jnp.linspace(0.0, 1.0, seq, dtype=F32)[:, None]
    f = jnp.linspace(1e-4, HYENA_BANDS - 1, HYENA_BANDS, dtype=F32)
    w = 2.0 * math.pi * jnp.arange(seq, dtype=F32) / seq
    fw = w[:, None] * f[None, :]
    z = jnp.concatenate([t, jnp.cos(fw), -jnp.sin(fw)], axis=-1)
    return jnp.pad(z, ((0, 0), (0, LANE - HYENA_EMB)))


def _hy_deltas():
    max_decay = math.log(HYENA_TARGET) / HYENA_FAST_PCT
    min_decay = math.log(HYENA_TARGET) / HYENA_SLOW_PCT
    return jnp.abs(jnp.linspace(min_decay, max_decay, HYENA_WIDTH, dtype=F32)).reshape(1, HYENA_WIDTH)


def _hy_filter_body(z_ref, w1_ref, b1_ref, w2_ref, b2_ref, w3_ref, fr_ref, dl_ref, k_ref, nyq_ref, *, seq):
    hp = lax.Precision.HIGHEST
    freq = fr_ref[...]
    h = jnp.sin(freq * (jnp.dot(z_ref[...], w1_ref[...], precision=hp, preferred_element_type=F32) + b1_ref[...]))
    h = jnp.sin(freq * (jnp.dot(h, w2_ref[...], precision=hp, preferred_element_type=F32) + b2_ref[...]))
    h = jnp.dot(h, w3_ref[...], precision=hp, preferred_element_type=F32)
    ti = lax.broadcasted_iota(jnp.int32, (seq, HYENA_WIDTH), 0)
    decay = jnp.exp(-(ti.astype(F32) * (1.0 / (seq - 1))) * dl_ref[...])
    hf = h[:, :HYENA_WIDTH] * decay
    hb = jnp.where(ti == 0, 0.0, h[:, HYENA_WIDTH:] * decay)
    ks = hf + hb
    k_ref[:, :HYENA_WIDTH] = ks.astype(BF16)
    k_ref[:, HYENA_WIDTH:] = (hf - hb).astype(BF16)
    nyq = jnp.sum(jnp.where(ti % 2 == 0, ks, -ks), axis=0, keepdims=True)
    nyq_ref[...] = jnp.broadcast_to(nyq, nyq_ref.shape)


def _hy_filter(lp, z, deltas, seq):
    w1 = jnp.pad(lp['hy_f1_w'], ((0, LANE - HYENA_EMB), (0, 0)))
    args = (z, w1, lp['hy_f1_b'].reshape(1, -1), lp['hy_f2_w'], lp['hy_f2_b'].reshape(1, -1), lp['hy_f3_w'],
            lp['hy_freq'].reshape(1, -1), deltas)
    full = lambda a: pl.BlockSpec(a.shape, lambda i: (0,) * a.ndim)
    return pl.pallas_call(
        functools.partial(_hy_filter_body, seq=seq),
        grid=(1,),
        in_specs=[full(a) for a in args],
        out_specs=[pl.BlockSpec((seq, 2 * HYENA_WIDTH), lambda i: (0, 0)),
                   pl.BlockSpec((8, HYENA_WIDTH), lambda i: (0, 0))],
        out_shape=[jax.ShapeDtypeStruct((seq, 2 * HYENA_WIDTH), BF16),
                   jax.ShapeDtypeStruct((8, HYENA_WIDTH), F32)],
        compiler_params=_cparams("arbitrary"),
        name="hy_filter",
    )(*args)


DFT_SPLIT = 64


def _dft_matrices(seq):
    n = 2 * seq
    f = jnp.arange(seq, dtype=jnp.int32)[:, None]

    def table(step, count):
        idx = (f * (jnp.arange(count, dtype=jnp.int32)[None, :] * step)) % n
        ang = idx.astype(F32) * (2.0 * math.pi / n)
        return jnp.cos(ang), jnp.sin(ang)

    hc, hs = table(DFT_SPLIT, seq // DFT_SPLIT)
    lc, ls = table(1, DFT_SPLIT)
    cos = (hc[:, :, None] * lc[:, None, :] - hs[:, :, None] * ls[:, None, :]).reshape(seq, seq)
    sin = (hs[:, :, None] * lc[:, None, :] + hc[:, :, None] * ls[:, None, :]).reshape(seq, seq)
    s = jnp.arange(seq, dtype=jnp.int32)[None, :]
    nyq = jnp.where(s % 2 == 0, 1.0, -1.0).astype(F32)
    wf = jnp.concatenate([cos, jnp.where(f == 0, nyq, -sin)], axis=0).astype(BF16)
    return wf, wf.T


def _mm_body(a_ref, b_ref, o_ref):
    o_ref[...] = _dot(a_ref[...], b_ref[...])


def _dft_fwd_body(wc_ref, ws_ref, x_ref, hr_ref, hi_ref, nyq_ref, pr_ref, pi_ref, *, n):
    tf = wc_ref.shape[0]
    freq = lax.broadcasted_iota(jnp.int32, (tf, MXU_COLS), 0) + pl.program_id(0) * tf
    first = freq == 0
    w = jnp.where(first, 1.0 / n, 2.0 / n)
    chunks = [slice(c, c + MXU_COLS) for c in range(0, x_ref.shape[1], MXU_COLS)]
    spectra = [(_dot(wc_ref[...], x_ref[:, sl]), _dot(ws_ref[...], x_ref[:, sl])) for sl in chunks]
    for sl, (xr, xi) in zip(chunks, spectra):
        hr = hr_ref[:, sl]
        hi = hi_ref[:, sl]
        pr_ref[:, sl] = (jnp.where(first, xr * hr, xr * hr - xi * hi) * w).astype(BF16)
        pi_ref[:, sl] = (jnp.where(first, xi * nyq_ref[0:1, sl], xr * hi + xi * hr) * w).astype(BF16)


def _dft_fwd(wf, x, hf, nyq):
    n, seq = wf.shape
    tf = min(seq, 512)
    nf = seq // tf
    out = jax.ShapeDtypeStruct((seq, BATCH * HYENA_WIDTH), BF16)
    return pl.pallas_call(
        functools.partial(_dft_fwd_body, n=n),
        grid=(nf, BATCH),
        in_specs=[
            pl.BlockSpec((tf, seq), lambda f, b: (f, 0)),
            pl.BlockSpec((tf, seq), lambda f, b: (nf + f, 0)),
            pl.BlockSpec((seq, HYENA_WIDTH), lambda f, b: (0, b)),
            pl.BlockSpec((tf, HYENA_WIDTH), lambda f, b: (f, 0)),
            pl.BlockSpec((tf, HYENA_WIDTH), lambda f, b: (nf + f, 0)),
            pl.BlockSpec((8, HYENA_WIDTH), lambda f, b: (0, 0)),
        ],
        out_specs=[pl.BlockSpec((tf, HYENA_WIDTH), lambda f, b: (f, b))] * 2,
        out_shape=[out, out],
        compiler_params=_cparams("arbitrary", "arbitrary"),
        name="dft_fwd",
    )(wf, wf, x, hf, hf, nyq)


def _dft_filter(wf, k):
    n, seq = wf.shape
    tm = min(seq, 1024)
    return pl.pallas_call(
        _mm_body,
        grid=(n // tm,),
        in_specs=[
            pl.BlockSpec((tm, seq), lambda i: (i, 0)),
            pl.BlockSpec((seq, HYENA_WIDTH), lambda i: (0, (i * tm) // seq)),
        ],
        out_specs=pl.BlockSpec((tm, HYENA_WIDTH), lambda i: (i, 0)),
        out_shape=jax.ShapeDtypeStruct((n, HYENA_WIDTH), F32),
        compiler_params=_cparams("arbitrary"),
        name="dft_filter",
    )(wf, k)


def _dft_inv_body(wc_ref, ws_ref, pr_ref, pi_ref, vx_ref, bias_ref, x0_ref, o_ref):
    y = _dot(wc_ref[...], pr_ref[...]) + _dot(ws_ref[...], pi_ref[...])
    o_ref[...] = ((y + vx_ref[...] * bias_ref[...]) * x0_ref[...]).astype(BF16)


def _dft_inv(wi, p_re, p_im, vx, bias, x0, bb, row0):
    seq, n = wi.shape
    tm = min(seq, 1024)
    tn = HYENA_WIDTH
    nt = seq // tm
    return pl.pallas_call(
        _keep_branch_buffer(_dft_inv_body, 7),
        grid=(nt, BATCH),
        in_specs=[
            pl.BlockSpec((tm, seq), lambda i, b: (i, 0)),
            pl.BlockSpec((tm, seq), lambda i, b: (i, 1)),
            pl.BlockSpec((seq, tn), lambda i, b: (0, b)),
            pl.BlockSpec((seq, tn), lambda i, b: (0, b)),
            pl.BlockSpec((tm, tn), lambda i, b: (b * nt + i, 0)),
            pl.BlockSpec((1, tn), lambda i, b: (0, 0)),
            pl.BlockSpec((tm, tn), lambda i, b: (b * nt + i, 0)),
            _ANY,
        ],
        out_specs=pl.BlockSpec((None, tm, tn), lambda i, b: (SLOT_HYENA, row0 // tm + b * nt + i, 0)),
        out_shape=jax.ShapeDtypeStruct(bb.shape, bb.dtype),
        input_output_aliases={7: 0},
        compiler_params=_cparams("arbitrary", "arbitrary"),
        name="dft_inv",
    )(wi, wi, p_re, p_im, vx, bias.reshape(1, tn), x0, bb)


def _hyena(proj, lp, consts, bb, seq, row0):
    z, deltas, (wf, wi) = consts
    x0, vx, vb = _hy_prep(proj, lp['hy_short_w'], lp['hy_short_b'], seq, row0 // seq)
    k, nyq = _hy_filter(lp, z, deltas, seq)
    hf = _dft_filter(wf, k)
    p_re, p_im = _dft_fwd(wf, vb, hf, nyq)
    return _dft_inv(wi, p_re, p_im, vx, lp['hy_bias'], x0, bb, row0)


def _s5_params(lp, n_seg_steps):
    a_re, a_im = lp['s5_a_re'], lp['s5_a_im']
    dt = jnp.exp(lp['s5_log_dt'])[..., None]
    mag = jnp.exp(a_re * dt)
    ab_re, ab_im = mag * jnp.cos(a_im * dt), mag * jnp.sin(a_im * dt)
    den = a_re * a_re + a_im * a_im
    nr, ni = ab_re - 1.0, ab_im
    cf_re = (nr * a_re + ni * a_im) / den
    cf_im = (ni * a_re - nr * a_im) / den
    b_re, b_im = lp['s5_b_re'], lp['s5_b_im']
    bb_re = cf_re[..., None] * b_re - cf_im[..., None] * b_im
    bb_im = cf_re[..., None] * b_im + cf_im[..., None] * b_re
    eye = jnp.eye(S5_CHUNK_GROUPS, dtype=F32)

    def bdiag_in(m):
        m = m.reshape(2, S5_NCHUNK, S5_CHUNK_GROUPS, S5_STATE, S5_GC)
        return jnp.einsum('dqgpc,gh->dqgchp', m, eye).reshape(2, S5_NCHUNK, S5_CHUNK_CH, S5_CHUNK_ST)

    def bdiag_out(m):
        m = m.reshape(2, S5_NCHUNK, S5_CHUNK_GROUPS, S5_GC, S5_STATE)
        return jnp.einsum('dqgcp,gh->dqhpgc', m, eye).reshape(2, S5_NCHUNK, S5_CHUNK_ST, S5_CHUNK_CH)

    bbd = jnp.concatenate([bdiag_in(bb_re), bdiag_in(bb_im)], axis=-1).astype(BF16)
    cbd = jnp.concatenate([bdiag_out(lp['s5_c_re']), -bdiag_out(lp['s5_c_im'])], axis=-2).astype(BF16)
    a = jnp.stack([ab_re.reshape(2, -1), ab_im.reshape(2, -1)], axis=1)
    aks = []
    for steps in n_seg_steps:
        pr, pi = ab_re, ab_im
        for _ in range(int(math.log2(steps))):
            pr, pi = pr * pr - pi * pi, 2.0 * pr * pi
        aks.append(jnp.stack([pr.reshape(2, -1), pi.reshape(2, -1)], axis=1))
    return [(bbd, cbd, a, ak) for ak in aks]


S5_BLOCK = 256


def _s5_scan_body(u_ref, bbd_ref, cbd_ref, a_ref, ak_ref, dsk_ref, h0_ref, y_ref, hl_ref,
                  xr0_ref, xr1_ref, xi0_ref, xi1_ref, yd0_ref, yd1_ref, up_ref, ub_ref, *, seq):
    nk = seq // S5_LANES
    nblk = seq // S5_BLOCK
    tiles = S5_BLOCK // S5_LANES
    shape = (S5_LANES, S5_CHUNK_ST)
    xr_ref, xi_ref, yd_ref = (xr0_ref, xr1_ref), (xi0_ref, xi1_ref), (yd0_ref, yd1_ref)
    for j in range(S5_LANES):
        up_ref[pl.ds(j, nk, stride=S5_LANES), :] = u_ref[pl.ds(j * nk, nk), :]
    ub_ref[...] = up_ref[...].astype(BF16)
    row = lax.broadcasted_iota(jnp.int32, shape, 0)
    zero = jnp.zeros(shape, F32)

    def block(r):
        return pl.ds(pl.multiple_of(r * S5_BLOCK, S5_BLOCK), S5_BLOCK)

    def in_proj(d, rows):
        xr_ref[d][rows, :] = _dot(ub_ref[rows, :], bbd_ref[d, :, :S5_CHUNK_ST])
        xi_ref[d][rows, :] = _dot(ub_ref[rows, :], bbd_ref[d, :, S5_CHUNK_ST:])

    def out_proj(d, rows):
        yd_ref[d][rows, :] = (_dot(xr_ref[d][rows, :].astype(BF16), cbd_ref[d, :S5_CHUNK_ST, :])
                              + _dot(xi_ref[d][rows, :].astype(BF16), cbd_ref[d, S5_CHUNK_ST:, :]))

    def coeffs(d):
        return jnp.broadcast_to(a_ref[d, 0:1, :], shape), jnp.broadcast_to(a_ref[d, 1:2, :], shape)

    def block_tiles(d, r):
        base = pl.multiple_of(r * S5_BLOCK, S5_BLOCK)
        order = range(tiles) if d == 0 else range(tiles - 1, -1, -1)
        return [pl.ds(base + t * S5_LANES, S5_LANES) for t in order]

    def scan_block(d, r, carry):
        ar, ai = coeffs(d)
        xr, xi = carry
        for rows in block_tiles(d, r):
            xr, xi = (ar * xr - ai * xi + xr_ref[d][rows, :], ar * xi + ai * xr + xi_ref[d][rows, :])
            xr_ref[d][rows, :] = xr
            xi_ref[d][rows, :] = xi
        return xr, xi

    def fix_block(d, r, carry):
        ar, ai = coeffs(d)
        gr, gi = carry
        for rows in block_tiles(d, r):
            gr, gi = ar * gr - ai * gi, ar * gi + ai * gr
            xr_ref[d][rows, :] += gr
            xi_ref[d][rows, :] += gi
        return gr, gi

    def visit(d, s):
        return s if d == 0 else nblk - 1 - s

    def entering_states(d, er, ei):
        akr, aki = ak_ref[d, 0:1, :], ak_ref[d, 1:2, :]
        hr, hi = h0_ref[2 * d:2 * d + 1, :], h0_ref[2 * d + 1:2 * d + 2, :]
        in_r, in_i = zero, zero
        for j in (range(S5_LANES) if d == 0 else range(S5_LANES - 1, -1, -1)):
            in_r = jnp.where(row == j, hr, in_r)
            in_i = jnp.where(row == j, hi, in_i)
            hr, hi = (akr * hr - aki * hi + er[j:j + 1, :], akr * hi + aki * hr + ei[j:j + 1, :])
        hl_ref[2 * d:2 * d + 1, :] = hr
        hl_ref[2 * d + 1:2 * d + 2, :] = hi
        return in_r, in_i

    def fix_all(d, ins):
        lax.fori_loop(0, nblk, lambda s, carry: fix_block(d, visit(d, s), carry), ins)

    in_proj(0, slice(None))

    def scan0_body(s, carry):
        in_proj(1, block(s))
        return scan_block(0, s, carry)
    ends0 = lax.fori_loop(0, nblk, scan0_body, (zero, zero))
    fix_all(0, entering_states(0, *ends0))

    def scan1_body(s, carry):
        out_proj(0, block(s))
        return scan_block(1, visit(1, s), carry)
    ends1 = lax.fori_loop(0, nblk, scan1_body, (zero, zero))
    fix_all(1, entering_states(1, *ends1))
    out_proj(1, slice(None))

    up_ref[...] = yd_ref[0][...] + yd_ref[1][...] + up_ref[...] * (dsk_ref[0] + dsk_ref[1])
    for j in range(S5_LANES):
        y_ref[pl.ds(j * nk, nk), :] = up_ref[pl.ds(j, nk, stride=S5_LANES), :]


def _s5_scan(proj, params, dskip, h0, seq, row_blk0):
    bbd, cbd, a, ak = params
    nq = S5_NCHUNK
    col0 = OFF_S5 // S5_CHUNK_CH
    return pl.pallas_call(
        functools.partial(_s5_scan_body, seq=seq),
        grid=(BATCH, nq),
        in_specs=[
            pl.BlockSpec((seq, S5_CHUNK_CH), lambda b, q: (row_blk0 + b, col0 + q)),
            pl.BlockSpec((2, None, S5_CHUNK_CH, 2 * S5_CHUNK_ST), lambda b, q: (0, q, 0, 0)),
            pl.BlockSpec((2, None, 2 * S5_CHUNK_ST, S5_CHUNK_CH), lambda b, q: (0, q, 0, 0)),
            pl.BlockSpec((2, 2, S5_CHUNK_ST), lambda b, q: (0, 0, q)),
            pl.BlockSpec((2, 2, S5_CHUNK_ST), lambda b, q: (0, 0, q)),
            pl.BlockSpec((2, 1, S5_CHUNK_CH), lambda b, q: (0, 0, q)),
            pl.BlockSpec((None, 4, S5_CHUNK_ST), lambda b, q: (b, 0, q)),
        ],
        out_specs=[
            pl.BlockSpec((seq, S5_CHUNK_CH), lambda b, q: (b, q)),
            pl.BlockSpec((None, 4, S5_CHUNK_ST), lambda b, q: (b, 0, q)),
        ],
        out_shape=[
            jax.ShapeDtypeStruct((BATCH * seq, S5_WIDTH), F32),
            jax.ShapeDtypeStruct((BATCH, 4, S5_GROUPS * S5_STATE), F32),
        ],
        scratch_shapes=[pltpu.VMEM((seq, S5_CHUNK_ST), F32)] * 4 + [pltpu.VMEM((seq, S5_CHUNK_CH), F32)] * 3
        + [pltpu.VMEM((seq, S5_CHUNK_CH), BF16)],
        compiler_params=_cparams("arbitrary", "arbitrary"),
        name="s5_scan",
    )(proj, bbd, cbd, a, ak, dskip.reshape(2, 1, S5_WIDTH), h0)


def _s5_glu_body(y_ref, w_ref, b_ref, o_ref):
    w = w_ref[...].astype(BF16)
    half = y_ref.shape[0] // 2
    halves = [slice(0, half), slice(half, 2 * half)]
    gs = [_dot(jax.nn.gelu(y_ref[rows, :]).astype(BF16), w) + b_ref[...] for rows in halves]
    for rows, g in zip(halves, gs):
        o_ref[rows, :] = (g[:, :S5_WIDTH] * jax.nn.sigmoid(g[:, S5_WIDTH:])).astype(BF16)


def _s5_glu(y, w, b, bb, row0):
    n_rows = y.shape[0]
    tm = TM_EW
    return pl.pallas_call(
        _keep_branch_buffer(_s5_glu_body, 3),
        grid=(n_rows // tm,),
        in_specs=[
            pl.BlockSpec((tm, S5_WIDTH), lambda i: (i, 0)),
            pl.BlockSpec((S5_WIDTH, 2 * S5_WIDTH), lambda i: (0, 0)),
            pl.BlockSpec((1, 2 * S5_WIDTH), lambda i: (0, 0)),
            _ANY,
        ],
        out_specs=pl.BlockSpec((None, tm, S5_WIDTH), lambda i: (SLOT_S5, row0 // tm + i, 0)),
        out_shape=jax.ShapeDtypeStruct(bb.shape, bb.dtype),
        input_output_aliases={3: 0},
        compiler_params=_cparams("arbitrary"),
        name="s5_glu",
    )(y, w, b.reshape(1, -1), bb)


def _merge_body(u_ref, *refs):
    wg_refs, bg_refs, y_refs, wb_refs = (refs[k * N_BRANCH:(k + 1) * N_BRANCH] for k in range(4))
    o_ref = refs[4 * N_BRANCH]
    u = u_ref[...]
    acc = None
    for n in range(N_BRANCH):
        gate = jax.nn.sigmoid(_dot(u, wg_refs[n][...].astype(BF16)) + bg_refs[n][...])
        contrib = gate * _dot(y_refs[n][...], wb_refs[n][...].astype(BF16))
        acc = contrib if acc is None else acc + contrib
    o_ref[...] = acc.astype(BF16)


def _merge(u, w_gate, b_gate, branches, w_branch, l, n_rows):
    tc = MXU_COLS
    ncol = D_MODEL // tc
    b_gate = b_gate.reshape(1, -1)
    per_branch = lambda make: [make(n) for n in range(N_BRANCH)]
    return pl.pallas_call(
        _merge_body,
        grid=(n_rows // TM, ncol),
        in_specs=[pl.BlockSpec((TM, D_MODEL), lambda i, c: (i, 0))]
        + per_branch(lambda n: pl.BlockSpec((None, D_MODEL, tc), lambda i, c: (l, 0, n * ncol + c)))
        + per_branch(lambda n: pl.BlockSpec((1, tc), lambda i, c: (0, n * ncol + c)))
        + per_branch(lambda n: pl.BlockSpec((None, TM, BRANCH_WIDTH), lambda i, c: (n, i, 0)))
        + per_branch(lambda n: pl.BlockSpec((None, None, BRANCH_WIDTH, tc), lambda i, c: (l, n, 0, c))),
        out_specs=pl.BlockSpec((TM, tc), lambda i, c: (i, c)),
        out_shape=jax.ShapeDtypeStruct((n_rows, D_MODEL), BF16),
        compiler_params=_cparams("arbitrary", "arbitrary"),
        name="merge",
    )(u, *([w_gate] * N_BRANCH), *([b_gate] * N_BRANCH), *([branches] * N_BRANCH), *([w_branch] * N_BRANCH))


def _out_body(m_ref, w_ref, x_ref, g_ref, o_ref):
    o_ref[...] = x_ref[...] + g_ref[...] * _dot(m_ref[...], w_ref[...].astype(BF16))


def _out_proj(merged, w_out, l, x, mods, n_rows):
    tn = 1024
    gate = pl.BlockSpec((None, None, 1, tn), lambda j, i: (_mod_row(i, TM), 5, 0, j))
    return pl.pallas_call(
        _out_body,
        grid=(D_MODEL // tn, n_rows // TM),
        in_specs=[
            pl.BlockSpec((TM, D_MODEL), lambda j, i: (i, 0)),
            pl.BlockSpec((None, D_MODEL, tn), lambda j, i: (l, 0, j)),
            pl.BlockSpec((TM, tn), lambda j, i: (i, j)),
            gate,
        ],
        out_specs=pl.BlockSpec((TM, tn), lambda j, i: (i, j)),
        out_shape=jax.ShapeDtypeStruct((n_rows, D_MODEL), F32),
        compiler_params=_cparams("arbitrary", "arbitrary"),
        name="out_proj",
    )(merged, w_out, x, mods)


def _mixer(x, g, lp, big, w_in_bf16, l, mods, consts, bb, last):
    cos, sin, hy_lat, hy_ctx = consts
    proj, u = _proj(x, g, mods, w_in_bf16)
    q, k, v = _qkv(proj, cos, sin, lp['q_norm'], lp['k_norm'])

    ctx_blk = N_LAT // CTX_LEN
    zero_h = jnp.zeros((BATCH, 4, S5_GROUPS * S5_STATE), F32)
    par_ctx, par_lat = _s5_params(lp, (CTX_LEN // S5_LANES, SEQ // S5_LANES))
    ys_ctx, h_ctx = _s5_scan(proj, par_ctx, lp['s5_d'], zero_h, CTX_LEN, ctx_blk)
    ys_lat, _ = _s5_scan(proj, par_lat, lp['s5_d'], h_ctx, SEQ, 0)

    bb = _s5_glu(ys_lat, lp['s5_glu_w'], lp['s5_glu_b'], bb, 0)
    bb = _attn_lat(q, k, v, bb)
    bb = _pool(proj, lp['pool_w'], lp['pool_scale'], bb, SEQ, 0)
    bb = _hyena(proj, lp, hy_lat, bb, SEQ, 0)
    if not last:
        bb = _s5_glu(ys_ctx, lp['s5_glu_w'], lp['s5_glu_b'], bb, N_LAT)
        bb = _attn_ctx(q, k, v, bb)
        bb = _pool(proj, lp['pool_w'], lp['pool_scale'], bb, CTX_LEN, ctx_blk)
        bb = _hyena(proj, lp, hy_ctx, bb, CTX_LEN, N_LAT)

    n_rows = N_LAT if last else MT
    merged = _merge(u, big['w_gate'], lp['b_gate'], bb, big['w_branch'], l, n_rows)
    return _out_proj(merged, big['w_out'], l, x, mods, n_rows), bb


def kernel(x, c, ctx, c_ctx, w_ada, b_ada, norm_ffn1, norm_mix, norm_ffn2, norm_final, ffn1_wi, ffn1_wo, ffn2_wi, ffn2_wo, w_in, w_gate, b_gate, w_branch, w_out, pool_w, pool_scale, q_norm, k_norm, hy_short_w, hy_short_b, hy_f1_w, hy_f1_b, hy_f2_w, hy_f2_b, hy_f3_w, hy_freq, hy_bias, s5_a_re, s5_a_im, s5_log_dt, s5_b_re, s5_b_im, s5_c_re, s5_c_im, s5_d, s5_glu_w, s5_glu_b):
    big = dict(w_gate=w_gate, w_branch=w_branch, w_out=w_out)
    per_layer = dict(
        b_gate=b_gate, pool_w=pool_w,
        pool_scale=pool_scale, q_norm=q_norm, k_norm=k_norm, hy_short_w=hy_short_w, hy_short_b=hy_short_b,
        hy_f1_w=hy_f1_w, hy_f1_b=hy_f1_b, hy_f2_w=hy_f2_w, hy_f2_b=hy_f2_b, hy_f3_w=hy_f3_w, hy_freq=hy_freq,
        hy_bias=hy_bias, s5_a_re=s5_a_re, s5_a_im=s5_a_im, s5_log_dt=s5_log_dt, s5_b_re=s5_b_re, s5_b_im=s5_b_im,
        s5_c_re=s5_c_re, s5_c_im=s5_c_im, s5_d=s5_d, s5_glu_w=s5_glu_w, s5_glu_b=s5_glu_b)

    cos, sin = _rope_tables()
    deltas = _hy_deltas()
    consts = (cos, sin,
              (_hy_feats(SEQ), deltas, _dft_matrices(SEQ)),
              (_hy_feats(CTX_LEN), deltas, _dft_matrices(CTX_LEN)))

    cc = jnp.concatenate([c, c_ctx[None], jnp.zeros((8 - BATCH - 1, D_MODEL), F32)], axis=0)
    mods_all = _ada(cc, w_ada, b_ada).reshape(DEPTH, 8, N_MOD, 1, D_MODEL)

    xs = jnp.concatenate([x.reshape(N_LAT, D_MODEL), ctx.reshape(N_CTX, D_MODEL)], axis=0)
    bb = jnp.zeros((N_BRANCH, MT, BRANCH_WIDTH), BF16)
    for l in range(DEPTH):
        last = l == DEPTH - 1
        lp = {name: w[l] for name, w in per_layer.items()}
        mods = mods_all[l]
        xs, w_in_bf16 = _ffn(xs, mods, norm_ffn1[l], ffn1_wi, ffn1_wo, l, 0, MT, side=w_in)
        xs, bb = _mixer(xs, norm_mix[l], lp, big, w_in_bf16, l, mods, consts, bb, last)
        xs = _ffn(xs, mods, norm_ffn2[l], ffn2_wi, ffn2_wo, l, 6, N_LAT if last else MT)
    return _final_norm(xs, norm_final, N_LAT).reshape(BATCH, SEQ, D_MODEL)
```

```python
import functools
import math

import jax
import jax.numpy as jnp
from jax import lax
from jax.experimental import pallas as pl
from jax.experimental.pallas import tpu as pltpu

F32 = jnp.float32
BF16 = jnp.bfloat16

D_MODEL = 2048
BATCH = 4
SEQ = 2048
DEPTH = 4
GRID_W = 64
CTX_LEN = 256
D_FF = 5632
N_MOD = 9
EPS = 1e-6

POOL_WINDOWS = (2, 4, 8, 16)
HEAD_DIM = 128
N_Q_HEADS = 4
N_KV_HEADS = 2
Q_GROUP = N_Q_HEADS // N_KV_HEADS
ROPE_THETA = 10000.0
ROPE_FREQS = HEAD_DIM // 4

HYENA_WIDTH = 512
HYENA_EMB = 33
HYENA_BANDS = (HYENA_EMB - 1) // 2
HYENA_HIDDEN = 64
HYENA_TARGET = 1e-2
HYENA_FAST_PCT = 0.3
HYENA_SLOW_PCT = 1.5

S5_WIDTH = 512
S5_GC = 16
S5_GROUPS = S5_WIDTH // S5_GC
S5_STATE = 64
S5_LANES = 8
S5_CHUNK_GROUPS = 8
S5_CHUNK_CH = S5_CHUNK_GROUPS * S5_GC
S5_CHUNK_ST = S5_CHUNK_GROUPS * S5_STATE
S5_NCHUNK = S5_GROUPS // S5_CHUNK_GROUPS

N_BRANCH = 4
BRANCH_WIDTH = 512
LANE = 128
COL = 512

OFF_Q = 512
OFF_KV = 1024
OFF_HY = 1536
OFF_S5 = 3072
IN_WIDTH = 3584

N_LAT = BATCH * SEQ
N_CTX = BATCH * CTX_LEN
MT = N_LAT + N_CTX

TM = 1024
TM_EW = 512
VMEM_LIMIT = 56 * 1024 * 1024


def _cparams(*sem):
    return pltpu.CompilerParams(dimension_semantics=sem, vmem_limit_bytes=VMEM_LIMIT)


def _dot(a, b):
    return jnp.dot(a, b, preferred_element_type=F32)


SLOT_POOL, SLOT_ATTN, SLOT_HYENA, SLOT_S5 = range(4)
_ANY = pl.BlockSpec(memory_space=pl.ANY)


def _keep_branch_buffer(body, n_in):
    def wrapped(*refs):
        return body(*refs[:n_in], *refs[n_in + 1:])
    return wrapped


def _mod_row(i, tm):
    return jnp.minimum((i * tm) // SEQ, BATCH)


def _mod_spec(tm, k):
    return pl.BlockSpec((None, None, 1, D_MODEL), lambda i, *_: (_mod_row(i, tm), k, 0, 0))


def _ada_body(c_ref, w_ref, b_ref, o_ref):
    c = c_ref[...]
    a = (c * jax.nn.sigmoid(c)).astype(BF16)
    o_ref[...] = _dot(a, w_ref[...].astype(BF16)) + b_ref[...]


def _ada(cc, w_ada, b_ada):
    tn = 1024
    nw = N_MOD * D_MODEL
    return pl.pallas_call(
        _ada_body,
        grid=(DEPTH, nw // tn),
        in_specs=[
            pl.BlockSpec((8, D_MODEL), lambda l, j: (0, 0)),
            pl.BlockSpec((None, D_MODEL, tn), lambda l, j: (l, 0, j)),
            pl.BlockSpec((None, 1, tn), lambda l, j: (l, 0, j)),
        ],
        out_specs=pl.BlockSpec((None, 8, tn), lambda l, j: (l, 0, j)),
        out_shape=jax.ShapeDtypeStruct((DEPTH, 8, nw), F32),
        compiler_params=_cparams("arbitrary", "arbitrary"),
        name="ada",
    )(cc, w_ada, b_ada.reshape(DEPTH, 1, nw))


NORM_ROWS = 16


def _rows_loop(n_rows, step):
    for c in range(n_rows // NORM_ROWS):
        step(pl.ds(c * NORM_ROWS, NORM_ROWS))


def _rms_scale_rows(x_ref, r_ref, scale, shift, o_ref):
    def stats(rows):
        x = x_ref[rows, :]
        r_ref[rows, :] = lax.rsqrt(jnp.mean(x * x, axis=-1, keepdims=True) + EPS)

    def apply(rows):
        y = x_ref[rows, :] * r_ref[rows, :] * scale
        if shift is not None:
            y = y + shift
        o_ref[rows, :] = y.astype(o_ref.dtype)

    _rows_loop(x_ref.shape[0], stats)
    _rows_loop(x_ref.shape[0], apply)


def _normmod_body(x_ref, g_ref, sh_ref, sc_ref, o_ref, r_ref):
    _rms_scale_rows(x_ref, r_ref, g_ref[...] * (1.0 + sc_ref[...]), sh_ref[...], o_ref)


def _norm_body(x_ref, g_ref, o_ref, r_ref):
    _rms_scale_rows(x_ref, r_ref, g_ref[...], None, o_ref)


def _normmod(x, g, mods, base, n_rows):
    tm = TM_EW
    return pl.pallas_call(
        _normmod_body,
        grid=(n_rows // tm,),
        in_specs=[
            pl.BlockSpec((tm, D_MODEL), lambda i: (i, 0)),
            pl.BlockSpec((1, D_MODEL), lambda i: (0, 0)),
            _mod_spec(tm, base),
            _mod_spec(tm, base + 1),
        ],
        out_specs=pl.BlockSpec((tm, D_MODEL), lambda i: (i, 0)),
        out_shape=jax.ShapeDtypeStruct((n_rows, D_MODEL), BF16),
        scratch_shapes=[pltpu.VMEM((tm, 1), F32)],
        compiler_params=_cparams("arbitrary"),
        name="normmod",
    )(x, g.reshape(1, D_MODEL), mods, mods)


def _final_norm(x, g, n_rows):
    tm = TM_EW
    return pl.pallas_call(
        _norm_body,
        grid=(n_rows // tm,),
        in_specs=[
            pl.BlockSpec((tm, D_MODEL), lambda i: (i, 0)),
            pl.BlockSpec((1, D_MODEL), lambda i: (0, 0)),
        ],
        out_specs=pl.BlockSpec((tm, D_MODEL), lambda i: (i, 0)),
        out_shape=jax.ShapeDtypeStruct((n_rows, D_MODEL), F32),
        scratch_shapes=[pltpu.VMEM((tm, 1), F32)],
        compiler_params=_cparams("arbitrary"),
        name="final_norm",
    )(x, g.reshape(1, D_MODEL))


MXU_COLS = 256


WO_SLAB = 64
WO_NSLAB = D_FF // WO_SLAB


NEXT_ROWS = 128
NEXT_SLICES = TM // NEXT_ROWS


def _ffn_a_body(u0_ref, xn_ref, g_ref, sh_ref, sc_ref, wa_ref, wb_ref, wo_ref, h_ref, wob_ref, u2_ref, r_ref):
    i, j = pl.program_id(0), pl.program_id(1)
    slot = i % 2

    @pl.when((i == 0) & (j == 0))
    def _():
        u2_ref[0] = u0_ref[...]

    u = u2_ref[slot]
    for c in range(0, h_ref.shape[1], MXU_COLS):
        sl = slice(c, c + MXU_COLS)
        a = _dot(u, wa_ref[:, sl].astype(BF16))
        b = _dot(u, wb_ref[:, sl].astype(BF16))
        h_ref[:, sl] = (a * jax.nn.sigmoid(a) * b).astype(BF16)

    dst = pl.ds(pl.multiple_of(jnp.minimum(j, NEXT_SLICES - 1) * NEXT_ROWS, NEXT_ROWS), NEXT_ROWS)
    _rms_scale_rows(xn_ref, r_ref, g_ref[...] * (1.0 + sc_ref[...]), sh_ref[...], u2_ref.at[1 - slot, dst, :])

    @pl.when(pl.program_id(0) * pl.num_programs(1) + pl.program_id(1) < WO_NSLAB)
    def _():
        wob_ref[...] = wo_ref[...].astype(BF16)


def _ffn_a(x, g, mods, base, wi, wo, l, n_rows):
    tf = 512
    nf = D_FF // tf
    ni = n_rows // TM
    assert ni * nf >= WO_NSLAB and nf >= NEXT_SLICES

    def wo_blk(i, j):
        return jnp.minimum(i * nf + j, WO_NSLAB - 1)

    def nxt(i):
        return jnp.minimum(i + 1, ni - 1)

    def next_mod(k):
        return pl.BlockSpec((None, None, 1, D_MODEL), lambda i, j: (_mod_row(nxt(i), TM), k, 0, 0))

    u0 = _normmod(x, g, mods, base, TM)
    return pl.pallas_call(
        _ffn_a_body,
        grid=(ni, nf),
        in_specs=[
            pl.BlockSpec((TM, D_MODEL), lambda i, j: (0, 0)),
            pl.BlockSpec((NEXT_ROWS, D_MODEL),
                         lambda i, j: (nxt(i) * NEXT_SLICES + jnp.minimum(j, NEXT_SLICES - 1), 0)),
            pl.BlockSpec((1, D_MODEL), lambda i, j: (0, 0)),
            next_mod(base),
            next_mod(base + 1),
            pl.BlockSpec((None, D_MODEL, tf), lambda i, j: (l, 0, j)),
            pl.BlockSpec((None, D_MODEL, tf), lambda i, j: (l, 0, j + nf)),
            pl.BlockSpec((None, WO_SLAB, D_MODEL), lambda i, j: (l, wo_blk(i, j), 0)),
        ],
        out_specs=[
            pl.BlockSpec((TM, tf), lambda i, j: (i, j)),
            pl.BlockSpec((WO_SLAB, D_MODEL), lambda i, j: (wo_blk(i, j), 0)),
        ],
        out_shape=[
            jax.ShapeDtypeStruct((n_rows, D_FF), BF16),
            jax.ShapeDtypeStruct((D_FF, D_MODEL), BF16),
        ],
        scratch_shapes=[pltpu.VMEM((2, TM, D_MODEL), BF16), pltpu.VMEM((NEXT_ROWS, 1), F32)],
        compiler_params=_cparams("arbitrary", "arbitrary"),
        name="ffn_a",
    )(u0, x, g.reshape(1, D_MODEL), mods, mods, wi, wi, wo)


SIDE_SLAB = 64


def _ffn_b_body(h_ref, w_ref, x_ref, g_ref, *rest, n_slab):
    o_ref = rest[-2] if n_slab else rest[-1]
    h = h_ref[...]
    for c in range(0, o_ref.shape[1], MXU_COLS):
        sl = slice(c, c + MXU_COLS)
        o_ref[:, sl] = x_ref[:, sl] + (0.5 * g_ref[:, sl]) * _dot(h, w_ref[:, sl])

    if n_slab:
        side_ref, _, side_out_ref = rest

        @pl.when(pl.program_id(0) * pl.num_programs(1) + pl.program_id(1) < n_slab)
        def _():
            side_out_ref[...] = side_ref[...].astype(BF16)


def _ffn_b(h, wo_bf16, x, mods, gate_idx, n_rows, side=None, l=None):
    tn = 512
    nj = D_MODEL // tn
    gate = pl.BlockSpec((None, None, 1, tn), lambda i, j: (_mod_row(i, TM), gate_idx, 0, j))
    in_specs = [
        pl.BlockSpec((TM, D_FF), lambda i, j: (i, 0)),
        pl.BlockSpec((D_FF, tn), lambda i, j: (0, j)),
        pl.BlockSpec((TM, tn), lambda i, j: (i, j)),
        gate,
    ]
    out_specs = [pl.BlockSpec((TM, tn), lambda i, j: (i, j))]
    out_shape = [jax.ShapeDtypeStruct((n_rows, D_MODEL), F32)]
    args = [h, wo_bf16, x, mods]
    n_slab = 0
    if side is not None:
        _, rows, cols = side.shape
        n_slab = rows // SIDE_SLAB
        assert (n_rows // TM) * nj >= n_slab

        def slab(i, j):
            return jnp.minimum(i * nj + j, n_slab - 1)

        in_specs.append(pl.BlockSpec((None, SIDE_SLAB, cols), lambda i, j: (l, slab(i, j), 0)))
        out_specs.append(pl.BlockSpec((SIDE_SLAB, cols), lambda i, j: (slab(i, j), 0)))
        out_shape.append(jax.ShapeDtypeStruct((rows, cols), BF16))
        args.append(side)
    res = pl.pallas_call(
        functools.partial(_ffn_b_body, n_slab=n_slab),
        grid=(n_rows // TM, nj),
        in_specs=in_specs,
        out_specs=out_specs,
        out_shape=out_shape,
        compiler_params=_cparams("arbitrary", "arbitrary"),
        name="ffn_b",
    )(*args)
    return res if side is not None else res[0]


def _ffn(x, mods, g, wi, wo, l, base, n_rows, side=None):
    h, wo_bf16 = _ffn_a(x, g, mods, base, wi, wo, l, n_rows)
    return _ffn_b(h, wo_bf16, x, mods, base + 2, n_rows, side, l)


PROJ_TM = 1536
PROJ_ROWS = 256
PROJ_SLICES = PROJ_TM // PROJ_ROWS


def _proj_body(u0_ref, xn_ref, g_ref, sh_ref, sc_ref, w_ref, o_ref, uo_ref, u2_ref, r_ref):
    i, j = pl.program_id(0), pl.program_id(1)
    slot = i % 2

    @pl.when((i == 0) & (j == 0))
    def _():
        u2_ref[0] = u0_ref[...]

    @pl.when(j == 0)
    def _():
        uo_ref[...] = u2_ref[slot]

    o_ref[...] = _dot(u2_ref[slot], w_ref[...])

    dst = pl.ds(pl.multiple_of(jnp.minimum(j, PROJ_SLICES - 1) * PROJ_ROWS, PROJ_ROWS), PROJ_ROWS)
    _rms_scale_rows(xn_ref, r_ref, g_ref[...] * (1.0 + sc_ref[...]), sh_ref[...], u2_ref.at[1 - slot, dst, :])


def _proj(x, g, mods, w_in_bf16):
    tn = COL
    ni = MT // PROJ_TM
    nj = IN_WIDTH // tn
    assert nj >= PROJ_SLICES and SEQ % PROJ_ROWS == 0 and N_LAT % PROJ_ROWS == 0

    def nxt_slice(i, j):
        return jnp.minimum(i + 1, ni - 1) * PROJ_SLICES + jnp.minimum(j, PROJ_SLICES - 1)

    def next_mod(k):
        return pl.BlockSpec((None, None, 1, D_MODEL), lambda i, j: (_mod_row(nxt_slice(i, j), PROJ_ROWS), k, 0, 0))

    u0 = _normmod(x, g, mods, 3, PROJ_TM)
    return pl.pallas_call(
        _proj_body,
        grid=(ni, nj),
        in_specs=[
            pl.BlockSpec((PROJ_TM, D_MODEL), lambda i, j: (0, 0)),
            pl.BlockSpec((PROJ_ROWS, D_MODEL), lambda i, j: (nxt_slice(i, j), 0)),
            pl.BlockSpec((1, D_MODEL), lambda i, j: (0, 0)),
            next_mod(3),
            next_mod(4),
            pl.BlockSpec((D_MODEL, tn), lambda i, j: (0, j)),
        ],
        out_specs=[
            pl.BlockSpec((PROJ_TM, tn), lambda i, j: (i, j)),
            pl.BlockSpec((PROJ_TM, D_MODEL), lambda i, j: (i, 0)),
        ],
        out_shape=[
            jax.ShapeDtypeStruct((MT, IN_WIDTH), F32),
            jax.ShapeDtypeStruct((MT, D_MODEL), BF16),
        ],
        scratch_shapes=[pltpu.VMEM((2, PROJ_TM, D_MODEL), BF16), pltpu.VMEM((PROJ_ROWS, 1), F32)],
        compiler_params=_cparams("arbitrary", "arbitrary"),
        name="proj",
    )(u0, x, g.reshape(1, D_MODEL), mods, mods, w_in_bf16)


QKV_TM = 512


def _rope_tables():
    t = jnp.arange(SEQ)
    rows = (t // GRID_W).astype(F32)
    cols = (t % GRID_W).astype(F32)
    freqs = ROPE_THETA ** (-jnp.arange(ROPE_FREQS, dtype=F32) / ROPE_FREQS)
    ar = rows[:, None] * freqs[None, :]
    ac = cols[:, None] * freqs[None, :]
    cos = jnp.concatenate([jnp.cos(ar), jnp.cos(ar), jnp.cos(ac), jnp.cos(ac)], axis=-1)
    sin = jnp.concatenate([-jnp.sin(ar), jnp.sin(ar), -jnp.sin(ac), jnp.sin(ac)], axis=-1)
    cos = jnp.concatenate([cos, jnp.ones((QKV_TM, HEAD_DIM), F32)], axis=0)
    sin = jnp.concatenate([sin, jnp.zeros((QKV_TM, HEAD_DIM), F32)], axis=0)
    return cos, sin


def _lane_matrices():
    lane = jnp.arange(HEAD_DIM)
    partner = jnp.where(lane % (2 * ROPE_FREQS) < ROPE_FREQS, lane + ROPE_FREQS, lane - ROPE_FREQS)
    swap = (lane[:, None] == partner[None, :]).astype(BF16)
    return jnp.ones((HEAD_DIM, HEAD_DIM), BF16), swap


def _dot_split(x, m_ref):
    hi = x.astype(BF16)
    lo = (x - hi.astype(F32)).astype(BF16)
    return _dot(hi, m_ref[...]) + _dot(lo, m_ref[...])


def _qkv_body(q_ref, kv_ref, cos_ref, sin_ref, qn_ref, kn_ref, ones_ref, swap_ref, qo_ref, ko_ref, vo_ref):
    cos = cos_ref[...]
    sin = sin_ref[...]

    def norm_rope(xh, g):
        ss = _dot_split(xh * xh, ones_ref)
        y = xh * lax.rsqrt(ss * (1.0 / HEAD_DIM) + EPS) * g
        return y * cos + _dot_split(y, swap_ref) * sin

    scale = math.log2(math.e) / math.sqrt(HEAD_DIM)
    for h in range(N_Q_HEADS):
        sl = slice(h * HEAD_DIM, (h + 1) * HEAD_DIM)
        qo_ref[:, sl] = (norm_rope(q_ref[:, sl], qn_ref[...]) * scale).astype(BF16)
    for h in range(N_KV_HEADS):
        sl = slice(h * HEAD_DIM, (h + 1) * HEAD_DIM)
        ko_ref[:, sl] = norm_rope(kv_ref[:, sl], kn_ref[...]).astype(BF16)
    kvw = N_KV_HEADS * HEAD_DIM
    for h in range(N_KV_HEADS):
        vo_ref[:, 2 * h * HEAD_DIM:(2 * h + 1) * HEAD_DIM] = (
            kv_ref[:, kvw + h * HEAD_DIM:kvw + (h + 1) * HEAD_DIM].astype(BF16))
        vo_ref[:, (2 * h + 1) * HEAD_DIM:(2 * h + 2) * HEAD_DIM] = jnp.ones((vo_ref.shape[0], HEAD_DIM), BF16)


def _qkv(proj, cos, sin, q_norm, k_norm):
    tm = QKV_TM
    n_lat_tiles = N_LAT // tm
    per_seq = SEQ // tm
    kvw = N_KV_HEADS * HEAD_DIM

    def tab(i):
        return (jnp.where(i < n_lat_tiles, i % per_seq, per_seq), 0)

    return pl.pallas_call(
        _qkv_body,
        grid=(MT // tm,),
        in_specs=[
            pl.BlockSpec((tm, COL), lambda i: (i, OFF_Q // COL)),
            pl.BlockSpec((tm, COL), lambda i: (i, OFF_KV // COL)),
            pl.BlockSpec((tm, HEAD_DIM), tab),
            pl.BlockSpec((tm, HEAD_DIM), tab),
            pl.BlockSpec((1, HEAD_DIM), lambda i: (0, 0)),
            pl.BlockSpec((1, HEAD_DIM), lambda i: (0, 0)),
            pl.BlockSpec((HEAD_DIM, HEAD_DIM), lambda i: (0, 0)),
            pl.BlockSpec((HEAD_DIM, HEAD_DIM), lambda i: (0, 0)),
        ],
        out_specs=[
            pl.BlockSpec((tm, COL), lambda i: (i, 0)),
            pl.BlockSpec((tm, kvw), lambda i: (i, 0)),
            pl.BlockSpec((tm, 2 * kvw), lambda i: (i, 0)),
        ],
        out_shape=[
            jax.ShapeDtypeStruct((MT, COL), BF16),
            jax.ShapeDtypeStruct((MT, kvw), BF16),
            jax.ShapeDtypeStruct((MT, 2 * kvw), BF16),
        ],
        compiler_params=_cparams("arbitrary"),
        name="qkv",
    )(proj, proj, cos, sin, q_norm.reshape(1, HEAD_DIM), k_norm.reshape(1, HEAD_DIM), *_lane_matrices())


def _attn_body(*refs, with_lat):
    if with_lat:
        q_ref, kl_ref, vl_ref, kc_ref, vc_ref, o_ref = refs
    else:
        q_ref, kc_ref, vc_ref, o_ref = refs
    nt = (((1,), (1,)), ((), ()))
    half = q_ref.shape[0] // 2
    units = [(slice(r, r + half), slice(g * HEAD_DIM, (g + 1) * HEAD_DIM))
             for g in range(Q_GROUP) for r in (0, half)]
    scores = []
    for rows, sl in units:
        q = q_ref[rows, sl]
        sc = lax.dot_general(q, kc_ref[...], nt, preferred_element_type=F32)
        s_lat = lax.dot_general(q, kl_ref[...], nt, preferred_element_type=F32) if with_lat else None
        scores.append((sc, s_lat))
    for (rows, sl), (sc, s_lat) in zip(units, scores):
        m = jnp.max(sc, axis=-1, keepdims=True)
        if with_lat:
            m = jnp.maximum(m, jnp.max(s_lat, axis=-1, keepdims=True))
        o = _dot(jnp.exp2(sc - m).astype(BF16), vc_ref[...])
        if with_lat:
            o = o + _dot(jnp.exp2(s_lat - m).astype(BF16), vl_ref[...])
        o_ref[rows, sl] = (o[:, :HEAD_DIM] / o[:, HEAD_DIM:]).astype(BF16)


def _attn_lat(q, k, v, bb):
    tq = 512
    nq = SEQ // tq
    ctx_blk = N_LAT // CTX_LEN
    gw = Q_GROUP * HEAD_DIM
    return pl.pallas_call(
        _keep_branch_buffer(functools.partial(_attn_body, with_lat=True), 5),
        grid=(BATCH, N_KV_HEADS, nq),
        in_specs=[
            pl.BlockSpec((tq, gw), lambda b, h, i: (b * nq + i, h)),
            pl.BlockSpec((SEQ, HEAD_DIM), lambda b, h, i: (b, h)),
            pl.BlockSpec((SEQ, 2 * HEAD_DIM), lambda b, h, i: (b, h)),
            pl.BlockSpec((CTX_LEN, HEAD_DIM), lambda b, h, i: (ctx_blk + b, h)),
            pl.BlockSpec((CTX_LEN, 2 * HEAD_DIM), lambda b, h, i: (ctx_blk + b, h)),
            _ANY,
        ],
        out_specs=pl.BlockSpec((None, tq, gw), lambda b, h, i: (SLOT_ATTN, b * nq + i, h)),
        out_shape=jax.ShapeDtypeStruct(bb.shape, bb.dtype),
        input_output_aliases={5: 0},
        compiler_params=_cparams("arbitrary", "arbitrary", "arbitrary"),
        name="attn_lat",
    )(q, k, v, k, v, bb)


def _attn_ctx(q, k, v, bb):
    ctx_blk = N_LAT // CTX_LEN
    gw = Q_GROUP * HEAD_DIM
    return pl.pallas_call(
        _keep_branch_buffer(functools.partial(_attn_body, with_lat=False), 3),
        grid=(BATCH, N_KV_HEADS),
        in_specs=[
            pl.BlockSpec((CTX_LEN, gw), lambda b, h: (ctx_blk + b, h)),
            pl.BlockSpec((CTX_LEN, HEAD_DIM), lambda b, h: (ctx_blk + b, h)),
            pl.BlockSpec((CTX_LEN, 2 * HEAD_DIM), lambda b, h: (ctx_blk + b, h)),
            _ANY,
        ],
        out_specs=pl.BlockSpec((None, CTX_LEN, gw), lambda b, h: (SLOT_ATTN, ctx_blk + b, h)),
        out_shape=jax.ShapeDtypeStruct(bb.shape, bb.dtype),
        input_output_aliases={3: 0},
        compiler_params=_cparams("arbitrary", "arbitrary"),
        name="attn_ctx",
    )(q, k, v, bb)


POOL_PAD = 8


def _pool_body(a_ref, w_ref, s_ref, o_ref, *, seq):
    lp = seq + 2 * POOL_PAD
    t = lax.broadcasted_iota(jnp.int32, (seq, LANE), 0)
    zpad = jnp.zeros((POOL_PAD, LANE), F32)
    for gi, win in enumerate(POOL_WINDOWS):
        sl = slice(gi * LANE, (gi + 1) * LANE)
        a = a_ref[:, sl]
        s = jnp.concatenate([zpad, a, zpad], axis=0)
        s = s + pltpu.roll(s, 1, 0)
        half = 1
        while 2 * half < win:
            s = pltpu.roll(s, half, 0) + pltpu.roll(s, lp - half, 0)
            half *= 2
        s = s[POOL_PAD:POOL_PAD + seq]
        lo = jnp.maximum(t - win // 2, 0)
        hi = jnp.minimum(t + win // 2, seq)
        pooled = s / (hi - lo).astype(F32) - a
        y = _dot(pooled.astype(BF16), w_ref[gi].astype(BF16))
        o_ref[:, sl] = (y * s_ref[:, sl]).astype(BF16)


def _pool(proj, pool_w, pool_scale, bb, seq, row_blk0):
    width = len(POOL_WINDOWS) * LANE
    return pl.pallas_call(
        _keep_branch_buffer(functools.partial(_pool_body, seq=seq), 3),
        grid=(BATCH,),
        in_specs=[
            pl.BlockSpec((seq, width), lambda b: (row_blk0 + b, 0)),
            pl.BlockSpec((len(POOL_WINDOWS), LANE, LANE), lambda b: (0, 0, 0)),
            pl.BlockSpec((1, width), lambda b: (0, 0)),
            _ANY,
        ],
        out_specs=pl.BlockSpec((None, seq, width), lambda b: (SLOT_POOL, row_blk0 + b, 0)),
        out_shape=jax.ShapeDtypeStruct(bb.shape, bb.dtype),
        input_output_aliases={3: 0},
        compiler_params=_cparams("arbitrary"),
        name="pool",
    )(proj, pool_w, pool_scale.reshape(1, width), bb)


def _hy_prep_body(x0_ref, x1_ref, v_ref, w0_ref, w1_ref, wv_ref, b0_ref, b1_ref, bv_ref,
                  x0o_ref, vx_ref, vb_ref, *, seq):
    t = lax.broadcasted_iota(jnp.int32, x0_ref.shape, 0)

    def conv(x_ref, w_ref, b_ref):
        x = x_ref[...]
        prev = jnp.where(t >= 1, pltpu.roll(x, 1, 0), 0.0)
        nxt = jnp.where(t <= seq - 2, pltpu.roll(x, seq - 1, 0), 0.0)
        return prev * w_ref[0:1, :] + x * w_ref[1:2, :] + nxt * w_ref[2:3, :] + b_ref[...]

    x0o_ref[...] = conv(x0_ref, w0_ref, b0_ref)
    vx = conv(v_ref, wv_ref, bv_ref) * conv(x1_ref, w1_ref, b1_ref)
    vx_ref[...] = vx
    vb_ref[...] = vx.astype(BF16)


def _hy_prep(proj, short_w, short_b, seq, row_blk0):
    tc = HYENA_WIDTH
    nc = HYENA_WIDTH // tc
    c0 = OFF_HY // tc
    short_b = short_b.reshape(1, 3 * HYENA_WIDTH)

    def xspec(part):
        return pl.BlockSpec((seq, tc), lambda b, c: (row_blk0 + b, c0 + part * nc + c))

    def wspec(part, rows):
        return pl.BlockSpec((rows, tc), lambda b, c: (0, part * nc + c))

    return pl.pallas_call(
        functools.partial(_hy_prep_body, seq=seq),
        grid=(BATCH, nc),
        in_specs=[xspec(0), xspec(1), xspec(2), wspec(0, 3), wspec(1, 3), wspec(2, 3),
                  wspec(0, 1), wspec(1, 1), wspec(2, 1)],
        out_specs=[
            pl.BlockSpec((seq, tc), lambda b, c: (b, c)),
            pl.BlockSpec((seq, tc), lambda b, c: (b, c)),
            pl.BlockSpec((seq, tc), lambda b, c: (0, b * nc + c)),
        ],
        out_shape=[
            jax.ShapeDtypeStruct((BATCH * seq, HYENA_WIDTH), F32),
            jax.ShapeDtypeStruct((BATCH * seq, HYENA_WIDTH), F32),
            jax.ShapeDtypeStruct((seq, BATCH * HYENA_WIDTH), BF16),
        ],
        compiler_params=_cparams("arbitrary", "arbitrary"),
        name="hy_prep",
    )(proj, proj, proj, short_w, short_w, short_w, short_b, short_b, short_b)


def _hy_feats(seq):
    t = jnp.linspace(0.0, 1.0, seq, dtype=F32)[:, None]
    f = jnp.linspace(1e-4, HYENA_BANDS - 1, HYENA_BANDS, dtype=F32)
    w = 2.0 * math.pi * jnp.arange(seq, dtype=F32) / seq
    fw = w[:, None] * f[None, :]
    z = jnp.concatenate([t, jnp.cos(fw), -jnp.sin(fw)], axis=-1)
    return jnp.pad(z, ((0, 0), (0, LANE - HYENA_EMB)))


def _hy_deltas():
    max_decay = math.log(HYENA_TARGET) / HYENA_FAST_PCT
    min_decay = math.log(HYENA_TARGET) / HYENA_SLOW_PCT
    return jnp.abs(jnp.linspace(min_decay, max_decay, HYENA_WIDTH, dtype=F32)).reshape(1, HYENA_WIDTH)


def _hy_filter_body(z_ref, w1_ref, b1_ref, w2_ref, b2_ref, w3_ref, fr_ref, dl_ref, k_ref, nyq_ref, *, seq):
    hp = lax.Precision.HIGHEST
    freq = fr_ref[...]
    h = jnp.sin(freq * (jnp.dot(z_ref[...], w1_ref[...], precision=hp, preferred_element_type=F32) + b1_ref[...]))
    h = jnp.sin(freq * (jnp.dot(h, w2_ref[...], precision=hp, preferred_element_type=F32) + b2_ref[...]))
    h = jnp.dot(h, w3_ref[...], precision=hp, preferred_element_type=F32)
    ti = lax.broadcasted_iota(jnp.int32, (seq, HYENA_WIDTH), 0)
    decay = jnp.exp(-(ti.astype(F32) * (1.0 / (seq - 1))) * dl_ref[...])
    hf = h[:, :HYENA_WIDTH] * decay
    hb = jnp.where(ti == 0, 0.0, h[:, HYENA_WIDTH:] * decay)
    ks = hf + hb
    k_ref[:, :HYENA_WIDTH] = ks.astype(BF16)
    k_ref[:, HYENA_WIDTH:] = (hf - hb).astype(BF16)
    nyq = jnp.sum(jnp.where(ti % 2 == 0, ks, -ks), axis=0, keepdims=True)
    nyq_ref[...] = jnp.broadcast_to(nyq, nyq_ref.shape)


def _hy_filter(lp, z, deltas, seq):
    w1 = jnp.pad(lp['hy_f1_w'], ((0, LANE - HYENA_EMB), (0, 0)))
    args = (z, w1, lp['hy_f1_b'].reshape(1, -1), lp['hy_f2_w'], lp['hy_f2_b'].reshape(1, -1), lp['hy_f3_w'],
            lp['hy_freq'].reshape(1, -1), deltas)
    full = lambda a: pl.BlockSpec(a.shape, lambda i: (0,) * a.ndim)
    return pl.pallas_call(
        functools.partial(_hy_filter_body, seq=seq),
        grid=(1,),
        in_specs=[full(a) for a in args],
        out_specs=[pl.BlockSpec((seq, 2 * HYENA_WIDTH), lambda i: (0, 0)),
                   pl.BlockSpec((8, HYENA_WIDTH), lambda i: (0, 0))],
        out_shape=[jax.ShapeDtypeStruct((seq, 2 * HYENA_WIDTH), BF16),
                   jax.ShapeDtypeStruct((8, HYENA_WIDTH), F32)],
        compiler_params=_cparams("arbitrary"),
        name="hy_filter",
    )(*args)


DFT_SPLIT = 64


def _dft_matrices(seq):
    n = 2 * seq
    f = jnp.arange(seq, dtype=jnp.int32)[:, None]

    def table(step, count):
        idx = (f * (jnp.arange(count, dtype=jnp.int32)[None, :] * step)) % n
        ang = idx.astype(F32) * (2.0 * math.pi / n)
        return jnp.cos(ang), jnp.sin(ang)

    hc, hs = table(DFT_SPLIT, seq // DFT_SPLIT)
    lc, ls = table(1, DFT_SPLIT)
    cos = (hc[:, :, None] * lc[:, None, :] - hs[:, :, None] * ls[:, None, :]).reshape(seq, seq)
    sin = (hs[:, :, None] * lc[:, None, :] + hc[:, :, None] * ls[:, None, :]).reshape(seq, seq)
    s = jnp.arange(seq, dtype=jnp.int32)[None, :]
    nyq = jnp.where(s % 2 == 0, 1.0, -1.0).astype(F32)
    wf = jnp.concatenate([cos, jnp.where(f == 0, nyq, -sin)], axis=0).astype(BF16)
    return wf, wf.T


def _mm_body(a_ref, b_ref, o_ref):
    o_ref[...] = _dot(a_ref[...], b_ref[...])


def _dft_fwd_body(wc_ref, ws_ref, x_ref, hr_ref, hi_ref, nyq_ref, pr_ref, pi_ref, *, n):
    tf = wc_ref.shape[0]
    freq = lax.broadcasted_iota(jnp.int32, (tf, MXU_COLS), 0) + pl.program_id(0) * tf
    first = freq == 0
    w = jnp.where(first, 1.0 / n, 2.0 / n)
    chunks = [slice(c, c + MXU_COLS) for c in range(0, x_ref.shape[1], MXU_COLS)]
    spectra = [(_dot(wc_ref[...], x_ref[:, sl]), _dot(ws_ref[...], x_ref[:, sl])) for sl in chunks]
    for sl, (xr, xi) in zip(chunks, spectra):
        hr = hr_ref[:, sl]
        hi = hi_ref[:, sl]
        pr_ref[:, sl] = (jnp.where(first, xr * hr, xr * hr - xi * hi) * w).astype(BF16)
        pi_ref[:, sl] = (jnp.where(first, xi * nyq_ref[0:1, sl], xr * hi + xi * hr) * w).astype(BF16)


def _dft_fwd(wf, x, hf, nyq):
    n, seq = wf.shape
    tf = min(seq, 512)
    nf = seq // tf
    out = jax.ShapeDtypeStruct((seq, BATCH * HYENA_WIDTH), BF16)
    return pl.pallas_call(
        functools.partial(_dft_fwd_body, n=n),
        grid=(nf, BATCH),
        in_specs=[
            pl.BlockSpec((tf, seq), lambda f, b: (f, 0)),
            pl.BlockSpec((tf, seq), lambda f, b: (nf + f, 0)),
            pl.BlockSpec((seq, HYENA_WIDTH), lambda f, b: (0, b)),
            pl.BlockSpec((tf, HYENA_WIDTH), lambda f, b: (f, 0)),
            pl.BlockSpec((tf, HYENA_WIDTH), lambda f, b: (nf + f, 0)),
            pl.BlockSpec((8, HYENA_WIDTH), lambda f, b: (0, 0)),
        ],
        out_specs=[pl.BlockSpec((tf, HYENA_WIDTH), lambda f, b: (f, b))] * 2,
        out_shape=[out, out],
        compiler_params=_cparams("arbitrary", "arbitrary"),
        name="dft_fwd",
    )(wf, wf, x, hf, hf, nyq)


def _dft_filter(wf, k):
    n, seq = wf.shape
    tm = min(seq, 1024)
    return pl.pallas_call(
        _mm_body,
        grid=(n // tm,),
        in_specs=[
            pl.BlockSpec((tm, seq), lambda i: (i, 0)),
            pl.BlockSpec((seq, HYENA_WIDTH), lambda i: (0, (i * tm) // seq)),
        ],
        out_specs=pl.BlockSpec((tm, HYENA_WIDTH), lambda i: (i, 0)),
        out_shape=jax.ShapeDtypeStruct((n, HYENA_WIDTH), F32),
        compiler_params=_cparams("arbitrary"),
        name="dft_filter",
    )(wf, k)


def _dft_inv_body(wc_ref, ws_ref, pr_ref, pi_ref, vx_ref, bias_ref, x0_ref, o_ref):
    y = _dot(wc_ref[...], pr_ref[...]) + _dot(ws_ref[...], pi_ref[...])
    o_ref[...] = ((y + vx_ref[...] * bias_ref[...]) * x0_ref[...]).astype(BF16)


def _dft_inv(wi, p_re, p_im, vx, bias, x0, bb, row0):
    seq, n = wi.shape
    tm = min(seq, 1024)
    tn = HYENA_WIDTH
    nt = seq // tm
    return pl.pallas_call(
        _keep_branch_buffer(_dft_inv_body, 7),
        grid=(nt, BATCH),
        in_specs=[
            pl.BlockSpec((tm, seq), lambda i, b: (i, 0)),
            pl.BlockSpec((tm, seq), lambda i, b: (i, 1)),
            pl.BlockSpec((seq, tn), lambda i, b: (0, b)),
            pl.BlockSpec((seq, tn), lambda i, b: (0, b)),
            pl.BlockSpec((tm, tn), lambda i, b: (b * nt + i, 0)),
            pl.BlockSpec((1, tn), lambda i, b: (0, 0)),
            pl.BlockSpec((tm, tn), lambda i, b: (b * nt + i, 0)),
            _ANY,
        ],
        out_specs=pl.BlockSpec((None, tm, tn), lambda i, b: (SLOT_HYENA, row0 // tm + b * nt + i, 0)),
        out_shape=jax.ShapeDtypeStruct(bb.shape, bb.dtype),
        input_output_aliases={7: 0},
        compiler_params=_cparams("arbitrary", "arbitrary"),
        name="dft_inv",
    )(wi, wi, p_re, p_im, vx, bias.reshape(1, tn), x0, bb)


def _hyena(proj, lp, consts, bb, seq, row0):
    z, deltas, (wf, wi) = consts
    x0, vx, vb = _hy_prep(proj, lp['hy_short_w'], lp['hy_short_b'], seq, row0 // seq)
    k, nyq = _hy_filter(lp, z, deltas, seq)
    hf = _dft_filter(wf, k)
    p_re, p_im = _dft_fwd(wf, vb, hf, nyq)
    return _dft_inv(wi, p_re, p_im, vx, lp['hy_bias'], x0, bb, row0)


def _s5_params(lp, n_seg_steps):
    a_re, a_im = lp['s5_a_re'], lp['s5_a_im']
    dt = jnp.exp(lp['s5_log_dt'])[..., None]
    mag = jnp.exp(a_re * dt)
    ab_re, ab_im = mag * jnp.cos(a_im * dt), mag * jnp.sin(a_im * dt)
    den = a_re * a_re + a_im * a_im
    nr, ni = ab_re - 1.0, ab_im
    cf_re = (nr * a_re + ni * a_im) / den
    cf_im = (ni * a_re - nr * a_im) / den
    b_re, b_im = lp['s5_b_re'], lp['s5_b_im']
    bb_re = cf_re[..., None] * b_re - cf_im[..., None] * b_im
    bb_im = cf_re[..., None] * b_im + cf_im[..., None] * b_re
    eye = jnp.eye(S5_CHUNK_GROUPS, dtype=F32)

    def bdiag_in(m):
        m = m.reshape(2, S5_NCHUNK, S5_CHUNK_GROUPS, S5_STATE, S5_GC)
        return jnp.einsum('dqgpc,gh->dqgchp', m, eye).reshape(2, S5_NCHUNK, S5_CHUNK_CH, S5_CHUNK_ST)

    def bdiag_out(m):
        m = m.reshape(2, S5_NCHUNK, S5_CHUNK_GROUPS, S5_GC, S5_STATE)
        return jnp.einsum('dqgcp,gh->dqhpgc', m, eye).reshape(2, S5_NCHUNK, S5_CHUNK_ST, S5_CHUNK_CH)

    bbd = jnp.concatenate([bdiag_in(bb_re), bdiag_in(bb_im)], axis=-1).astype(BF16)
    cbd = jnp.concatenate([bdiag_out(lp['s5_c_re']), -bdiag_out(lp['s5_c_im'])], axis=-2).astype(BF16)
    a = jnp.stack([ab_re.reshape(2, -1), ab_im.reshape(2, -1)], axis=1)
    aks = []
    for steps in n_seg_steps:
        pr, pi = ab_re, ab_im
        for _ in range(int(math.log2(steps))):
            pr, pi = pr * pr - pi * pi, 2.0 * pr * pi
        aks.append(jnp.stack([pr.reshape(2, -1), pi.reshape(2, -1)], axis=1))
    return [(bbd, cbd, a, ak) for ak in aks]


S5_BLOCK = 512


def _s5_scan_body(u_ref, bbd_ref, cbd_ref, a_ref, ak_ref, dsk_ref, h0_ref, y_ref, hl_ref,
                  xr0_ref, xr1_ref, xi0_ref, xi1_ref, yd0_ref, yd1_ref, up_ref, ub_ref, *, seq):
    nk = seq // S5_LANES
    blk = min(seq, S5_BLOCK)
    nblk = seq // blk
    tiles = blk // S5_LANES
    shape = (S5_LANES, S5_CHUNK_ST)
    xr_ref, xi_ref, yd_ref = (xr0_ref, xr1_ref), (xi0_ref, xi1_ref), (yd0_ref, yd1_ref)
    for j in range(S5_LANES):
        up_ref[pl.ds(j, nk, stride=S5_LANES), :] = u_ref[pl.ds(j * nk, nk), :]
    ub_ref[...] = up_ref[...].astype(BF16)
    row = lax.broadcasted_iota(jnp.int32, shape, 0)
    zero = jnp.zeros(shape, F32)

    def block(r):
        return pl.ds(pl.multiple_of(r * blk, blk), blk)

    def in_proj(d, rows):
        xr_ref[d][rows, :] = _dot(ub_ref[rows, :], bbd_ref[d, :, :S5_CHUNK_ST])
        xi_ref[d][rows, :] = _dot(ub_ref[rows, :], bbd_ref[d, :, S5_CHUNK_ST:])

    def out_proj(d, rows):
        yd_ref[d][rows, :] = (_dot(xr_ref[d][rows, :].astype(BF16), cbd_ref[d, :S5_CHUNK_ST, :])
                              + _dot(xi_ref[d][rows, :].astype(BF16), cbd_ref[d, S5_CHUNK_ST:, :]))

    def coeffs(d):
        return jnp.broadcast_to(a_ref[d, 0:1, :], shape), jnp.broadcast_to(a_ref[d, 1:2, :], shape)

    def block_tiles(d, r):
        base = pl.multiple_of(r * blk, blk)
        order = range(tiles) if d == 0 else range(tiles - 1, -1, -1)
        return [pl.ds(base + t * S5_LANES, S5_LANES) for t in order]

    def scan_block(d, r, carry):
        ar, ai = coeffs(d)
        xr, xi = carry
        for rows in block_tiles(d, r):
            xr, xi = (ar * xr - ai * xi + xr_ref[d][rows, :], ar * xi + ai * xr + xi_ref[d][rows, :])
            xr_ref[d][rows, :] = xr
            xi_ref[d][rows, :] = xi
        return xr, xi

    def fix_block(d, r, carry):
        ar, ai = coeffs(d)
        gr, gi = carry
        for rows in block_tiles(d, r):
            gr, gi = ar * gr - ai * gi, ar * gi + ai * gr
            xr_ref[d][rows, :] += gr
            xi_ref[d][rows, :] += gi
        return gr, gi

    def visit(d, s):
        return s if d == 0 else nblk - 1 - s

    def entering_states(d, er, ei):
        akr, aki = ak_ref[d, 0:1, :], ak_ref[d, 1:2, :]
        hr, hi = h0_ref[2 * d:2 * d + 1, :], h0_ref[2 * d + 1:2 * d + 2, :]
        in_r, in_i = zero, zero
        for j in (range(S5_LANES) if d == 0 else range(S5_LANES - 1, -1, -1)):
            in_r = jnp.where(row == j, hr, in_r)
            in_i = jnp.where(row == j, hi, in_i)
            hr, hi = (akr * hr - aki * hi + er[j:j + 1, :], akr * hi + aki * hr + ei[j:j + 1, :])
        hl_ref[2 * d:2 * d + 1, :] = hr
        hl_ref[2 * d + 1:2 * d + 2, :] = hi
        return in_r, in_i

    def fix_all(d, ins):
        lax.fori_loop(0, nblk, lambda s, carry: fix_block(d, visit(d, s), carry), ins)

    in_proj(0, slice(None))

    def scan0_body(s, carry):
        in_proj(1, block(s))
        return scan_block(0, s, carry)
    ends0 = lax.fori_loop(0, nblk, scan0_body, (zero, zero))
    fix_all(0, entering_states(0, *ends0))

    def scan1_body(s, carry):
        out_proj(0, block(s))
        return scan_block(1, visit(1, s), carry)
    ends1 = lax.fori_loop(0, nblk, scan1_body, (zero, zero))
    fix_all(1, entering_states(1, *ends1))
    out_proj(1, slice(None))

    up_ref[...] = yd_ref[0][...] + yd_ref[1][...] + up_ref[...] * (dsk_ref[0] + dsk_ref[1])
    for j in range(S5_LANES):
        y_ref[pl.ds(j * nk, nk), :] = up_ref[pl.ds(j, nk, stride=S5_LANES), :]


def _s5_scan(proj, params, dskip, h0, seq, row_blk0):
    bbd, cbd, a, ak = params
    nq = S5_NCHUNK
    col0 = OFF_S5 // S5_CHUNK_CH
    return pl.pallas_call(
        functools.partial(_s5_scan_body, seq=seq),
        grid=(BATCH, nq),
        in_specs=[
            pl.BlockSpec((seq, S5_CHUNK_CH), lambda b, q: (row_blk0 + b, col0 + q)),
            pl.BlockSpec((2, None, S5_CHUNK_CH, 2 * S5_CHUNK_ST), lambda b, q: (0, q, 0, 0)),
            pl.BlockSpec((2, None, 2 * S5_CHUNK_ST, S5_CHUNK_CH), lambda b, q: (0, q, 0, 0)),
            pl.BlockSpec((2, 2, S5_CHUNK_ST), lambda b, q: (0, 0, q)),
            pl.BlockSpec((2, 2, S5_CHUNK_ST), lambda b, q: (0, 0, q)),
            pl.BlockSpec((2, 1, S5_CHUNK_CH), lambda b, q: (0, 0, q)),
            pl.BlockSpec((None, 4, S5_CHUNK_ST), lambda b, q: (b, 0, q)),
        ],
        out_specs=[
            pl.BlockSpec((seq, S5_CHUNK_CH), lambda b, q: (b, q)),
            pl.BlockSpec((None, 4, S5_CHUNK_ST), lambda b, q: (b, 0, q)),
        ],
        out_shape=[
            jax.ShapeDtypeStruct((BATCH * seq, S5_WIDTH), F32),
            jax.ShapeDtypeStruct((BATCH, 4, S5_GROUPS * S5_STATE), F32),
        ],
        scratch_shapes=[pltpu.VMEM((seq, S5_CHUNK_ST), F32)] * 4 + [pltpu.VMEM((seq, S5_CHUNK_CH), F32)] * 3
        + [pltpu.VMEM((seq, S5_CHUNK_CH), BF16)],
        compiler_params=_cparams("arbitrary", "arbitrary"),
        name="s5_scan",
    )(proj, bbd, cbd, a, ak, dskip.reshape(2, 1, S5_WIDTH), h0)


def _s5_glu_body(y_ref, w_ref, b_ref, o_ref):
    w = w_ref[...].astype(BF16)
    half = y_ref.shape[0] // 2
    halves = [slice(0, half), slice(half, 2 * half)]
    gs = [_dot(jax.nn.gelu(y_ref[rows, :]).astype(BF16), w) + b_ref[...] for rows in halves]
    for rows, g in zip(halves, gs):
        o_ref[rows, :] = (g[:, :S5_WIDTH] * jax.nn.sigmoid(g[:, S5_WIDTH:])).astype(BF16)


def _s5_glu(y, w, b, bb, row0):
    n_rows = y.shape[0]
    tm = TM_EW
    return pl.pallas_call(
        _keep_branch_buffer(_s5_glu_body, 3),
        grid=(n_rows // tm,),
        in_specs=[
            pl.BlockSpec((tm, S5_WIDTH), lambda i: (i, 0)),
            pl.BlockSpec((S5_WIDTH, 2 * S5_WIDTH), lambda i: (0, 0)),
            pl.BlockSpec((1, 2 * S5_WIDTH), lambda i: (0, 0)),
            _ANY,
        ],
        out_specs=pl.BlockSpec((None, tm, S5_WIDTH), lambda i: (SLOT_S5, row0 // tm + i, 0)),
        out_shape=jax.ShapeDtypeStruct(bb.shape, bb.dtype),
        input_output_aliases={3: 0},
        compiler_params=_cparams("arbitrary"),
        name="s5_glu",
    )(y, w, b.reshape(1, -1), bb)


def _merge_body(u_ref, *refs):
    wg_refs, bg_refs, y_refs, wb_refs = (refs[k * N_BRANCH:(k + 1) * N_BRANCH] for k in range(4))
    o_ref = refs[4 * N_BRANCH]
    u = u_ref[...]
    acc = None
    for n in range(N_BRANCH):
        gate = jax.nn.sigmoid(_dot(u, wg_refs[n][...].astype(BF16)) + bg_refs[n][...])
        contrib = gate * _dot(y_refs[n][...], wb_refs[n][...].astype(BF16))
        acc = contrib if acc is None else acc + contrib
    o_ref[...] = acc.astype(BF16)


def _merge(u, w_gate, b_gate, branches, w_branch, l, n_rows):
    tc = MXU_COLS
    ncol = D_MODEL // tc
    b_gate = b_gate.reshape(1, -1)
    per_branch = lambda make: [make(n) for n in range(N_BRANCH)]
    return pl.pallas_call(
        _merge_body,
        grid=(n_rows // TM, ncol),
        in_specs=[pl.BlockSpec((TM, D_MODEL), lambda i, c: (i, 0))]
        + per_branch(lambda n: pl.BlockSpec((None, D_MODEL, tc), lambda i, c: (l, 0, n * ncol + c)))
        + per_branch(lambda n: pl.BlockSpec((1, tc), lambda i, c: (0, n * ncol + c)))
        + per_branch(lambda n: pl.BlockSpec((None, TM, BRANCH_WIDTH), lambda i, c: (n, i, 0)))
        + per_branch(lambda n: pl.BlockSpec((None, None, BRANCH_WIDTH, tc), lambda i, c: (l, n, 0, c))),
        out_specs=pl.BlockSpec((TM, tc), lambda i, c: (i, c)),
        out_shape=jax.ShapeDtypeStruct((n_rows, D_MODEL), BF16),
        compiler_params=_cparams("arbitrary", "arbitrary"),
        name="merge",
    )(u, *([w_gate] * N_BRANCH), *([b_gate] * N_BRANCH), *([branches] * N_BRANCH), *([w_branch] * N_BRANCH))


def _out_body(m_ref, w_ref, x_ref, g_ref, o_ref):
    o_ref[...] = x_ref[...] + g_ref[...] * _dot(m_ref[...], w_ref[...].astype(BF16))


def _out_proj(merged, w_out, l, x, mods, n_rows):
    tn = 1024
    gate = pl.BlockSpec((None, None, 1, tn), lambda j, i: (_mod_row(i, TM), 5, 0, j))
    return pl.pallas_call(
        _out_body,
        grid=(D_MODEL // tn, n_rows // TM),
        in_specs=[
            pl.BlockSpec((TM, D_MODEL), lambda j, i: (i, 0)),
            pl.BlockSpec((None, D_MODEL, tn), lambda j, i: (l, 0, j)),
            pl.BlockSpec((TM, tn), lambda j, i: (i, j)),
            gate,
        ],
        out_specs=pl.BlockSpec((TM, tn), lambda j, i: (i, j)),
        out_shape=jax.ShapeDtypeStruct((n_rows, D_MODEL), F32),
        compiler_params=_cparams("arbitrary", "arbitrary"),
        name="out_proj",
    )(merged, w_out, x, mods)


def _mixer(x, g, lp, big, w_in_bf16, l, mods, consts, bb, last):
    cos, sin, hy_lat, hy_ctx = consts
    proj, u = _proj(x, g, mods, w_in_bf16)
    q, k, v = _qkv(proj, cos, sin, lp['q_norm'], lp['k_norm'])

    ctx_blk = N_LAT // CTX_LEN
    zero_h = jnp.zeros((BATCH, 4, S5_GROUPS * S5_STATE), F32)
    par_ctx, par_lat = _s5_params(lp, (CTX_LEN // S5_LANES, SEQ // S5_LANES))
    ys_ctx, h_ctx = _s5_scan(proj, par_ctx, lp['s5_d'], zero_h, CTX_LEN, ctx_blk)
    ys_lat, _ = _s5_scan(proj, par_lat, lp['s5_d'], h_ctx, SEQ, 0)

    bb = _s5_glu(ys_lat, lp['s5_glu_w'], lp['s5_glu_b'], bb, 0)
    bb = _attn_lat(q, k, v, bb)
    bb = _pool(proj, lp['pool_w'], lp['pool_scale'], bb, SEQ, 0)
    bb = _hyena(proj, lp, hy_lat, bb, SEQ, 0)
    if not last:
        bb = _s5_glu(ys_ctx, lp['s5_glu_w'], lp['s5_glu_b'], bb, N_LAT)
        bb = _attn_ctx(q, k, v, bb)
        bb = _pool(proj, lp['pool_w'], lp['pool_scale'], bb, CTX_LEN, ctx_blk)
        bb = _hyena(proj, lp, hy_ctx, bb, CTX_LEN, N_LAT)

    n_rows = N_LAT if last else MT
    merged = _merge(u, big['w_gate'], lp['b_gate'], bb, big['w_branch'], l, n_rows)
    return _out_proj(merged, big['w_out'], l, x, mods, n_rows), bb


def kernel(x, c, ctx, c_ctx, w_ada, b_ada, norm_ffn1, norm_mix, norm_ffn2, norm_final, ffn1_wi, ffn1_wo, ffn2_wi, ffn2_wo, w_in, w_gate, b_gate, w_branch, w_out, pool_w, pool_scale, q_norm, k_norm, hy_short_w, hy_short_b, hy_f1_w, hy_f1_b, hy_f2_w, hy_f2_b, hy_f3_w, hy_freq, hy_bias, s5_a_re, s5_a_im, s5_log_dt, s5_b_re, s5_b_im, s5_c_re, s5_c_im, s5_d, s5_glu_w, s5_glu_b):
    big = dict(w_gate=w_gate, w_branch=w_branch, w_out=w_out)
    per_layer = dict(
        b_gate=b_gate, pool_w=pool_w,
        pool_scale=pool_scale, q_norm=q_norm, k_norm=k_norm, hy_short_w=hy_short_w, hy_short_b=hy_short_b,
        hy_f1_w=hy_f1_w, hy_f1_b=hy_f1_b, hy_f2_w=hy_f2_w, hy_f2_b=hy_f2_b, hy_f3_w=hy_f3_w, hy_freq=hy_freq,
        hy_bias=hy_bias, s5_a_re=s5_a_re, s5_a_im=s5_a_im, s5_log_dt=s5_log_dt, s5_b_re=s5_b_re, s5_b_im=s5_b_im,
        s5_c_re=s5_c_re, s5_c_im=s5_c_im, s5_d=s5_d, s5_glu_w=s5_glu_w, s5_glu_b=s5_glu_b)

    cos, sin = _rope_tables()
    deltas = _hy_deltas()
    consts = (cos, sin,
              (_hy_feats(SEQ), deltas, _dft_matrices(SEQ)),
              (_hy_feats(CTX_LEN), deltas, _dft_matrices(CTX_LEN)))

    cc = jnp.concatenate([c, c_ctx[None], jnp.zeros((8 - BATCH - 1, D_MODEL), F32)], axis=0)
    mods_all = _ada(cc, w_ada, b_ada).reshape(DEPTH, 8, N_MOD, 1, D_MODEL)

    xs = jnp.concatenate([x.reshape(N_LAT, D_MODEL), ctx.reshape(N_CTX, D_MODEL)], axis=0)
    bb = jnp.zeros((N_BRANCH, MT, BRANCH_WIDTH), BF16)
    for l in range(DEPTH):
        last = l == DEPTH - 1
        lp = {name: w[l] for name, w in per_layer.items()}
        mods = mods_all[l]
        xs, w_in_bf16 = _ffn(xs, mods, norm_ffn1[l], ffn1_wi, ffn1_wo, l, 0, MT, side=w_in)
        xs, bb = _mixer(xs, norm_mix[l], lp, big, w_in_bf16, l, mods, consts, bb, last)
        xs = _ffn(xs, mods, norm_ffn2[l], ffn2_wi, ffn2_wo, l, 6, N_LAT if last else MT)
    return _final_norm(xs, norm_final, N_LAT).reshape(BATCH, SEQ, D_MODEL)
```

```python
import functools
import math

import jax
import jax.numpy as jnp
from jax import lax
from jax.experimental import pallas as pl
from jax.experimental.pallas import tpu as pltpu

F32 = jnp.float32
BF16 = jnp.bfloat16

D_MODEL = 2048
BATCH = 4
SEQ = 2048
DEPTH = 4
GRID_W = 64
CTX_LEN = 256
D_FF = 5632
N_MOD = 9
EPS = 1e-6

POOL_WINDOWS = (2, 4, 8, 16)
HEAD_DIM = 128
N_Q_HEADS = 4
N_KV_HEADS = 2
Q_GROUP = N_Q_HEADS // N_KV_HEADS
ROPE_THETA = 10000.0
ROPE_FREQS = HEAD_DIM // 4

HYENA_WIDTH = 512
HYENA_EMB = 33
HYENA_BANDS = (HYENA_EMB - 1) // 2
HYENA_HIDDEN = 64
HYENA_TARGET = 1e-2
HYENA_FAST_PCT = 0.3
HYENA_SLOW_PCT = 1.5

S5_WIDTH = 512
S5_GC = 16
S5_GROUPS = S5_WIDTH // S5_GC
S5_STATE = 64
S5_LANES = 8
S5_CHUNK_GROUPS = 8
S5_CHUNK_CH = S5_CHUNK_GROUPS * S5_GC
S5_CHUNK_ST = S5_CHUNK_GROUPS * S5_STATE
S5_NCHUNK = S5_GROUPS // S5_CHUNK_GROUPS

N_BRANCH = 4
BRANCH_WIDTH = 512
LANE = 128
COL = 512

OFF_Q = 512
OFF_KV = 1024
OFF_HY = 1536
OFF_S5 = 3072
IN_WIDTH = 3584

N_LAT = BATCH * SEQ
N_CTX = BATCH * CTX_LEN
MT = N_LAT + N_CTX

TM = 1024
TM_EW = 512
VMEM_LIMIT = 56 * 1024 * 1024


def _cparams(*sem):
    return pltpu.CompilerParams(dimension_semantics=sem, vmem_limit_bytes=VMEM_LIMIT)


def _dot(a, b):
    return jnp.dot(a, b, preferred_element_type=F32)


SLOT_POOL, SLOT_ATTN, SLOT_HYENA, SLOT_S5 = range(4)
_ANY = pl.BlockSpec(memory_space=pl.ANY)


def _keep_branch_buffer(body, n_in):
    def wrapped(*refs):
        return body(*refs[:n_in], *refs[n_in + 1:])
    return wrapped


def _mod_row(i, tm):
    return jnp.minimum((i * tm) // SEQ, BATCH)


def _mod_spec(tm, k):
    return pl.BlockSpec((None, None, 1, D_MODEL), lambda i, *_: (_mod_row(i, tm), k, 0, 0))


def _ada_body(c_ref, w_ref, b_ref, o_ref):
    c = c_ref[...]
    a = (c * jax.nn.sigmoid(c)).astype(BF16)
    o_ref[...] = _dot(a, w_ref[...].astype(BF16)) + b_ref[...]


def _ada(cc, w_ada, b_ada):
    tn = 1024
    nw = N_MOD * D_MODEL
    return pl.pallas_call(
        _ada_body,
        grid=(DEPTH, nw // tn),
        in_specs=[
            pl.BlockSpec((8, D_MODEL), lambda l, j: (0, 0)),
            pl.BlockSpec((None, D_MODEL, tn), lambda l, j: (l, 0, j)),
            pl.BlockSpec((None, 1, tn), lambda l, j: (l, 0, j)),
        ],
        out_specs=pl.BlockSpec((None, 8, tn), lambda l, j: (l, 0, j)),
        out_shape=jax.ShapeDtypeStruct((DEPTH, 8, nw), F32),
        compiler_params=_cparams("arbitrary", "arbitrary"),
        name="ada",
    )(cc, w_ada, b_ada.reshape(DEPTH, 1, nw))


NORM_ROWS = 16


def _rows_loop(n_rows, step):
    for c in range(n_rows // NORM_ROWS):
        step(pl.ds(c * NORM_ROWS, NORM_ROWS))


def _rms_scale_rows(x_ref, r_ref, scale, shift, o_ref):
    def stats(rows):
        x = x_ref[rows, :]
        r_ref[rows, :] = lax.rsqrt(jnp.mean(x * x, axis=-1, keepdims=True) + EPS)

    def apply(rows):
        y = x_ref[rows, :] * r_ref[rows, :] * scale
        if shift is not None:
            y = y + shift
        o_ref[rows, :] = y.astype(o_ref.dtype)

    _rows_loop(x_ref.shape[0], stats)
    _rows_loop(x_ref.shape[0], apply)


def _normmod_body(x_ref, g_ref, sh_ref, sc_ref, o_ref, r_ref):
    _rms_scale_rows(x_ref, r_ref, g_ref[...] * (1.0 + sc_ref[...]), sh_ref[...], o_ref)


def _norm_body(x_ref, g_ref, o_ref, r_ref):
    _rms_scale_rows(x_ref, r_ref, g_ref[...], None, o_ref)


def _normmod(x, g, mods, base, n_rows):
    tm = TM_EW
    return pl.pallas_call(
        _normmod_body,
        grid=(n_rows // tm,),
        in_specs=[
            pl.BlockSpec((tm, D_MODEL), lambda i: (i, 0)),
            pl.BlockSpec((1, D_MODEL), lambda i: (0, 0)),
            _mod_spec(tm, base),
            _mod_spec(tm, base + 1),
        ],
        out_specs=pl.BlockSpec((tm, D_MODEL), lambda i: (i, 0)),
        out_shape=jax.ShapeDtypeStruct((n_rows, D_MODEL), BF16),
        scratch_shapes=[pltpu.VMEM((tm, 1), F32)],
        compiler_params=_cparams("arbitrary"),
        name="normmod",
    )(x, g.reshape(1, D_MODEL), mods, mods)


def _final_norm(x, g, n_rows):
    tm = TM_EW
    return pl.pallas_call(
        _norm_body,
        grid=(n_rows // tm,),
        in_specs=[
            pl.BlockSpec((tm, D_MODEL), lambda i: (i, 0)),
            pl.BlockSpec((1, D_MODEL), lambda i: (0, 0)),
        ],
        out_specs=pl.BlockSpec((tm, D_MODEL), lambda i: (i, 0)),
        out_shape=jax.ShapeDtypeStruct((n_rows, D_MODEL), F32),
        scratch_shapes=[pltpu.VMEM((tm, 1), F32)],
        compiler_params=_cparams("arbitrary"),
        name="final_norm",
    )(x, g.reshape(1, D_MODEL))


MXU_COLS = 256


WO_SLAB = 64
WO_NSLAB = D_FF // WO_SLAB


NEXT_ROWS = 128
NEXT_SLICES = TM // NEXT_ROWS


def _ffn_a_body(u0_ref, xn_ref, g_ref, sh_ref, sc_ref, wa_ref, wb_ref, wo_ref, h_ref, wob_ref, u2_ref, r_ref):
    i, j = pl.program_id(0), pl.program_id(1)
    slot = i % 2

    @pl.when((i == 0) & (j == 0))
    def _():
        u2_ref[0] = u0_ref[...]

    u = u2_ref[slot]
    for c in range(0, h_ref.shape[1], MXU_COLS):
        sl = slice(c, c + MXU_COLS)
        a = _dot(u, wa_ref[:, sl].astype(BF16))
        b = _dot(u, wb_ref[:, sl].astype(BF16))
        h_ref[:, sl] = (a * jax.nn.sigmoid(a) * b).astype(BF16)

    dst = pl.ds(pl.multiple_of(jnp.minimum(j, NEXT_SLICES - 1) * NEXT_ROWS, NEXT_ROWS), NEXT_ROWS)
    _rms_scale_rows(xn_ref, r_ref, g_ref[...] * (1.0 + sc_ref[...]), sh_ref[...], u2_ref.at[1 - slot, dst, :])

    @pl.when(pl.program_id(0) * pl.num_programs(1) + pl.program_id(1) < WO_NSLAB)
    def _():
        wob_ref[...] = wo_ref[...].astype(BF16)


def _ffn_a(x, g, mods, base, wi, wo, l, n_rows):
    tf = 512
    nf = D_FF // tf
    ni = n_rows // TM
    assert ni * nf >= WO_NSLAB and nf >= NEXT_SLICES

    def wo_blk(i, j):
        return jnp.minimum(i * nf + j, WO_NSLAB - 1)

    def nxt(i):
        return jnp.minimum(i + 1, ni - 1)

    def next_mod(k):
        return pl.BlockSpec((None, None, 1, D_MODEL), lambda i, j: (_mod_row(nxt(i), TM), k, 0, 0))

    u0 = _normmod(x, g, mods, base, TM)
    return pl.pallas_call(
        _ffn_a_body,
        grid=(ni, nf),
        in_specs=[
            pl.BlockSpec((TM, D_MODEL), lambda i, j: (0, 0)),
            pl.BlockSpec((NEXT_ROWS, D_MODEL),
                         lambda i, j: (nxt(i) * NEXT_SLICES + jnp.minimum(j, NEXT_SLICES - 1), 0)),
            pl.BlockSpec((1, D_MODEL), lambda i, j: (0, 0)),
            next_mod(base),
            next_mod(base + 1),
            pl.BlockSpec((None, D_MODEL, tf), lambda i, j: (l, 0, j)),
            pl.BlockSpec((None, D_MODEL, tf), lambda i, j: (l, 0, j + nf)),
            pl.BlockSpec((None, WO_SLAB, D_MODEL), lambda i, j: (l, wo_blk(i, j), 0)),
        ],
        out_specs=[
            pl.BlockSpec((TM, tf), lambda i, j: (i, j)),
            pl.BlockSpec((WO_SLAB, D_MODEL), lambda i, j: (wo_blk(i, j), 0)),
        ],
        out_shape=[
            jax.ShapeDtypeStruct((n_rows, D_FF), BF16),
            jax.ShapeDtypeStruct((D_FF, D_MODEL), BF16),
        ],
        scratch_shapes=[pltpu.VMEM((2, TM, D_MODEL), BF16), pltpu.VMEM((NEXT_ROWS, 1), F32)],
        compiler_params=_cparams("arbitrary", "arbitrary"),
        name="ffn_a",
    )(u0, x, g.reshape(1, D_MODEL), mods, mods, wi, wi, wo)


SIDE_SLAB = 64


def _ffn_b_body(h_ref, w_ref, x_ref, g_ref, *rest, n_slab):
    o_ref = rest[-2] if n_slab else rest[-1]
    h = h_ref[...]
    for c in range(0, o_ref.shape[1], MXU_COLS):
        sl = slice(c, c + MXU_COLS)
        o_ref[:, sl] = x_ref[:, sl] + (0.5 * g_ref[:, sl]) * _dot(h, w_ref[:, sl])

    if n_slab:
        side_ref, _, side_out_ref = rest

        @pl.when(pl.program_id(0) * pl.num_programs(1) + pl.program_id(1) < n_slab)
        def _():
            side_out_ref[...] = side_ref[...].astype(BF16)


def _ffn_b(h, wo_bf16, x, mods, gate_idx, n_rows, side=None, l=None):
    tn = 512
    nj = D_MODEL // tn
    gate = pl.BlockSpec((None, None, 1, tn), lambda i, j: (_mod_row(i, TM), gate_idx, 0, j))
    in_specs = [
        pl.BlockSpec((TM, D_FF), lambda i, j: (i, 0)),
        pl.BlockSpec((D_FF, tn), lambda i, j: (0, j)),
        pl.BlockSpec((TM, tn), lambda i, j: (i, j)),
        gate,
    ]
    out_specs = [pl.BlockSpec((TM, tn), lambda i, j: (i, j))]
    out_shape = [jax.ShapeDtypeStruct((n_rows, D_MODEL), F32)]
    args = [h, wo_bf16, x, mods]
    n_slab = 0
    if side is not None:
        _, rows, cols = side.shape
        n_slab = rows // SIDE_SLAB
        assert (n_rows // TM) * nj >= n_slab

        def slab(i, j):
            return jnp.minimum(i * nj + j, n_slab - 1)

        in_specs.append(pl.BlockSpec((None, SIDE_SLAB, cols), lambda i, j: (l, slab(i, j), 0)))
        out_specs.append(pl.BlockSpec((SIDE_SLAB, cols), lambda i, j: (slab(i, j), 0)))
        out_shape.append(jax.ShapeDtypeStruct((rows, cols), BF16))
        args.append(side)
    res = pl.pallas_call(
        functools.partial(_ffn_b_body, n_slab=n_slab),
        grid=(n_rows // TM, nj),
        in_specs=in_specs,
        out_specs=out_specs,
        out_shape=out_shape,
        compiler_params=_cparams("arbitrary", "arbitrary"),
        name="ffn_b",
    )(*args)
    return res if side is not None else res[0]


def _ffn(x, mods, g, wi, wo, l, base, n_rows, side=None):
    h, wo_bf16 = _ffn_a(x, g, mods, base, wi, wo, l, n_rows)
    return _ffn_b(h, wo_bf16, x, mods, base + 2, n_rows, side, l)


PROJ_TM = 1536
PROJ_ROWS = 256
PROJ_SLICES = PROJ_TM // PROJ_ROWS


def _proj_body(u0_ref, xn_ref, g_ref, sh_ref, sc_ref, w_ref, o_ref, uo_ref, u2_ref, r_ref):
    i, j = pl.program_id(0), pl.program_id(1)
    slot = i % 2

    @pl.when((i == 0) & (j == 0))
    def _():
        u2_ref[0] = u0_ref[...]

    @pl.when(j == 0)
    def _():
        uo_ref[...] = u2_ref[slot]

    o_ref[...] = _dot(u2_ref[slot], w_ref[...])

    dst = pl.ds(pl.multiple_of(jnp.minimum(j, PROJ_SLICES - 1) * PROJ_ROWS, PROJ_ROWS), PROJ_ROWS)
    _rms_scale_rows(xn_ref, r_ref, g_ref[...] * (1.0 + sc_ref[...]), sh_ref[...], u2_ref.at[1 - slot, dst, :])


def _proj(x, g, mods, w_in_bf16):
    tn = COL
    ni = MT // PROJ_TM
    nj = IN_WIDTH // tn
    assert nj >= PROJ_SLICES and SEQ % PROJ_ROWS == 0 and N_LAT % PROJ_ROWS == 0

    def nxt_slice(i, j):
        return jnp.minimum(i + 1, ni - 1) * PROJ_SLICES + jnp.minimum(j, PROJ_SLICES - 1)

    def next_mod(k):
        return pl.BlockSpec((None, None, 1, D_MODEL), lambda i, j: (_mod_row(nxt_slice(i, j), PROJ_ROWS), k, 0, 0))

    u0 = _normmod(x, g, mods, 3, PROJ_TM)
    return pl.pallas_call(
        _proj_body,
        grid=(ni, nj),
        in_specs=[
            pl.BlockSpec((PROJ_TM, D_MODEL), lambda i, j: (0, 0)),
            pl.BlockSpec((PROJ_ROWS, D_MODEL), lambda i, j: (nxt_slice(i, j), 0)),
            pl.BlockSpec((1, D_MODEL), lambda i, j: (0, 0)),
            next_mod(3),
            next_mod(4),
            pl.BlockSpec((D_MODEL, tn), lambda i, j: (0, j)),
        ],
        out_specs=[
            pl.BlockSpec((PROJ_TM, tn), lambda i, j: (i, j)),
            pl.BlockSpec((PROJ_TM, D_MODEL), lambda i, j: (i, 0)),
        ],
        out_shape=[
            jax.ShapeDtypeStruct((MT, IN_WIDTH), F32),
            jax.ShapeDtypeStruct((MT, D_MODEL), BF16),
        ],
        scratch_shapes=[pltpu.VMEM((2, PROJ_TM, D_MODEL), BF16), pltpu.VMEM((PROJ_ROWS, 1), F32)],
        compiler_params=_cparams("arbitrary", "arbitrary"),
        name="proj",
    )(u0, x, g.reshape(1, D_MODEL), mods, mods, w_in_bf16)


QKV_TM = 512


def _rope_tables():
    t = jnp.arange(SEQ)
    rows = (t // GRID_W).astype(F32)
    cols = (t % GRID_W).astype(F32)
    freqs = ROPE_THETA ** (-jnp.arange(ROPE_FREQS, dtype=F32) / ROPE_FREQS)
    ar = rows[:, None] * freqs[None, :]
    ac = cols[:, None] * freqs[None, :]
    cos = jnp.concatenate([jnp.cos(ar), jnp.cos(ar), jnp.cos(ac), jnp.cos(ac)], axis=-1)
    sin = jnp.concatenate([-jnp.sin(ar), jnp.sin(ar), -jnp.sin(ac), jnp.sin(ac)], axis=-1)
    cos = jnp.concatenate([cos, jnp.ones((QKV_TM, HEAD_DIM), F32)], axis=0)
    sin = jnp.concatenate([sin, jnp.zeros((QKV_TM, HEAD_DIM), F32)], axis=0)
    return cos, sin


def _lane_matrices():
    lane = jnp.arange(HEAD_DIM)
    partner = jnp.where(lane % (2 * ROPE_FREQS) < ROPE_FREQS, lane + ROPE_FREQS, lane - ROPE_FREQS)
    swap = (lane[:, None] == partner[None, :]).astype(BF16)
    return jnp.ones((HEAD_DIM, HEAD_DIM), BF16), swap


def _dot_split(x, m_ref):
    hi = x.astype(BF16)
    lo = (x - hi.astype(F32)).astype(BF16)
    return _dot(hi, m_ref[...]) + _dot(lo, m_ref[...])


def _qkv_body(q_ref, kv_ref, cos_ref, sin_ref, qn_ref, kn_ref, ones_ref, swap_ref, qo_ref, ko_ref, vo_ref):
    cos = cos_ref[...]
    sin = sin_ref[...]

    def norm_rope(xh, g):
        ss = _dot_split(xh * xh, ones_ref)
        y = xh * lax.rsqrt(ss * (1.0 / HEAD_DIM) + EPS) * g
        return y * cos + _dot_split(y, swap_ref) * sin

    scale = math.log2(math.e) / math.sqrt(HEAD_DIM)
    for h in range(N_Q_HEADS):
        sl = slice(h * HEAD_DIM, (h + 1) * HEAD_DIM)
        qo_ref[:, sl] = (norm_rope(q_ref[:, sl], qn_ref[...]) * scale).astype(BF16)
    for h in range(N_KV_HEADS):
        sl = slice(h * HEAD_DIM, (h + 1) * HEAD_DIM)
        ko_ref[:, sl] = norm_rope(kv_ref[:, sl], kn_ref[...]).astype(BF16)
    kvw = N_KV_HEADS * HEAD_DIM
    for h in range(N_KV_HEADS):
        vo_ref[:, 2 * h * HEAD_DIM:(2 * h + 1) * HEAD_DIM] = (
            kv_ref[:, kvw + h * HEAD_DIM:kvw + (h + 1) * HEAD_DIM].astype(BF16))
        vo_ref[:, (2 * h + 1) * HEAD_DIM:(2 * h + 2) * HEAD_DIM] = jnp.ones((vo_ref.shape[0], HEAD_DIM), BF16)


def _qkv(proj, cos, sin, q_norm, k_norm):
    tm = QKV_TM
    n_lat_tiles = N_LAT // tm
    per_seq = SEQ // tm
    kvw = N_KV_HEADS * HEAD_DIM

    def tab(i):
        return (jnp.where(i < n_lat_tiles, i % per_seq, per_seq), 0)

    return pl.pallas_call(
        _qkv_body,
        grid=(MT // tm,),
        in_specs=[
            pl.BlockSpec((tm, COL), lambda i: (i, OFF_Q // COL)),
            pl.BlockSpec((tm, COL), lambda i: (i, OFF_KV // COL)),
            pl.BlockSpec((tm, HEAD_DIM), tab),
            pl.BlockSpec((tm, HEAD_DIM), tab),
            pl.BlockSpec((1, HEAD_DIM), lambda i: (0, 0)),
            pl.BlockSpec((1, HEAD_DIM), lambda i: (0, 0)),
            pl.BlockSpec((HEAD_DIM, HEAD_DIM), lambda i: (0, 0)),
            pl.BlockSpec((HEAD_DIM, HEAD_DIM), lambda i: (0, 0)),
        ],
        out_specs=[
            pl.BlockSpec((tm, COL), lambda i: (i, 0)),
            pl.BlockSpec((tm, kvw), lambda i: (i, 0)),
            pl.BlockSpec((tm, 2 * kvw), lambda i: (i, 0)),
        ],
        out_shape=[
            jax.ShapeDtypeStruct((MT, COL), BF16),
            jax.ShapeDtypeStruct((MT, kvw), BF16),
            jax.ShapeDtypeStruct((MT, 2 * kvw), BF16),
        ],
        compiler_params=_cparams("arbitrary"),
        name="qkv",
    )(proj, proj, cos, sin, q_norm.reshape(1, HEAD_DIM), k_norm.reshape(1, HEAD_DIM), *_lane_matrices())


def _attn_body(*refs, with_lat):
    if with_lat:
        q_ref, kl_ref, vl_ref, kc_ref, vc_ref, o_ref = refs
    else:
        q_ref, kc_ref, vc_ref, o_ref = refs
    nt = (((1,), (1,)), ((), ()))
    half = q_ref.shape[0] // 2
    units = [(slice(r, r + half), slice(g * HEAD_DIM, (g + 1) * HEAD_DIM))
             for g in range(Q_GROUP) for r in (0, half)]
    scores = []
    for rows, sl in units:
        q = q_ref[rows, sl]
        sc = lax.dot_general(q, kc_ref[...], nt, preferred_element_type=F32)
        s_lat = lax.dot_general(q, kl_ref[...], nt, preferred_element_type=F32) if with_lat else None
        scores.append((sc, s_lat))
    for (rows, sl), (sc, s_lat) in zip(units, scores):
        m = jnp.max(sc, axis=-1, keepdims=True)
        if with_lat:
            m = jnp.maximum(m, jnp.max(s_lat, axis=-1, keepdims=True))
        o = _dot(jnp.exp2(sc - m).astype(BF16), vc_ref[...])
        if with_lat:
            o = o + _dot(jnp.exp2(s_lat - m).astype(BF16), vl_ref[...])
        o_ref[rows, sl] = (o[:, :HEAD_DIM] / o[:, HEAD_DIM:]).astype(BF16)


def _attn_lat(q, k, v, bb):
    tq = 512
    nq = SEQ // tq
    ctx_blk = N_LAT // CTX_LEN
    gw = Q_GROUP * HEAD_DIM
    return pl.pallas_call(
        _keep_branch_buffer(functools.partial(_attn_body, with_lat=True), 5),
        grid=(BATCH, N_KV_HEADS, nq),
        in_specs=[
            pl.BlockSpec((tq, gw), lambda b, h, i: (b * nq + i, h)),
            pl.BlockSpec((SEQ, HEAD_DIM), lambda b, h, i: (b, h)),
            pl.BlockSpec((SEQ, 2 * HEAD_DIM), lambda b, h, i: (b, h)),
            pl.BlockSpec((CTX_LEN, HEAD_DIM), lambda b, h, i: (ctx_blk + b, h)),
            pl.BlockSpec((CTX_LEN, 2 * HEAD_DIM), lambda b, h, i: (ctx_blk + b, h)),
            _ANY,
        ],
        out_specs=pl.BlockSpec((None, tq, gw), lambda b, h, i: (SLOT_ATTN, b * nq + i, h)),
        out_shape=jax.ShapeDtypeStruct(bb.shape, bb.dtype),
        input_output_aliases={5: 0},
        compiler_params=_cparams("arbitrary", "arbitrary", "arbitrary"),
        name="attn_lat",
    )(q, k, v, k, v, bb)


def _attn_ctx(q, k, v, bb):
    ctx_blk = N_LAT // CTX_LEN
    gw = Q_GROUP * HEAD_DIM
    return pl.pallas_call(
        _keep_branch_buffer(functools.partial(_attn_body, with_lat=False), 3),
        grid=(BATCH, N_KV_HEADS),
        in_specs=[
            pl.BlockSpec((CTX_LEN, gw), lambda b, h: (ctx_blk + b, h)),
            pl.BlockSpec((CTX_LEN, HEAD_DIM), lambda b, h: (ctx_blk + b, h)),
            pl.BlockSpec((CTX_LEN, 2 * HEAD_DIM), lambda b, h: (ctx_blk + b, h)),
            _ANY,
        ],
        out_specs=pl.BlockSpec((None, CTX_LEN, gw), lambda b, h: (SLOT_ATTN, ctx_blk + b, h)),
        out_shape=jax.ShapeDtypeStruct(bb.shape, bb.dtype),
        input_output_aliases={3: 0},
        compiler_params=_cparams("arbitrary", "arbitrary"),
        name="attn_ctx",
    )(q, k, v, bb)


POOL_PAD = 8


def _pool_body(a_ref, w_ref, s_ref, o_ref, *, seq):
    lp = seq + 2 * POOL_PAD
    t = lax.broadcasted_iota(jnp.int32, (seq, LANE), 0)
    zpad = jnp.zeros((POOL_PAD, LANE), F32)
    for gi, win in enumerate(POOL_WINDOWS):
        sl = slice(gi * LANE, (gi + 1) * LANE)
        a = a_ref[:, sl]
        s = jnp.concatenate([zpad, a, zpad], axis=0)
        s = s + pltpu.roll(s, 1, 0)
        half = 1
        while 2 * half < win:
            s = pltpu.roll(s, half, 0) + pltpu.roll(s, lp - half, 0)
            half *= 2
        s = s[POOL_PAD:POOL_PAD + seq]
        lo = jnp.maximum(t - win // 2, 0)
        hi = jnp.minimum(t + win // 2, seq)
        pooled = s / (hi - lo).astype(F32) - a
        y = _dot(pooled.astype(BF16), w_ref[gi].astype(BF16))
        o_ref[:, sl] = (y * s_ref[:, sl]).astype(BF16)


def _pool(proj, pool_w, pool_scale, bb, seq, row_blk0):
    width = len(POOL_WINDOWS) * LANE
    return pl.pallas_call(
        _keep_branch_buffer(functools.partial(_pool_body, seq=seq), 3),
        grid=(BATCH,),
        in_specs=[
            pl.BlockSpec((seq, width), lambda b: (row_blk0 + b, 0)),
            pl.BlockSpec((len(POOL_WINDOWS), LANE, LANE), lambda b: (0, 0, 0)),
            pl.BlockSpec((1, width), lambda b: (0, 0)),
            _ANY,
        ],
        out_specs=pl.BlockSpec((None, seq, width), lambda b: (SLOT_POOL, row_blk0 + b, 0)),
        out_shape=jax.ShapeDtypeStruct(bb.shape, bb.dtype),
        input_output_aliases={3: 0},
        compiler_params=_cparams("arbitrary"),
        name="pool",
    )(proj, pool_w, pool_scale.reshape(1, width), bb)


def _hy_prep_body(x0_ref, x1_ref, v_ref, w0_ref, w1_ref, wv_ref, b0_ref, b1_ref, bv_ref,
                  x0o_ref, vx_ref, vb_ref, *, seq):
    t = lax.broadcasted_iota(jnp.int32, x0_ref.shape, 0)

    def conv(x_ref, w_ref, b_ref):
        x = x_ref[...]
        prev = jnp.where(t >= 1, pltpu.roll(x, 1, 0), 0.0)
        nxt = jnp.where(t <= seq - 2, pltpu.roll(x, seq - 1, 0), 0.0)
        return prev * w_ref[0:1, :] + x * w_ref[1:2, :] + nxt * w_ref[2:3, :] + b_ref[...]

    x0o_ref[...] = conv(x0_ref, w0_ref, b0_ref)
    vx = conv(v_ref, wv_ref, bv_ref) * conv(x1_ref, w1_ref, b1_ref)
    vx_ref[...] = vx
    vb_ref[...] = vx.astype(BF16)


def _hy_prep(proj, short_w, short_b, seq, row_blk0):
    tc = HYENA_WIDTH
    nc = HYENA_WIDTH // tc
    c0 = OFF_HY // tc
    short_b = short_b.reshape(1, 3 * HYENA_WIDTH)

    def xspec(part):
        return pl.BlockSpec((seq, tc), lambda b, c: (row_blk0 + b, c0 + part * nc + c))

    def wspec(part, rows):
        return pl.BlockSpec((rows, tc), lambda b, c: (0, part * nc + c))

    return pl.pallas_call(
        functools.partial(_hy_prep_body, seq=seq),
        grid=(BATCH, nc),
        in_specs=[xspec(0), xspec(1), xspec(2), wspec(0, 3), wspec(1, 3), wspec(2, 3),
                  wspec(0, 1), wspec(1, 1), wspec(2, 1)],
        out_specs=[
            pl.BlockSpec((seq, tc), lambda b, c: (b, c)),
            pl.BlockSpec((seq, tc), lambda b, c: (b, c)),
            pl.BlockSpec((seq, tc), lambda b, c: (0, b * nc + c)),
        ],
        out_shape=[
            jax.ShapeDtypeStruct((BATCH * seq, HYENA_WIDTH), F32),
            jax.ShapeDtypeStruct((BATCH * seq, HYENA_WIDTH), F32),
            jax.ShapeDtypeStruct((seq, BATCH * HYENA_WIDTH), BF16),
        ],
        compiler_params=_cparams("arbitrary", "arbitrary"),
        name="hy_prep",
    )(proj, proj, proj, short_w, short_w, short_w, short_b, short_b, short_b)


def _hy_feats(seq):
    t = jnp.linspace(0.0, 1.0, seq, dtype=F32)[:, None]
    f = jnp.linspace(1e-4, HYENA_BANDS - 1, HYENA_BANDS, dtype=F32)
    w = 2.0 * math.pi * jnp.arange(seq, dtype=F32) / seq
    fw = w[:, None] * f[None, :]
    z = jnp.concatenate([t, jnp.cos(fw), -jnp.sin(fw)], axis=-1)
    return jnp.pad(z, ((0, 0), (0, LANE - HYENA_EMB)))


def _hy_deltas():
    max_decay = math.log(HYENA_TARGET) / HYENA_FAST_PCT
    min_decay = math.log(HYENA_TARGET) / HYENA_SLOW_PCT
    return jnp.abs(jnp.linspace(min_decay, max_decay, HYENA_WIDTH, dtype=F32)).reshape(1, HYENA_WIDTH)


def _hy_filter_body(z_ref, w1_ref, b1_ref, w2_ref, b2_ref, w3_ref, fr_ref, dl_ref, k_ref, nyq_ref, *, seq):
    hp = lax.Precision.HIGHEST
    freq = fr_ref[...]
    h = jnp.sin(freq * (jnp.dot(z_ref[...], w1_ref[...], precision=hp, preferred_element_type=F32) + b1_ref[...]))
    h = jnp.sin(freq * (jnp.dot(h, w2_ref[...], precision=hp, preferred_element_type=F32) + b2_ref[...]))
    h = jnp.dot(h, w3_ref[...], precision=hp, preferred_element_type=F32)
    ti = lax.broadcasted_iota(jnp.int32, (seq, HYENA_WIDTH), 0)
    decay = jnp.exp(-(ti.astype(F32) * (1.0 / (seq - 1))) * dl_ref[...])
    hf = h[:, :HYENA_WIDTH] * decay
    hb = jnp.where(ti == 0, 0.0, h[:, HYENA_WIDTH:] * decay)
    ks = hf + hb
    k_ref[:, :HYENA_WIDTH] = ks.astype(BF16)
    k_ref[:, HYENA_WIDTH:] = (hf - hb).astype(BF16)
    nyq = jnp.sum(jnp.where(ti % 2 == 0, ks, -ks), axis=0, keepdims=True)
    nyq_ref[...] = jnp.broadcast_to(nyq, nyq_ref.shape)


def _hy_filter(lp, z, deltas, seq):
    w1 = jnp.pad(lp['hy_f1_w'], ((0, LANE - HYENA_EMB), (0, 0)))
    args = (z, w1, lp['hy_f1_b'].reshape(1, -1), lp['hy_f2_w'], lp['hy_f2_b'].reshape(1, -1), lp['hy_f3_w'],
            lp['hy_freq'].reshape(1, -1), deltas)
    full = lambda a: pl.BlockSpec(a.shape, lambda i: (0,) * a.ndim)
    return pl.pallas_call(
        functools.partial(_hy_filter_body, seq=seq),
        grid=(1,),
        in_specs=[full(a) for a in args],
        out_specs=[pl.BlockSpec((seq, 2 * HYENA_WIDTH), lambda i: (0, 0)),
                   pl.BlockSpec((8, HYENA_WIDTH), lambda i: (0, 0))],
        out_shape=[jax.ShapeDtypeStruct((seq, 2 * HYENA_WIDTH), BF16),
                   jax.ShapeDtypeStruct((8, HYENA_WIDTH), F32)],
        compiler_params=_cparams("arbitrary"),
        name="hy_filter",
    )(*args)


DFT_SPLIT = 64


def _dft_matrices(seq):
    n = 2 * seq
    f = jnp.arange(seq, dtype=jnp.int32)[:, None]

    def table(step, count):
        idx = (f * (jnp.arange(count, dtype=jnp.int32)[None, :] * step)) % n
        ang = idx.astype(F32) * (2.0 * math.pi / n)
        return jnp.cos(ang), jnp.sin(ang)

    hc, hs = table(DFT_SPLIT, seq // DFT_SPLIT)
    lc, ls = table(1, DFT_SPLIT)
    cos = (hc[:, :, None] * lc[:, None, :] - hs[:, :, None] * ls[:, None, :]).reshape(seq, seq)
    sin = (hs[:, :, None] * lc[:, None, :] + hc[:, :, None] * ls[:, None, :]).reshape(seq, seq)
    s = jnp.arange(seq, dtype=jnp.int32)[None, :]
    nyq = jnp.where(s % 2 == 0, 1.0, -1.0).astype(F32)
    wf = jnp.concatenate([cos, jnp.where(f == 0, nyq, -sin)], axis=0).astype(BF16)
    return wf, wf.T


def _mm_body(a_ref, b_ref, o_ref):
    o_ref[...] = _dot(a_ref[...], b_ref[...])


def _dft_fwd_body(wc_ref, ws_ref, x_ref, hr_ref, hi_ref, nyq_ref, pr_ref, pi_ref, *, n):
    tf = wc_ref.shape[0]
    freq = lax.broadcasted_iota(jnp.int32, (tf, MXU_COLS), 0) + pl.program_id(0) * tf
    first = freq == 0
    w = jnp.where(first, 1.0 / n, 2.0 / n)
    chunks = [slice(c, c + MXU_COLS) for c in range(0, x_ref.shape[1], MXU_COLS)]
    spectra = [(_dot(wc_ref[...], x_ref[:, sl]), _dot(ws_ref[...], x_ref[:, sl])) for sl in chunks]
    for sl, (xr, xi) in zip(chunks, spectra):
        hr = hr_ref[:, sl]
        hi = hi_ref[:, sl]
        pr_ref[:, sl] = (jnp.where(first, xr * hr, xr * hr - xi * hi) * w).astype(BF16)
        pi_ref[:, sl] = (jnp.where(first, xi * nyq_ref[0:1, sl], xr * hi + xi * hr) * w).astype(BF16)


def _dft_fwd(wf, x, hf, nyq):
    n, seq = wf.shape
    tf = min(seq, 512)
    nf = seq // tf
    out = jax.ShapeDtypeStruct((seq, BATCH * HYENA_WIDTH), BF16)
    return pl.pallas_call(
        functools.partial(_dft_fwd_body, n=n),
        grid=(nf, BATCH),
        in_specs=[
            pl.BlockSpec((tf, seq), lambda f, b: (f, 0)),
            pl.BlockSpec((tf, seq), lambda f, b: (nf + f, 0)),
            pl.BlockSpec((seq, HYENA_WIDTH), lambda f, b: (0, b)),
            pl.BlockSpec((tf, HYENA_WIDTH), lambda f, b: (f, 0)),
            pl.BlockSpec((tf, HYENA_WIDTH), lambda f, b: (nf + f, 0)),
            pl.BlockSpec((8, HYENA_WIDTH), lambda f, b: (0, 0)),
        ],
        out_specs=[pl.BlockSpec((tf, HYENA_WIDTH), lambda f, b: (f, b))] * 2,
        out_shape=[out, out],
        compiler_params=_cparams("arbitrary", "arbitrary"),
        name="dft_fwd",
    )(wf, wf, x, hf, hf, nyq)


def _dft_filter(wf, k):
    n, seq = wf.shape
    tm = min(seq, 1024)
    return pl.pallas_call(
        _mm_body,
        grid=(n // tm,),
        in_specs=[
            pl.BlockSpec((tm, seq), lambda i: (i, 0)),
            pl.BlockSpec((seq, HYENA_WIDTH), lambda i: (0, (i * tm) // seq)),
        ],
        out_specs=pl.BlockSpec((tm, HYENA_WIDTH), lambda i: (i, 0)),
        out_shape=jax.ShapeDtypeStruct((n, HYENA_WIDTH), F32),
        compiler_params=_cparams("arbitrary"),
        name="dft_filter",
    )(wf, k)


def _dft_inv_body(wc_ref, ws_ref, pr_ref, pi_ref, vx_ref, bias_ref, x0_ref, o_ref):
    y = _dot(wc_ref[...], pr_ref[...]) + _dot(ws_ref[...], pi_ref[...])
    o_ref[...] = ((y + vx_ref[...] * bias_ref[...]) * x0_ref[...]).astype(BF16)


def _dft_inv(wi, p_re, p_im, vx, bias, x0, bb, row0):
    seq, n = wi.shape
    tm = min(seq, 1024)
    tn = HYENA_WIDTH
    nt = seq // tm
    return pl.pallas_call(
        _keep_branch_buffer(_dft_inv_body, 7),
        grid=(nt, BATCH),
        in_specs=[
            pl.BlockSpec((tm, seq), lambda i, b: (i, 0)),
            pl.BlockSpec((tm, seq), lambda i, b: (i, 1)),
            pl.BlockSpec((seq, tn), lambda i, b: (0, b)),
            pl.BlockSpec((seq, tn), lambda i, b: (0, b)),
            pl.BlockSpec((tm, tn), lambda i, b: (b * nt + i, 0)),
            pl.BlockSpec((1, tn), lambda i, b: (0, 0)),
            pl.BlockSpec((tm, tn), lambda i, b: (b * nt + i, 0)),
            _ANY,
        ],
        out_specs=pl.BlockSpec((None, tm, tn), lambda i, b: (SLOT_HYENA, row0 // tm + b * nt + i, 0)),
        out_shape=jax.ShapeDtypeStruct(bb.shape, bb.dtype),
        input_output_aliases={7: 0},
        compiler_params=_cparams("arbitrary", "arbitrary"),
        name="dft_inv",
    )(wi, wi, p_re, p_im, vx, bias.reshape(1, tn), x0, bb)


def _hyena(proj, lp, consts, bb, seq, row0):
    z, deltas, (wf, wi) = consts
    x0, vx, vb = _hy_prep(proj, lp['hy_short_w'], lp['hy_short_b'], seq, row0 // seq)
    k, nyq = _hy_filter(lp, z, deltas, seq)
    hf = _dft_filter(wf, k)
    p_re, p_im = _dft_fwd(wf, vb, hf, nyq)
    return _dft_inv(wi, p_re, p_im, vx, lp['hy_bias'], x0, bb, row0)


def _s5_params(lp, n_seg_steps):
    a_re, a_im = lp['s5_a_re'], lp['s5_a_im']
    dt = jnp.exp(lp['s5_log_dt'])[..., None]
    mag = jnp.exp(a_re * dt)
    ab_re, ab_im = mag * jnp.cos(a_im * dt), mag * jnp.sin(a_im * dt)
    den = a_re * a_re + a_im * a_im
    nr, ni = ab_re - 1.0, ab_im
    cf_re = (nr * a_re + ni * a_im) / den
    cf_im = (ni * a_re - nr * a_im) / den
    b_re, b_im = lp['s5_b_re'], lp['s5_b_im']
    bb_re = cf_re[..., None] * b_re - cf_im[..., None] * b_im
    bb_im = cf_re[..., None] * b_im + cf_im[..., None] * b_re
    eye = jnp.eye(S5_CHUNK_GROUPS, dtype=F32)

    def bdiag_in(m):
        m = m.reshape(2, S5_NCHUNK, S5_CHUNK_GROUPS, S5_STATE, S5_GC)
        return jnp.einsum('dqgpc,gh->dqgchp', m, eye).reshape(2, S5_NCHUNK, S5_CHUNK_CH, S5_CHUNK_ST)

    def bdiag_out(m):
        m = m.reshape(2, S5_NCHUNK, S5_CHUNK_GROUPS, S5_GC, S5_STATE)
        return jnp.einsum('dqgcp,gh->dqhpgc', m, eye).reshape(2, S5_NCHUNK, S5_CHUNK_ST, S5_CHUNK_CH)

    bbd = jnp.concatenate([bdiag_in(bb_re), bdiag_in(bb_im)], axis=-1).astype(BF16)
    cbd = jnp.concatenate([bdiag_out(lp['s5_c_re']), -bdiag_out(lp['s5_c_im'])], axis=-2).astype(BF16)
    a = jnp.stack([ab_re.reshape(2, -1), ab_im.reshape(2, -1)], axis=1)
    aks = []
    for steps in n_seg_steps:
        pr, pi = ab_re, ab_im
        for _ in range(int(math.log2(steps))):
            pr, pi = pr * pr - pi * pi, 2.0 * pr * pi
        aks.append(jnp.stack([pr.reshape(2, -1), pi.reshape(2, -1)], axis=1))
    return [(bbd, cbd, a, ak) for ak in aks]


S5_BLOCK = 1024


def _s5_scan_body(u_ref, bbd_ref, cbd_ref, a_ref, ak_ref, dsk_ref, h0_ref, y_ref, hl_ref,
                  xr0_ref, xr1_ref, xi0_ref, xi1_ref, yd0_ref, yd1_ref, up_ref, ub_ref, *, seq):
    nk = seq // S5_LANES
    blk = min(seq, S5_BLOCK)
    nblk = seq // blk
    tiles = blk // S5_LANES
    shape = (S5_LANES, S5_CHUNK_ST)
    xr_ref, xi_ref, yd_ref = (xr0_ref, xr1_ref), (xi0_ref, xi1_ref), (yd0_ref, yd1_ref)
    for j in range(S5_LANES):
        up_ref[pl.ds(j, nk, stride=S5_LANES), :] = u_ref[pl.ds(j * nk, nk), :]
    ub_ref[...] = up_ref[...].astype(BF16)
    row = lax.broadcasted_iota(jnp.int32, shape, 0)
    zero = jnp.zeros(shape, F32)

    def block(r):
        return pl.ds(pl.multiple_of(r * blk, blk), blk)

    def in_proj(d, rows):
        xr_ref[d][rows, :] = _dot(ub_ref[rows, :], bbd_ref[d, :, :S5_CHUNK_ST])
        xi_ref[d][rows, :] = _dot(ub_ref[rows, :], bbd_ref[d, :, S5_CHUNK_ST:])

    def out_proj(d, rows):
        yd_ref[d][rows, :] = (_dot(xr_ref[d][rows, :].astype(BF16), cbd_ref[d, :S5_CHUNK_ST, :])
                              + _dot(xi_ref[d][rows, :].astype(BF16), cbd_ref[d, S5_CHUNK_ST:, :]))

    def coeffs(d):
        return jnp.broadcast_to(a_ref[d, 0:1, :], shape), jnp.broadcast_to(a_ref[d, 1:2, :], shape)

    def block_tiles(d, r):
        base = pl.multiple_of(r * blk, blk)
        order = range(tiles) if d == 0 else range(tiles - 1, -1, -1)
        return [pl.ds(base + t * S5_LANES, S5_LANES) for t in order]

    def scan_block(d, r, carry):
        ar, ai = coeffs(d)
        xr, xi = carry
        for rows in block_tiles(d, r):
            xr, xi = (ar * xr - ai * xi + xr_ref[d][rows, :], ar * xi + ai * xr + xi_ref[d][rows, :])
            xr_ref[d][rows, :] = xr
            xi_ref[d][rows, :] = xi
        return xr, xi

    def fix_block(d, r, carry):
        ar, ai = coeffs(d)
        gr, gi = carry
        for rows in block_tiles(d, r):
            gr, gi = ar * gr - ai * gi, ar * gi + ai * gr
            xr_ref[d][rows, :] += gr
            xi_ref[d][rows, :] += gi
        return gr, gi

    def visit(d, s):
        return s if d == 0 else nblk - 1 - s

    def entering_states(d, er, ei):
        akr, aki = ak_ref[d, 0:1, :], ak_ref[d, 1:2, :]
        hr, hi = h0_ref[2 * d:2 * d + 1, :], h0_ref[2 * d + 1:2 * d + 2, :]
        in_r, in_i = zero, zero
        for j in (range(S5_LANES) if d == 0 else range(S5_LANES - 1, -1, -1)):
            in_r = jnp.where(row == j, hr, in_r)
            in_i = jnp.where(row == j, hi, in_i)
            hr, hi = (akr * hr - aki * hi + er[j:j + 1, :], akr * hi + aki * hr + ei[j:j + 1, :])
        hl_ref[2 * d:2 * d + 1, :] = hr
        hl_ref[2 * d + 1:2 * d + 2, :] = hi
        return in_r, in_i

    def fix_all(d, ins):
        lax.fori_loop(0, nblk, lambda s, carry: fix_block(d, visit(d, s), carry), ins)

    in_proj(0, slice(None))

    def scan0_body(s, carry):
        in_proj(1, block(s))
        return scan_block(0, s, carry)
    ends0 = lax.fori_loop(0, nblk, scan0_body, (zero, zero))
    fix_all(0, entering_states(0, *ends0))

    def scan1_body(s, carry):
        out_proj(0, block(s))
        return scan_block(1, visit(1, s), carry)
    ends1 = lax.fori_loop(0, nblk, scan1_body, (zero, zero))
    fix_all(1, entering_states(1, *ends1))
    out_proj(1, slice(None))

    up_ref[...] = yd_ref[0][...] + yd_ref[1][...] + up_ref[...] * (dsk_ref[0] + dsk_ref[1])
    for j in range(S5_LANES):
        y_ref[pl.ds(j * nk, nk), :] = up_ref[pl.ds(j, nk, stride=S5_LANES), :]


def _s5_scan(proj, params, dskip, h0, seq, row_blk0):
    bbd, cbd, a, ak = params
    nq = S5_NCHUNK
    col0 = OFF_S5 // S5_CHUNK_CH
    return pl.pallas_call(
        functools.partial(_s5_scan_body, seq=seq),
        grid=(BATCH, nq),
        in_specs=[
            pl.BlockSpec((seq, S5_CHUNK_CH), lambda b, q: (row_blk0 + b, col0 + q)),
            pl.BlockSpec((2, None, S5_CHUNK_CH, 2 * S5_CHUNK_ST), lambda b, q: (0, q, 0, 0)),
            pl.BlockSpec((2, None, 2 * S5_CHUNK_ST, S5_CHUNK_CH), lambda b, q: (0, q, 0, 0)),
            pl.BlockSpec((2, 2, S5_CHUNK_ST), lambda b, q: (0, 0, q)),
            pl.BlockSpec((2, 2, S5_CHUNK_ST), lambda b, q: (0, 0, q)),
            pl.BlockSpec((2, 1, S5_CHUNK_CH), lambda b, q: (0, 0, q)),
            pl.BlockSpec((None, 4, S5_CHUNK_ST), lambda b, q: (b, 0, q)),
        ],
        out_specs=[
            pl.BlockSpec((seq, S5_CHUNK_CH), lambda b, q: (b, q)),
            pl.BlockSpec((None, 4, S5_CHUNK_ST), lambda b, q: (b, 0, q)),
        ],
        out_shape=[
            jax.ShapeDtypeStruct((BATCH * seq, S5_WIDTH), F32),
            jax.ShapeDtypeStruct((BATCH, 4, S5_GROUPS * S5_STATE), F32),
        ],
        scratch_shapes=[pltpu.VMEM((seq, S5_CHUNK_ST), F32)] * 4 + [pltpu.VMEM((seq, S5_CHUNK_CH), F32)] * 3
        + [pltpu.VMEM((seq, S5_CHUNK_CH), BF16)],
        compiler_params=_cparams("arbitrary", "arbitrary"),
        name="s5_scan",
    )(proj, bbd, cbd, a, ak, dskip.reshape(2, 1, S5_WIDTH), h0)


def _s5_glu_body(y_ref, w_ref, b_ref, o_ref):
    w = w_ref[...].astype(BF16)
    half = y_ref.shape[0] // 2
    halves = [slice(0, half), slice(half, 2 * half)]
    gs = [_dot(jax.nn.gelu(y_ref[rows, :]).astype(BF16), w) + b_ref[...] for rows in halves]
    for rows, g in zip(halves, gs):
        o_ref[rows, :] = (g[:, :S5_WIDTH] * jax.nn.sigmoid(g[:, S5_WIDTH:])).astype(BF16)


def _s5_glu(y, w, b, bb, row0):
    n_rows = y.shape[0]
    tm = TM_EW
    return pl.pallas_call(
        _keep_branch_buffer(_s5_glu_body, 3),
        grid=(n_rows // tm,),
        in_specs=[
            pl.BlockSpec((tm, S5_WIDTH), lambda i: (i, 0)),
            pl.BlockSpec((S5_WIDTH, 2 * S5_WIDTH), lambda i: (0, 0)),
            pl.BlockSpec((1, 2 * S5_WIDTH), lambda i: (0, 0)),
            _ANY,
        ],
        out_specs=pl.BlockSpec((None, tm, S5_WIDTH), lambda i: (SLOT_S5, row0 // tm + i, 0)),
        out_shape=jax.ShapeDtypeStruct(bb.shape, bb.dtype),
        input_output_aliases={3: 0},
        compiler_params=_cparams("arbitrary"),
        name="s5_glu",
    )(y, w, b.reshape(1, -1), bb)


def _merge_body(u_ref, *refs):
    wg_refs, bg_refs, y_refs, wb_refs = (refs[k * N_BRANCH:(k + 1) * N_BRANCH] for k in range(4))
    o_ref = refs[4 * N_BRANCH]
    u = u_ref[...]
    acc = None
    for n in range(N_BRANCH):
        gate = jax.nn.sigmoid(_dot(u, wg_refs[n][...].astype(BF16)) + bg_refs[n][...])
        contrib = gate * _dot(y_refs[n][...], wb_refs[n][...].astype(BF16))
        acc = contrib if acc is None else acc + contrib
    o_ref[...] = acc.astype(BF16)


def _merge(u, w_gate, b_gate, branches, w_branch, l, n_rows):
    tc = MXU_COLS
    ncol = D_MODEL // tc
    b_gate = b_gate.reshape(1, -1)
    per_branch = lambda make: [make(n) for n in range(N_BRANCH)]
    return pl.pallas_call(
        _merge_body,
        grid=(n_rows // TM, ncol),
        in_specs=[pl.BlockSpec((TM, D_MODEL), lambda i, c: (i, 0))]
        + per_branch(lambda n: pl.BlockSpec((None, D_MODEL, tc), lambda i, c: (l, 0, n * ncol + c)))
        + per_branch(lambda n: pl.BlockSpec((1, tc), lambda i, c: (0, n * ncol + c)))
        + per_branch(lambda n: pl.BlockSpec((None, TM, BRANCH_WIDTH), lambda i, c: (n, i, 0)))
        + per_branch(lambda n: pl.BlockSpec((None, None, BRANCH_WIDTH, tc), lambda i, c: (l, n, 0, c))),
        out_specs=pl.BlockSpec((TM, tc), lambda i, c: (i, c)),
        out_shape=jax.ShapeDtypeStruct((n_rows, D_MODEL), BF16),
        compiler_params=_cparams("arbitrary", "arbitrary"),
        name="merge",
    )(u, *([w_gate] * N_BRANCH), *([b_gate] * N_BRANCH), *([branches] * N_BRANCH), *([w_branch] * N_BRANCH))


def _out_body(m_ref, w_ref, x_ref, g_ref, o_ref):
    o_ref[...] = x_ref[...] + g_ref[...] * _dot(m_ref[...], w_ref[...].astype(BF16))


def _out_proj(merged, w_out, l, x, mods, n_rows):
    tn = 1024
    gate = pl.BlockSpec((None, None, 1, tn), lambda j, i: (_mod_row(i, TM), 5, 0, j))
    return pl.pallas_call(
        _out_body,
        grid=(D_MODEL // tn, n_rows // TM),
        in_specs=[
            pl.BlockSpec((TM, D_MODEL), lambda j, i: (i, 0)),
            pl.BlockSpec((None, D_MODEL, tn), lambda j, i: (l, 0, j)),
            pl.BlockSpec((TM, tn), lambda j, i: (i, j)),
            gate,
        ],
        out_specs=pl.BlockSpec((TM, tn), lambda j, i: (i, j)),
        out_shape=jax.ShapeDtypeStruct((n_rows, D_MODEL), F32),
        compiler_params=_cparams("arbitrary", "arbitrary"),
        name="out_proj",
    )(merged, w_out, x, mods)


def _mixer(x, g, lp, big, w_in_bf16, l, mods, consts, bb, last):
    cos, sin, hy_lat, hy_ctx = consts
    proj, u = _proj(x, g, mods, w_in_bf16)
    q, k, v = _qkv(proj, cos, sin, lp['q_norm'], lp['k_norm'])

    ctx_blk = N_LAT // CTX_LEN
    zero_h = jnp.zeros((BATCH, 4, S5_GROUPS * S5_STATE), F32)
    par_ctx, par_lat = _s5_params(lp, (CTX_LEN // S5_LANES, SEQ // S5_LANES))
    ys_ctx, h_ctx = _s5_scan(proj, par_ctx, lp['s5_d'], zero_h, CTX_LEN, ctx_blk)
    ys_lat, _ = _s5_scan(proj, par_lat, lp['s5_d'], h_ctx, SEQ, 0)

    bb = _s5_glu(ys_lat, lp['s5_glu_w'], lp['s5_glu_b'], bb, 0)
    bb = _attn_lat(q, k, v, bb)
    bb = _pool(proj, lp['pool_w'], lp['pool_scale'], bb, SEQ, 0)
    bb = _hyena(proj, lp, hy_lat, bb, SEQ, 0)
    if not last:
        bb = _s5_glu(ys_ctx, lp['s5_glu_w'], lp['s5_glu_b'], bb, N_LAT)
        bb = _attn_ctx(q, k, v, bb)
        bb = _pool(proj, lp['pool_w'], lp['pool_scale'], bb, CTX_LEN, ctx_blk)
        bb = _hyena(proj, lp, hy_ctx, bb, CTX_LEN, N_LAT)

    n_rows = N_LAT if last else MT
    merged = _merge(u, big['w_gate'], lp['b_gate'], bb, big['w_branch'], l, n_rows)
    return _out_proj(merged, big['w_out'], l, x, mods, n_rows), bb


def kernel(x, c, ctx, c_ctx, w_ada, b_ada, norm_ffn1, norm_mix, norm_ffn2, norm_final, ffn1_wi, ffn1_wo, ffn2_wi, ffn2_wo, w_in, w_gate, b_gate, w_branch, w_out, pool_w, pool_scale, q_norm, k_norm, hy_short_w, hy_short_b, hy_f1_w, hy_f1_b, hy_f2_w, hy_f2_b, hy_f3_w, hy_freq, hy_bias, s5_a_re, s5_a_im, s5_log_dt, s5_b_re, s5_b_im, s5_c_re, s5_c_im, s5_d, s5_glu_w, s5_glu_b):
    big = dict(w_gate=w_gate, w_branch=w_branch, w_out=w_out)
    per_layer = dict(
        b_gate=b_gate, pool_w=pool_w,
        pool_scale=pool_scale, q_norm=q_norm, k_norm=k_norm, hy_short_w=hy_short_w, hy_short_b=hy_short_b,
        hy_f1_w=hy_f1_w, hy_f1_b=hy_f1_b, hy_f2_w=hy_f2_w, hy_f2_b=hy_f2_b, hy_f3_w=hy_f3_w, hy_freq=hy_freq,
        hy_bias=hy_bias, s5_a_re=s5_a_re, s5_a_im=s5_a_im, s5_log_dt=s5_log_dt, s5_b_re=s5_b_re, s5_b_im=s5_b_im,
        s5_c_re=s5_c_re, s5_c_im=s5_c_im, s5_d=s5_d, s5_glu_w=s5_glu_w, s5_glu_b=s5_glu_b)

    cos, sin = _rope_tables()
    deltas = _hy_deltas()
    consts = (cos, sin,
              (_hy_feats(SEQ), deltas, _dft_matrices(SEQ)),
              (_hy_feats(CTX_LEN), deltas, _dft_matrices(CTX_LEN)))

    cc = jnp.concatenate([c, c_ctx[None], jnp.zeros((8 - BATCH - 1, D_MODEL), F32)], axis=0)
    mods_all = _ada(cc, w_ada, b_ada).reshape(DEPTH, 8, N_MOD, 1, D_MODEL)

    xs = jnp.concatenate([x.reshape(N_LAT, D_MODEL), ctx.reshape(N_CTX, D_MODEL)], axis=0)
    bb = jnp.zeros((N_BRANCH, MT, BRANCH_WIDTH), BF16)
    for l in range(DEPTH):
        last = l == DEPTH - 1
        lp = {name: w[l] for name, w in per_layer.items()}
        mods = mods_all[l]
        xs, w_in_bf16 = _ffn(xs, mods, norm_ffn1[l], ffn1_wi, ffn1_wo, l, 0, MT, side=w_in)
        xs, bb = _mixer(xs, norm_mix[l], lp, big, w_in_bf16, l, mods, consts, bb, last)
        xs = _ffn(xs, mods, norm_ffn2[l], ffn2_wi, ffn2_wo, l, 6, N_LAT if last else MT)
    return _final_norm(xs, norm_final, N_LAT).reshape(BATCH, SEQ, D_MODEL)
```

```python
import functools
import math

import jax
import jax.numpy as jnp
from jax import lax
from jax.experimental import pallas as pl
from jax.experimental.pallas import tpu as pltpu

F32 = jnp.float32
BF16 = jnp.bfloat16

D_MODEL = 2048
BATCH = 4
SEQ = 2048
DEPTH = 4
GRID_W = 64
CTX_LEN = 256
D_FF = 5632
N_MOD = 9
EPS = 1e-6

POOL_WINDOWS = (2, 4, 8, 16)
HEAD_DIM = 128
N_Q_HEADS = 4
N_KV_HEADS = 2
Q_GROUP = N_Q_HEADS // N_KV_HEADS
ROPE_THETA = 10000.0
ROPE_FREQS = HEAD_DIM // 4

HYENA_WIDTH = 512
HYENA_EMB = 33
HYENA_BANDS = (HYENA_EMB - 1) // 2
HYENA_HIDDEN = 64
HYENA_TARGET = 1e-2
HYENA_FAST_PCT = 0.3
HYENA_SLOW_PCT = 1.5

S5_WIDTH = 512
S5_GC = 16
S5_GROUPS = S5_WIDTH // S5_GC
S5_STATE = 64
S5_LANES = 8
S5_CHUNK_GROUPS = 8
S5_CHUNK_CH = S5_CHUNK_GROUPS * S5_GC
S5_CHUNK_ST = S5_CHUNK_GROUPS * S5_STATE
S5_NCHUNK = S5_GROUPS // S5_CHUNK_GROUPS

N_BRANCH = 4
BRANCH_WIDTH = 512
LANE = 128
COL = 512

OFF_Q = 512
OFF_KV = 1024
OFF_HY = 1536
OFF_S5 = 3072
IN_WIDTH = 3584

N_LAT = BATCH * SEQ
N_CTX = BATCH * CTX_LEN
MT = N_LAT + N_CTX

TM = 1024
TM_EW = 512
VMEM_LIMIT = 56 * 1024 * 1024


def _cparams(*sem):
    return pltpu.CompilerParams(dimension_semantics=sem, vmem_limit_bytes=VMEM_LIMIT)


def _dot(a, b):
    return jnp.dot(a, b, preferred_element_type=F32)


SLOT_POOL, SLOT_ATTN, SLOT_HYENA, SLOT_S5 = range(4)
_ANY = pl.BlockSpec(memory_space=pl.ANY)


def _keep_branch_buffer(body, n_in):
    def wrapped(*refs):
        return body(*refs[:n_in], *refs[n_in + 1:])
    return wrapped


def _mod_row(i, tm):
    return jnp.minimum((i * tm) // SEQ, BATCH)


def _mod_spec(tm, k):
    return pl.BlockSpec((None, None, 1, D_MODEL), lambda i, *_: (_mod_row(i, tm), k, 0, 0))


def _ada_body(c_ref, w_ref, b_ref, o_ref):
    c = c_ref[...]
    a = (c * jax.nn.sigmoid(c)).astype(BF16)
    o_ref[...] = _dot(a, w_ref[...].astype(BF16)) + b_ref[...]


def _ada(cc, w_ada, b_ada):
    tn = 1024
    nw = N_MOD * D_MODEL
    return pl.pallas_call(
        _ada_body,
        grid=(DEPTH, nw // tn),
        in_specs=[
            pl.BlockSpec((8, D_MODEL), lambda l, j: (0, 0)),
            pl.BlockSpec((None, D_MODEL, tn), lambda l, j: (l, 0, j)),
            pl.BlockSpec((None, 1, tn), lambda l, j: (l, 0, j)),
        ],
        out_specs=pl.BlockSpec((None, 8, tn), lambda l, j: (l, 0, j)),
        out_shape=jax.ShapeDtypeStruct((DEPTH, 8, nw), F32),
        compiler_params=_cparams("arbitrary", "arbitrary"),
        name="ada",
    )(cc, w_ada, b_ada.reshape(DEPTH, 1, nw))


NORM_ROWS = 16


def _rows_loop(n_rows, step):
    for c in range(n_rows // NORM_ROWS):
        step(pl.ds(c * NORM_ROWS, NORM_ROWS))


def _rms_scale_rows(x_ref, r_ref, scale, shift, o_ref):
    def stats(rows):
        x = x_ref[rows, :]
        r_ref[rows, :] = lax.rsqrt(jnp.mean(x * x, axis=-1, keepdims=True) + EPS)

    def apply(rows):
        y = x_ref[rows, :] * r_ref[rows, :] * scale
        if shift is not None:
            y = y + shift
        o_ref[rows, :] = y.astype(o_ref.dtype)

    _rows_loop(x_ref.shape[0], stats)
    _rows_loop(x_ref.shape[0], apply)


def _normmod_body(x_ref, g_ref, sh_ref, sc_ref, o_ref, r_ref):
    _rms_scale_rows(x_ref, r_ref, g_ref[...] * (1.0 + sc_ref[...]), sh_ref[...], o_ref)


def _norm_body(x_ref, g_ref, o_ref, r_ref):
    _rms_scale_rows(x_ref, r_ref, g_ref[...], None, o_ref)


def _normmod(x, g, mods, base, n_rows):
    tm = TM_EW
    return pl.pallas_call(
        _normmod_body,
        grid=(n_rows // tm,),
        in_specs=[
            pl.BlockSpec((tm, D_MODEL), lambda i: (i, 0)),
            pl.BlockSpec((1, D_MODEL), lambda i: (0, 0)),
            _mod_spec(tm, base),
            _mod_spec(tm, base + 1),
        ],
        out_specs=pl.BlockSpec((tm, D_MODEL), lambda i: (i, 0)),
        out_shape=jax.ShapeDtypeStruct((n_rows, D_MODEL), BF16),
        scratch_shapes=[pltpu.VMEM((tm, 1), F32)],
        compiler_params=_cparams("arbitrary"),
        name="normmod",
    )(x, g.reshape(1, D_MODEL), mods, mods)


def _final_norm(x, g, n_rows):
    tm = TM_EW
    return pl.pallas_call(
        _norm_body,
        grid=(n_rows // tm,),
        in_specs=[
            pl.BlockSpec((tm, D_MODEL), lambda i: (i, 0)),
            pl.BlockSpec((1, D_MODEL), lambda i: (0, 0)),
        ],
        out_specs=pl.BlockSpec((tm, D_MODEL), lambda i: (i, 0)),
        out_shape=jax.ShapeDtypeStruct((n_rows, D_MODEL), F32),
        scratch_shapes=[pltpu.VMEM((tm, 1), F32)],
        compiler_params=_cparams("arbitrary"),
        name="final_norm",
    )(x, g.reshape(1, D_MODEL))


MXU_COLS = 256


WO_SLAB = 64
WO_NSLAB = D_FF // WO_SLAB


NEXT_ROWS = 128
NEXT_SLICES = TM // NEXT_ROWS


def _ffn_a_body(u0_ref, xn_ref, g_ref, sh_ref, sc_ref, wa_ref, wb_ref, wo_ref, h_ref, wob_ref, u2_ref, r_ref):
    i, j = pl.program_id(0), pl.program_id(1)
    slot = i % 2

    @pl.when((i == 0) & (j == 0))
    def _():
        u2_ref[0] = u0_ref[...]

    u = u2_ref[slot]
    for c in range(0, h_ref.shape[1], MXU_COLS):
        sl = slice(c, c + MXU_COLS)
        a = _dot(u, wa_ref[:, sl].astype(BF16))
        b = _dot(u, wb_ref[:, sl].astype(BF16))
        h_ref[:, sl] = (a * jax.nn.sigmoid(a) * b).astype(BF16)

    dst = pl.ds(pl.multiple_of(jnp.minimum(j, NEXT_SLICES - 1) * NEXT_ROWS, NEXT_ROWS), NEXT_ROWS)
    _rms_scale_rows(xn_ref, r_ref, g_ref[...] * (1.0 + sc_ref[...]), sh_ref[...], u2_ref.at[1 - slot, dst, :])

    @pl.when(pl.program_id(0) * pl.num_programs(1) + pl.program_id(1) < WO_NSLAB)
    def _():
        wob_ref[...] = wo_ref[...].astype(BF16)


def _ffn_a(x, g, mods, base, wi, wo, l, n_rows):
    tf = 512
    nf = D_FF // tf
    ni = n_rows // TM
    assert ni * nf >= WO_NSLAB and nf >= NEXT_SLICES

    def wo_blk(i, j):
        return jnp.minimum(i * nf + j, WO_NSLAB - 1)

    def nxt(i):
        return jnp.minimum(i + 1, ni - 1)

    def next_mod(k):
        return pl.BlockSpec((None, None, 1, D_MODEL), lambda i, j: (_mod_row(nxt(i), TM), k, 0, 0))

    u0 = _normmod(x, g, mods, base, TM)
    return pl.pallas_call(
        _ffn_a_body,
        grid=(ni, nf),
        in_specs=[
            pl.BlockSpec((TM, D_MODEL), lambda i, j: (0, 0)),
            pl.BlockSpec((NEXT_ROWS, D_MODEL),
                         lambda i, j: (nxt(i) * NEXT_SLICES + jnp.minimum(j, NEXT_SLICES - 1), 0)),
            pl.BlockSpec((1, D_MODEL), lambda i, j: (0, 0)),
            next_mod(base),
            next_mod(base + 1),
            pl.BlockSpec((None, D_MODEL, tf), lambda i, j: (l, 0, j)),
            pl.BlockSpec((None, D_MODEL, tf), lambda i, j: (l, 0, j + nf)),
            pl.BlockSpec((None, WO_SLAB, D_MODEL), lambda i, j: (l, wo_blk(i, j), 0)),
        ],
        out_specs=[
            pl.BlockSpec((TM, tf), lambda i, j: (i, j)),
            pl.BlockSpec((WO_SLAB, D_MODEL), lambda i, j: (wo_blk(i, j), 0)),
        ],
        out_shape=[
            jax.ShapeDtypeStruct((n_rows, D_FF), BF16),
            jax.ShapeDtypeStruct((D_FF, D_MODEL), BF16),
        ],
        scratch_shapes=[pltpu.VMEM((2, TM, D_MODEL), BF16), pltpu.VMEM((NEXT_ROWS, 1), F32)],
        compiler_params=_cparams("arbitrary", "arbitrary"),
        name="ffn_a",
    )(u0, x, g.reshape(1, D_MODEL), mods, mods, wi, wi, wo)


SIDE_SLAB = 64


def _ffn_b_body(h_ref, w_ref, x_ref, g_ref, *rest, n_slab):
    o_ref = rest[-2] if n_slab else rest[-1]
    h = h_ref[...]
    for c in range(0, o_ref.shape[1], MXU_COLS):
        sl = slice(c, c + MXU_COLS)
        o_ref[:, sl] = x_ref[:, sl] + (0.5 * g_ref[:, sl]) * _dot(h, w_ref[:, sl])

    if n_slab:
        side_ref, _, side_out_ref = rest

        @pl.when(pl.program_id(0) * pl.num_programs(1) + pl.program_id(1) < n_slab)
        def _():
            side_out_ref[...] = side_ref[...].astype(BF16)


def _ffn_b(h, wo_bf16, x, mods, gate_idx, n_rows, side=None, l=None):
    tn = 512
    nj = D_MODEL // tn
    gate = pl.BlockSpec((None, None, 1, tn), lambda i, j: (_mod_row(i, TM), gate_idx, 0, j))
    in_specs = [
        pl.BlockSpec((TM, D_FF), lambda i, j: (i, 0)),
        pl.BlockSpec((D_FF, tn), lambda i, j: (0, j)),
        pl.BlockSpec((TM, tn), lambda i, j: (i, j)),
        gate,
    ]
    out_specs = [pl.BlockSpec((TM, tn), lambda i, j: (i, j))]
    out_shape = [jax.ShapeDtypeStruct((n_rows, D_MODEL), F32)]
    args = [h, wo_bf16, x, mods]
    n_slab = 0
    if side is not None:
        _, rows, cols = side.shape
        n_slab = rows // SIDE_SLAB
        assert (n_rows // TM) * nj >= n_slab

        def slab(i, j):
            return jnp.minimum(i * nj + j, n_slab - 1)

        in_specs.append(pl.BlockSpec((None, SIDE_SLAB, cols), lambda i, j: (l, slab(i, j), 0)))
        out_specs.append(pl.BlockSpec((SIDE_SLAB, cols), lambda i, j: (slab(i, j), 0)))
        out_shape.append(jax.ShapeDtypeStruct((rows, cols), BF16))
        args.append(side)
    res = pl.pallas_call(
        functools.partial(_ffn_b_body, n_slab=n_slab),
        grid=(n_rows // TM, nj),
        in_specs=in_specs,
        out_specs=out_specs,
        out_shape=out_shape,
        compiler_params=_cparams("arbitrary", "arbitrary"),
        name="ffn_b",
    )(*args)
    return res if side is not None else res[0]


def _ffn(x, mods, g, wi, wo, l, base, n_rows, side=None):
    h, wo_bf16 = _ffn_a(x, g, mods, base, wi, wo, l, n_rows)
    return _ffn_b(h, wo_bf16, x, mods, base + 2, n_rows, side, l)


PROJ_TM = 1536
PROJ_ROWS = 256
PROJ_SLICES = PROJ_TM // PROJ_ROWS


def _proj_body(u0_ref, xn_ref, g_ref, sh_ref, sc_ref, w_ref, o_ref, uo_ref, u2_ref, r_ref):
    i, j = pl.program_id(0), pl.program_id(1)
    slot = i % 2

    @pl.when((i == 0) & (j == 0))
    def _():
        u2_ref[0] = u0_ref[...]

    @pl.when(j == 0)
    def _():
        uo_ref[...] = u2_ref[slot]

    o_ref[...] = _dot(u2_ref[slot], w_ref[...])

    dst = pl.ds(pl.multiple_of(jnp.minimum(j, PROJ_SLICES - 1) * PROJ_ROWS, PROJ_ROWS), PROJ_ROWS)
    _rms_scale_rows(xn_ref, r_ref, g_ref[...] * (1.0 + sc_ref[...]), sh_ref[...], u2_ref.at[1 - slot, dst, :])


def _proj(x, g, mods, w_in_bf16):
    tn = COL
    ni = MT // PROJ_TM
    nj = IN_WIDTH // tn
    assert nj >= PROJ_SLICES and SEQ % PROJ_ROWS == 0 and N_LAT % PROJ_ROWS == 0

    def nxt_slice(i, j):
        return jnp.minimum(i + 1, ni - 1) * PROJ_SLICES + jnp.minimum(j, PROJ_SLICES - 1)

    def next_mod(k):
        return pl.BlockSpec((None, None, 1, D_MODEL), lambda i, j: (_mod_row(nxt_slice(i, j), PROJ_ROWS), k, 0, 0))

    u0 = _normmod(x, g, mods, 3, PROJ_TM)
    return pl.pallas_call(
        _proj_body,
        grid=(ni, nj),
        in_specs=[
            pl.BlockSpec((PROJ_TM, D_MODEL), lambda i, j: (0, 0)),
            pl.BlockSpec((PROJ_ROWS, D_MODEL), lambda i, j: (nxt_slice(i, j), 0)),
            pl.BlockSpec((1, D_MODEL), lambda i, j: (0, 0)),
            next_mod(3),
            next_mod(4),
            pl.BlockSpec((D_MODEL, tn), lambda i, j: (0, j)),
        ],
        out_specs=[
            pl.BlockSpec((PROJ_TM, tn), lambda i, j: (i, j)),
            pl.BlockSpec((PROJ_TM, D_MODEL), lambda i, j: (i, 0)),
        ],
        out_shape=[
            jax.ShapeDtypeStruct((MT, IN_WIDTH), F32),
            jax.ShapeDtypeStruct((MT, D_MODEL), BF16),
        ],
        scratch_shapes=[pltpu.VMEM((2, PROJ_TM, D_MODEL), BF16), pltpu.VMEM((PROJ_ROWS, 1), F32)],
        compiler_params=_cparams("arbitrary", "arbitrary"),
        name="proj",
    )(u0, x, g.reshape(1, D_MODEL), mods, mods, w_in_bf16)


QKV_TM = 512


def _rope_tables():
    t = jnp.arange(SEQ)
    rows = (t // GRID_W).astype(F32)
    cols = (t % GRID_W).astype(F32)
    freqs = ROPE_THETA ** (-jnp.arange(ROPE_FREQS, dtype=F32) / ROPE_FREQS)
    ar = rows[:, None] * freqs[None, :]
    ac = cols[:, None] * freqs[None, :]
    cos = jnp.concatenate([jnp.cos(ar), jnp.cos(ar), jnp.cos(ac), jnp.cos(ac)], axis=-1)
    sin = jnp.concatenate([-jnp.sin(ar), jnp.sin(ar), -jnp.sin(ac), jnp.sin(ac)], axis=-1)
    cos = jnp.concatenate([cos, jnp.ones((QKV_TM, HEAD_DIM), F32)], axis=0)
    sin = jnp.concatenate([sin, jnp.zeros((QKV_TM, HEAD_DIM), F32)], axis=0)
    return cos, sin


def _lane_matrices():
    lane = jnp.arange(HEAD_DIM)
    partner = jnp.where(lane % (2 * ROPE_FREQS) < ROPE_FREQS, lane + ROPE_FREQS, lane - ROPE_FREQS)
    swap = (lane[:, None] == partner[None, :]).astype(BF16)
    return jnp.ones((HEAD_DIM, HEAD_DIM), BF16), swap


def _dot_split(x, m_ref):
    hi = x.astype(BF16)
    lo = (x - hi.astype(F32)).astype(BF16)
    return _dot(hi, m_ref[...]) + _dot(lo, m_ref[...])


def _qkv_body(q_ref, kv_ref, cos_ref, sin_ref, qn_ref, kn_ref, ones_ref, swap_ref, qo_ref, ko_ref, vo_ref):
    cos = cos_ref[...]
    sin = sin_ref[...]

    def norm_rope(xh, g):
        ss = _dot_split(xh * xh, ones_ref)
        y = xh * lax.rsqrt(ss * (1.0 / HEAD_DIM) + EPS) * g
        return y * cos + _dot_split(y, swap_ref) * sin

    scale = math.log2(math.e) / math.sqrt(HEAD_DIM)
    for h in range(N_Q_HEADS):
        sl = slice(h * HEAD_DIM, (h + 1) * HEAD_DIM)
        qo_ref[:, sl] = (norm_rope(q_ref[:, sl], qn_ref[...]) * scale).astype(BF16)
    for h in range(N_KV_HEADS):
        sl = slice(h * HEAD_DIM, (h + 1) * HEAD_DIM)
        ko_ref[:, sl] = norm_rope(kv_ref[:, sl], kn_ref[...]).astype(BF16)
    kvw = N_KV_HEADS * HEAD_DIM
    for h in range(N_KV_HEADS):
        vo_ref[:, 2 * h * HEAD_DIM:(2 * h + 1) * HEAD_DIM] = (
            kv_ref[:, kvw + h * HEAD_DIM:kvw + (h + 1) * HEAD_DIM].astype(BF16))
        vo_ref[:, (2 * h + 1) * HEAD_DIM:(2 * h + 2) * HEAD_DIM] = jnp.ones((vo_ref.shape[0], HEAD_DIM), BF16)


def _qkv(proj, cos, sin, q_norm, k_norm):
    tm = QKV_TM
    n_lat_tiles = N_LAT // tm
    per_seq = SEQ // tm
    kvw = N_KV_HEADS * HEAD_DIM

    def tab(i):
        return (jnp.where(i < n_lat_tiles, i % per_seq, per_seq), 0)

    return pl.pallas_call(
        _qkv_body,
        grid=(MT // tm,),
        in_specs=[
            pl.BlockSpec((tm, COL), lambda i: (i, OFF_Q // COL)),
            pl.BlockSpec((tm, COL), lambda i: (i, OFF_KV // COL)),
            pl.BlockSpec((tm, HEAD_DIM), tab),
            pl.BlockSpec((tm, HEAD_DIM), tab),
            pl.BlockSpec((1, HEAD_DIM), lambda i: (0, 0)),
            pl.BlockSpec((1, HEAD_DIM), lambda i: (0, 0)),
            pl.BlockSpec((HEAD_DIM, HEAD_DIM), lambda i: (0, 0)),
            pl.BlockSpec((HEAD_DIM, HEAD_DIM), lambda i: (0, 0)),
        ],
        out_specs=[
            pl.BlockSpec((tm, COL), lambda i: (i, 0)),
            pl.BlockSpec((tm, kvw), lambda i: (i, 0)),
            pl.BlockSpec((tm, 2 * kvw), lambda i: (i, 0)),
        ],
        out_shape=[
            jax.ShapeDtypeStruct((MT, COL), BF16),
            jax.ShapeDtypeStruct((MT, kvw), BF16),
            jax.ShapeDtypeStruct((MT, 2 * kvw), BF16),
        ],
        compiler_params=_cparams("arbitrary"),
        name="qkv",
    )(proj, proj, cos, sin, q_norm.reshape(1, HEAD_DIM), k_norm.reshape(1, HEAD_DIM), *_lane_matrices())


def _attn_body(*refs, with_lat):
    if with_lat:
        q_ref, kl_ref, vl_ref, kc_ref, vc_ref, o_ref = refs
    else:
        q_ref, kc_ref, vc_ref, o_ref = refs
    nt = (((1,), (1,)), ((), ()))
    half = q_ref.shape[0] // 2
    units = [(slice(r, r + half), slice(g * HEAD_DIM, (g + 1) * HEAD_DIM))
             for g in range(Q_GROUP) for r in (0, half)]
    scores = []
    for rows, sl in units:
        q = q_ref[rows, sl]
        sc = lax.dot_general(q, kc_ref[...], nt, preferred_element_type=F32)
        s_lat = lax.dot_general(q, kl_ref[...], nt, preferred_element_type=F32) if with_lat else None
        scores.append((sc, s_lat))
    for (rows, sl), (sc, s_lat) in zip(units, scores):
        m = jnp.max(sc, axis=-1, keepdims=True)
        if with_lat:
            m = jnp.maximum(m, jnp.max(s_lat, axis=-1, keepdims=True))
        o = _dot(jnp.exp2(sc - m).astype(BF16), vc_ref[...])
        if with_lat:
            o = o + _dot(jnp.exp2(s_lat - m).astype(BF16), vl_ref[...])
        o_ref[rows, sl] = (o[:, :HEAD_DIM] / o[:, HEAD_DIM:]).astype(BF16)


def _attn_lat(q, k, v, bb):
    tq = 512
    nq = SEQ // tq
    ctx_blk = N_LAT // CTX_LEN
    gw = Q_GROUP * HEAD_DIM
    return pl.pallas_call(
        _keep_branch_buffer(functools.partial(_attn_body, with_lat=True), 5),
        grid=(BATCH, N_KV_HEADS, nq),
        in_specs=[
            pl.BlockSpec((tq, gw), lambda b, h, i: (b * nq + i, h)),
            pl.BlockSpec((SEQ, HEAD_DIM), lambda b, h, i: (b, h)),
            pl.BlockSpec((SEQ, 2 * HEAD_DIM), lambda b, h, i: (b, h)),
            pl.BlockSpec((CTX_LEN, HEAD_DIM), lambda b, h, i: (ctx_blk + b, h)),
            pl.BlockSpec((CTX_LEN, 2 * HEAD_DIM), lambda b, h, i: (ctx_blk + b, h)),
            _ANY,
        ],
        out_specs=pl.BlockSpec((None, tq, gw), lambda b, h, i: (SLOT_ATTN, b * nq + i, h)),
        out_shape=jax.ShapeDtypeStruct(bb.shape, bb.dtype),
        input_output_aliases={5: 0},
        compiler_params=_cparams("arbitrary", "arbitrary", "arbitrary"),
        name="attn_lat",
    )(q, k, v, k, v, bb)


def _attn_ctx(q, k, v, bb):
    ctx_blk = N_LAT // CTX_LEN
    gw = Q_GROUP * HEAD_DIM
    return pl.pallas_call(
        _keep_branch_buffer(functools.partial(_attn_body, with_lat=False), 3),
        grid=(BATCH, N_KV_HEADS),
        in_specs=[
            pl.BlockSpec((CTX_LEN, gw), lambda b, h: (ctx_blk + b, h)),
            pl.BlockSpec((CTX_LEN, HEAD_DIM), lambda b, h: (ctx_blk + b, h)),
            pl.BlockSpec((CTX_LEN, 2 * HEAD_DIM), lambda b, h: (ctx_blk + b, h)),
            _ANY,
        ],
        out_specs=pl.BlockSpec((None, CTX_LEN, gw), lambda b, h: (SLOT_ATTN, ctx_blk + b, h)),
        out_shape=jax.ShapeDtypeStruct(bb.shape, bb.dtype),
        input_output_aliases={3: 0},
        compiler_params=_cparams("arbitrary", "arbitrary"),
        name="attn_ctx",
    )(q, k, v, bb)


POOL_PAD = 8


def _pool_body(a_ref, w_ref, s_ref, o_ref, *, seq):
    lp = seq + 2 * POOL_PAD
    t = lax.broadcasted_iota(jnp.int32, (seq, LANE), 0)
    zpad = jnp.zeros((POOL_PAD, LANE), F32)
    for gi, win in enumerate(POOL_WINDOWS):
        sl = slice(gi * LANE, (gi + 1) * LANE)
        a = a_ref[:, sl]
        s = jnp.concatenate([zpad, a, zpad], axis=0)
        s = s + pltpu.roll(s, 1, 0)
        half = 1
        while 2 * half < win:
            s = pltpu.roll(s, half, 0) + pltpu.roll(s, lp - half, 0)
            half *= 2
        s = s[POOL_PAD:POOL_PAD + seq]
        lo = jnp.maximum(t - win // 2, 0)
        hi = jnp.minimum(t + win // 2, seq)
        pooled = s / (hi - lo).astype(F32) - a
        y = _dot(pooled.astype(BF16), w_ref[gi].astype(BF16))
        o_ref[:, sl] = (y * s_ref[:, sl]).astype(BF16)


def _pool(proj, pool_w, pool_scale, bb, seq, row_blk0):
    width = len(POOL_WINDOWS) * LANE
    return pl.pallas_call(
        _keep_branch_buffer(functools.partial(_pool_body, seq=seq), 3),
        grid=(BATCH,),
        in_specs=[
            pl.BlockSpec((seq, width), lambda b: (row_blk0 + b, 0)),
            pl.BlockSpec((len(POOL_WINDOWS), LANE, LANE), lambda b: (0, 0, 0)),
            pl.BlockSpec((1, width), lambda b: (0, 0)),
            _ANY,
        ],
        out_specs=pl.BlockSpec((None, seq, width), lambda b: (SLOT_POOL, row_blk0 + b, 0)),
        out_shape=jax.ShapeDtypeStruct(bb.shape, bb.dtype),
        input_output_aliases={3: 0},
        compiler_params=_cparams("arbitrary"),
        name="pool",
    )(proj, pool_w, pool_scale.reshape(1, width), bb)


def _hy_prep_body(x0_ref, x1_ref, v_ref, w0_ref, w1_ref, wv_ref, b0_ref, b1_ref, bv_ref,
                  x0o_ref, vx_ref, vb_ref, *, seq):
    t = lax.broadcasted_iota(jnp.int32, x0_ref.shape, 0)

    def conv(x_ref, w_ref, b_ref):
        x = x_ref[...]
        prev = jnp.where(t >= 1, pltpu.roll(x, 1, 0), 0.0)
        nxt = jnp.where(t <= seq - 2, pltpu.roll(x, seq - 1, 0), 0.0)
        return prev * w_ref[0:1, :] + x * w_ref[1:2, :] + nxt * w_ref[2:3, :] + b_ref[...]

    x0o_ref[...] = conv(x0_ref, w0_ref, b0_ref)
    vx = conv(v_ref, wv_ref, bv_ref) * conv(x1_ref, w1_ref, b1_ref)
    vx_ref[...] = vx
    vb_ref[...] = vx.astype(BF16)


def _hy_prep(proj, short_w, short_b, seq, row_blk0):
    tc = HYENA_WIDTH
    nc = HYENA_WIDTH // tc
    c0 = OFF_HY // tc
    short_b = short_b.reshape(1, 3 * HYENA_WIDTH)

    def xspec(part):
        return pl.BlockSpec((seq, tc), lambda b, c: (row_blk0 + b, c0 + part * nc + c))

    def wspec(part, rows):
        return pl.BlockSpec((rows, tc), lambda b, c: (0, part * nc + c))

    return pl.pallas_call(
        functools.partial(_hy_prep_body, seq=seq),
        grid=(BATCH, nc),
        in_specs=[xspec(0), xspec(1), xspec(2), wspec(0, 3), wspec(1, 3), wspec(2, 3),
                  wspec(0, 1), wspec(1, 1), wspec(2, 1)],
        out_specs=[
            pl.BlockSpec((seq, tc), lambda b, c: (b, c)),
            pl.BlockSpec((seq, tc), lambda b, c: (b, c)),
            pl.BlockSpec((seq, tc), lambda b, c: (0, b * nc + c)),
        ],
        out_shape=[
            jax.ShapeDtypeStruct((BATCH * seq, HYENA_WIDTH), F32),
            jax.ShapeDtypeStruct((BATCH * seq, HYENA_WIDTH), F32),
            jax.ShapeDtypeStruct((seq, BATCH * HYENA_WIDTH), BF16),
        ],
        compiler_params=_cparams("arbitrary", "arbitrary"),
        name="hy_prep",
    )(proj, proj, proj, short_w, short_w, short_w, short_b, short_b, short_b)


def _hy_feats(seq):
    t = jnp.linspace(0.0, 1.0, seq, dtype=F32)[:, None]
    f = jnp.linspace(1e-4, HYENA_BANDS - 1, HYENA_BANDS, dtype=F32)
    w = 2.0 * math.pi * jnp.arange(seq, dtype=F32) / seq
    fw = w[:, None] * f[None, :]
    z = jnp.concatenate([t, jnp.cos(fw), -jnp.sin(fw)], axis=-1)
    return jnp.pad(z, ((0, 0), (0, LANE - HYENA_EMB)))


def _hy_deltas():
    max_decay = math.log(HYENA_TARGET) / HYENA_FAST_PCT
    min_decay = math.log(HYENA_TARGET) / HYENA_SLOW_PCT
    return jnp.abs(jnp.linspace(min_decay, max_decay, HYENA_WIDTH, dtype=F32)).reshape(1, HYENA_WIDTH)


def _hy_filter_body(z_ref, w1_ref, b1_ref, w2_ref, b2_ref, w3_ref, fr_ref, dl_ref, k_ref, nyq_ref, *, seq):
    hp = lax.Precision.HIGHEST
    freq = fr_ref[...]
    h = jnp.sin(freq * (jnp.dot(z_ref[...], w1_ref[...], precision=hp, preferred_element_type=F32) + b1_ref[...]))
    h = jnp.sin(freq * (jnp.dot(h, w2_ref[...], precision=hp, preferred_element_type=F32) + b2_ref[...]))
    h = jnp.dot(h, w3_ref[...], precision=hp, preferred_element_type=F32)
    ti = lax.broadcasted_iota(jnp.int32, (seq, HYENA_WIDTH), 0)
    decay = jnp.exp(-(ti.astype(F32) * (1.0 / (seq - 1))) * dl_ref[...])
    hf = h[:, :HYENA_WIDTH] * decay
    hb = jnp.where(ti == 0, 0.0, h[:, HYENA_WIDTH:] * decay)
    ks = hf + hb
    k_ref[:, :HYENA_WIDTH] = ks.astype(BF16)
    k_ref[:, HYENA_WIDTH:] = (hf - hb).astype(BF16)
    nyq = jnp.sum(jnp.where(ti % 2 == 0, ks, -ks), axis=0, keepdims=True)
    nyq_ref[...] = jnp.broadcast_to(nyq, nyq_ref.shape)


def _hy_filter(lp, z, deltas, seq):
    w1 = jnp.pad(lp['hy_f1_w'], ((0, LANE - HYENA_EMB), (0, 0)))
    args = (z, w1, lp['hy_f1_b'].reshape(1, -1), lp['hy_f2_w'], lp['hy_f2_b'].reshape(1, -1), lp['hy_f3_w'],
            lp['hy_freq'].reshape(1, -1), deltas)
    full = lambda a: pl.BlockSpec(a.shape, lambda i: (0,) * a.ndim)
    return pl.pallas_call(
        functools.partial(_hy_filter_body, seq=seq),
        grid=(1,),
        in_specs=[full(a) for a in args],
        out_specs=[pl.BlockSpec((seq, 2 * HYENA_WIDTH), lambda i: (0, 0)),
                   pl.BlockSpec((8, HYENA_WIDTH), lambda i: (0, 0))],
        out_shape=[jax.ShapeDtypeStruct((seq, 2 * HYENA_WIDTH), BF16),
                   jax.ShapeDtypeStruct((8, HYENA_WIDTH), F32)],
        compiler_params=_cparams("arbitrary"),
        name="hy_filter",
    )(*args)


DFT_SPLIT = 64


def _dft_matrices(seq):
    n = 2 * seq
    f = jnp.arange(seq, dtype=jnp.int32)[:, None]

    def table(step, count):
        idx = (f * (jnp.arange(count, dtype=jnp.int32)[None, :] * step)) % n
        ang = idx.astype(F32) * (2.0 * math.pi / n)
        return jnp.cos(ang), jnp.sin(ang)

    hc, hs = table(DFT_SPLIT, seq // DFT_SPLIT)
    lc, ls = table(1, DFT_SPLIT)
    cos = (hc[:, :, None] * lc[:, None, :] - hs[:, :, None] * ls[:, None, :]).reshape(seq, seq)
    sin = (hs[:, :, None] * lc[:, None, :] + hc[:, :, None] * ls[:, None, :]).reshape(seq, seq)
    s = jnp.arange(seq, dtype=jnp.int32)[None, :]
    nyq = jnp.where(s % 2 == 0, 1.0, -1.0).astype(F32)
    wf = jnp.concatenate([cos, jnp.where(f == 0, nyq, -sin)], axis=0).astype(BF16)
    return wf, wf.T


def _mm_body(a_ref, b_ref, o_ref):
    o_ref[...] = _dot(a_ref[...], b_ref[...])


def _dft_fwd_body(wc_ref, ws_ref, x_ref, hr_ref, hi_ref, nyq_ref, pr_ref, pi_ref, *, n):
    tf = wc_ref.shape[0]
    freq = lax.broadcasted_iota(jnp.int32, (tf, MXU_COLS), 0) + pl.program_id(0) * tf
    first = freq == 0
    w = jnp.where(first, 1.0 / n, 2.0 / n)
    chunks = [slice(c, c + MXU_COLS) for c in range(0, x_ref.shape[1], MXU_COLS)]
    spectra = [(_dot(wc_ref[...], x_ref[:, sl]), _dot(ws_ref[...], x_ref[:, sl])) for sl in chunks]
    for sl, (xr, xi) in zip(chunks, spectra):
        hr = hr_ref[:, sl]
        hi = hi_ref[:, sl]
        pr_ref[:, sl] = (jnp.where(first, xr * hr, xr * hr - xi * hi) * w).astype(BF16)
        pi_ref[:, sl] = (jnp.where(first, xi * nyq_ref[0:1, sl], xr * hi + xi * hr) * w).astype(BF16)


def _dft_fwd(wf, x, hf, nyq):
    n, seq = wf.shape
    tf = min(seq, 512)
    nf = seq // tf
    out = jax.ShapeDtypeStruct((seq, BATCH * HYENA_WIDTH), BF16)
    return pl.pallas_call(
        functools.partial(_dft_fwd_body, n=n),
        grid=(nf, BATCH),
        in_specs=[
            pl.BlockSpec((tf, seq), lambda f, b: (f, 0)),
            pl.BlockSpec((tf, seq), lambda f, b: (nf + f, 0)),
            pl.BlockSpec((seq, HYENA_WIDTH), lambda f, b: (0, b)),
            pl.BlockSpec((tf, HYENA_WIDTH), lambda f, b: (f, 0)),
            pl.BlockSpec((tf, HYENA_WIDTH), lambda f, b: (nf + f, 0)),
            pl.BlockSpec((8, HYENA_WIDTH), lambda f, b: (0, 0)),
        ],
        out_specs=[pl.BlockSpec((tf, HYENA_WIDTH), lambda f, b: (f, b))] * 2,
        out_shape=[out, out],
        compiler_params=_cparams("arbitrary", "arbitrary"),
        name="dft_fwd",
    )(wf, wf, x, hf, hf, nyq)


def _dft_filter(wf, k):
    n, seq = wf.shape
    tm = min(seq, 1024)
    return pl.pallas_call(
        _mm_body,
        grid=(n // tm,),
        in_specs=[
            pl.BlockSpec((tm, seq), lambda i: (i, 0)),
            pl.BlockSpec((seq, HYENA_WIDTH), lambda i: (0, (i * tm) // seq)),
        ],
        out_specs=pl.BlockSpec((tm, HYENA_WIDTH), lambda i: (i, 0)),
        out_shape=jax.ShapeDtypeStruct((n, HYENA_WIDTH), F32),
        compiler_params=_cparams("arbitrary"),
        name="dft_filter",
    )(wf, k)


def _dft_inv_body(wc_ref, ws_ref, pr_ref, pi_ref, vx_ref, bias_ref, x0_ref, o_ref):
    y = _dot(wc_ref[...], pr_ref[...]) + _dot(ws_ref[...], pi_ref[...])
    o_ref[...] = ((y + vx_ref[...] * bias_ref[...]) * x0_ref[...]).astype(BF16)


def _dft_inv(wi, p_re, p_im, vx, bias, x0, bb, row0):
    seq, n = wi.shape
    tm = min(seq, 1024)
    tn = HYENA_WIDTH
    nt = seq // tm
    return pl.pallas_call(
        _keep_branch_buffer(_dft_inv_body, 7),
        grid=(nt, BATCH),
        in_specs=[
            pl.BlockSpec((tm, seq), lambda i, b: (i, 0)),
            pl.BlockSpec((tm, seq), lambda i, b: (i, 1)),
            pl.BlockSpec((seq, tn), lambda i, b: (0, b)),
            pl.BlockSpec((seq, tn), lambda i, b: (0, b)),
            pl.BlockSpec((tm, tn), lambda i, b: (b * nt + i, 0)),
            pl.BlockSpec((1, tn), lambda i, b: (0, 0)),
            pl.BlockSpec((tm, tn), lambda i, b: (b * nt + i, 0)),
            _ANY,
        ],
        out_specs=pl.BlockSpec((None, tm, tn), lambda i, b: (SLOT_HYENA, row0 // tm + b * nt + i, 0)),
        out_shape=jax.ShapeDtypeStruct(bb.shape, bb.dtype),
        input_output_aliases={7: 0},
        compiler_params=_cparams("arbitrary", "arbitrary"),
        name="dft_inv",
    )(wi, wi, p_re, p_im, vx, bias.reshape(1, tn), x0, bb)


def _hyena(proj, lp, consts, bb, seq, row0):
    z, deltas, (wf, wi) = consts
    x0, vx, vb = _hy_prep(proj, lp['hy_short_w'], lp['hy_short_b'], seq, row0 // seq)
    k, nyq = _hy_filter(lp, z, deltas, seq)
    hf = _dft_filter(wf, k)
    p_re, p_im = _dft_fwd(wf, vb, hf, nyq)
    return _dft_inv(wi, p_re, p_im, vx, lp['hy_bias'], x0, bb, row0)


def _s5_params(lp, n_seg_steps):
    a_re, a_im = lp['s5_a_re'], lp['s5_a_im']
    dt = jnp.exp(lp['s5_log_dt'])[..., None]
    mag = jnp.exp(a_re * dt)
    ab_re, ab_im = mag * jnp.cos(a_im * dt), mag * jnp.sin(a_im * dt)
    den = a_re * a_re + a_im * a_im
    nr, ni = ab_re - 1.0, ab_im
    cf_re = (nr * a_re + ni * a_im) / den
    cf_im = (ni * a_re - nr * a_im) / den
    b_re, b_im = lp['s5_b_re'], lp['s5_b_im']
    bb_re = cf_re[..., None] * b_re - cf_im[..., None] * b_im
    bb_im = cf_re[..., None] * b_im + cf_im[..., None] * b_re
    eye = jnp.eye(S5_CHUNK_GROUPS, dtype=F32)

    def bdiag_in(m):
        m = m.reshape(2, S5_NCHUNK, S5_CHUNK_GROUPS, S5_STATE, S5_GC)
        return jnp.einsum('dqgpc,gh->dqgchp', m, eye).reshape(2, S5_NCHUNK, S5_CHUNK_CH, S5_CHUNK_ST)

    def bdiag_out(m):
        m = m.reshape(2, S5_NCHUNK, S5_CHUNK_GROUPS, S5_GC, S5_STATE)
        return jnp.einsum('dqgcp,gh->dqhpgc', m, eye).reshape(2, S5_NCHUNK, S5_CHUNK_ST, S5_CHUNK_CH)

    bbd = jnp.concatenate([bdiag_in(bb_re), bdiag_in(bb_im)], axis=-1).astype(BF16)
    cbd = jnp.concatenate([bdiag_out(lp['s5_c_re']), -bdiag_out(lp['s5_c_im'])], axis=-2).astype(BF16)
    a = jnp.stack([ab_re.reshape(2, -1), ab_im.reshape(2, -1)], axis=1)
    aks = []
    for steps in n_seg_steps:
        pr, pi = ab_re, ab_im
        for _ in range(int(math.log2(steps))):
            pr, pi = pr * pr - pi * pi, 2.0 * pr * pi
        aks.append(jnp.stack([pr.reshape(2, -1), pi.reshape(2, -1)], axis=1))
    return [(bbd, cbd, a, ak) for ak in aks]


S5_BLOCK = 2048


def _s5_scan_body(u_ref, bbd_ref, cbd_ref, a_ref, ak_ref, dsk_ref, h0_ref, y_ref, hl_ref,
                  xr0_ref, xr1_ref, xi0_ref, xi1_ref, yd0_ref, yd1_ref, up_ref, ub_ref, *, seq):
    nk = seq // S5_LANES
    blk = min(seq, S5_BLOCK)
    nblk = seq // blk
    tiles = blk // S5_LANES
    shape = (S5_LANES, S5_CHUNK_ST)
    xr_ref, xi_ref, yd_ref = (xr0_ref, xr1_ref), (xi0_ref, xi1_ref), (yd0_ref, yd1_ref)
    for j in range(S5_LANES):
        up_ref[pl.ds(j, nk, stride=S5_LANES), :] = u_ref[pl.ds(j * nk, nk), :]
    ub_ref[...] = up_ref[...].astype(BF16)
    row = lax.broadcasted_iota(jnp.int32, shape, 0)
    zero = jnp.zeros(shape, F32)

    def block(r):
        return pl.ds(pl.multiple_of(r * blk, blk), blk)

    def in_proj(d, rows):
        xr_ref[d][rows, :] = _dot(ub_ref[rows, :], bbd_ref[d, :, :S5_CHUNK_ST])
        xi_ref[d][rows, :] = _dot(ub_ref[rows, :], bbd_ref[d, :, S5_CHUNK_ST:])

    def out_proj(d, rows):
        yd_ref[d][rows, :] = (_dot(xr_ref[d][rows, :].astype(BF16), cbd_ref[d, :S5_CHUNK_ST, :])
                              + _dot(xi_ref[d][rows, :].astype(BF16), cbd_ref[d, S5_CHUNK_ST:, :]))

    def coeffs(d):
        return jnp.broadcast_to(a_ref[d, 0:1, :], shape), jnp.broadcast_to(a_ref[d, 1:2, :], shape)

    def block_tiles(d, r):
        base = pl.multiple_of(r * blk, blk)
        order = range(tiles) if d == 0 else range(tiles - 1, -1, -1)
        return [pl.ds(base + t * S5_LANES, S5_LANES) for t in order]

    def scan_block(d, r, carry):
        ar, ai = coeffs(d)
        xr, xi = carry
        for rows in block_tiles(d, r):
            xr, xi = (ar * xr - ai * xi + xr_ref[d][rows, :], ar * xi + ai * xr + xi_ref[d][rows, :])
            xr_ref[d][rows, :] = xr
            xi_ref[d][rows, :] = xi
        return xr, xi

    def fix_block(d, r, carry):
        ar, ai = coeffs(d)
        gr, gi = carry
        for rows in block_tiles(d, r):
            gr, gi = ar * gr - ai * gi, ar * gi + ai * gr
            xr_ref[d][rows, :] += gr
            xi_ref[d][rows, :] += gi
        return gr, gi

    def visit(d, s):
        return s if d == 0 else nblk - 1 - s

    def entering_states(d, er, ei):
        akr, aki = ak_ref[d, 0:1, :], ak_ref[d, 1:2, :]
        hr, hi = h0_ref[2 * d:2 * d + 1, :], h0_ref[2 * d + 1:2 * d + 2, :]
        in_r, in_i = zero, zero
        for j in (range(S5_LANES) if d == 0 else range(S5_LANES - 1, -1, -1)):
            in_r = jnp.where(row == j, hr, in_r)
            in_i = jnp.where(row == j, hi, in_i)
            hr, hi = (akr * hr - aki * hi + er[j:j + 1, :], akr * hi + aki * hr + ei[j:j + 1, :])
        hl_ref[2 * d:2 * d + 1, :] = hr
        hl_ref[2 * d + 1:2 * d + 2, :] = hi
        return in_r, in_i

    def fix_all(d, ins):
        lax.fori_loop(0, nblk, lambda s, carry: fix_block(d, visit(d, s), carry), ins)

    in_proj(0, slice(None))

    def scan0_body(s, carry):
        in_proj(1, block(s))
        return scan_block(0, s, carry)
    ends0 = lax.fori_loop(0, nblk, scan0_body, (zero, zero))
    fix_all(0, entering_states(0, *ends0))

    def scan1_body(s, carry):
        out_proj(0, block(s))
        return scan_block(1, visit(1, s), carry)
    ends1 = lax.fori_loop(0, nblk, scan1_body, (zero, zero))
    fix_all(1, entering_states(1, *ends1))
    out_proj(1, slice(None))

    up_ref[...] = yd_ref[0][...] + yd_ref[1][...] + up_ref[...] * (dsk_ref[0] + dsk_ref[1])
    for j in range(S5_LANES):
        y_ref[pl.ds(j * nk, nk), :] = up_ref[pl.ds(j, nk, stride=S5_LANES), :]


def _s5_scan(proj, params, dskip, h0, seq, row_blk0):
    bbd, cbd, a, ak = params
    nq = S5_NCHUNK
    col0 = OFF_S5 // S5_CHUNK_CH
    return pl.pallas_call(
        functools.partial(_s5_scan_body, seq=seq),
        grid=(BATCH, nq),
        in_specs=[
            pl.BlockSpec((seq, S5_CHUNK_CH), lambda b, q: (row_blk0 + b, col0 + q)),
            pl.BlockSpec((2, None, S5_CHUNK_CH, 2 * S5_CHUNK_ST), lambda b, q: (0, q, 0, 0)),
            pl.BlockSpec((2, None, 2 * S5_CHUNK_ST, S5_CHUNK_CH), lambda b, q: (0, q, 0, 0)),
            pl.BlockSpec((2, 2, S5_CHUNK_ST), lambda b, q: (0, 0, q)),
            pl.BlockSpec((2, 2, S5_CHUNK_ST), lambda b, q: (0, 0, q)),
            pl.BlockSpec((2, 1, S5_CHUNK_CH), lambda b, q: (0, 0, q)),
            pl.BlockSpec((None, 4, S5_CHUNK_ST), lambda b, q: (b, 0, q)),
        ],
        out_specs=[
            pl.BlockSpec((seq, S5_CHUNK_CH), lambda b, q: (b, q)),
            pl.BlockSpec((None, 4, S5_CHUNK_ST), lambda b, q: (b, 0, q)),
        ],
        out_shape=[
            jax.ShapeDtypeStruct((BATCH * seq, S5_WIDTH), F32),
            jax.ShapeDtypeStruct((BATCH, 4, S5_GROUPS * S5_STATE), F32),
        ],
        scratch_shapes=[pltpu.VMEM((seq, S5_CHUNK_ST), F32)] * 4 + [pltpu.VMEM((seq, S5_CHUNK_CH), F32)] * 3
        + [pltpu.VMEM((seq, S5_CHUNK_CH), BF16)],
        compiler_params=_cparams("arbitrary", "arbitrary"),
        name="s5_scan",
    )(proj, bbd, cbd, a, ak, dskip.reshape(2, 1, S5_WIDTH), h0)


def _s5_glu_body(y_ref, w_ref, b_ref, o_ref):
    w = w_ref[...].astype(BF16)
    half = y_ref.shape[0] // 2
    halves = [slice(0, half), slice(half, 2 * half)]
    gs = [_dot(jax.nn.gelu(y_ref[rows, :]).astype(BF16), w) + b_ref[...] for rows in halves]
    for rows, g in zip(halves, gs):
        o_ref[rows, :] = (g[:, :S5_WIDTH] * jax.nn.sigmoid(g[:, S5_WIDTH:])).astype(BF16)


def _s5_glu(y, w, b, bb, row0):
    n_rows = y.shape[0]
    tm = TM_EW
    return pl.pallas_call(
        _keep_branch_buffer(_s5_glu_body, 3),
        grid=(n_rows // tm,),
        in_specs=[
            pl.BlockSpec((tm, S5_WIDTH), lambda i: (i, 0)),
            pl.BlockSpec((S5_WIDTH, 2 * S5_WIDTH), lambda i: (0, 0)),
            pl.BlockSpec((1, 2 * S5_WIDTH), lambda i: (0, 0)),
            _ANY,
        ],
        out_specs=pl.BlockSpec((None, tm, S5_WIDTH), lambda i: (SLOT_S5, row0 // tm + i, 0)),
        out_shape=jax.ShapeDtypeStruct(bb.shape, bb.dtype),
        input_output_aliases={3: 0},
        compiler_params=_cparams("arbitrary"),
        name="s5_glu",
    )(y, w, b.reshape(1, -1), bb)


def _merge_body(u_ref, *refs):
    wg_refs, bg_refs, y_refs, wb_refs = (refs[k * N_BRANCH:(k + 1) * N_BRANCH] for k in range(4))
    o_ref = refs[4 * N_BRANCH]
    u = u_ref[...]
    acc = None
    for n in range(N_BRANCH):
        gate = jax.nn.sigmoid(_dot(u, wg_refs[n][...].astype(BF16)) + bg_refs[n][...])
        contrib = gate * _dot(y_refs[n][...], wb_refs[n][...].astype(BF16))
        acc = contrib if acc is None else acc + contrib
    o_ref[...] = acc.astype(BF16)


def _merge(u, w_gate, b_gate, branches, w_branch, l, n_rows):
    tc = MXU_COLS
    ncol = D_MODEL // tc
    b_gate = b_gate.reshape(1, -1)
    per_branch = lambda make: [make(n) for n in range(N_BRANCH)]
    return pl.pallas_call(
        _merge_body,
        grid=(n_rows // TM, ncol),
        in_specs=[pl.BlockSpec((TM, D_MODEL), lambda i, c: (i, 0))]
        + per_branch(lambda n: pl.BlockSpec((None, D_MODEL, tc), lambda i, c: (l, 0, n * ncol + c)))
        + per_branch(lambda n: pl.BlockSpec((1, tc), lambda i, c: (0, n * ncol + c)))
        + per_branch(lambda n: pl.BlockSpec((None, TM, BRANCH_WIDTH), lambda i, c: (n, i, 0)))
        + per_branch(lambda n: pl.BlockSpec((None, None, BRANCH_WIDTH, tc), lambda i, c: (l, n, 0, c))),
        out_specs=pl.BlockSpec((TM, tc), lambda i, c: (i, c)),
        out_shape=jax.ShapeDtypeStruct((n_rows, D_MODEL), BF16),
        compiler_params=_cparams("arbitrary", "arbitrary"),
        name="merge",
    )(u, *([w_gate] * N_BRANCH), *([b_gate] * N_BRANCH), *([branches] * N_BRANCH), *([w_branch] * N_BRANCH))


def _out_body(m_ref, w_ref, x_ref, g_ref, o_ref):
    o_ref[...] = x_ref[...] + g_ref[...] * _dot(m_ref[...], w_ref[...].astype(BF16))


def _out_proj(merged, w_out, l, x, mods, n_rows):
    tn = 1024
    gate = pl.BlockSpec((None, None, 1, tn), lambda j, i: (_mod_row(i, TM), 5, 0, j))
    return pl.pallas_call(
        _out_body,
        grid=(D_MODEL // tn, n_rows // TM),
        in_specs=[
            pl.BlockSpec((TM, D_MODEL), lambda j, i: (i, 0)),
            pl.BlockSpec((None, D_MODEL, tn), lambda j, i: (l, 0, j)),
            pl.BlockSpec((TM, tn), lambda j, i: (i, j)),
            gate,
        ],
        out_specs=pl.BlockSpec((TM, tn), lambda j, i: (i, j)),
        out_shape=jax.ShapeDtypeStruct((n_rows, D_MODEL), F32),
        compiler_params=_cparams("arbitrary", "arbitrary"),
        name="out_proj",
    )(merged, w_out, x, mods)


def _mixer(x, g, lp, big, w_in_bf16, l, mods, consts, bb, last):
    cos, sin, hy_lat, hy_ctx = consts
    proj, u = _proj(x, g, mods, w_in_bf16)
    q, k, v = _qkv(proj, cos, sin, lp['q_norm'], lp['k_norm'])

    ctx_blk = N_LAT // CTX_LEN
    zero_h = jnp.zeros((BATCH, 4, S5_GROUPS * S5_STATE), F32)
    par_ctx, par_lat = _s5_params(lp, (CTX_LEN // S5_LANES, SEQ // S5_LANES))
    ys_ctx, h_ctx = _s5_scan(proj, par_ctx, lp['s5_d'], zero_h, CTX_LEN, ctx_blk)
    ys_lat, _ = _s5_scan(proj, par_lat, lp['s5_d'], h_ctx, SEQ, 0)

    bb = _s5_glu(ys_lat, lp['s5_glu_w'], lp['s5_glu_b'], bb, 0)
    bb = _attn_lat(q, k, v, bb)
    bb = _pool(proj, lp['pool_w'], lp['pool_scale'], bb, SEQ, 0)
    bb = _hyena(proj, lp, hy_lat, bb, SEQ, 0)
    if not last:
        bb = _s5_glu(ys_ctx, lp['s5_glu_w'], lp['s5_glu_b'], bb, N_LAT)
        bb = _attn_ctx(q, k, v, bb)
        bb = _pool(proj, lp['pool_w'], lp['pool_scale'], bb, CTX_LEN, ctx_blk)
        bb = _hyena(proj, lp, hy_ctx, bb, CTX_LEN, N_LAT)

    n_rows = N_LAT if last else MT
    merged = _merge(u, big['w_gate'], lp['b_gate'], bb, big['w_branch'], l, n_rows)
    return _out_proj(merged, big['w_out'], l, x, mods, n_rows), bb


def kernel(x, c, ctx, c_ctx, w_ada, b_ada, norm_ffn1, norm_mix, norm_ffn2, norm_final, ffn1_wi, ffn1_wo, ffn2_wi, ffn2_wo, w_in, w_gate, b_gate, w_branch, w_out, pool_w, pool_scale, q_norm, k_norm, hy_short_w, hy_short_b, hy_f1_w, hy_f1_b, hy_f2_w, hy_f2_b, hy_f3_w, hy_freq, hy_bias, s5_a_re, s5_a_im, s5_log_dt, s5_b_re, s5_b_im, s5_c_re, s5_c_im, s5_d, s5_glu_w, s5_glu_b):
    big = dict(w_gate=w_gate, w_branch=w_branch, w_out=w_out)
    per_layer = dict(
        b_gate=b_gate, pool_w=pool_w,
        pool_scale=pool_scale, q_norm=q_norm, k_norm=k_norm, hy_short_w=hy_short_w, hy_short_b=hy_short_b,
        hy_f1_w=hy_f1_w, hy_f1_b=hy_f1_b, hy_f2_w=hy_f2_w, hy_f2_b=hy_f2_b, hy_f3_w=hy_f3_w, hy_freq=hy_freq,
        hy_bias=hy_bias, s5_a_re=s5_a_re, s5_a_im=s5_a_im, s5_log_dt=s5_log_dt, s5_b_re=s5_b_re, s5_b_im=s5_b_im,
        s5_c_re=s5_c_re, s5_c_im=s5_c_im, s5_d=s5_d, s5_glu_w=s5_glu_w, s5_glu_b=s5_glu_b)

    cos, sin = _rope_tables()
    deltas = _hy_deltas()
    consts = (cos, sin,
              (_hy_feats(SEQ), deltas, _dft_matrices(SEQ)),
              (_hy_feats(CTX_LEN), deltas, _dft_matrices(CTX_LEN)))

    cc = jnp.concatenate([c, c_ctx[None], jnp.zeros((8 - BATCH - 1, D_MODEL), F32)], axis=0)
    mods_all = _ada(cc, w_ada, b_ada).reshape(DEPTH, 8, N_MOD, 1, D_MODEL)

    xs = jnp.concatenate([x.reshape(N_LAT, D_MODEL), ctx.reshape(N_CTX, D_MODEL)], axis=0)
    bb = jnp.zeros((N_BRANCH, MT, BRANCH_WIDTH), BF16)
    for l in range(DEPTH):
        last = l == DEPTH - 1
        lp = {name: w[l] for name, w in per_layer.items()}
        mods = mods_all[l]
        xs, w_in_bf16 = _ffn(xs, mods, norm_ffn1[l], ffn1_wi, ffn1_wo, l, 0, MT, side=w_in)
        xs, bb = _mixer(xs, norm_mix[l], lp, big, w_in_bf16, l, mods, consts, bb, last)
        xs = _ffn(xs, mods, norm_ffn2[l], ffn2_wi, ffn2_wo, l, 6, N_LAT if last else MT)
    return _final_norm(xs, norm_final, N_LAT).reshape(BATCH, SEQ, D_MODEL)
```

```python
import functools
import math

import jax
import jax.numpy as jnp
from jax import lax
from jax.experimental import pallas as pl
from jax.experimental.pallas import tpu as pltpu

F32 = jnp.float32
BF16 = jnp.bfloat16

D_MODEL = 2048
BATCH = 4
SEQ = 2048
DEPTH = 4
GRID_W = 64
CTX_LEN = 256
D_FF = 5632
N_MOD = 9
EPS = 1e-6

POOL_WINDOWS = (2, 4, 8, 16)
HEAD_DIM = 128
N_Q_HEADS = 4
N_KV_HEADS = 2
Q_GROUP = N_Q_HEADS // N_KV_HEADS
ROPE_THETA = 10000.0
ROPE_FREQS = HEAD_DIM // 4

HYENA_WIDTH = 512
HYENA_EMB = 33
HYENA_BANDS = (HYENA_EMB - 1) // 2
HYENA_HIDDEN = 64
HYENA_TARGET = 1e-2
HYENA_FAST_PCT = 0.3
HYENA_SLOW_PCT = 1.5

S5_WIDTH = 512
S5_GC = 16
S5_GROUPS = S5_WIDTH // S5_GC
S5_STATE = 64
S5_LANES = 8
S5_CHUNK_GROUPS = 8
S5_CHUNK_CH = S5_CHUNK_GROUPS * S5_GC
S5_CHUNK_ST = S5_CHUNK_GROUPS * S5_STATE
S5_NCHUNK = S5_GROUPS // S5_CHUNK_GROUPS

N_BRANCH = 4
BRANCH_WIDTH = 512
LANE = 128
COL = 512

OFF_Q = 512
OFF_KV = 1024
OFF_HY = 1536
OFF_S5 = 3072
IN_WIDTH = 3584

N_LAT = BATCH * SEQ
N_CTX = BATCH * CTX_LEN
MT = N_LAT + N_CTX

TM = 1024
TM_EW = 512
VMEM_LIMIT = 56 * 1024 * 1024


def _cparams(*sem):
    return pltpu.CompilerParams(dimension_semantics=sem, vmem_limit_bytes=VMEM_LIMIT)


def _dot(a, b):
    return jnp.dot(a, b, preferred_element_type=F32)


SLOT_POOL, SLOT_ATTN, SLOT_HYENA, SLOT_S5 = range(4)
_ANY = pl.BlockSpec(memory_space=pl.ANY)


def _keep_branch_buffer(body, n_in):
    def wrapped(*refs):
        return body(*refs[:n_in], *refs[n_in + 1:])
    return wrapped


def _mod_row(i, tm):
    return jnp.minimum((i * tm) // SEQ, BATCH)


def _mod_spec(tm, k):
    return pl.BlockSpec((None, None, 1, D_MODEL), lambda i, *_: (_mod_row(i, tm), k, 0, 0))


def _ada_body(c_ref, w_ref, b_ref, o_ref):
    c = c_ref[...]
    a = (c * jax.nn.sigmoid(c)).astype(BF16)
    o_ref[...] = _dot(a, w_ref[...].astype(BF16)) + b_ref[...]


def _ada(cc, w_ada, b_ada):
    tn = 1024
    nw = N_MOD * D_MODEL
    return pl.pallas_call(
        _ada_body,
        grid=(DEPTH, nw // tn),
        in_specs=[
            pl.BlockSpec((8, D_MODEL), lambda l, j: (0, 0)),
            pl.BlockSpec((None, D_MODEL, tn), lambda l, j: (l, 0, j)),
            pl.BlockSpec((None, 1, tn), lambda l, j: (l, 0, j)),
        ],
        out_specs=pl.BlockSpec((None, 8, tn), lambda l, j: (l, 0, j)),
        out_shape=jax.ShapeDtypeStruct((DEPTH, 8, nw), F32),
        compiler_params=_cparams("arbitrary", "arbitrary"),
        name="ada",
    )(cc, w_ada, b_ada.reshape(DEPTH, 1, nw))


NORM_ROWS = 16


def _rows_loop(n_rows, step):
    for c in range(n_rows // NORM_ROWS):
        step(pl.ds(c * NORM_ROWS, NORM_ROWS))


def _rms_scale_rows(x_ref, r_ref, scale, shift, o_ref):
    def stats(rows):
        x = x_ref[rows, :]
        r_ref[rows, :] = lax.rsqrt(jnp.mean(x * x, axis=-1, keepdims=True) + EPS)

    def apply(rows):
        y = x_ref[rows, :] * r_ref[rows, :] * scale
        if shift is not None:
            y = y + shift
        o_ref[rows, :] = y.astype(o_ref.dtype)

    _rows_loop(x_ref.shape[0], stats)
    _rows_loop(x_ref.shape[0], apply)


def _normmod_body(x_ref, g_ref, sh_ref, sc_ref, o_ref, r_ref):
    _rms_scale_rows(x_ref, r_ref, g_ref[...] * (1.0 + sc_ref[...]), sh_ref[...], o_ref)


def _norm_body(x_ref, g_ref, o_ref, r_ref):
    _rms_scale_rows(x_ref, r_ref, g_ref[...], None, o_ref)


def _normmod(x, g, mods, base, n_rows):
    tm = TM_EW
    return pl.pallas_call(
        _normmod_body,
        grid=(n_rows // tm,),
        in_specs=[
            pl.BlockSpec((tm, D_MODEL), lambda i: (i, 0)),
            pl.BlockSpec((1, D_MODEL), lambda i: (0, 0)),
            _mod_spec(tm, base),
            _mod_spec(tm, base + 1),
        ],
        out_specs=pl.BlockSpec((tm, D_MODEL), lambda i: (i, 0)),
        out_shape=jax.ShapeDtypeStruct((n_rows, D_MODEL), BF16),
        scratch_shapes=[pltpu.VMEM((tm, 1), F32)],
        compiler_params=_cparams("arbitrary"),
        name="normmod",
    )(x, g.reshape(1, D_MODEL), mods, mods)


def _final_norm(x, g, n_rows):
    tm = TM_EW
    return pl.pallas_call(
        _norm_body,
        grid=(n_rows // tm,),
        in_specs=[
            pl.BlockSpec((tm, D_MODEL), lambda i: (i, 0)),
            pl.BlockSpec((1, D_MODEL), lambda i: (0, 0)),
        ],
        out_specs=pl.BlockSpec((tm, D_MODEL), lambda i: (i, 0)),
        out_shape=jax.ShapeDtypeStruct((n_rows, D_MODEL), F32),
        scratch_shapes=[pltpu.VMEM((tm, 1), F32)],
        compiler_params=_cparams("arbitrary"),
        name="final_norm",
    )(x, g.reshape(1, D_MODEL))


MXU_COLS = 256


WO_SLAB = 64
WO_NSLAB = D_FF // WO_SLAB


NEXT_ROWS = 128
NEXT_SLICES = TM // NEXT_ROWS


def _ffn_a_body(u0_ref, xn_ref, g_ref, sh_ref, sc_ref, wa_ref, wb_ref, wo_ref, h_ref, wob_ref, u2_ref, r_ref):
    i, j = pl.program_id(0), pl.program_id(1)
    slot = i % 2

    @pl.when((i == 0) & (j == 0))
    def _():
        u2_ref[0] = u0_ref[...]

    u = u2_ref[slot]
    for c in range(0, h_ref.shape[1], MXU_COLS):
        sl = slice(c, c + MXU_COLS)
        a = _dot(u, wa_ref[:, sl].astype(BF16))
        b = _dot(u, wb_ref[:, sl].astype(BF16))
        h_ref[:, sl] = (a * jax.nn.sigmoid(a) * b).astype(BF16)

    dst = pl.ds(pl.multiple_of(jnp.minimum(j, NEXT_SLICES - 1) * NEXT_ROWS, NEXT_ROWS), NEXT_ROWS)
    _rms_scale_rows(xn_ref, r_ref, g_ref[...] * (1.0 + sc_ref[...]), sh_ref[...], u2_ref.at[1 - slot, dst, :])

    @pl.when(pl.program_id(0) * pl.num_programs(1) + pl.program_id(1) < WO_NSLAB)
    def _():
        wob_ref[...] = wo_ref[...].astype(BF16)


def _ffn_a(x, g, mods, base, wi, wo, l, n_rows):
    tf = 512
    nf = D_FF // tf
    ni = n_rows // TM
    assert ni * nf >= WO_NSLAB and nf >= NEXT_SLICES

    def wo_blk(i, j):
        return jnp.minimum(i * nf + j, WO_NSLAB - 1)

    def nxt(i):
        return jnp.minimum(i + 1, ni - 1)

    def next_mod(k):
        return pl.BlockSpec((None, None, 1, D_MODEL), lambda i, j: (_mod_row(nxt(i), TM), k, 0, 0))

    u0 = _normmod(x, g, mods, base, TM)
    return pl.pallas_call(
        _ffn_a_body,
        grid=(ni, nf),
        in_specs=[
            pl.BlockSpec((TM, D_MODEL), lambda i, j: (0, 0)),
            pl.BlockSpec((NEXT_ROWS, D_MODEL),
                         lambda i, j: (nxt(i) * NEXT_SLICES + jnp.minimum(j, NEXT_SLICES - 1), 0)),
            pl.BlockSpec((1, D_MODEL), lambda i, j: (0, 0)),
            next_mod(base),
            next_mod(base + 1),
            pl.BlockSpec((None, D_MODEL, tf), lambda i, j: (l, 0, j)),
            pl.BlockSpec((None, D_MODEL, tf), lambda i, j: (l, 0, j + nf)),
            pl.BlockSpec((None, WO_SLAB, D_MODEL), lambda i, j: (l, wo_blk(i, j), 0)),
        ],
        out_specs=[
            pl.BlockSpec((TM, tf), lambda i, j: (i, j)),
            pl.BlockSpec((WO_SLAB, D_MODEL), lambda i, j: (wo_blk(i, j), 0)),
        ],
        out_shape=[
            jax.ShapeDtypeStruct((n_rows, D_FF), BF16),
            jax.ShapeDtypeStruct((D_FF, D_MODEL), BF16),
        ],
        scratch_shapes=[pltpu.VMEM((2, TM, D_MODEL), BF16), pltpu.VMEM((NEXT_ROWS, 1), F32)],
        compiler_params=_cparams("arbitrary", "arbitrary"),
        name="ffn_a",
    )(u0, x, g.reshape(1, D_MODEL), mods, mods, wi, wi, wo)


SIDE_SLAB = 64


def _ffn_b_body(h_ref, w_ref, x_ref, g_ref, *rest, n_slab):
    o_ref = rest[-2] if n_slab else rest[-1]
    h = h_ref[...]
    for c in range(0, o_ref.shape[1], MXU_COLS):
        sl = slice(c, c + MXU_COLS)
        o_ref[:, sl] = x_ref[:, sl] + (0.5 * g_ref[:, sl]) * _dot(h, w_ref[:, sl])

    if n_slab:
        side_ref, _, side_out_ref = rest

        @pl.when(pl.program_id(0) * pl.num_programs(1) + pl.program_id(1) < n_slab)
        def _():
            side_out_ref[...] = side_ref[...].astype(BF16)


def _ffn_b(h, wo_bf16, x, mods, gate_idx, n_rows, side=None, l=None):
    tn = 512
    nj = D_MODEL // tn
    gate = pl.BlockSpec((None, None, 1, tn), lambda i, j: (_mod_row(i, TM), gate_idx, 0, j))
    in_specs = [
        pl.BlockSpec((TM, D_FF), lambda i, j: (i, 0)),
        pl.BlockSpec((D_FF, tn), lambda i, j: (0, j)),
        pl.BlockSpec((TM, tn), lambda i, j: (i, j)),
        gate,
    ]
    out_specs = [pl.BlockSpec((TM, tn), lambda i, j: (i, j))]
    out_shape = [jax.ShapeDtypeStruct((n_rows, D_MODEL), F32)]
    args = [h, wo_bf16, x, mods]
    n_slab = 0
    if side is not None:
        _, rows, cols = side.shape
        n_slab = rows // SIDE_SLAB
        assert (n_rows // TM) * nj >= n_slab

        def slab(i, j):
            return jnp.minimum(i * nj + j, n_slab - 1)

        in_specs.append(pl.BlockSpec((None, SIDE_SLAB, cols), lambda i, j: (l, slab(i, j), 0)))
        out_specs.append(pl.BlockSpec((SIDE_SLAB, cols), lambda i, j: (slab(i, j), 0)))
        out_shape.append(jax.ShapeDtypeStruct((rows, cols), BF16))
        args.append(side)
    res = pl.pallas_call(
        functools.partial(_ffn_b_body, n_slab=n_slab),
        grid=(n_rows // TM, nj),
        in_specs=in_specs,
        out_specs=out_specs,
        out_shape=out_shape,
        compiler_params=_cparams("arbitrary", "arbitrary"),
        name="ffn_b",
    )(*args)
    return res if side is not None else res[0]


def _ffn(x, mods, g, wi, wo, l, base, n_rows, side=None):
    h, wo_bf16 = _ffn_a(x, g, mods, base, wi, wo, l, n_rows)
    return _ffn_b(h, wo_bf16, x, mods, base + 2, n_rows, side, l)


PROJ_TM = 1536
PROJ_ROWS = 256
PROJ_SLICES = PROJ_TM // PROJ_ROWS


def _proj_body(u0_ref, xn_ref, g_ref, sh_ref, sc_ref, w_ref, o_ref, uo_ref, u2_ref, r_ref):
    i, j = pl.program_id(0), pl.program_id(1)
    slot = i % 2

    @pl.when((i == 0) & (j == 0))
    def _():
        u2_ref[0] = u0_ref[...]

    @pl.when(j == 0)
    def _():
        uo_ref[...] = u2_ref[slot]

    o_ref[...] = _dot(u2_ref[slot], w_ref[...])

    dst = pl.ds(pl.multiple_of(jnp.minimum(j, PROJ_SLICES - 1) * PROJ_ROWS, PROJ_ROWS), PROJ_ROWS)
    _rms_scale_rows(xn_ref, r_ref, g_ref[...] * (1.0 + sc_ref[...]), sh_ref[...], u2_ref.at[1 - slot, dst, :])


def _proj(x, g, mods, w_in_bf16):
    tn = COL
    ni = MT // PROJ_TM
    nj = IN_WIDTH // tn
    assert nj >= PROJ_SLICES and SEQ % PROJ_ROWS == 0 and N_LAT % PROJ_ROWS == 0

    def nxt_slice(i, j):
        return jnp.minimum(i + 1, ni - 1) * PROJ_SLICES + jnp.minimum(j, PROJ_SLICES - 1)

    def next_mod(k):
        return pl.BlockSpec((None, None, 1, D_MODEL), lambda i, j: (_mod_row(nxt_slice(i, j), PROJ_ROWS), k, 0, 0))

    u0 = _normmod(x, g, mods, 3, PROJ_TM)
    return pl.pallas_call(
        _proj_body,
        grid=(ni, nj),
        in_specs=[
            pl.BlockSpec((PROJ_TM, D_MODEL), lambda i, j: (0, 0)),
            pl.BlockSpec((PROJ_ROWS, D_MODEL), lambda i, j: (nxt_slice(i, j), 0)),
            pl.BlockSpec((1, D_MODEL), lambda i, j: (0, 0)),
            next_mod(3),
            next_mod(4),
            pl.BlockSpec((D_MODEL, tn), lambda i, j: (0, j)),
        ],
        out_specs=[
            pl.BlockSpec((PROJ_TM, tn), lambda i, j: (i, j)),
            pl.BlockSpec((PROJ_TM, D_MODEL), lambda i, j: (i, 0)),
        ],
        out_shape=[
            jax.ShapeDtypeStruct((MT, IN_WIDTH), F32),
            jax.ShapeDtypeStruct((MT, D_MODEL), BF16),
        ],
        scratch_shapes=[pltpu.VMEM((2, PROJ_TM, D_MODEL), BF16), pltpu.VMEM((PROJ_ROWS, 1), F32)],
        compiler_params=_cparams("arbitrary", "arbitrary"),
        name="proj",
    )(u0, x, g.reshape(1, D_MODEL), mods, mods, w_in_bf16)


QKV_TM = 512


def _rope_tables():
    t = jnp.arange(SEQ)
    rows = (t // GRID_W).astype(F32)
    cols = (t % GRID_W).astype(F32)
    freqs = ROPE_THETA ** (-jnp.arange(ROPE_FREQS, dtype=F32) / ROPE_FREQS)
    ar = rows[:, None] * freqs[None, :]
    ac = cols[:, None] * freqs[None, :]
    cos = jnp.concatenate([jnp.cos(ar), jnp.cos(ar), jnp.cos(ac), jnp.cos(ac)], axis=-1)
    sin = jnp.concatenate([-jnp.sin(ar), jnp.sin(ar), -jnp.sin(ac), jnp.sin(ac)], axis=-1)
    cos = jnp.concatenate([cos, jnp.ones((QKV_TM, HEAD_DIM), F32)], axis=0)
    sin = jnp.concatenate([sin, jnp.zeros((QKV_TM, HEAD_DIM), F32)], axis=0)
    return cos, sin


def _lane_matrices():
    lane = jnp.arange(HEAD_DIM)
    partner = jnp.where(lane % (2 * ROPE_FREQS) < ROPE_FREQS, lane + ROPE_FREQS, lane - ROPE_FREQS)
    swap = (lane[:, None] == partner[None, :]).astype(BF16)
    return jnp.ones((HEAD_DIM, HEAD_DIM), BF16), swap


def _dot_split(x, m_ref):
    hi = x.astype(BF16)
    lo = (x - hi.astype(F32)).astype(BF16)
    return _dot(hi, m_ref[...]) + _dot(lo, m_ref[...])


def _qkv_body(q_ref, kv_ref, cos_ref, sin_ref, qn_ref, kn_ref, ones_ref, swap_ref, qo_ref, ko_ref, vo_ref):
    cos = cos_ref[...]
    sin = sin_ref[...]

    def norm_rope(xh, g):
        ss = _dot_split(xh * xh, ones_ref)
        y = xh * lax.rsqrt(ss * (1.0 / HEAD_DIM) + EPS) * g
        return y * cos + _dot_split(y, swap_ref) * sin

    scale = math.log2(math.e) / math.sqrt(HEAD_DIM)
    for h in range(N_Q_HEADS):
        sl = slice(h * HEAD_DIM, (h + 1) * HEAD_DIM)
        qo_ref[:, sl] = (norm_rope(q_ref[:, sl], qn_ref[...]) * scale).astype(BF16)
    for h in range(N_KV_HEADS):
        sl = slice(h * HEAD_DIM, (h + 1) * HEAD_DIM)
        ko_ref[:, sl] = norm_rope(kv_ref[:, sl], kn_ref[...]).astype(BF16)
    kvw = N_KV_HEADS * HEAD_DIM
    for h in range(N_KV_HEADS):
        vo_ref[:, 2 * h * HEAD_DIM:(2 * h + 1) * HEAD_DIM] = (
            kv_ref[:, kvw + h * HEAD_DIM:kvw + (h + 1) * HEAD_DIM].astype(BF16))
        vo_ref[:, (2 * h + 1) * HEAD_DIM:(2 * h + 2) * HEAD_DIM] = jnp.ones((vo_ref.shape[0], HEAD_DIM), BF16)


def _qkv(proj, cos, sin, q_norm, k_norm):
    tm = QKV_TM
    n_lat_tiles = N_LAT // tm
    per_seq = SEQ // tm
    kvw = N_KV_HEADS * HEAD_DIM

    def tab(i):
        return (jnp.where(i < n_lat_tiles, i % per_seq, per_seq), 0)

    return pl.pallas_call(
        _qkv_body,
        grid=(MT // tm,),
        in_specs=[
            pl.BlockSpec((tm, COL), lambda i: (i, OFF_Q // COL)),
            pl.BlockSpec((tm, COL), lambda i: (i, OFF_KV // COL)),
            pl.BlockSpec((tm, HEAD_DIM), tab),
            pl.BlockSpec((tm, HEAD_DIM), tab),
            pl.BlockSpec((1, HEAD_DIM), lambda i: (0, 0)),
            pl.BlockSpec((1, HEAD_DIM), lambda i: (0, 0)),
            pl.BlockSpec((HEAD_DIM, HEAD_DIM), lambda i: (0, 0)),
            pl.BlockSpec((HEAD_DIM, HEAD_DIM), lambda i: (0, 0)),
        ],
        out_specs=[
            pl.BlockSpec((tm, COL), lambda i: (i, 0)),
            pl.BlockSpec((tm, kvw), lambda i: (i, 0)),
            pl.BlockSpec((tm, 2 * kvw), lambda i: (i, 0)),
        ],
        out_shape=[
            jax.ShapeDtypeStruct((MT, COL), BF16),
            jax.ShapeDtypeStruct((MT, kvw), BF16),
            jax.ShapeDtypeStruct((MT, 2 * kvw), BF16),
        ],
        compiler_params=_cparams("arbitrary"),
        name="qkv",
    )(proj, proj, cos, sin, q_norm.reshape(1, HEAD_DIM), k_norm.reshape(1, HEAD_DIM), *_lane_matrices())


def _attn_body(*refs, with_lat):
    if with_lat:
        q_ref, kl_ref, vl_ref, kc_ref, vc_ref, o_ref = refs
    else:
        q_ref, kc_ref, vc_ref, o_ref = refs
    nt = (((1,), (1,)), ((), ()))
    half = q_ref.shape[0] // 2
    units = [(slice(r, r + half), slice(g * HEAD_DIM, (g + 1) * HEAD_DIM))
             for g in range(Q_GROUP) for r in (0, half)]
    scores = []
    for rows, sl in units:
        q = q_ref[rows, sl]
        sc = lax.dot_general(q, kc_ref[...], nt, preferred_element_type=F32)
        s_lat = lax.dot_general(q, kl_ref[...], nt, preferred_element_type=F32) if with_lat else None
        scores.append((sc, s_lat))
    for (rows, sl), (sc, s_lat) in zip(units, scores):
        m = jnp.max(sc, axis=-1, keepdims=True)
        if with_lat:
            m = jnp.maximum(m, jnp.max(s_lat, axis=-1, keepdims=True))
        o = _dot(jnp.exp2(sc - m).astype(BF16), vc_ref[...])
        if with_lat:
            o = o + _dot(jnp.exp2(s_lat - m).astype(BF16), vl_ref[...])
        o_ref[rows, sl] = (o[:, :HEAD_DIM] / o[:, HEAD_DIM:]).astype(BF16)


def _attn_lat(q, k, v, bb):
    tq = 512
    nq = SEQ // tq
    ctx_blk = N_LAT // CTX_LEN
    gw = Q_GROUP * HEAD_DIM
    return pl.pallas_call(
        _keep_branch_buffer(functools.partial(_attn_body, with_lat=True), 5),
        grid=(BATCH, N_KV_HEADS, nq),
        in_specs=[
            pl.BlockSpec((tq, gw), lambda b, h, i: (b * nq + i, h)),
            pl.BlockSpec((SEQ, HEAD_DIM), lambda b, h, i: (b, h)),
            pl.BlockSpec((SEQ, 2 * HEAD_DIM), lambda b, h, i: (b, h)),
            pl.BlockSpec((CTX_LEN, HEAD_DIM), lambda b, h, i: (ctx_blk + b, h)),
            pl.BlockSpec((CTX_LEN, 2 * HEAD_DIM), lambda b, h, i: (ctx_blk + b, h)),
            _ANY,
        ],
        out_specs=pl.BlockSpec((None, tq, gw), lambda b, h, i: (SLOT_ATTN, b * nq + i, h)),
        out_shape=jax.ShapeDtypeStruct(bb.shape, bb.dtype),
        input_output_aliases={5: 0},
        compiler_params=_cparams("arbitrary", "arbitrary", "arbitrary"),
        name="attn_lat",
    )(q, k, v, k, v, bb)


def _attn_ctx(q, k, v, bb):
    ctx_blk = N_LAT // CTX_LEN
    gw = Q_GROUP * HEAD_DIM
    return pl.pallas_call(
        _keep_branch_buffer(functools.partial(_attn_body, with_lat=False), 3),
        grid=(BATCH, N_KV_HEADS),
        in_specs=[
            pl.BlockSpec((CTX_LEN, gw), lambda b, h: (ctx_blk + b, h)),
            pl.BlockSpec((CTX_LEN, HEAD_DIM), lambda b, h: (ctx_blk + b, h)),
            pl.BlockSpec((CTX_LEN, 2 * HEAD_DIM), lambda b, h: (ctx_blk + b, h)),
            _ANY,
        ],
        out_specs=pl.BlockSpec((None, CTX_LEN, gw), lambda b, h: (SLOT_ATTN, ctx_blk + b, h)),
        out_shape=jax.ShapeDtypeStruct(bb.shape, bb.dtype),
        input_output_aliases={3: 0},
        compiler_params=_cparams("arbitrary", "arbitrary"),
        name="attn_ctx",
    )(q, k, v, bb)


POOL_PAD = 8


def _pool_body(a_ref, w_ref, s_ref, o_ref, *, seq):
    lp = seq + 2 * POOL_PAD
    t = lax.broadcasted_iota(jnp.int32, (seq, LANE), 0)
    zpad = jnp.zeros((POOL_PAD, LANE), F32)
    for gi, win in enumerate(POOL_WINDOWS):
        sl = slice(gi * LANE, (gi + 1) * LANE)
        a = a_ref[:, sl]
        s = jnp.concatenate([zpad, a, zpad], axis=0)
        s = s + pltpu.roll(s, 1, 0)
        half = 1
        while 2 * half < win:
            s = pltpu.roll(s, half, 0) + pltpu.roll(s, lp - half, 0)
            half *= 2
        s = s[POOL_PAD:POOL_PAD + seq]
        lo = jnp.maximum(t - win // 2, 0)
        hi = jnp.minimum(t + win // 2, seq)
        pooled = s / (hi - lo).astype(F32) - a
        y = _dot(pooled.astype(BF16), w_ref[gi].astype(BF16))
        o_ref[:, sl] = (y * s_ref[:, sl]).astype(BF16)


def _pool(proj, pool_w, pool_scale, bb, seq, row_blk0):
    width = len(POOL_WINDOWS) * LANE
    return pl.pallas_call(
        _keep_branch_buffer(functools.partial(_pool_body, seq=seq), 3),
        grid=(BATCH,),
        in_specs=[
            pl.BlockSpec((seq, width), lambda b: (row_blk0 + b, 0)),
            pl.BlockSpec((len(POOL_WINDOWS), LANE, LANE), lambda b: (0, 0, 0)),
            pl.BlockSpec((1, width), lambda b: (0, 0)),
            _ANY,
        ],
        out_specs=pl.BlockSpec((None, seq, width), lambda b: (SLOT_POOL, row_blk0 + b, 0)),
        out_shape=jax.ShapeDtypeStruct(bb.shape, bb.dtype),
        input_output_aliases={3: 0},
        compiler_params=_cparams("arbitrary"),
        name="pool",
    )(proj, pool_w, pool_scale.reshape(1, width), bb)


def _hy_prep_body(x0_ref, x1_ref, v_ref, w0_ref, w1_ref, wv_ref, b0_ref, b1_ref, bv_ref,
                  x0o_ref, vx_ref, vb_ref, *, seq):
    t = lax.broadcasted_iota(jnp.int32, x0_ref.shape, 0)

    def conv(x_ref, w_ref, b_ref):
        x = x_ref[...]
        prev = jnp.where(t >= 1, pltpu.roll(x, 1, 0), 0.0)
        nxt = jnp.where(t <= seq - 2, pltpu.roll(x, seq - 1, 0), 0.0)
        return prev * w_ref[0:1, :] + x * w_ref[1:2, :] + nxt * w_ref[2:3, :] + b_ref[...]

    x0o_ref[...] = conv(x0_ref, w0_ref, b0_ref)
    vx = conv(v_ref, wv_ref, bv_ref) * conv(x1_ref, w1_ref, b1_ref)
    vx_ref[...] = vx
    vb_ref[...] = vx.astype(BF16)


def _hy_prep(proj, short_w, short_b, seq, row_blk0):
    tc = HYENA_WIDTH
    nc = HYENA_WIDTH // tc
    c0 = OFF_HY // tc
    short_b = short_b.reshape(1, 3 * HYENA_WIDTH)

    def xspec(part):
        return pl.BlockSpec((seq, tc), lambda b, c: (row_blk0 + b, c0 + part * nc + c))

    def wspec(part, rows):
        return pl.BlockSpec((rows, tc), lambda b, c: (0, part * nc + c))

    return pl.pallas_call(
        functools.partial(_hy_prep_body, seq=seq),
        grid=(BATCH, nc),
        in_specs=[xspec(0), xspec(1), xspec(2), wspec(0, 3), wspec(1, 3), wspec(2, 3),
                  wspec(0, 1), wspec(1, 1), wspec(2, 1)],
        out_specs=[
            pl.BlockSpec((seq, tc), lambda b, c: (b, c)),
            pl.BlockSpec((seq, tc), lambda b, c: (b, c)),
            pl.BlockSpec((seq, tc), lambda b, c: (0, b * nc + c)),
        ],
        out_shape=[
            jax.ShapeDtypeStruct((BATCH * seq, HYENA_WIDTH), F32),
            jax.ShapeDtypeStruct((BATCH * seq, HYENA_WIDTH), F32),
            jax.ShapeDtypeStruct((seq, BATCH * HYENA_WIDTH), BF16),
        ],
        compiler_params=_cparams("arbitrary", "arbitrary"),
        name="hy_prep",
    )(proj, proj, proj, short_w, short_w, short_w, short_b, short_b, short_b)


def _hy_feats(seq):
    t = jnp.linspace(0.0, 1.0, seq, dtype=F32)[:, None]
    f = jnp.linspace(1e-4, HYENA_BANDS - 1, HYENA_BANDS, dtype=F32)
    w = 2.0 * math.pi * jnp.arange(seq, dtype=F32) / seq
    fw = w[:, None] * f[None, :]
    z = jnp.concatenate([t, jnp.cos(fw), -jnp.sin(fw)], axis=-1)
    return jnp.pad(z, ((0, 0), (0, LANE - HYENA_EMB)))


def _hy_deltas():
    max_decay = math.log(HYENA_TARGET) / HYENA_FAST_PCT
    min_decay = math.log(HYENA_TARGET) / HYENA_SLOW_PCT
    return jnp.abs(jnp.linspace(min_decay, max_decay, HYENA_WIDTH, dtype=F32)).reshape(1, HYENA_WIDTH)


def _hy_filter_body(z_ref, w1_ref, b1_ref, w2_ref, b2_ref, w3_ref, fr_ref, dl_ref, k_ref, nyq_ref, *, seq):
    hp = lax.Precision.HIGHEST
    freq = fr_ref[...]
    h = jnp.sin(freq * (jnp.dot(z_ref[...], w1_ref[...], precision=hp, preferred_element_type=F32) + b1_ref[...]))
    h = jnp.sin(freq * (jnp.dot(h, w2_ref[...], precision=hp, preferred_element_type=F32) + b2_ref[...]))
    h = jnp.dot(h, w3_ref[...], precision=hp, preferred_element_type=F32)
    ti = lax.broadcasted_iota(jnp.int32, (seq, HYENA_WIDTH), 0)
    decay = jnp.exp(-(ti.astype(F32) * (1.0 / (seq - 1))) * dl_ref[...])
    hf = h[:, :HYENA_WIDTH] * decay
    hb = jnp.where(ti == 0, 0.0, h[:, HYENA_WIDTH:] * decay)
    ks = hf + hb
    k_ref[:, :HYENA_WIDTH] = ks.astype(BF16)
    k_ref[:, HYENA_WIDTH:] = (hf - hb).astype(BF16)
    nyq = jnp.sum(jnp.where(ti % 2 == 0, ks, -ks), axis=0, keepdims=True)
    nyq_ref[...] = jnp.broadcast_to(nyq, nyq_ref.shape)


def _hy_filter(lp, z, deltas, seq):
    w1 = jnp.pad(lp['hy_f1_w'], ((0, LANE - HYENA_EMB), (0, 0)))
    args = (z, w1, lp['hy_f1_b'].reshape(1, -1), lp['hy_f2_w'], lp['hy_f2_b'].reshape(1, -1), lp['hy_f3_w'],
            lp['hy_freq'].reshape(1, -1), deltas)
    full = lambda a: pl.BlockSpec(a.shape, lambda i: (0,) * a.ndim)
    return pl.pallas_call(
        functools.partial(_hy_filter_body, seq=seq),
        grid=(1,),
        in_specs=[full(a) for a in args],
        out_specs=[pl.BlockSpec((seq, 2 * HYENA_WIDTH), lambda i: (0, 0)),
                   pl.BlockSpec((8, HYENA_WIDTH), lambda i: (0, 0))],
        out_shape=[jax.ShapeDtypeStruct((seq, 2 * HYENA_WIDTH), BF16),
                   jax.ShapeDtypeStruct((8, HYENA_WIDTH), F32)],
        compiler_params=_cparams("arbitrary"),
        name="hy_filter",
    )(*args)


DFT_SPLIT = 64


def _dft_matrices(seq):
    n = 2 * seq
    f = jnp.arange(seq, dtype=jnp.int32)[:, None]

    def table(step, count):
        idx = (f * (jnp.arange(count, dtype=jnp.int32)[None, :] * step)) % n
        ang = idx.astype(F32) * (2.0 * math.pi / n)
        return jnp.cos(ang), jnp.sin(ang)

    hc, hs = table(DFT_SPLIT, seq // DFT_SPLIT)
    lc, ls = table(1, DFT_SPLIT)
    cos = (hc[:, :, None] * lc[:, None, :] - hs[:, :, None] * ls[:, None, :]).reshape(seq, seq)
    sin = (hs[:, :, None] * lc[:, None, :] + hc[:, :, None] * ls[:, None, :]).reshape(seq, seq)
    s = jnp.arange(seq, dtype=jnp.int32)[None, :]
    nyq = jnp.where(s % 2 == 0, 1.0, -1.0).astype(F32)
    wf = jnp.concatenate([cos, jnp.where(f == 0, nyq, -sin)], axis=0).astype(BF16)
    return wf, wf.T


def _dft_fwd_body(wc_ref, ws_ref, x_ref, k_ref, nyq_ref, pr_ref, pi_ref, hr_ref, hi_ref, *, n):
    tf = wc_ref.shape[0]

    @pl.when(pl.program_id(1) == 0)
    def _():
        hr_ref[...] = _dot(wc_ref[...], k_ref[:, :HYENA_WIDTH])
        hi_ref[...] = _dot(ws_ref[...], k_ref[:, HYENA_WIDTH:])

    freq = lax.broadcasted_iota(jnp.int32, (tf, MXU_COLS), 0) + pl.program_id(0) * tf
    first = freq == 0
    w = jnp.where(first, 1.0 / n, 2.0 / n)
    chunks = [slice(c, c + MXU_COLS) for c in range(0, x_ref.shape[1], MXU_COLS)]
    spectra = [(_dot(wc_ref[...], x_ref[:, sl]), _dot(ws_ref[...], x_ref[:, sl])) for sl in chunks]
    for sl, (xr, xi) in zip(chunks, spectra):
        hr = hr_ref[:, sl]
        hi = hi_ref[:, sl]
        pr_ref[:, sl] = (jnp.where(first, xr * hr, xr * hr - xi * hi) * w).astype(BF16)
        pi_ref[:, sl] = (jnp.where(first, xi * nyq_ref[0:1, sl], xr * hi + xi * hr) * w).astype(BF16)


def _dft_fwd(wf, x, k, nyq):
    n, seq = wf.shape
    tf = min(seq, 512)
    nf = seq // tf
    out = jax.ShapeDtypeStruct((seq, BATCH * HYENA_WIDTH), BF16)
    return pl.pallas_call(
        functools.partial(_dft_fwd_body, n=n),
        grid=(nf, BATCH),
        in_specs=[
            pl.BlockSpec((tf, seq), lambda f, b: (f, 0)),
            pl.BlockSpec((tf, seq), lambda f, b: (nf + f, 0)),
            pl.BlockSpec((seq, HYENA_WIDTH), lambda f, b: (0, b)),
            pl.BlockSpec((seq, 2 * HYENA_WIDTH), lambda f, b: (0, 0)),
            pl.BlockSpec((8, HYENA_WIDTH), lambda f, b: (0, 0)),
        ],
        out_specs=[pl.BlockSpec((tf, HYENA_WIDTH), lambda f, b: (f, b))] * 2,
        out_shape=[out, out],
        scratch_shapes=[pltpu.VMEM((tf, HYENA_WIDTH), F32), pltpu.VMEM((tf, HYENA_WIDTH), F32)],
        compiler_params=_cparams("arbitrary", "arbitrary"),
        name="dft_fwd",
    )(wf, wf, x, k, nyq)


def _dft_inv_body(wc_ref, ws_ref, pr_ref, pi_ref, vx_ref, bias_ref, x0_ref, o_ref):
    y = _dot(wc_ref[...], pr_ref[...]) + _dot(ws_ref[...], pi_ref[...])
    o_ref[...] = ((y + vx_ref[...] * bias_ref[...]) * x0_ref[...]).astype(BF16)


def _dft_inv(wi, p_re, p_im, vx, bias, x0, bb, row0):
    seq, n = wi.shape
    tm = min(seq, 1024)
    tn = HYENA_WIDTH
    nt = seq // tm
    return pl.pallas_call(
        _keep_branch_buffer(_dft_inv_body, 7),
        grid=(nt, BATCH),
        in_specs=[
            pl.BlockSpec((tm, seq), lambda i, b: (i, 0)),
            pl.BlockSpec((tm, seq), lambda i, b: (i, 1)),
            pl.BlockSpec((seq, tn), lambda i, b: (0, b)),
            pl.BlockSpec((seq, tn), lambda i, b: (0, b)),
            pl.BlockSpec((tm, tn), lambda i, b: (b * nt + i, 0)),
            pl.BlockSpec((1, tn), lambda i, b: (0, 0)),
            pl.BlockSpec((tm, tn), lambda i, b: (b * nt + i, 0)),
            _ANY,
        ],
        out_specs=pl.BlockSpec((None, tm, tn), lambda i, b: (SLOT_HYENA, row0 // tm + b * nt + i, 0)),
        out_shape=jax.ShapeDtypeStruct(bb.shape, bb.dtype),
        input_output_aliases={7: 0},
        compiler_params=_cparams("arbitrary", "arbitrary"),
        name="dft_inv",
    )(wi, wi, p_re, p_im, vx, bias.reshape(1, tn), x0, bb)


def _hyena(proj, lp, consts, bb, seq, row0):
    z, deltas, (wf, wi) = consts
    x0, vx, vb = _hy_prep(proj, lp['hy_short_w'], lp['hy_short_b'], seq, row0 // seq)
    k, nyq = _hy_filter(lp, z, deltas, seq)
    p_re, p_im = _dft_fwd(wf, vb, k, nyq)
    return _dft_inv(wi, p_re, p_im, vx, lp['hy_bias'], x0, bb, row0)


def _s5_params(lp, n_seg_steps):
    a_re, a_im = lp['s5_a_re'], lp['s5_a_im']
    dt = jnp.exp(lp['s5_log_dt'])[..., None]
    mag = jnp.exp(a_re * dt)
    ab_re, ab_im = mag * jnp.cos(a_im * dt), mag * jnp.sin(a_im * dt)
    den = a_re * a_re + a_im * a_im
    nr, ni = ab_re - 1.0, ab_im
    cf_re = (nr * a_re + ni * a_im) / den
    cf_im = (ni * a_re - nr * a_im) / den
    b_re, b_im = lp['s5_b_re'], lp['s5_b_im']
    bb_re = cf_re[..., None] * b_re - cf_im[..., None] * b_im
    bb_im = cf_re[..., None] * b_im + cf_im[..., None] * b_re
    eye = jnp.eye(S5_CHUNK_GROUPS, dtype=F32)

    def bdiag_in(m):
        m = m.reshape(2, S5_NCHUNK, S5_CHUNK_GROUPS, S5_STATE, S5_GC)
        return jnp.einsum('dqgpc,gh->dqgchp', m, eye).reshape(2, S5_NCHUNK, S5_CHUNK_CH, S5_CHUNK_ST)

    def bdiag_out(m):
        m = m.reshape(2, S5_NCHUNK, S5_CHUNK_GROUPS, S5_GC, S5_STATE)
        return jnp.einsum('dqgcp,gh->dqhpgc', m, eye).reshape(2, S5_NCHUNK, S5_CHUNK_ST, S5_CHUNK_CH)

    bbd = jnp.concatenate([bdiag_in(bb_re), bdiag_in(bb_im)], axis=-1).astype(BF16)
    cbd = jnp.concatenate([bdiag_out(lp['s5_c_re']), -bdiag_out(lp['s5_c_im'])], axis=-2).astype(BF16)
    a = jnp.stack([ab_re.reshape(2, -1), ab_im.reshape(2, -1)], axis=1)
    aks = []
    for steps in n_seg_steps:
        pr, pi = ab_re, ab_im
        for _ in range(int(math.log2(steps))):
            pr, pi = pr * pr - pi * pi, 2.0 * pr * pi
        aks.append(jnp.stack([pr.reshape(2, -1), pi.reshape(2, -1)], axis=1))
    return [(bbd, cbd, a, ak) for ak in aks]


S5_BLOCK = 2048


def _s5_scan_body(u_ref, bbd_ref, cbd_ref, a_ref, ak_ref, dsk_ref, h0_ref, y_ref, hl_ref,
                  xr0_ref, xr1_ref, xi0_ref, xi1_ref, yd0_ref, yd1_ref, up_ref, ub_ref, *, seq):
    nk = seq // S5_LANES
    blk = min(seq, S5_BLOCK)
    nblk = seq // blk
    tiles = blk // S5_LANES
    shape = (S5_LANES, S5_CHUNK_ST)
    xr_ref, xi_ref, yd_ref = (xr0_ref, xr1_ref), (xi0_ref, xi1_ref), (yd0_ref, yd1_ref)
    for j in range(S5_LANES):
        up_ref[pl.ds(j, nk, stride=S5_LANES), :] = u_ref[pl.ds(j * nk, nk), :]
    ub_ref[...] = up_ref[...].astype(BF16)
    row = lax.broadcasted_iota(jnp.int32, shape, 0)
    zero = jnp.zeros(shape, F32)

    def block(r):
        return pl.ds(pl.multiple_of(r * blk, blk), blk)

    def in_proj(d, rows):
        xr_ref[d][rows, :] = _dot(ub_ref[rows, :], bbd_ref[d, :, :S5_CHUNK_ST])
        xi_ref[d][rows, :] = _dot(ub_ref[rows, :], bbd_ref[d, :, S5_CHUNK_ST:])

    def out_proj(d, rows):
        yd_ref[d][rows, :] = (_dot(xr_ref[d][rows, :].astype(BF16), cbd_ref[d, :S5_CHUNK_ST, :])
                              + _dot(xi_ref[d][rows, :].astype(BF16), cbd_ref[d, S5_CHUNK_ST:, :]))

    def coeffs(d):
        return jnp.broadcast_to(a_ref[d, 0:1, :], shape), jnp.broadcast_to(a_ref[d, 1:2, :], shape)

    def block_tiles(d, r):
        base = pl.multiple_of(r * blk, blk)
        order = range(tiles) if d == 0 else range(tiles - 1, -1, -1)
        return [pl.ds(base + t * S5_LANES, S5_LANES) for t in order]

    def scan_block(d, r, carry):
        ar, ai = coeffs(d)
        xr, xi = carry
        for rows in block_tiles(d, r):
            xr, xi = (ar * xr - ai * xi + xr_ref[d][rows, :], ar * xi + ai * xr + xi_ref[d][rows, :])
            xr_ref[d][rows, :] = xr
            xi_ref[d][rows, :] = xi
        return xr, xi

    def fix_block(d, r, carry):
        ar, ai = coeffs(d)
        gr, gi = carry
        for rows in block_tiles(d, r):
            gr, gi = ar * gr - ai * gi, ar * gi + ai * gr
            xr_ref[d][rows, :] += gr
            xi_ref[d][rows, :] += gi
        return gr, gi

    def visit(d, s):
        return s if d == 0 else nblk - 1 - s

    def entering_states(d, er, ei):
        akr, aki = ak_ref[d, 0:1, :], ak_ref[d, 1:2, :]
        hr, hi = h0_ref[2 * d:2 * d + 1, :], h0_ref[2 * d + 1:2 * d + 2, :]
        in_r, in_i = zero, zero
        for j in (range(S5_LANES) if d == 0 else range(S5_LANES - 1, -1, -1)):
            in_r = jnp.where(row == j, hr, in_r)
            in_i = jnp.where(row == j, hi, in_i)
            hr, hi = (akr * hr - aki * hi + er[j:j + 1, :], akr * hi + aki * hr + ei[j:j + 1, :])
        hl_ref[2 * d:2 * d + 1, :] = hr
        hl_ref[2 * d + 1:2 * d + 2, :] = hi
        return in_r, in_i

    def fix_all(d, ins):
        lax.fori_loop(0, nblk, lambda s, carry: fix_block(d, visit(d, s), carry), ins)

    in_proj(0, slice(None))

    def scan0_body(s, carry):
        in_proj(1, block(s))
        return scan_block(0, s, carry)
    ends0 = lax.fori_loop(0, nblk, scan0_body, (zero, zero))
    fix_all(0, entering_states(0, *ends0))

    def scan1_body(s, carry):
        out_proj(0, block(s))
        return scan_block(1, visit(1, s), carry)
    ends1 = lax.fori_loop(0, nblk, scan1_body, (zero, zero))
    fix_all(1, entering_states(1, *ends1))
    out_proj(1, slice(None))

    up_ref[...] = yd_ref[0][...] + yd_ref[1][...] + up_ref[...] * (dsk_ref[0] + dsk_ref[1])
    for j in range(S5_LANES):
        y_ref[pl.ds(j * nk, nk), :] = up_ref[pl.ds(j, nk, stride=S5_LANES), :]


def _s5_scan(proj, params, dskip, h0, seq, row_blk0):
    bbd, cbd, a, ak = params
    nq = S5_NCHUNK
    col0 = OFF_S5 // S5_CHUNK_CH
    return pl.pallas_call(
        functools.partial(_s5_scan_body, seq=seq),
        grid=(BATCH, nq),
        in_specs=[
            pl.BlockSpec((seq, S5_CHUNK_CH), lambda b, q: (row_blk0 + b, col0 + q)),
            pl.BlockSpec((2, None, S5_CHUNK_CH, 2 * S5_CHUNK_ST), lambda b, q: (0, q, 0, 0)),
            pl.BlockSpec((2, None, 2 * S5_CHUNK_ST, S5_CHUNK_CH), lambda b, q: (0, q, 0, 0)),
            pl.BlockSpec((2, 2, S5_CHUNK_ST), lambda b, q: (0, 0, q)),
            pl.BlockSpec((2, 2, S5_CHUNK_ST), lambda b, q: (0, 0, q)),
            pl.BlockSpec((2, 1, S5_CHUNK_CH), lambda b, q: (0, 0, q)),
            pl.BlockSpec((None, 4, S5_CHUNK_ST), lambda b, q: (b, 0, q)),
        ],
        out_specs=[
            pl.BlockSpec((seq, S5_CHUNK_CH), lambda b, q: (b, q)),
            pl.BlockSpec((None, 4, S5_CHUNK_ST), lambda b, q: (b, 0, q)),
        ],
        out_shape=[
            jax.ShapeDtypeStruct((BATCH * seq, S5_WIDTH), F32),
            jax.ShapeDtypeStruct((BATCH, 4, S5_GROUPS * S5_STATE), F32),
        ],
        scratch_shapes=[pltpu.VMEM((seq, S5_CHUNK_ST), F32)] * 4 + [pltpu.VMEM((seq, S5_CHUNK_CH), F32)] * 3
        + [pltpu.VMEM((seq, S5_CHUNK_CH), BF16)],
        compiler_params=_cparams("arbitrary", "arbitrary"),
        name="s5_scan",
    )(proj, bbd, cbd, a, ak, dskip.reshape(2, 1, S5_WIDTH), h0)


def _s5_glu_body(y_ref, w_ref, b_ref, o_ref):
    w = w_ref[...].astype(BF16)
    half = y_ref.shape[0] // 2
    halves = [slice(0, half), slice(half, 2 * half)]
    gs = [_dot(jax.nn.gelu(y_ref[rows, :]).astype(BF16), w) + b_ref[...] for rows in halves]
    for rows, g in zip(halves, gs):
        o_ref[rows, :] = (g[:, :S5_WIDTH] * jax.nn.sigmoid(g[:, S5_WIDTH:])).astype(BF16)


def _s5_glu(y, w, b, bb, row0):
    n_rows = y.shape[0]
    tm = TM_EW
    return pl.pallas_call(
        _keep_branch_buffer(_s5_glu_body, 3),
        grid=(n_rows // tm,),
        in_specs=[
            pl.BlockSpec((tm, S5_WIDTH), lambda i: (i, 0)),
            pl.BlockSpec((S5_WIDTH, 2 * S5_WIDTH), lambda i: (0, 0)),
            pl.BlockSpec((1, 2 * S5_WIDTH), lambda i: (0, 0)),
            _ANY,
        ],
        out_specs=pl.BlockSpec((None, tm, S5_WIDTH), lambda i: (SLOT_S5, row0 // tm + i, 0)),
        out_shape=jax.ShapeDtypeStruct(bb.shape, bb.dtype),
        input_output_aliases={3: 0},
        compiler_params=_cparams("arbitrary"),
        name="s5_glu",
    )(y, w, b.reshape(1, -1), bb)


def _merge_body(u_ref, *refs):
    wg_refs, bg_refs, y_refs, wb_refs = (refs[k * N_BRANCH:(k + 1) * N_BRANCH] for k in range(4))
    o_ref = refs[4 * N_BRANCH]
    u = u_ref[...]
    acc = None
    for n in range(N_BRANCH):
        gate = jax.nn.sigmoid(_dot(u, wg_refs[n][...].astype(BF16)) + bg_refs[n][...])
        contrib = gate * _dot(y_refs[n][...], wb_refs[n][...].astype(BF16))
        acc = contrib if acc is None else acc + contrib
    o_ref[...] = acc.astype(BF16)


def _merge(u, w_gate, b_gate, branches, w_branch, l, n_rows):
    tc = MXU_COLS
    ncol = D_MODEL // tc
    b_gate = b_gate.reshape(1, -1)
    per_branch = lambda make: [make(n) for n in range(N_BRANCH)]
    return pl.pallas_call(
        _merge_body,
        grid=(n_rows // TM, ncol),
        in_specs=[pl.BlockSpec((TM, D_MODEL), lambda i, c: (i, 0))]
        + per_branch(lambda n: pl.BlockSpec((None, D_MODEL, tc), lambda i, c: (l, 0, n * ncol + c)))
        + per_branch(lambda n: pl.BlockSpec((1, tc), lambda i, c: (0, n * ncol + c)))
        + per_branch(lambda n: pl.BlockSpec((None, TM, BRANCH_WIDTH), lambda i, c: (n, i, 0)))
        + per_branch(lambda n: pl.BlockSpec((None, None, BRANCH_WIDTH, tc), lambda i, c: (l, n, 0, c))),
        out_specs=pl.BlockSpec((TM, tc), lambda i, c: (i, c)),
        out_shape=jax.ShapeDtypeStruct((n_rows, D_MODEL), BF16),
        compiler_params=_cparams("arbitrary", "arbitrary"),
        name="merge",
    )(u, *([w_gate] * N_BRANCH), *([b_gate] * N_BRANCH), *([branches] * N_BRANCH), *([w_branch] * N_BRANCH))


def _out_body(m_ref, w_ref, x_ref, g_ref, o_ref):
    o_ref[...] = x_ref[...] + g_ref[...] * _dot(m_ref[...], w_ref[...].astype(BF16))


def _out_proj(merged, w_out, l, x, mods, n_rows):
    tn = 1024
    gate = pl.BlockSpec((None, None, 1, tn), lambda j, i: (_mod_row(i, TM), 5, 0, j))
    return pl.pallas_call(
        _out_body,
        grid=(D_MODEL // tn, n_rows // TM),
        in_specs=[
            pl.BlockSpec((TM, D_MODEL), lambda j, i: (i, 0)),
            pl.BlockSpec((None, D_MODEL, tn), lambda j, i: (l, 0, j)),
            pl.BlockSpec((TM, tn), lambda j, i: (i, j)),
            gate,
        ],
        out_specs=pl.BlockSpec((TM, tn), lambda j, i: (i, j)),
        out_shape=jax.ShapeDtypeStruct((n_rows, D_MODEL), F32),
        compiler_params=_cparams("arbitrary", "arbitrary"),
        name="out_proj",
    )(merged, w_out, x, mods)


def _mixer(x, g, lp, big, w_in_bf16, l, mods, consts, bb, last):
    cos, sin, hy_lat, hy_ctx = consts
    proj, u = _proj(x, g, mods, w_in_bf16)
    q, k, v = _qkv(proj, cos, sin, lp['q_norm'], lp['k_norm'])

    ctx_blk = N_LAT // CTX_LEN
    zero_h = jnp.zeros((BATCH, 4, S5_GROUPS * S5_STATE), F32)
    par_ctx, par_lat = _s5_params(lp, (CTX_LEN // S5_LANES, SEQ // S5_LANES))
    ys_ctx, h_ctx = _s5_scan(proj, par_ctx, lp['s5_d'], zero_h, CTX_LEN, ctx_blk)
    ys_lat, _ = _s5_scan(proj, par_lat, lp['s5_d'], h_ctx, SEQ, 0)

    bb = _s5_glu(ys_lat, lp['s5_glu_w'], lp['s5_glu_b'], bb, 0)
    bb = _attn_lat(q, k, v, bb)
    bb = _pool(proj, lp['pool_w'], lp['pool_scale'], bb, SEQ, 0)
    bb = _hyena(proj, lp, hy_lat, bb, SEQ, 0)
    if not last:
        bb = _s5_glu(ys_ctx, lp['s5_glu_w'], lp['s5_glu_b'], bb, N_LAT)
        bb = _attn_ctx(q, k, v, bb)
        bb = _pool(proj, lp['pool_w'], lp['pool_scale'], bb, CTX_LEN, ctx_blk)
        bb = _hyena(proj, lp, hy_ctx, bb, CTX_LEN, N_LAT)

    n_rows = N_LAT if last else MT
    merged = _merge(u, big['w_gate'], lp['b_gate'], bb, big['w_branch'], l, n_rows)
    return _out_proj(merged, big['w_out'], l, x, mods, n_rows), bb


def kernel(x, c, ctx, c_ctx, w_ada, b_ada, norm_ffn1, norm_mix, norm_ffn2, norm_final, ffn1_wi, ffn1_wo, ffn2_wi, ffn2_wo, w_in, w_gate, b_gate, w_branch, w_out, pool_w, pool_scale, q_norm, k_norm, hy_short_w, hy_short_b, hy_f1_w, hy_f1_b, hy_f2_w, hy_f2_b, hy_f3_w, hy_freq, hy_bias, s5_a_re, s5_a_im, s5_log_dt, s5_b_re, s5_b_im, s5_c_re, s5_c_im, s5_d, s5_glu_w, s5_glu_b):
    big = dict(w_gate=w_gate, w_branch=w_branch, w_out=w_out)
    per_layer = dict(
        b_gate=b_gate, pool_w=pool_w,
        pool_scale=pool_scale, q_norm=q_norm, k_norm=k_norm, hy_short_w=hy_short_w, hy_short_b=hy_short_b,
        hy_f1_w=hy_f1_w, hy_f1_b=hy_f1_b, hy_f2_w=hy_f2_w, hy_f2_b=hy_f2_b, hy_f3_w=hy_f3_w, hy_freq=hy_freq,
        hy_bias=hy_bias, s5_a_re=s5_a_re, s5_a_im=s5_a_im, s5_log_dt=s5_log_dt, s5_b_re=s5_b_re, s5_b_im=s5_b_im,
        s5_c_re=s5_c_re, s5_c_im=s5_c_im, s5_d=s5_d, s5_glu_w=s5_glu_w, s5_glu_b=s5_glu_b)

    cos, sin = _rope_tables()
    deltas = _hy_deltas()
    consts = (cos, sin,
              (_hy_feats(SEQ), deltas, _dft_matrices(SEQ)),
              (_hy_feats(CTX_LEN), deltas, _dft_matrices(CTX_LEN)))

    cc = jnp.concatenate([c, c_ctx[None], jnp.zeros((8 - BATCH - 1, D_MODEL), F32)], axis=0)
    mods_all = _ada(cc, w_ada, b_ada).reshape(DEPTH, 8, N_MOD, 1, D_MODEL)

    xs = jnp.concatenate([x.reshape(N_LAT, D_MODEL), ctx.reshape(N_CTX, D_MODEL)], axis=0)
    bb = jnp.zeros((N_BRANCH, MT, BRANCH_WIDTH), BF16)
    for l in range(DEPTH):
        last = l == DEPTH - 1
        lp = {name: w[l] for name, w in per_layer.items()}
        mods = mods_all[l]
        xs, w_in_bf16 = _ffn(xs, mods, norm_ffn1[l], ffn1_wi, ffn1_wo, l, 0, MT, side=w_in)
        xs, bb = _mixer(xs, norm_mix[l], lp, big, w_in_bf16, l, mods, consts, bb, last)
        xs = _ffn(xs, mods, norm_ffn2[l], ffn2_wi, ffn2_wo, l, 6, N_LAT if last else MT)
    return _final_norm(xs, norm_final, N_LAT).reshape(BATCH, SEQ, D_MODEL)
```
